```python
import jax
import jax.numpy as jnp
from jax import lax
import numpy as np

D_MODEL = 1024
BATCH = 2
SEQ = 8192
DEPTH = 2

GRID_W = 64
CTX_LEN = 256
EPS = 1e-6

FOURIER_GROUPS = 4
FOURIER_GW = D_MODEL // 16
FOURIER_W = FOURIER_GROUPS * FOURIER_GW
CONV_W = D_MODEL // 4
CONV_K = 31
POOL_WINDOWS = (2, 4, 8, 16)
POOL_GW = D_MODEL // 16
POOL_W = len(POOL_WINDOWS) * POOL_GW
HEAD_DIM = 64
N_HEADS = D_MODEL // 128
N_KV_HEADS = N_HEADS // 4
Q_PER_KV = N_HEADS // N_KV_HEADS
Q_W = N_HEADS * HEAD_DIM
KV_W = N_KV_HEADS * HEAD_DIM
ROPE_THETA = 10000.0
Q_BLOCK = 128
N_BRANCH = 4

OFF_F = 0
OFF_C = OFF_F + FOURIER_W
OFF_P = OFF_C + 2 * CONV_W
OFF_Q = OFF_P + POOL_W
OFF_K = OFF_Q + Q_W
OFF_V = OFF_K + KV_W
OFF_G = OFF_V + KV_W
IN_W = OFF_G + N_BRANCH * D_MODEL

D_FF = ((8 * D_MODEL // 3 + 127) // 128) * 128
N_EXPERTS = 8
TOP_K = 2
D_FF_EXPERT = 7 * D_MODEL // 2
N_DENSE = (DEPTH + 1) // 2
N_MOE = DEPTH // 2

kernel_name = 'hybrid_parallel_fnet_conformer_pool_gqa_moe_dit'


def rmsnorm(x, g):
    xf = x.astype(jnp.float32)
    y = xf * lax.rsqrt(jnp.mean(xf * xf, axis=-1, keepdims=True) + EPS)
    return (y * g.astype(jnp.float32)).astype(x.dtype)


def modulation(cond, w_mod, b_mod):
    m = jax.nn.silu(cond) @ w_mod + b_mod
    return jnp.split(m, 6, axis=-1)


def axial_rope_tables(rows):
    n_freq = HEAD_DIM // 4
    freqs = ROPE_THETA ** (-jnp.arange(n_freq, dtype=jnp.float32) / n_freq)
    row = jnp.repeat(jnp.arange(rows, dtype=jnp.float32), GRID_W)
    col = jnp.tile(jnp.arange(GRID_W, dtype=jnp.float32), rows)
    ang_r = row[:, None] * freqs
    ang_c = col[:, None] * freqs
    return (jnp.cos(ang_r), jnp.sin(ang_r), jnp.cos(ang_c), jnp.sin(ang_c))


def _rotate_half(xh, cos, sin):
    x1, x2 = jnp.split(xh, 2, axis=-1)
    cos = cos[None, :, None, :]
    sin = sin[None, :, None, :]
    return jnp.concatenate([x1 * cos - x2 * sin, x1 * sin + x2 * cos], axis=-1)


def apply_axial_rope(x, rope):
    cos_r, sin_r, cos_c, sin_c = rope
    xf = x.astype(jnp.float32)
    half = HEAD_DIM // 2
    out = jnp.concatenate([_rotate_half(xf[..., :half], cos_r, sin_r),
                           _rotate_half(xf[..., half:], cos_c, sin_c)], axis=-1)
    return out.astype(x.dtype)


def q_heads(proj, q_norm_g):
    b, l = proj.shape[0], proj.shape[1]
    return rmsnorm(proj[..., OFF_Q:OFF_K].reshape(b, l, N_HEADS, HEAD_DIM), q_norm_g)


def kv_heads(proj_kv, k_norm_g):
    b, l = proj_kv.shape[0], proj_kv.shape[1]
    k = rmsnorm(proj_kv[..., :KV_W].reshape(b, l, N_KV_HEADS, HEAD_DIM), k_norm_g)
    v = proj_kv[..., KV_W:].reshape(b, l, N_KV_HEADS, HEAD_DIM)
    return k, v


def gqa_attend(qg, k, v):
    s = jnp.einsum('bqkgd,bskd->bkgqs', qg, k).astype(jnp.float32) * (HEAD_DIM ** -0.5)
    p = jax.nn.softmax(s, axis=-1).astype(v.dtype)
    return jnp.einsum('bkgqs,bskd->bqkgd', p, v)


def latent_attention(q, k_all, v_all):
    b, l = q.shape[0], q.shape[1]
    n_blocks = l // Q_BLOCK
    qb = q.reshape(b, n_blocks, Q_BLOCK, N_KV_HEADS, Q_PER_KV, HEAD_DIM).transpose(1, 0, 2, 3, 4, 5)
    ob = lax.map(lambda q_blk: gqa_attend(q_blk, k_all, v_all), qb)
    return ob.transpose(1, 0, 2, 3, 4, 5).reshape(b, l, Q_W)


def fourier_mix(u):
    b, l = u.shape[0], u.shape[1]
    ug = u.astype(jnp.float32).reshape(b, l, FOURIER_GROUPS, FOURIER_GW)
    f = jnp.fft.fft2(ug, axes=(1, 3), norm='ortho').real
    return f.reshape(b, l, FOURIER_W).astype(u.dtype)


def conv_module(u, conv_dw, conv_b, conv_norm_g):
    a, gate = jnp.split(u, 2, axis=-1)
    y = a * jax.nn.sigmoid(gate)
    y = lax.conv_general_dilated(
        y, conv_dw[:, None, :].astype(y.dtype), window_strides=(1,),
        padding=((CONV_K // 2, CONV_K // 2),),
        dimension_numbers=('NWC', 'WIO', 'NWC'), feature_group_count=CONV_W) + conv_b
    return jax.nn.silu(rmsnorm(y, conv_norm_g))


def pool_mix(u, pool_w, pool_scale):
    b, l = u.shape[0], u.shape[1]
    t = jnp.arange(l)
    uf = u.astype(jnp.float32)
    outs = []
    for i, w in enumerate(POOL_WINDOWS):
        xg = uf[..., i * POOL_GW:(i + 1) * POOL_GW]
        csum = jnp.concatenate([jnp.zeros((b, 1, POOL_GW), jnp.float32), jnp.cumsum(xg, axis=1)], axis=1)
        lo = jnp.clip(t - w // 2, 0, l)
        hi = jnp.clip(t + w // 2, 0, l)
        mean = (csum[:, hi] - csum[:, lo]) / (hi - lo).astype(jnp.float32)[None, :, None]
        outs.append(mean - xg)
    y = jnp.stack(outs, axis=2)
    y = jnp.einsum('blgc,gcd->blgd', y, pool_w.astype(jnp.float32)).reshape(b, l, POOL_W)
    return (y * pool_scale.astype(jnp.float32)).astype(u.dtype)


def merge_branches(proj, attn, lp):
    f = fourier_mix(proj[..., OFF_F:OFF_C]) @ lp['w_br_fourier']
    cv = conv_module(proj[..., OFF_C:OFF_P], lp['conv_dw'], lp['conv_b'], lp['conv_norm_g']) @ lp['w_br_conv']
    pl = pool_mix(proj[..., OFF_P:OFF_Q], lp['pool_w'], lp['pool_scale']) @ lp['w_br_pool']
    at = attn @ lp['w_br_attn']
    merged = sum(jax.nn.sigmoid(proj[..., OFF_G + i * D_MODEL:OFF_G + (i + 1) * D_MODEL]) * br
                 for i, br in enumerate((f, cv, pl, at)))
    return merged @ lp['w_out']


def swiglu(h, w1, w3, w2):
    return (jax.nn.silu(h @ w1) * (h @ w3)) @ w2


def moe_swiglu(h, router, w1, w3, w2):
    logits = (h @ router).astype(jnp.float32)
    top_v, top_i = lax.top_k(logits, TOP_K)
    wts = jax.nn.softmax(top_v, axis=-1)
    comb = jnp.sum(jax.nn.one_hot(top_i, N_EXPERTS, dtype=jnp.float32) * wts[..., None], axis=-2)
    return sum(comb[..., e:e + 1].astype(h.dtype) * swiglu(h, w1[e], w3[e], w2[e])
               for e in range(N_EXPERTS))


def channel_mixer(h, layer, ffn_w1, ffn_w3, ffn_w2, moe_router, moe_w1, moe_w3, moe_w2):
    j = layer // 2
    if layer % 2 == 0:
        return swiglu(h, ffn_w1[j], ffn_w3[j], ffn_w2[j])
    return moe_swiglu(h, moe_router[j], moe_w1[j], moe_w3[j], moe_w2[j])


def _normal(key, shape, scale):
    return jax.random.normal(key, shape, jnp.float32) * scale


def setup_inputs(seed: int = 0) -> dict:
    key = jax.random.key(seed)
    ks = jax.random.split(key, 28)
    L, D = DEPTH, D_MODEL
    return {
        'x': _normal(ks[0], (BATCH, SEQ, D), 1.0),
        'c': _normal(ks[1], (BATCH, D), 1.0),
        'ctx': _normal(ks[2], (BATCH, CTX_LEN, D), 1.0),
        'c_ctx': _normal(ks[3], (D,), 1.0),
        'w_mod': _normal(ks[4], (L, D, 6 * D), D ** -0.5),
        'b_mod': _normal(ks[5], (L, 6 * D), 0.02),
        'norm1_g': 1.0 + _normal(ks[6], (L, D), 0.02),
        'norm2_g': 1.0 + _normal(ks[7], (L, D), 0.02),
        'w_in': _normal(ks[8], (L, D, IN_W), D ** -0.5),
        'w_br_fourier': _normal(ks[9], (L, FOURIER_W, D), FOURIER_W ** -0.5),
        'conv_dw': _normal(ks[10], (L, CONV_K, CONV_W), CONV_K ** -0.5),
        'conv_b': _normal(ks[11], (L, CONV_W), 0.02),
        'conv_norm_g': 1.0 + _normal(ks[12], (L, CONV_W), 0.02),
        'w_br_conv': _normal(ks[13], (L, CONV_W, D), CONV_W ** -0.5),
        'pool_w': _normal(ks[14], (L, len(POOL_WINDOWS), POOL_GW, POOL_GW), POOL_GW ** -0.5),
        'pool_scale': 1.0 + _normal(ks[15], (L, POOL_W), 0.02),
        'w_br_pool': _normal(ks[16], (L, POOL_W, D), POOL_W ** -0.5),
        'q_norm_g': 1.0 + _normal(ks[17], (L, HEAD_DIM), 0.02),
        'k_norm_g': 1.0 + _normal(ks[18], (L, HEAD_DIM), 0.02),
        'w_br_attn': _normal(ks[19], (L, Q_W, D), Q_W ** -0.5),
        'w_out': _normal(ks[20], (L, D, D), D ** -0.5),
        'ffn_w1': _normal(ks[21], (N_DENSE, D, D_FF), D ** -0.5),
        'ffn_w3': _normal(ks[22], (N_DENSE, D, D_FF), D ** -0.5),
        'ffn_w2': _normal(ks[23], (N_DENSE, D_FF, D), D_FF ** -0.5),
        'moe_router': _normal(ks[24], (N_MOE, D, N_EXPERTS), D ** -0.5),
        'moe_w1': _normal(ks[25], (N_MOE, N_EXPERTS, D, D_FF_EXPERT), D ** -0.5),
        'moe_w3': _normal(ks[26], (N_MOE, N_EXPERTS, D, D_FF_EXPERT), D ** -0.5),
        'moe_w2': _normal(ks[27], (N_MOE, N_EXPERTS, D_FF_EXPERT, D), D_FF_EXPERT ** -0.5),
    }


def reference(x, c, ctx, c_ctx, w_mod, b_mod, norm1_g, norm2_g, w_in, w_br_fourier, conv_dw, conv_b,
              conv_norm_g, w_br_conv, pool_w, pool_scale, w_br_pool, q_norm_g, k_norm_g, w_br_attn, w_out,
              ffn_w1, ffn_w3, ffn_w2, moe_router, moe_w1, moe_w3, moe_w2):
    rows = x.shape[1] // GRID_W
    rope = axial_rope_tables(rows)
    xc = ctx
    for layer in range(DEPTH):
        lp = {
            'w_br_fourier': w_br_fourier[layer], 'conv_dw': conv_dw[layer], 'conv_b': conv_b[layer],
            'conv_norm_g': conv_norm_g[layer], 'w_br_conv': w_br_conv[layer], 'pool_w': pool_w[layer],
            'pool_scale': pool_scale[layer], 'w_br_pool': w_br_pool[layer], 'w_br_attn': w_br_attn[layer],
            'w_out': w_out[layer],
        }
        is_last = layer == DEPTH - 1
        w_in_l = w_in[layer]
        sh1, sc1, g1, sh2, sc2, g2 = [m[:, None, :] for m in modulation(c, w_mod[layer], b_mod[layer])]
        csh1, csc1, cg1, csh2, csc2, cg2 = modulation(c_ctx, w_mod[layer], b_mod[layer])

        hc = rmsnorm(xc, norm1_g[layer]) * (1 + csc1) + csh1
        if is_last:
            kc, vc = kv_heads(hc @ w_in_l[:, OFF_K:OFF_G], k_norm_g[layer])
        else:
            proj_c = hc @ w_in_l
            kc, vc = kv_heads(proj_c[..., OFF_K:OFF_G], k_norm_g[layer])
            qc = q_heads(proj_c, q_norm_g[layer])
            bc, lc = qc.shape[0], qc.shape[1]
            attn_c = gqa_attend(qc.reshape(bc, lc, N_KV_HEADS, Q_PER_KV, HEAD_DIM), kc, vc).reshape(bc, lc, Q_W)

        h = rmsnorm(x, norm1_g[layer]) * (1 + sc1) + sh1
        proj = h @ w_in_l
        q = apply_axial_rope(q_heads(proj, q_norm_g[layer]), rope)
        k, v = kv_heads(proj[..., OFF_K:OFF_G], k_norm_g[layer])
        k = apply_axial_rope(k, rope)
        attn = latent_attention(q, jnp.concatenate([k, kc], axis=1), jnp.concatenate([v, vc], axis=1))
        x = x + g1 * merge_branches(proj, attn, lp)

        h2 = rmsnorm(x, norm2_g[layer]) * (1 + sc2) + sh2
        x = x + g2 * channel_mixer(h2, layer, ffn_w1, ffn_w3, ffn_w2, moe_router, moe_w1, moe_w3, moe_w2)

        if not is_last:
            xc = xc + cg1 * merge_branches(proj_c, attn_c, lp)
            hc2 = rmsnorm(xc, norm2_g[layer]) * (1 + csc2) + csh2
            xc = xc + cg2 * channel_mixer(hc2, layer, ffn_w1, ffn_w3, ffn_w2, moe_router, moe_w1, moe_w3, moe_w2)
    return x
```

```python
import functools
import math

import numpy as np
import jax
import jax.numpy as jnp
from jax import lax
from jax.experimental import pallas as pl
from jax.experimental.pallas import tpu as pltpu

F32 = jnp.float32
BF16 = jnp.bfloat16

D_MODEL = 1024
GRID_W = 64
EPS = 1e-6
FOURIER_GW = 64
FOURIER_W = 256
CONV_W = 256
CONV_K = 31
CONV_HALF = CONV_K // 2
POOL_WINDOWS = (2, 4, 8, 16)
POOL_GW = 64
POOL_W = 256
HEAD_DIM = 64
N_HEADS = 8
N_KV_HEADS = 2
Q_PER_KV = 4
Q_W = 512
KV_W = 128
ROPE_THETA = 10000.0
N_EXPERTS = 8
IN_W = 5888

P_OFF_G = 0
P_OFF_Q = 4096
P_OFF_CP = 4608
P_OFF_F = 5376
P_OFF_KV = 5632
CP_W = 2 * CONV_W + POOL_W

HALO = 16
VMEM_LIMIT = 56 * 1024 * 1024


def _cparams(n_axes):
    return pltpu.CompilerParams(dimension_semantics=("arbitrary",) * n_axes, vmem_limit_bytes=VMEM_LIMIT)


def _sigmoid(v):
    return 1.0 / (1.0 + jnp.exp(-v))


def _silu(v):
    return v * _sigmoid(v)


def _norm_mod(x, g, shift, scale):
    ms = jnp.mean(x * x, axis=-1, keepdims=True)
    return x * lax.rsqrt(ms + EPS) * g * (1.0 + scale) + shift


def _mod_kernel(c_ref, w_ref, b_ref, o_ref):
    s = _silu(c_ref[...])
    o_ref[0] = jnp.dot(s, w_ref[0], preferred_element_type=F32, precision=lax.Precision.HIGHEST) + b_ref[0]


def modulation_all(c_rows, w_mod, b_mod):
    n_layers, d, n = w_mod.shape
    tn = 1536
    return pl.pallas_call(
        _mod_kernel,
        grid=(n_layers, n // tn),
        in_specs=[
            pl.BlockSpec((8, d), lambda l, j: (0, 0)),
            pl.BlockSpec((1, d, tn), lambda l, j: (l, 0, j)),
            pl.BlockSpec((1, 1, tn), lambda l, j: (l, 0, j)),
        ],
        out_specs=pl.BlockSpec((1, 8, tn), lambda l, j: (l, 0, j)),
        out_shape=jax.ShapeDtypeStruct((n_layers, 8, n), F32),
        compiler_params=_cparams(2),
        name="modulation",
    )(c_rows, w_mod, b_mod.reshape(n_layers, 1, n))


def _inproj_kernel(x_ref, mod_ref, g_ref, w_ref, o_ref, *, chunks):
    h = _norm_mod(x_ref[...], g_ref[...], mod_ref[0, 0:1, :], mod_ref[0, 1:2, :]).astype(BF16)
    for c0, cw in chunks:
        o_ref[:, c0:c0 + cw] = jnp.dot(h, w_ref[:, c0:c0 + cw], preferred_element_type=F32).astype(o_ref.dtype)


def input_projection(x2d, mod, g, w_bf16, *, tm, tiles_per_mod):
    m, d = x2d.shape
    n = w_bf16.shape[1]
    chunks = tuple((c0, min(512, n - c0)) for c0 in range(0, n, 512))
    n_mod = mod.shape[0]
    mod_idx = (lambda i: (i // tiles_per_mod, 0, 0)) if n_mod > 1 else (lambda i: (0, 0, 0))
    return pl.pallas_call(
        functools.partial(_inproj_kernel, chunks=chunks),
        grid=(m // tm,),
        in_specs=[
            pl.BlockSpec((tm, d), lambda i: (i, 0)),
            pl.BlockSpec((1, 6, d), mod_idx),
            pl.BlockSpec((1, d), lambda i: (0, 0)),
            pl.BlockSpec((d, n), lambda i: (0, 0), pipeline_mode=pl.Buffered(1)),
        ],
        out_specs=pl.BlockSpec((tm, n), lambda i: (i, 0)),
        out_shape=jax.ShapeDtypeStruct((m, n), BF16),
        compiler_params=_cparams(1),
        name="input_projection",
    )(x2d, mod, g.reshape(1, d), w_bf16)


def _seg_sum64(v, ones_bd):
    hi = v.astype(BF16)
    lo = (v - hi.astype(F32)).astype(BF16)
    return (jnp.dot(hi, ones_bd, preferred_element_type=F32) + jnp.dot(lo, ones_bd, preferred_element_type=F32))


def _head_norm_rope(x, g, ones_bd, cos, sin, low_mask):
    y = x * lax.rsqrt(_seg_sum64(x * x, ones_bd) * (1.0 / HEAD_DIM) + EPS) * g
    if cos is None:
        return y
    partner = jnp.where(low_mask, pltpu.roll(y, 128 - 16, axis=1), pltpu.roll(y, 16, axis=1))
    return y * cos + partner * sin


def _prep_kernel(*refs, use_rope):
    if use_rope:
        q_ref, kv_ref, gq_ref, gk_ref, ones_ref, cos_ref, sin_ref, qo_ref, kt_ref, v_ref = refs
        cos, sin = cos_ref[...], sin_ref[...]
    else:
        q_ref, kv_ref, gq_ref, gk_ref, ones_ref, qo_ref, kt_ref, v_ref = refs
        cos = sin = None
    t = q_ref.shape[0]
    ones_bd = ones_ref[...]
    lane = lax.broadcasted_iota(jnp.int32, (t, 128), 1)
    low_mask = (lane % 32) < 16
    gq = gq_ref[...]
    for c in range(Q_W // 128):
        xq = q_ref[:, 128 * c:128 * (c + 1)].astype(F32)
        yq = _head_norm_rope(xq, gq, ones_bd, cos, sin, low_mask) * (HEAD_DIM ** -0.5)
        qo_ref[:, 128 * c:128 * (c + 1)] = yq.astype(BF16)
    xk = kv_ref[:, 0:128].astype(F32)
    yk = _head_norm_rope(xk, gk_ref[...], ones_bd, cos, sin, low_mask)
    kt = yk.T.astype(BF16)
    for h in range(N_KV_HEADS):
        kh = kt[64 * h:64 * (h + 1), :]
        kt_ref[0, h] = jnp.concatenate([kh] * Q_PER_KV, axis=0)
    v = kv_ref[:, 128:256]
    vr = pltpu.roll(v.astype(F32), 64, axis=1).astype(BF16)
    first = lane < 64
    v0 = jnp.where(first, v, vr)
    v1 = jnp.where(first, vr, v)
    v_ref[0, 0] = jnp.concatenate([v0, v0], axis=1)
    v_ref[0, 1] = jnp.concatenate([v1, v1], axis=1)


def qkv_prepare(proj, gq, gk, rope, *, batch, seq, tp):
    m = proj.shape[0]
    tps = seq // tp
    use_rope = rope is not None
    ones_bd = jnp.asarray(np.kron(np.eye(2, dtype=np.float32), np.ones((64, 64), np.float32)), BF16)
    gq2 = jnp.tile(gq, 2).reshape(1, 128)
    gk2 = jnp.tile(gk, 2).reshape(1, 128)
    in_specs = [
        pl.BlockSpec((tp, Q_W), lambda i: (i, P_OFF_Q // Q_W)),
        pl.BlockSpec((tp, 256), lambda i: (i, P_OFF_KV // 256)),
        pl.BlockSpec((1, 128), lambda i: (0, 0)),
        pl.BlockSpec((1, 128), lambda i: (0, 0)),
        pl.BlockSpec((128, 128), lambda i: (0, 0)),
    ]
    args = [proj, proj, gq2, gk2, ones_bd]
    if use_rope:
        in_specs += [pl.BlockSpec((tp, 128), lambda i: (i % tps, 0))] * 2
        args += list(rope)
    return pl.pallas_call(
        functools.partial(_prep_kernel, use_rope=use_rope),
        grid=(m // tp,),
        in_specs=in_specs,
        out_specs=[
            pl.BlockSpec((tp, Q_W), lambda i: (i, 0)),
            pl.BlockSpec((1, N_KV_HEADS, 256, tp), lambda i: (i // tps, 0, 0, i % tps)),
            pl.BlockSpec((1, N_KV_HEADS, tp, 256), lambda i: (i // tps, 0, i % tps, 0)),
        ],
        out_shape=[
            jax.ShapeDtypeStruct((m, Q_W), BF16),
            jax.ShapeDtypeStruct((batch, N_KV_HEADS, 256, seq), BF16),
            jax.ShapeDtypeStruct((batch, N_KV_HEADS, seq, 256), BF16),
        ],
        compiler_params=_cparams(1),
        name="qkv_prepare",
    )(*args)


def rope_tables(seq):
    n_freq = HEAD_DIM // 4
    freqs = ROPE_THETA ** (-jnp.arange(n_freq, dtype=F32) / n_freq)
    t = jnp.arange(seq)
    row = (t // GRID_W).astype(F32)
    col = (t % GRID_W).astype(F32)
    ang_r = row[:, None] * freqs
    ang_c = col[:, None] * freqs
    cos = jnp.concatenate([jnp.cos(ang_r)] * 2 + [jnp.cos(ang_c)] * 2, axis=1)
    sin = jnp.concatenate([-jnp.sin(ang_r), jnp.sin(ang_r), -jnp.sin(ang_c), jnp.sin(ang_c)], axis=1)
    return jnp.tile(cos, (1, 2)), jnp.tile(sin, (1, 2))


def _attn_kernel(q_ref, kt_ref, v_ref, o_ref, qs_ref, m_ref, l_ref, acc_ref, *, tq, tk, nk):
    group = lax.broadcasted_iota(jnp.int32, (tq, 256), 1) // HEAD_DIM
    q = q_ref[...]
    for g in range(Q_PER_KV):
        qs_ref[g * tq:(g + 1) * tq, :] = jnp.where(group == g, q, jnp.zeros_like(q))
    m_ref[...] = jnp.full(m_ref.shape, -jnp.inf, F32)
    l_ref[...] = jnp.zeros(l_ref.shape, F32)
    acc_ref[...] = jnp.zeros(acc_ref.shape, F32)

    def body(ki, carry):
        off = pl.multiple_of(ki * tk, tk)
        s = jnp.dot(qs_ref[...], kt_ref[0, 0, :, pl.ds(off, tk)], preferred_element_type=F32)
        m_prev = m_ref[...]
        m_new = jnp.maximum(m_prev, jnp.max(s, axis=-1, keepdims=True))
        alpha = jnp.exp(m_prev - m_new)
        p = jnp.exp(s - m_new)
        l_ref[...] = alpha * l_ref[...] + jnp.sum(p, axis=-1, keepdims=True)
        pv = jnp.dot(p.astype(BF16), v_ref[0, 0, pl.ds(off, tk), :], preferred_element_type=F32)
        acc_ref[...] = alpha * acc_ref[...] + pv
        m_ref[...] = m_new
        return carry

    lax.fori_loop(0, nk, body, 0)
    o = acc_ref[...] / l_ref[...]
    out = jnp.zeros((tq, 256), F32)
    for g in range(Q_PER_KV):
        out = out + jnp.where(group == g, o[g * tq:(g + 1) * tq, :], 0.0)
    o_ref[...] = out.astype(o_ref.dtype)


def attention(q, kt4, v4, *, batch, seq_q, tq, tk):
    lk = kt4.shape[3]
    nq = seq_q // tq
    nk = lk // tk
    return pl.pallas_call(
        functools.partial(_attn_kernel, tq=tq, tk=tk, nk=nk),
        grid=(batch, N_KV_HEADS, nq),
        in_specs=[
            pl.BlockSpec((tq, 256), lambda b, h, i: (b * nq + i, h)),
            pl.BlockSpec((1, 1, 256, lk), lambda b, h, i: (b, h, 0, 0)),
            pl.BlockSpec((1, 1, lk, 256), lambda b, h, i: (b, h, 0, 0)),
        ],
        out_specs=pl.BlockSpec((tq, 256), lambda b, h, i: (b * nq + i, h)),
        out_shape=jax.ShapeDtypeStruct((batch * seq_q, Q_W), BF16),
        scratch_shapes=[
            pltpu.VMEM((Q_PER_KV * tq, 256), BF16),
            pltpu.VMEM((Q_PER_KV * tq, 1), F32),
            pltpu.VMEM((Q_PER_KV * tq, 1), F32),
            pltpu.VMEM((Q_PER_KV * tq, 256), F32),
        ],
        compiler_params=_cparams(3),
        name="attention",
    )(q, kt4, v4)


def _dft_cs(n):
    k = np.arange(n)
    ang = 2.0 * np.pi * ((k[:, None] * k[None, :]) % n) / n
    return np.cos(ang), np.sin(ang)


def _fft1_kernel(x_ref, f_ref, c_ref, s_ref, o_ref, *, n1):
    y = jnp.dot(f_ref[...], x_ref[0], preferred_element_type=F32)
    yr, yi = y[:n1], y[n1:]
    c, s = c_ref[...], s_ref[...]
    o_ref[0, 0] = (yr * c + yi * s).astype(o_ref.dtype)
    o_ref[0, 1] = (yi * c - yr * s).astype(o_ref.dtype)


def _fft2_kernel(y_ref, f_ref, bc_ref, bs_ref, o_ref, *, n2, kb):
    for j in range(kb):
        y2 = jnp.concatenate([y_ref[0, 0, j], y_ref[0, 1, j]], axis=0)
        x2 = jnp.dot(f_ref[...], y2, preferred_element_type=F32)
        xr = x2[:n2].astype(BF16)
        xi = x2[n2:].astype(BF16)
        z = (jnp.dot(xr, bc_ref[...], preferred_element_type=F32) + jnp.dot(xi, bs_ref[...], preferred_element_type=F32))
        o_ref[0, j] = z.astype(o_ref.dtype)


def fourier_mix(u, *, batch, seq, n1, n2):
    cw = u.shape[1]
    lanes = n2 * cw
    tl = min(lanes, 4096)
    c1, s1 = _dft_cs(n1)
    f1 = jnp.asarray(np.concatenate([c1, -s1], axis=0), BF16)
    k1 = np.arange(n1)[:, None]
    t2 = np.arange(n2)[None, :]
    ang = 2.0 * np.pi * ((k1 * t2) % seq) / seq
    twc = jnp.asarray(np.repeat(np.cos(ang), cw, axis=1), F32)
    tws = jnp.asarray(np.repeat(np.sin(ang), cw, axis=1), F32)
    x2 = u.reshape(batch, n1, lanes)
    yp = pl.pallas_call(
        functools.partial(_fft1_kernel, n1=n1),
        grid=(batch, lanes // tl),
        in_specs=[
            pl.BlockSpec((1, n1, tl), lambda b, j: (b, 0, j)),
            pl.BlockSpec((2 * n1, n1), lambda b, j: (0, 0)),
            pl.BlockSpec((n1, tl), lambda b, j: (0, j)),
            pl.BlockSpec((n1, tl), lambda b, j: (0, j)),
        ],
        out_specs=pl.BlockSpec((1, 2, n1, tl), lambda b, j: (b, 0, 0, j)),
        out_shape=jax.ShapeDtypeStruct((batch, 2, n1, lanes), BF16),
        compiler_params=_cparams(2),
        name="fft_stage1",
    )(x2, f1, twc, tws)

    c2, s2 = _dft_cs(n2)
    f2 = jnp.asarray(np.block([[c2, s2], [-s2, c2]]), BF16)
    cg, sg = _dft_cs(FOURIER_GW)
    norm = 1.0 / math.sqrt(seq * FOURIER_GW)
    bdc = jnp.asarray(np.kron(np.eye(cw // FOURIER_GW), cg) * norm, BF16)
    bds = jnp.asarray(np.kron(np.eye(cw // FOURIER_GW), sg) * norm, BF16)
    kb = min(n1, 16)
    y5 = yp.reshape(batch, 2, n1, n2, cw)
    z = pl.pallas_call(
        functools.partial(_fft2_kernel, n2=n2, kb=kb),
        grid=(batch, n1 // kb),
        in_specs=[
            pl.BlockSpec((1, 2, kb, n2, cw), lambda b, j: (b, 0, j, 0, 0)),
            pl.BlockSpec((2 * n2, 2 * n2), lambda b, j: (0, 0)),
            pl.BlockSpec((cw, cw), lambda b, j: (0, 0)),
            pl.BlockSpec((cw, cw), lambda b, j: (0, 0)),
        ],
        out_specs=pl.BlockSpec((1, kb, n2, cw), lambda b, j: (b, j, 0, 0)),
        out_shape=jax.ShapeDtypeStruct((batch, n1, n2, cw), BF16),
        compiler_params=_cparams(2),
        name="fft_stage2",
    )(y5, f2, bdc, bds)
    return z.transpose(0, 2, 1, 3).reshape(batch * seq, cw)


def _merge_kernel(x_ref, mod_ref, gate_ref, cp_ref, cpp_ref, cpn_ref, yf_ref, at_ref,
                  wf_ref, wc_ref, wp_ref, wa_ref, wo_ref, dw_ref, cb_ref, cg_ref, pw_ref, ps_ref,
                  o_ref, ybuf, xbuf, *, t, tps, seq):
    i = pl.program_id(0)
    pos_tile = i % tps
    keep_prev = jnp.where(pos_tile != 0, 1.0, 0.0).astype(F32)
    keep_next = jnp.where(pos_tile != tps - 1, 1.0, 0.0).astype(F32)

    def glu(blk):
        return blk[:, 0:CONV_W].astype(F32) * _sigmoid(blk[:, CONV_W:2 * CONV_W].astype(F32))

    cp, cpp, cpn = cp_ref[...], cpp_ref[...], cpn_ref[...]
    ybuf[0:HALO, :] = glu(cpp) * keep_prev
    ybuf[HALO:HALO + t, :] = glu(cp)
    ybuf[HALO + t:HALO + t + HALO, :] = glu(cpn) * keep_next
    xbuf[0:HALO, :] = cpp[:, 2 * CONV_W:].astype(F32) * keep_prev
    xbuf[HALO:HALO + t, :] = cp[:, 2 * CONV_W:].astype(F32)
    xbuf[HALO + t:HALO + t + HALO, :] = cpn[:, 2 * CONV_W:].astype(F32) * keep_next

    acc = jnp.zeros((t, CONV_W), F32)
    for k in range(CONV_K):
        acc = acc + dw_ref[k:k + 1, :] * ybuf[pl.ds(HALO - CONV_HALF + k, t), :]
    acc = acc + cb_ref[...]
    ms = jnp.mean(acc * acc, axis=-1, keepdims=True)
    conv_out = _silu(acc * lax.rsqrt(ms + EPS) * cg_ref[...]).astype(BF16)

    def xs(d):
        return xbuf[pl.ds(HALO + d, t), :]

    x0 = xs(0)
    s2 = xs(-1) + x0
    s4 = s2 + xs(-2) + xs(1)
    s8 = s4 + xs(-4) + xs(-3) + xs(2) + xs(3)
    s16 = s8 + xs(-8) + xs(-7) + xs(-6) + xs(-5) + xs(4) + xs(5) + xs(6) + xs(7)
    grp = lax.broadcasted_iota(jnp.int32, (t, POOL_W), 1) // POOL_GW
    pos = pos_tile * t + lax.broadcasted_iota(jnp.int32, (t, POOL_W), 0)
    half = jnp.where(grp == 0, 1, jnp.where(grp == 1, 2, jnp.where(grp == 2, 4, 8)))
    cnt = (jnp.minimum(pos + half, seq) - jnp.maximum(pos - half, 0)).astype(F32)
    wsum = jnp.where(grp == 0, s2, jnp.where(grp == 1, s4, jnp.where(grp == 2, s8, s16)))
    pool_in = (wsum / cnt - x0).astype(BF16)
    pool_out = (jnp.dot(pool_in, pw_ref[...], preferred_element_type=F32) * ps_ref[...]).astype(BF16)

    def gate(b):
        return _sigmoid(gate_ref[:, b * D_MODEL:(b + 1) * D_MODEL].astype(F32))

    merged = gate(0) * jnp.dot(yf_ref[...], wf_ref[...], preferred_element_type=F32)
    merged = merged + gate(1) * jnp.dot(conv_out, wc_ref[...], preferred_element_type=F32)
    merged = merged + gate(2) * jnp.dot(pool_out, wp_ref[...], preferred_element_type=F32)
    merged = merged + gate(3) * jnp.dot(at_ref[...], wa_ref[...], preferred_element_type=F32)
    out = jnp.dot(merged.astype(BF16), wo_ref[...], preferred_element_type=F32)
    o_ref[...] = x_ref[...] + mod_ref[0, 2:3, :] * out


def merge_branches(x2d, mod, proj, yf, attn, lw, *, seq, t):
    m, d = x2d.shape
    tps = seq // t
    hb = t // HALO
    n_halo = m // HALO
    n_mod = mod.shape[0]
    mod_idx = (lambda i: (i // tps, 0, 0)) if n_mod > 1 else (lambda i: (0, 0, 0))
    const = lambda i: (0, 0)
    cp_blk = P_OFF_CP // CP_W
    return pl.pallas_call(
        functools.partial(_merge_kernel, t=t, tps=tps, seq=seq),
        grid=(m // t,),
        in_specs=[
            pl.BlockSpec((t, d), lambda i: (i, 0)),
            pl.BlockSpec((1, 6, d), mod_idx),
            pl.BlockSpec((t, 4 * d), lambda i: (i, 0)),
            pl.BlockSpec((t, CP_W), lambda i: (i, cp_blk)),
            pl.BlockSpec((HALO, CP_W), lambda i: (jnp.maximum(i * hb - 1, 0), cp_blk)),
            pl.BlockSpec((HALO, CP_W), lambda i: (jnp.minimum((i + 1) * hb, n_halo - 1), cp_blk)),
            pl.BlockSpec((t, FOURIER_W), lambda i: (i, 0)),
            pl.BlockSpec((t, Q_W), lambda i: (i, 0)),
            pl.BlockSpec((FOURIER_W, d), const),
            pl.BlockSpec((CONV_W, d), const),
            pl.BlockSpec((POOL_W, d), const),
            pl.BlockSpec((Q_W, d), const),
            pl.BlockSpec((d, d), const),
            pl.BlockSpec((CONV_K, CONV_W), const),
            pl.BlockSpec((1, CONV_W), const),
            pl.BlockSpec((1, CONV_W), const),
            pl.BlockSpec((POOL_W, POOL_W), const),
            pl.BlockSpec((1, POOL_W), const),
        ],
        out_specs=pl.BlockSpec((t, d), lambda i: (i, 0)),
        out_shape=jax.ShapeDtypeStruct((m, d), F32),
        scratch_shapes=[pltpu.VMEM((t + 2 * HALO, CONV_W), F32), pltpu.VMEM((t + 2 * HALO, POOL_W), F32)],
        compiler_params=_cparams(1),
        name="merge_branches",
    )(x2d, mod, proj, proj, proj, proj, yf, attn,
      lw["wf"], lw["wc"], lw["wp"], lw["wa"], lw["wo"], lw["dw"], lw["cb"], lw["cg"], lw["pw"], lw["ps"])


def _ffn_kernel(x_ref, mod_ref, g_ref, w1_ref, w3_ref, w2_ref, o_ref, h_ref, acc_ref):
    j = pl.program_id(1)

    @pl.when(j == 0)
    def _():
        h_ref[...] = _norm_mod(x_ref[...], g_ref[...], mod_ref[0, 3:4, :], mod_ref[0, 4:5, :]).astype(BF16)
        acc_ref[...] = jnp.zeros(acc_ref.shape, F32)

    h = h_ref[...]
    a = jnp.dot(h, w1_ref[...], preferred_element_type=F32)
    b = jnp.dot(h, w3_ref[...], preferred_element_type=F32)
    acc_ref[...] += jnp.dot((_silu(a) * b).astype(BF16), w2_ref[...], preferred_element_type=F32)

    @pl.when(j == pl.num_programs(1) - 1)
    def _():
        o_ref[...] = x_ref[...] + mod_ref[0, 5:6, :] * acc_ref[...]


def ffn_dense(x2d, mod, g, w1, w3, w2, *, tm, tf, tiles_per_mod):
    m, d = x2d.shape
    dff = w1.shape[1]
    n_mod = mod.shape[0]
    mod_idx = (lambda i, j: (i // tiles_per_mod, 0, 0)) if n_mod > 1 else (lambda i, j: (0, 0, 0))
    return pl.pallas_call(
        _ffn_kernel,
        grid=(m // tm, dff // tf),
        in_specs=[
            pl.BlockSpec((tm, d), lambda i, j: (i, 0)),
            pl.BlockSpec((1, 6, d), mod_idx),
            pl.BlockSpec((1, d), lambda i, j: (0, 0)),
            pl.BlockSpec((d, tf), lambda i, j: (0, j)),
            pl.BlockSpec((d, tf), lambda i, j: (0, j)),
            pl.BlockSpec((tf, d), lambda i, j: (j, 0)),
        ],
        out_specs=pl.BlockSpec((tm, d), lambda i, j: (i, 0)),
        out_shape=jax.ShapeDtypeStruct((m, d), F32),
        scratch_shapes=[pltpu.VMEM((tm, d), BF16), pltpu.VMEM((tm, d), F32)],
        compiler_params=_cparams(2),
        name="ffn_dense",
    )(x2d, mod, g.reshape(1, d), w1, w3, w2)


def _moe_kernel(x_ref, mod_ref, g_ref, r_ref, w1_ref, w3_ref, w2_ref, o_ref, h_ref, comb_ref, acc_ref):
    e = pl.program_id(1)
    j = pl.program_id(2)
    tm = x_ref.shape[0]

    @pl.when(jnp.logical_and(e == 0, j == 0))
    def _():
        h = _norm_mod(x_ref[...], g_ref[...], mod_ref[0, 3:4, :], mod_ref[0, 4:5, :])
        h_ref[...] = h.astype(BF16)
        acc_ref[...] = jnp.zeros(acc_ref.shape, F32)
        logits = jnp.dot(h, r_ref[...], preferred_element_type=F32, precision=lax.Precision.HIGHEST)
        lane = lax.broadcasted_iota(jnp.int32, (tm, 128), 1).astype(F32)
        neg = jnp.float32(-jnp.inf)
        lg = jnp.where(lane < N_EXPERTS, logits, neg)
        v1 = jnp.max(lg, axis=-1, keepdims=True)
        i1 = jnp.min(jnp.where(lg == v1, lane, 128.0), axis=-1, keepdims=True)
        lg2 = jnp.where(lane == i1, neg, lg)
        v2 = jnp.max(lg2, axis=-1, keepdims=True)
        i2 = jnp.min(jnp.where(lg2 == v2, lane, 128.0), axis=-1, keepdims=True)
        e2 = jnp.exp(v2 - v1)
        w_first = 1.0 / (1.0 + e2)
        w_second = e2 / (1.0 + e2)
        comb_ref[...] = jnp.where(lane == i1, w_first, 0.0) + jnp.where(lane == i2, w_second, 0.0)

    h = h_ref[...]
    a = jnp.dot(h, w1_ref[0], preferred_element_type=F32)
    b = jnp.dot(h, w3_ref[0], preferred_element_type=F32)
    y = jnp.dot((_silu(a) * b).astype(BF16), w2_ref[0], preferred_element_type=F32)
    lane = lax.broadcasted_iota(jnp.int32, (tm, 128), 1)
    ce = jnp.sum(jnp.where(lane == e, comb_ref[...], 0.0), axis=-1, keepdims=True)
    acc_ref[...] += ce * y

    @pl.when(jnp.logical_and(e == pl.num_programs(1) - 1, j == pl.num_programs(2) - 1))
    def _():
        o_ref[...] = x_ref[...] + mod_ref[0, 5:6, :] * acc_ref[...]


def moe_dense(x2d, mod, g, router_pad, w1, w3, w2, *, tm, tf, tiles_per_mod):
    m, d = x2d.shape
    n_e, _, dff = w1.shape
    n_mod = mod.shape[0]
    mod_idx = (lambda i, e, j: (i // tiles_per_mod, 0, 0)) if n_mod > 1 else (lambda i, e, j: (0, 0, 0))
    return pl.pallas_call(
        _moe_kernel,
        grid=(m // tm, n_e, dff // tf),
        in_specs=[
            pl.BlockSpec((tm, d), lambda i, e, j: (i, 0)),
            pl.BlockSpec((1, 6, d), mod_idx),
            pl.BlockSpec((1, d), lambda i, e, j: (0, 0)),
            pl.BlockSpec((d, 128), lambda i, e, j: (0, 0)),
            pl.BlockSpec((1, d, tf), lambda i, e, j: (e, 0, j)),
            pl.BlockSpec((1, d, tf), lambda i, e, j: (e, 0, j)),
            pl.BlockSpec((1, tf, d), lambda i, e, j: (e, j, 0)),
        ],
        out_specs=pl.BlockSpec((tm, d), lambda i, e, j: (i, 0)),
        out_shape=jax.ShapeDtypeStruct((m, d), F32),
        scratch_shapes=[pltpu.VMEM((tm, d), BF16), pltpu.VMEM((tm, 128), F32), pltpu.VMEM((tm, d), F32)],
        compiler_params=_cparams(3),
        name="moe_experts",
    )(x2d, mod, g.reshape(1, d), router_pad, w1, w3, w2)


def _permute_w_in(w):
    f, c, p, q, kv, gts = w[:, 0:256], w[:, 256:768], w[:, 768:1024], w[:, 1024:1536], w[:, 1536:1792], w[:, 1792:]
    return jnp.concatenate([gts, q, c, p, f, kv], axis=1).astype(BF16)


def _layer_weights(layer, w_br_fourier, conv_dw, conv_b, conv_norm_g, w_br_conv, pool_w, pool_scale, w_br_pool,
                   w_br_attn, w_out):
    pw = jax.scipy.linalg.block_diag(*[pool_w[layer, i] for i in range(len(POOL_WINDOWS))])
    return {
        "wf": w_br_fourier[layer].astype(BF16), "wc": w_br_conv[layer].astype(BF16),
        "wp": w_br_pool[layer].astype(BF16), "wa": w_br_attn[layer].astype(BF16), "wo": w_out[layer].astype(BF16),
        "dw": conv_dw[layer], "cb": conv_b[layer].reshape(1, CONV_W), "cg": conv_norm_g[layer].reshape(1, CONV_W),
        "pw": pw.astype(BF16), "ps": pool_scale[layer].reshape(1, POOL_W),
    }


def kernel(x, c, ctx, c_ctx, w_mod, b_mod, norm1_g, norm2_g, w_in, w_br_fourier, conv_dw, conv_b, conv_norm_g,
           w_br_conv, pool_w, pool_scale, w_br_pool, q_norm_g, k_norm_g, w_br_attn, w_out, ffn_w1, ffn_w3, ffn_w2,
           moe_router, moe_w1, moe_w3, moe_w2):
    batch, seq, d = x.shape
    ctx_len = ctx.shape[1]
    depth = w_in.shape[0]
    rope = rope_tables(seq)

    c_rows = jnp.zeros((8, d), F32).at[0:batch].set(c).at[batch].set(c_ctx)
    mods = modulation_all(c_rows, w_mod, b_mod).reshape(depth, 8, 6, d)

    xl = x.reshape(batch * seq, d)
    xc = ctx.reshape(batch * ctx_len, d)
    for layer in range(depth):
        is_last = layer == depth - 1
        mod_l = mods[layer, 0:batch]
        mod_c = mods[layer, batch:batch + 1]
        w_in_l = _permute_w_in(w_in[layer])
        lw = _layer_weights(layer, w_br_fourier, conv_dw, conv_b, conv_norm_g, w_br_conv, pool_w, pool_scale,
                            w_br_pool, w_br_attn, w_out)

        proj_c = input_projection(xc, mod_c, norm1_g[layer], w_in_l, tm=256, tiles_per_mod=1)
        qc, ktc, vc = qkv_prepare(proj_c, q_norm_g[layer], k_norm_g[layer], None, batch=batch, seq=ctx_len, tp=256)

        proj = input_projection(xl, mod_l, norm1_g[layer], w_in_l, tm=512, tiles_per_mod=seq // 512)
        q, kt, v = qkv_prepare(proj, q_norm_g[layer], k_norm_g[layer], rope, batch=batch, seq=seq, tp=512)
        kt_all = jnp.concatenate([kt, ktc], axis=3)
        v_all = jnp.concatenate([v, vc], axis=2)
        attn = attention(q, kt_all, v_all, batch=batch, seq_q=seq, tq=256, tk=768)
        yf = fourier_mix(proj[:, P_OFF_F:P_OFF_F + FOURIER_W], batch=batch, seq=seq, n1=64, n2=seq // 64)
        xl = merge_branches(xl, mod_l, proj, yf, attn, lw, seq=seq, t=512)

        if not is_last:
            attn_c = attention(qc, ktc, vc, batch=batch, seq_q=ctx_len, tq=256, tk=ctx_len)
            yf_c = fourier_mix(proj_c[:, P_OFF_F:P_OFF_F + FOURIER_W], batch=batch, seq=ctx_len, n1=16,
                               n2=ctx_len // 16)
            xc = merge_branches(xc, mod_c, proj_c, yf_c, attn_c, lw, seq=ctx_len, t=256)

        j = layer // 2
        if layer % 2 == 0:
            w1, w3, w2 = ffn_w1[j].astype(BF16), ffn_w3[j].astype(BF16), ffn_w2[j].astype(BF16)
            xl = ffn_dense(xl, mod_l, norm2_g[layer], w1, w3, w2, tm=512, tf=1408, tiles_per_mod=seq // 512)
            if not is_last:
                xc = ffn_dense(xc, mod_c, norm2_g[layer], w1, w3, w2, tm=256, tf=1408, tiles_per_mod=1)
        else:
            w1, w3, w2 = moe_w1[j].astype(BF16), moe_w3[j].astype(BF16), moe_w2[j].astype(BF16)
            router_pad = jnp.zeros((d, 128), F32).at[:, 0:N_EXPERTS].set(moe_router[j])
            xl = moe_dense(xl, mod_l, norm2_g[layer], router_pad, w1, w3, w2, tm=1024, tf=896,
                           tiles_per_mod=seq // 1024)
            if not is_last:
                xc = moe_dense(xc, mod_c, norm2_g[layer], router_pad, w1, w3, w2, tm=256, tf=1792, tiles_per_mod=1)
    return xl.reshape(batch, seq, d)
```

```python
import functools
import math

import numpy as np
import jax
import jax.numpy as jnp
from jax import lax
from jax.experimental import pallas as pl
from jax.experimental.pallas import tpu as pltpu

F32 = jnp.float32
BF16 = jnp.bfloat16

D_MODEL = 1024
GRID_W = 64
EPS = 1e-6
FOURIER_GW = 64
FOURIER_W = 256
CONV_W = 256
CONV_K = 31
CONV_HALF = CONV_K // 2
POOL_WINDOWS = (2, 4, 8, 16)
POOL_GW = 64
POOL_W = 256
HEAD_DIM = 64
N_HEADS = 8
N_KV_HEADS = 2
Q_PER_KV = 4
Q_W = 512
KV_W = 128
ROPE_THETA = 10000.0
N_EXPERTS = 8
IN_W = 5888

P_OFF_G = 0
P_OFF_Q = 4096
P_OFF_CP = 4608
P_OFF_F = 5376
P_OFF_KV = 5632
CP_W = 2 * CONV_W + POOL_W

Q_SCALE = (HEAD_DIM ** -0.5) * math.log2(math.e)

HALO = 16
VMEM_LIMIT = 56 * 1024 * 1024


def _cparams(n_axes):
    return pltpu.CompilerParams(dimension_semantics=("arbitrary",) * n_axes, vmem_limit_bytes=VMEM_LIMIT)


def _sigmoid(v):
    return 1.0 / (1.0 + jnp.exp(-v))


def _silu(v):
    return v * _sigmoid(v)


def _norm_mod(x, g, shift, scale):
    ms = jnp.mean(x * x, axis=-1, keepdims=True)
    return x * lax.rsqrt(ms + EPS) * g * (1.0 + scale) + shift


def _mod_kernel(c_ref, w_ref, b_ref, o_ref):
    s = _silu(c_ref[...])
    o_ref[0] = jnp.dot(s, w_ref[0], preferred_element_type=F32, precision=lax.Precision.HIGHEST) + b_ref[0]


def modulation_all(c_rows, w_mod, b_mod):
    n_layers, d, n = w_mod.shape
    tn = 1536
    return pl.pallas_call(
        _mod_kernel,
        grid=(n_layers, n // tn),
        in_specs=[
            pl.BlockSpec((8, d), lambda l, j: (0, 0)),
            pl.BlockSpec((1, d, tn), lambda l, j: (l, 0, j)),
            pl.BlockSpec((1, 1, tn), lambda l, j: (l, 0, j)),
        ],
        out_specs=pl.BlockSpec((1, 8, tn), lambda l, j: (l, 0, j)),
        out_shape=jax.ShapeDtypeStruct((n_layers, 8, n), F32),
        compiler_params=_cparams(2),
        name="modulation",
    )(c_rows, w_mod, b_mod.reshape(n_layers, 1, n))


def _inproj_kernel(x_ref, mod_ref, g_ref, w_ref, o_ref, *, chunks):
    h = _norm_mod(x_ref[...], g_ref[...], mod_ref[0, 0:1, :], mod_ref[0, 1:2, :]).astype(BF16)
    for c0, cw in chunks:
        o_ref[:, c0:c0 + cw] = jnp.dot(h, w_ref[:, c0:c0 + cw], preferred_element_type=F32).astype(o_ref.dtype)


def input_projection(x2d, mod, g, w_bf16, *, tm, tiles_per_mod):
    m, d = x2d.shape
    n = w_bf16.shape[1]
    chunks = tuple((c0, min(512, n - c0)) for c0 in range(0, n, 512))
    n_mod = mod.shape[0]
    mod_idx = (lambda i: (i // tiles_per_mod, 0, 0)) if n_mod > 1 else (lambda i: (0, 0, 0))
    return pl.pallas_call(
        functools.partial(_inproj_kernel, chunks=chunks),
        grid=(m // tm,),
        in_specs=[
            pl.BlockSpec((tm, d), lambda i: (i, 0)),
            pl.BlockSpec((1, 6, d), mod_idx),
            pl.BlockSpec((1, d), lambda i: (0, 0)),
            pl.BlockSpec((d, n), lambda i: (0, 0), pipeline_mode=pl.Buffered(1)),
        ],
        out_specs=pl.BlockSpec((tm, n), lambda i: (i, 0)),
        out_shape=jax.ShapeDtypeStruct((m, n), BF16),
        compiler_params=_cparams(1),
        name="input_projection",
    )(x2d, mod, g.reshape(1, d), w_bf16)


def _seg_sum64(v, ones_bd):
    hi = v.astype(BF16)
    lo = (v - hi.astype(F32)).astype(BF16)
    return (jnp.dot(hi, ones_bd, preferred_element_type=F32) + jnp.dot(lo, ones_bd, preferred_element_type=F32))


def _head_norm_rope(x, g, ones_bd, cos, sin, low_mask):
    y = x * lax.rsqrt(_seg_sum64(x * x, ones_bd) * (1.0 / HEAD_DIM) + EPS) * g
    if cos is None:
        return y
    partner = jnp.where(low_mask, pltpu.roll(y, 128 - 16, axis=1), pltpu.roll(y, 16, axis=1))
    return y * cos + partner * sin


def _prep_kernel(*refs, use_rope):
    if use_rope:
        q_ref, kv_ref, gq_ref, gk_ref, ones_ref, cos_ref, sin_ref, qo_ref, kt_ref, v_ref = refs
        cos, sin = cos_ref[...], sin_ref[...]
    else:
        q_ref, kv_ref, gq_ref, gk_ref, ones_ref, qo_ref, kt_ref, v_ref = refs
        cos = sin = None
    t = q_ref.shape[0]
    ones_bd = ones_ref[...]
    lane = lax.broadcasted_iota(jnp.int32, (t, 128), 1)
    low_mask = (lane % 32) < 16
    gq = gq_ref[...]
    for c in range(Q_W // 128):
        xq = q_ref[:, 128 * c:128 * (c + 1)].astype(F32)
        yq = _head_norm_rope(xq, gq, ones_bd, cos, sin, low_mask) * Q_SCALE
        qo_ref[:, 128 * c:128 * (c + 1)] = yq.astype(BF16)
    xk = kv_ref[:, 0:128].astype(F32)
    yk = _head_norm_rope(xk, gk_ref[...], ones_bd, cos, sin, low_mask)
    kt = yk.T.astype(BF16)
    for h in range(N_KV_HEADS):
        kh = kt[64 * h:64 * (h + 1), :]
        kt_ref[0, h] = jnp.concatenate([kh] * Q_PER_KV, axis=0)
    v = kv_ref[:, 128:256].astype(F32)
    first = lane < 64
    v0 = jnp.where(first, v, 1.0).astype(BF16)
    v1 = jnp.where(first, pltpu.roll(v, 64, axis=1), 1.0).astype(BF16)
    v_ref[0, 0] = jnp.concatenate([v0, v0], axis=1)
    v_ref[0, 1] = jnp.concatenate([v1, v1], axis=1)


def qkv_prepare(proj, gq, gk, rope, *, batch, seq, tp):
    m = proj.shape[0]
    tps = seq // tp
    use_rope = rope is not None
    ones_bd = jnp.asarray(np.kron(np.eye(2, dtype=np.float32), np.ones((64, 64), np.float32)), BF16)
    gq2 = jnp.tile(gq, 2).reshape(1, 128)
    gk2 = jnp.tile(gk, 2).reshape(1, 128)
    in_specs = [
        pl.BlockSpec((tp, Q_W), lambda i: (i, P_OFF_Q // Q_W)),
        pl.BlockSpec((tp, 256), lambda i: (i, P_OFF_KV // 256)),
        pl.BlockSpec((1, 128), lambda i: (0, 0)),
        pl.BlockSpec((1, 128), lambda i: (0, 0)),
        pl.BlockSpec((128, 128), lambda i: (0, 0)),
    ]
    args = [proj, proj, gq2, gk2, ones_bd]
    if use_rope:
        in_specs += [pl.BlockSpec((tp, 128), lambda i: (i % tps, 0))] * 2
        args += list(rope)
    return pl.pallas_call(
        functools.partial(_prep_kernel, use_rope=use_rope),
        grid=(m // tp,),
        in_specs=in_specs,
        out_specs=[
            pl.BlockSpec((tp, Q_W), lambda i: (i, 0)),
            pl.BlockSpec((1, N_KV_HEADS, 256, tp), lambda i: (i // tps, 0, 0, i % tps)),
            pl.BlockSpec((1, N_KV_HEADS, tp, 256), lambda i: (i // tps, 0, i % tps, 0)),
        ],
        out_shape=[
            jax.ShapeDtypeStruct((m, Q_W), BF16),
            jax.ShapeDtypeStruct((batch, N_KV_HEADS, 256, seq), BF16),
            jax.ShapeDtypeStruct((batch, N_KV_HEADS, seq, 256), BF16),
        ],
        compiler_params=_cparams(1),
        name="qkv_prepare",
    )(*args)


def rope_tables(seq):
    n_freq = HEAD_DIM // 4
    freqs = ROPE_THETA ** (-jnp.arange(n_freq, dtype=F32) / n_freq)
    t = jnp.arange(seq)
    row = (t // GRID_W).astype(F32)
    col = (t % GRID_W).astype(F32)
    ang_r = row[:, None] * freqs
    ang_c = col[:, None] * freqs
    cos = jnp.concatenate([jnp.cos(ang_r)] * 2 + [jnp.cos(ang_c)] * 2, axis=1)
    sin = jnp.concatenate([-jnp.sin(ang_r), jnp.sin(ang_r), -jnp.sin(ang_c), jnp.sin(ang_c)], axis=1)
    return jnp.tile(cos, (1, 2)), jnp.tile(sin, (1, 2))


def _attn_kernel(q_ref, kt_ref, v_ref, o_ref, qs_ref, s0, s1, p0, p1, a0, a1, m_ref, acc_ref, *, tq, tk, nk):
    s_bufs, p_bufs, a_bufs = (s0, s1), (p0, p1), (a0, a1)
    group = lax.broadcasted_iota(jnp.int32, (tq, 256), 1) // HEAD_DIM
    q = q_ref[...]
    for g in range(Q_PER_KV):
        qs_ref[g * tq:(g + 1) * tq, :] = jnp.where(group == g, q, jnp.zeros_like(q))
    m_ref[...] = jnp.full(m_ref.shape, -jnp.inf, F32)
    acc_ref[...] = jnp.zeros(acc_ref.shape, F32)

    def scores(j, slot):
        off = pl.multiple_of(j * tk, tk)
        s_bufs[slot][...] = jnp.dot(qs_ref[...], kt_ref[0, 0, :, pl.ds(off, tk)], preferred_element_type=F32)

    def numerators(slot):
        s = s_bufs[slot][...]
        m_old = m_ref[...]
        m_new = jnp.maximum(m_old, jnp.max(s, axis=-1, keepdims=True))
        a_bufs[slot][...] = jnp.exp2(m_old - m_new)
        p_bufs[slot][...] = jnp.exp2(s - m_new).astype(BF16)
        m_ref[...] = m_new

    def weighted_sum(j, slot):
        off = pl.multiple_of(j * tk, tk)
        pv = jnp.dot(p_bufs[slot][...], v_ref[0, 0, pl.ds(off, tk), :], preferred_element_type=F32)
        acc_ref[...] = a_bufs[slot][...] * acc_ref[...] + pv

    def step(t, slot):
        scores(t, slot)
        numerators(1 - slot)
        weighted_sum(t - 2, slot)

    scores(0, 0)
    if nk > 1:
        scores(1, 1)
        numerators(0)
        n_pairs = (nk - 2) // 2

        def pair(i, carry):
            t = 2 + 2 * i
            step(t, 0)
            step(t + 1, 1)
            return carry

        lax.fori_loop(0, n_pairs, pair, 0)
        if (nk - 2) % 2:
            step(nk - 1, (nk - 1) % 2)
        numerators((nk - 1) % 2)
        weighted_sum(nk - 2, nk % 2)
    else:
        numerators(0)
    weighted_sum(nk - 1, (nk - 1) % 2)

    acc = acc_ref[...]
    low = lax.broadcasted_iota(jnp.int32, (tq, 128), 1) < HEAD_DIM
    for half in range(2):
        ae = acc[(2 * half) * tq:(2 * half + 1) * tq, 128 * half:128 * (half + 1)]
        ao = acc[(2 * half + 1) * tq:(2 * half + 2) * tq, 128 * half:128 * (half + 1)]
        even = ae / pltpu.roll(ae, 64, axis=1)
        odd = pltpu.roll(ao, 64, axis=1) / ao
        o_ref[:, 128 * half:128 * (half + 1)] = jnp.where(low, even, odd).astype(o_ref.dtype)


def attention(q, kt4, v4, *, batch, seq_q, tq, tk):
    lk = kt4.shape[3]
    nq = seq_q // tq
    nk = lk // tk
    rows = Q_PER_KV * tq
    return pl.pallas_call(
        functools.partial(_attn_kernel, tq=tq, tk=tk, nk=nk),
        grid=(batch, N_KV_HEADS, nq),
        in_specs=[
            pl.BlockSpec((tq, 256), lambda b, h, i: (b * nq + i, h)),
            pl.BlockSpec((1, 1, 256, lk), lambda b, h, i: (b, h, 0, 0)),
            pl.BlockSpec((1, 1, lk, 256), lambda b, h, i: (b, h, 0, 0)),
        ],
        out_specs=pl.BlockSpec((tq, 256), lambda b, h, i: (b * nq + i, h)),
        out_shape=jax.ShapeDtypeStruct((batch * seq_q, Q_W), BF16),
        scratch_shapes=[
            pltpu.VMEM((rows, 256), BF16),
            pltpu.VMEM((rows, tk), F32), pltpu.VMEM((rows, tk), F32),
            pltpu.VMEM((rows, tk), BF16), pltpu.VMEM((rows, tk), BF16),
            pltpu.VMEM((rows, 1), F32), pltpu.VMEM((rows, 1), F32),
            pltpu.VMEM((rows, 1), F32),
            pltpu.VMEM((rows, 256), F32),
        ],
        compiler_params=_cparams(3),
        name="attention",
    )(q, kt4, v4)


def _dft_cs(n):
    k = np.arange(n)
    ang = 2.0 * np.pi * ((k[:, None] * k[None, :]) % n) / n
    return np.cos(ang), np.sin(ang)


def _fft1_kernel(x_ref, f_ref, c_ref, s_ref, o_ref, *, n1):
    y = jnp.dot(f_ref[...], x_ref[0], preferred_element_type=F32)
    yr, yi = y[:n1], y[n1:]
    c, s = c_ref[...], s_ref[...]
    o_ref[0, 0] = (yr * c + yi * s).astype(o_ref.dtype)
    o_ref[0, 1] = (yi * c - yr * s).astype(o_ref.dtype)


def _fft2_kernel(y_ref, f_ref, bc_ref, bs_ref, o_ref, *, n2, kb):
    for j in range(kb):
        y2 = jnp.concatenate([y_ref[0, 0, j], y_ref[0, 1, j]], axis=0)
        x2 = jnp.dot(f_ref[...], y2, preferred_element_type=F32)
        xr = x2[:n2].astype(BF16)
        xi = x2[n2:].astype(BF16)
        z = (jnp.dot(xr, bc_ref[...], preferred_element_type=F32) + jnp.dot(xi, bs_ref[...], preferred_element_type=F32))
        o_ref[0, j] = z.astype(o_ref.dtype)


def fourier_mix(u, *, batch, seq, n1, n2):
    cw = u.shape[1]
    lanes = n2 * cw
    tl = min(lanes, 4096)
    c1, s1 = _dft_cs(n1)
    f1 = jnp.asarray(np.concatenate([c1, -s1], axis=0), BF16)
    k1 = np.arange(n1)[:, None]
    t2 = np.arange(n2)[None, :]
    ang = 2.0 * np.pi * ((k1 * t2) % seq) / seq
    twc = jnp.asarray(np.repeat(np.cos(ang), cw, axis=1), F32)
    tws = jnp.asarray(np.repeat(np.sin(ang), cw, axis=1), F32)
    x2 = u.reshape(batch, n1, lanes)
    yp = pl.pallas_call(
        functools.partial(_fft1_kernel, n1=n1),
        grid=(batch, lanes // tl),
        in_specs=[
            pl.BlockSpec((1, n1, tl), lambda b, j: (b, 0, j)),
            pl.BlockSpec((2 * n1, n1), lambda b, j: (0, 0)),
            pl.BlockSpec((n1, tl), lambda b, j: (0, j)),
            pl.BlockSpec((n1, tl), lambda b, j: (0, j)),
        ],
        out_specs=pl.BlockSpec((1, 2, n1, tl), lambda b, j: (b, 0, 0, j)),
        out_shape=jax.ShapeDtypeStruct((batch, 2, n1, lanes), BF16),
        compiler_params=_cparams(2),
        name="fft_stage1",
    )(x2, f1, twc, tws)

    c2, s2 = _dft_cs(n2)
    f2 = jnp.asarray(np.block([[c2, s2], [-s2, c2]]), BF16)
    cg, sg = _dft_cs(FOURIER_GW)
    norm = 1.0 / math.sqrt(seq * FOURIER_GW)
    bdc = jnp.asarray(np.kron(np.eye(cw // FOURIER_GW), cg) * norm, BF16)
    bds = jnp.asarray(np.kron(np.eye(cw // FOURIER_GW), sg) * norm, BF16)
    kb = min(n1, 16)
    y5 = yp.reshape(batch, 2, n1, n2, cw)
    z = pl.pallas_call(
        functools.partial(_fft2_kernel, n2=n2, kb=kb),
        grid=(batch, n1 // kb),
        in_specs=[
            pl.BlockSpec((1, 2, kb, n2, cw), lambda b, j: (b, 0, j, 0, 0)),
            pl.BlockSpec((2 * n2, 2 * n2), lambda b, j: (0, 0)),
            pl.BlockSpec((cw, cw), lambda b, j: (0, 0)),
            pl.BlockSpec((cw, cw), lambda b, j: (0, 0)),
        ],
        out_specs=pl.BlockSpec((1, kb, n2, cw), lambda b, j: (b, j, 0, 0)),
        out_shape=jax.ShapeDtypeStruct((batch, n1, n2, cw), BF16),
        compiler_params=_cparams(2),
        name="fft_stage2",
    )(y5, f2, bdc, bds)
    return z.transpose(0, 2, 1, 3).reshape(batch * seq, cw)


def _merge_kernel(x_ref, mod_ref, gate_ref, cp_ref, cpp_ref, cpn_ref, yf_ref, at_ref,
                  wf_ref, wc_ref, wp_ref, wa_ref, wo_ref, dw_ref, cb_ref, cg_ref, pw_ref, ps_ref,
                  o_ref, ybuf, xbuf, *, t, tps, seq):
    i = pl.program_id(0)
    pos_tile = i % tps
    keep_prev = jnp.where(pos_tile != 0, 1.0, 0.0).astype(F32)
    keep_next = jnp.where(pos_tile != tps - 1, 1.0, 0.0).astype(F32)

    def glu(blk):
        return blk[:, 0:CONV_W].astype(F32) * _sigmoid(blk[:, CONV_W:2 * CONV_W].astype(F32))

    cp, cpp, cpn = cp_ref[...], cpp_ref[...], cpn_ref[...]
    ybuf[0:HALO, :] = glu(cpp) * keep_prev
    ybuf[HALO:HALO + t, :] = glu(cp)
    ybuf[HALO + t:HALO + t + HALO, :] = glu(cpn) * keep_next
    xbuf[0:HALO, :] = cpp[:, 2 * CONV_W:].astype(F32) * keep_prev
    xbuf[HALO:HALO + t, :] = cp[:, 2 * CONV_W:].astype(F32)
    xbuf[HALO + t:HALO + t + HALO, :] = cpn[:, 2 * CONV_W:].astype(F32) * keep_next

    acc = jnp.zeros((t, CONV_W), F32)
    for k in range(CONV_K):
        acc = acc + dw_ref[k:k + 1, :] * ybuf[pl.ds(HALO - CONV_HALF + k, t), :]
    acc = acc + cb_ref[...]
    ms = jnp.mean(acc * acc, axis=-1, keepdims=True)
    conv_out = _silu(acc * lax.rsqrt(ms + EPS) * cg_ref[...]).astype(BF16)

    def xs(d):
        return xbuf[pl.ds(HALO + d, t), :]

    x0 = xs(0)
    s2 = xs(-1) + x0
    s4 = s2 + xs(-2) + xs(1)
    s8 = s4 + xs(-4) + xs(-3) + xs(2) + xs(3)
    s16 = s8 + xs(-8) + xs(-7) + xs(-6) + xs(-5) + xs(4) + xs(5) + xs(6) + xs(7)
    grp = lax.broadcasted_iota(jnp.int32, (t, POOL_W), 1) // POOL_GW
    pos = pos_tile * t + lax.broadcasted_iota(jnp.int32, (t, POOL_W), 0)
    half = jnp.where(grp == 0, 1, jnp.where(grp == 1, 2, jnp.where(grp == 2, 4, 8)))
    cnt = (jnp.minimum(pos + half, seq) - jnp.maximum(pos - half, 0)).astype(F32)
    wsum = jnp.where(grp == 0, s2, jnp.where(grp == 1, s4, jnp.where(grp == 2, s8, s16)))
    pool_in = (wsum / cnt - x0).astype(BF16)
    pool_out = (jnp.dot(pool_in, pw_ref[...], preferred_element_type=F32) * ps_ref[...]).astype(BF16)

    def gate(b):
        return _sigmoid(gate_ref[:, b * D_MODEL:(b + 1) * D_MODEL].astype(F32))

    merged = gate(0) * jnp.dot(yf_ref[...], wf_ref[...], preferred_element_type=F32)
    merged = merged + gate(1) * jnp.dot(conv_out, wc_ref[...], preferred_element_type=F32)
    merged = merged + gate(2) * jnp.dot(pool_out, wp_ref[...], preferred_element_type=F32)
    merged = merged + gate(3) * jnp.dot(at_ref[...], wa_ref[...], preferred_element_type=F32)
    out = jnp.dot(merged.astype(BF16), wo_ref[...], preferred_element_type=F32)
    o_ref[...] = x_ref[...] + mod_ref[0, 2:3, :] * out


def merge_branches(x2d, mod, proj, yf, attn, lw, *, seq, t):
    m, d = x2d.shape
    tps = seq // t
    hb = t // HALO
    n_halo = m // HALO
    n_mod = mod.shape[0]
    mod_idx = (lambda i: (i // tps, 0, 0)) if n_mod > 1 else (lambda i: (0, 0, 0))
    const = lambda i: (0, 0)
    cp_blk = P_OFF_CP // CP_W
    return pl.pallas_call(
        functools.partial(_merge_kernel, t=t, tps=tps, seq=seq),
        grid=(m // t,),
        in_specs=[
            pl.BlockSpec((t, d), lambda i: (i, 0)),
            pl.BlockSpec((1, 6, d), mod_idx),
            pl.BlockSpec((t, 4 * d), lambda i: (i, 0)),
            pl.BlockSpec((t, CP_W), lambda i: (i, cp_blk)),
            pl.BlockSpec((HALO, CP_W), lambda i: (jnp.maximum(i * hb - 1, 0), cp_blk)),
            pl.BlockSpec((HALO, CP_W), lambda i: (jnp.minimum((i + 1) * hb, n_halo - 1), cp_blk)),
            pl.BlockSpec((t, FOURIER_W), lambda i: (i, 0)),
            pl.BlockSpec((t, Q_W), lambda i: (i, 0)),
            pl.BlockSpec((FOURIER_W, d), const),
            pl.BlockSpec((CONV_W, d), const),
            pl.BlockSpec((POOL_W, d), const),
            pl.BlockSpec((Q_W, d), const),
            pl.BlockSpec((d, d), const),
            pl.BlockSpec((CONV_K, CONV_W), const),
            pl.BlockSpec((1, CONV_W), const),
            pl.BlockSpec((1, CONV_W), const),
            pl.BlockSpec((POOL_W, POOL_W), const),
            pl.BlockSpec((1, POOL_W), const),
        ],
        out_specs=pl.BlockSpec((t, d), lambda i: (i, 0)),
        out_shape=jax.ShapeDtypeStruct((m, d), F32),
        scratch_shapes=[pltpu.VMEM((t + 2 * HALO, CONV_W), F32), pltpu.VMEM((t + 2 * HALO, POOL_W), F32)],
        compiler_params=_cparams(1),
        name="merge_branches",
    )(x2d, mod, proj, proj, proj, proj, yf, attn,
      lw["wf"], lw["wc"], lw["wp"], lw["wa"], lw["wo"], lw["dw"], lw["cb"], lw["cg"], lw["pw"], lw["ps"])


def _ffn_kernel(x_ref, mod_ref, g_ref, w1_ref, w3_ref, w2_ref, o_ref, h_ref, acc_ref):
    j = pl.program_id(1)

    @pl.when(j == 0)
    def _():
        h_ref[...] = _norm_mod(x_ref[...], g_ref[...], mod_ref[0, 3:4, :], mod_ref[0, 4:5, :]).astype(BF16)
        acc_ref[...] = jnp.zeros(acc_ref.shape, F32)

    h = h_ref[...]
    a = jnp.dot(h, w1_ref[...], preferred_element_type=F32)
    b = jnp.dot(h, w3_ref[...], preferred_element_type=F32)
    acc_ref[...] += jnp.dot((_silu(a) * b).astype(BF16), w2_ref[...], preferred_element_type=F32)

    @pl.when(j == pl.num_programs(1) - 1)
    def _():
        o_ref[...] = x_ref[...] + mod_ref[0, 5:6, :] * acc_ref[...]


def ffn_dense(x2d, mod, g, w1, w3, w2, *, tm, tf, tiles_per_mod):
    m, d = x2d.shape
    dff = w1.shape[1]
    n_mod = mod.shape[0]
    mod_idx = (lambda i, j: (i // tiles_per_mod, 0, 0)) if n_mod > 1 else (lambda i, j: (0, 0, 0))
    return pl.pallas_call(
        _ffn_kernel,
        grid=(m // tm, dff // tf),
        in_specs=[
            pl.BlockSpec((tm, d), lambda i, j: (i, 0)),
            pl.BlockSpec((1, 6, d), mod_idx),
            pl.BlockSpec((1, d), lambda i, j: (0, 0)),
            pl.BlockSpec((d, tf), lambda i, j: (0, j)),
            pl.BlockSpec((d, tf), lambda i, j: (0, j)),
            pl.BlockSpec((tf, d), lambda i, j: (j, 0)),
        ],
        out_specs=pl.BlockSpec((tm, d), lambda i, j: (i, 0)),
        out_shape=jax.ShapeDtypeStruct((m, d), F32),
        scratch_shapes=[pltpu.VMEM((tm, d), BF16), pltpu.VMEM((tm, d), F32)],
        compiler_params=_cparams(2),
        name="ffn_dense",
    )(x2d, mod, g.reshape(1, d), w1, w3, w2)


def _moe_kernel(x_ref, mod_ref, g_ref, r_ref, w1_ref, w3_ref, w2_ref, o_ref, h_ref, comb_ref, acc_ref):
    e = pl.program_id(1)
    j = pl.program_id(2)
    tm = x_ref.shape[0]

    @pl.when(jnp.logical_and(e == 0, j == 0))
    def _():
        h = _norm_mod(x_ref[...], g_ref[...], mod_ref[0, 3:4, :], mod_ref[0, 4:5, :])
        h_ref[...] = h.astype(BF16)
        acc_ref[...] = jnp.zeros(acc_ref.shape, F32)
        logits = jnp.dot(h, r_ref[...], preferred_element_type=F32, precision=lax.Precision.HIGHEST)
        lane = lax.broadcasted_iota(jnp.int32, (tm, 128), 1).astype(F32)
        neg = jnp.float32(-jnp.inf)
        lg = jnp.where(lane < N_EXPERTS, logits, neg)
        v1 = jnp.max(lg, axis=-1, keepdims=True)
        i1 = jnp.min(jnp.where(lg == v1, lane, 128.0), axis=-1, keepdims=True)
        lg2 = jnp.where(lane == i1, neg, lg)
        v2 = jnp.max(lg2, axis=-1, keepdims=True)
        i2 = jnp.min(jnp.where(lg2 == v2, lane, 128.0), axis=-1, keepdims=True)
        e2 = jnp.exp(v2 - v1)
        w_first = 1.0 / (1.0 + e2)
        w_second = e2 / (1.0 + e2)
        comb_ref[...] = jnp.where(lane == i1, w_first, 0.0) + jnp.where(lane == i2, w_second, 0.0)

    h = h_ref[...]
    a = jnp.dot(h, w1_ref[0], preferred_element_type=F32)
    b = jnp.dot(h, w3_ref[0], preferred_element_type=F32)
    y = jnp.dot((_silu(a) * b).astype(BF16), w2_ref[0], preferred_element_type=F32)
    lane = lax.broadcasted_iota(jnp.int32, (tm, 128), 1)
    ce = jnp.sum(jnp.where(lane == e, comb_ref[...], 0.0), axis=-1, keepdims=True)
    acc_ref[...] += ce * y

    @pl.when(jnp.logical_and(e == pl.num_programs(1) - 1, j == pl.num_programs(2) - 1))
    def _():
        o_ref[...] = x_ref[...] + mod_ref[0, 5:6, :] * acc_ref[...]


def moe_dense(x2d, mod, g, router_pad, w1, w3, w2, *, tm, tf, tiles_per_mod):
    m, d = x2d.shape
    n_e, _, dff = w1.shape
    n_mod = mod.shape[0]
    mod_idx = (lambda i, e, j: (i // tiles_per_mod, 0, 0)) if n_mod > 1 else (lambda i, e, j: (0, 0, 0))
    return pl.pallas_call(
        _moe_kernel,
        grid=(m // tm, n_e, dff // tf),
        in_specs=[
            pl.BlockSpec((tm, d), lambda i, e, j: (i, 0)),
            pl.BlockSpec((1, 6, d), mod_idx),
            pl.BlockSpec((1, d), lambda i, e, j: (0, 0)),
            pl.BlockSpec((d, 128), lambda i, e, j: (0, 0)),
            pl.BlockSpec((1, d, tf), lambda i, e, j: (e, 0, j)),
            pl.BlockSpec((1, d, tf), lambda i, e, j: (e, 0, j)),
            pl.BlockSpec((1, tf, d), lambda i, e, j: (e, j, 0)),
        ],
        out_specs=pl.BlockSpec((tm, d), lambda i, e, j: (i, 0)),
        out_shape=jax.ShapeDtypeStruct((m, d), F32),
        scratch_shapes=[pltpu.VMEM((tm, d), BF16), pltpu.VMEM((tm, 128), F32), pltpu.VMEM((tm, d), F32)],
        compiler_params=_cparams(3),
        name="moe_experts",
    )(x2d, mod, g.reshape(1, d), router_pad, w1, w3, w2)


def _permute_w_in(w):
    f, c, p, q, kv, gts = w[:, 0:256], w[:, 256:768], w[:, 768:1024], w[:, 1024:1536], w[:, 1536:1792], w[:, 1792:]
    return jnp.concatenate([gts, q, c, p, f, kv], axis=1).astype(BF16)


def _layer_weights(layer, w_br_fourier, conv_dw, conv_b, conv_norm_g, w_br_conv, pool_w, pool_scale, w_br_pool,
                   w_br_attn, w_out):
    pw = jax.scipy.linalg.block_diag(*[pool_w[layer, i] for i in range(len(POOL_WINDOWS))])
    return {
        "wf": w_br_fourier[layer].astype(BF16), "wc": w_br_conv[layer].astype(BF16),
        "wp": w_br_pool[layer].astype(BF16), "wa": w_br_attn[layer].astype(BF16), "wo": w_out[layer].astype(BF16),
        "dw": conv_dw[layer], "cb": conv_b[layer].reshape(1, CONV_W), "cg": conv_norm_g[layer].reshape(1, CONV_W),
        "pw": pw.astype(BF16), "ps": pool_scale[layer].reshape(1, POOL_W),
    }


def kernel(x, c, ctx, c_ctx, w_mod, b_mod, norm1_g, norm2_g, w_in, w_br_fourier, conv_dw, conv_b, conv_norm_g,
           w_br_conv, pool_w, pool_scale, w_br_pool, q_norm_g, k_norm_g, w_br_attn, w_out, ffn_w1, ffn_w3, ffn_w2,
           moe_router, moe_w1, moe_w3, moe_w2):
    batch, seq, d = x.shape
    ctx_len = ctx.shape[1]
    depth = w_in.shape[0]
    rope = rope_tables(seq)

    c_rows = jnp.zeros((8, d), F32).at[0:batch].set(c).at[batch].set(c_ctx)
    mods = modulation_all(c_rows, w_mod, b_mod).reshape(depth, 8, 6, d)

    xl = x.reshape(batch * seq, d)
    xc = ctx.reshape(batch * ctx_len, d)
    for layer in range(depth):
        is_last = layer == depth - 1
        mod_l = mods[layer, 0:batch]
        mod_c = mods[layer, batch:batch + 1]
        w_in_l = _permute_w_in(w_in[layer])
        lw = _layer_weights(layer, w_br_fourier, conv_dw, conv_b, conv_norm_g, w_br_conv, pool_w, pool_scale,
                            w_br_pool, w_br_attn, w_out)

        proj_c = input_projection(xc, mod_c, norm1_g[layer], w_in_l, tm=256, tiles_per_mod=1)
        qc, ktc, vc = qkv_prepare(proj_c, q_norm_g[layer], k_norm_g[layer], None, batch=batch, seq=ctx_len, tp=256)

        proj = input_projection(xl, mod_l, norm1_g[layer], w_in_l, tm=512, tiles_per_mod=seq // 512)
        q, kt, v = qkv_prepare(proj, q_norm_g[layer], k_norm_g[layer], rope, batch=batch, seq=seq, tp=512)
        kt_all = jnp.concatenate([kt, ktc], axis=3)
        v_all = jnp.concatenate([v, vc], axis=2)
        attn = attention(q, kt_all, v_all, batch=batch, seq_q=seq, tq=256, tk=768)
        yf = fourier_mix(proj[:, P_OFF_F:P_OFF_F + FOURIER_W], batch=batch, seq=seq, n1=64, n2=seq // 64)
        xl = merge_branches(xl, mod_l, proj, yf, attn, lw, seq=seq, t=512)

        if not is_last:
            attn_c = attention(qc, ktc, vc, batch=batch, seq_q=ctx_len, tq=256, tk=ctx_len)
            yf_c = fourier_mix(proj_c[:, P_OFF_F:P_OFF_F + FOURIER_W], batch=batch, seq=ctx_len, n1=16,
                               n2=ctx_len // 16)
            xc = merge_branches(xc, mod_c, proj_c, yf_c, attn_c, lw, seq=ctx_len, t=256)

        j = layer // 2
        if layer % 2 == 0:
            w1, w3, w2 = ffn_w1[j].astype(BF16), ffn_w3[j].astype(BF16), ffn_w2[j].astype(BF16)
            xl = ffn_dense(xl, mod_l, norm2_g[layer], w1, w3, w2, tm=512, tf=1408, tiles_per_mod=seq // 512)
            if not is_last:
                xc = ffn_dense(xc, mod_c, norm2_g[layer], w1, w3, w2, tm=256, tf=1408, tiles_per_mod=1)
        else:
            w1, w3, w2 = moe_w1[j].astype(BF16), moe_w3[j].astype(BF16), moe_w2[j].astype(BF16)
            router_pad = jnp.zeros((d, 128), F32).at[:, 0:N_EXPERTS].set(moe_router[j])
            xl = moe_dense(xl, mod_l, norm2_g[layer], router_pad, w1, w3, w2, tm=1024, tf=896,
                           tiles_per_mod=seq // 1024)
            if not is_last:
                xc = moe_dense(xc, mod_c, norm2_g[layer], router_pad, w1, w3, w2, tm=256, tf=1792, tiles_per_mod=1)
    return xl.reshape(batch, seq, d)
```

```python
import functools
import math

import numpy as np
import jax
import jax.numpy as jnp
from jax import lax
from jax.experimental import pallas as pl
from jax.experimental.pallas import tpu as pltpu

F32 = jnp.float32
BF16 = jnp.bfloat16

D_MODEL = 1024
GRID_W = 64
EPS = 1e-6
FOURIER_GW = 64
FOURIER_W = 256
CONV_W = 256
CONV_K = 31
CONV_HALF = CONV_K // 2
POOL_WINDOWS = (2, 4, 8, 16)
POOL_GW = 64
POOL_W = 256
HEAD_DIM = 64
N_HEADS = 8
N_KV_HEADS = 2
Q_PER_KV = 4
Q_W = 512
KV_W = 128
ROPE_THETA = 10000.0
N_EXPERTS = 8
IN_W = 5888

P_OFF_G = 0
P_OFF_Q = 4096
P_OFF_CP = 4608
P_OFF_F = 5376
P_OFF_KV = 5632
CP_W = 2 * CONV_W + POOL_W

Q_SCALE = (HEAD_DIM ** -0.5) * math.log2(math.e)

MOE_ROUTE_TM = 512
MOE_DISPATCH_TM = 512
MOE_COMBINE_TM = 256
MOE_EXPERT_TM = 512
MOE_EXPERT_TF = 896

HALO = 16
VMEM_LIMIT = 56 * 1024 * 1024


def _cparams(n_axes):
    return pltpu.CompilerParams(dimension_semantics=("arbitrary",) * n_axes, vmem_limit_bytes=VMEM_LIMIT)


def _sigmoid(v):
    return 1.0 / (1.0 + jnp.exp(-v))


def _silu(v):
    return v * _sigmoid(v)


def _norm_mod(x, g, shift, scale):
    ms = jnp.mean(x * x, axis=-1, keepdims=True)
    return x * lax.rsqrt(ms + EPS) * g * (1.0 + scale) + shift


def _mod_kernel(c_ref, w_ref, b_ref, o_ref):
    s = _silu(c_ref[...])
    o_ref[0] = jnp.dot(s, w_ref[0], preferred_element_type=F32, precision=lax.Precision.HIGHEST) + b_ref[0]


def modulation_all(c_rows, w_mod, b_mod):
    n_layers, d, n = w_mod.shape
    tn = 1536
    return pl.pallas_call(
        _mod_kernel,
        grid=(n_layers, n // tn),
        in_specs=[
            pl.BlockSpec((8, d), lambda l, j: (0, 0)),
            pl.BlockSpec((1, d, tn), lambda l, j: (l, 0, j)),
            pl.BlockSpec((1, 1, tn), lambda l, j: (l, 0, j)),
        ],
        out_specs=pl.BlockSpec((1, 8, tn), lambda l, j: (l, 0, j)),
        out_shape=jax.ShapeDtypeStruct((n_layers, 8, n), F32),
        compiler_params=_cparams(2),
        name="modulation",
    )(c_rows, w_mod, b_mod.reshape(n_layers, 1, n))


def _inproj_kernel(x_ref, mod_ref, g_ref, w_ref, o_ref, *, chunks):
    h = _norm_mod(x_ref[...], g_ref[...], mod_ref[0, 0:1, :], mod_ref[0, 1:2, :]).astype(BF16)
    for c0, cw in chunks:
        o_ref[:, c0:c0 + cw] = jnp.dot(h, w_ref[:, c0:c0 + cw], preferred_element_type=F32).astype(o_ref.dtype)


def input_projection(x2d, mod, g, w_bf16, *, tm, tiles_per_mod):
    m, d = x2d.shape
    n = w_bf16.shape[1]
    chunks = tuple((c0, min(512, n - c0)) for c0 in range(0, n, 512))
    n_mod = mod.shape[0]
    mod_idx = (lambda i: (i // tiles_per_mod, 0, 0)) if n_mod > 1 else (lambda i: (0, 0, 0))
    return pl.pallas_call(
        functools.partial(_inproj_kernel, chunks=chunks),
        grid=(m // tm,),
        in_specs=[
            pl.BlockSpec((tm, d), lambda i: (i, 0)),
            pl.BlockSpec((1, 6, d), mod_idx),
            pl.BlockSpec((1, d), lambda i: (0, 0)),
            pl.BlockSpec((d, n), lambda i: (0, 0), pipeline_mode=pl.Buffered(1)),
        ],
        out_specs=pl.BlockSpec((tm, n), lambda i: (i, 0)),
        out_shape=jax.ShapeDtypeStruct((m, n), BF16),
        compiler_params=_cparams(1),
        name="input_projection",
    )(x2d, mod, g.reshape(1, d), w_bf16)


def _seg_sum64(v, ones_bd):
    hi = v.astype(BF16)
    lo = (v - hi.astype(F32)).astype(BF16)
    return (jnp.dot(hi, ones_bd, preferred_element_type=F32) + jnp.dot(lo, ones_bd, preferred_element_type=F32))


def _head_norm_rope(x, g, ones_bd, cos, sin, low_mask):
    y = x * lax.rsqrt(_seg_sum64(x * x, ones_bd) * (1.0 / HEAD_DIM) + EPS) * g
    if cos is None:
        return y
    partner = jnp.where(low_mask, pltpu.roll(y, 128 - 16, axis=1), pltpu.roll(y, 16, axis=1))
    return y * cos + partner * sin


def _prep_kernel(*refs, use_rope):
    if use_rope:
        q_ref, kv_ref, gq_ref, gk_ref, ones_ref, cos_ref, sin_ref, qo_ref, kt_ref, v_ref = refs
        cos, sin = cos_ref[...], sin_ref[...]
    else:
        q_ref, kv_ref, gq_ref, gk_ref, ones_ref, qo_ref, kt_ref, v_ref = refs
        cos = sin = None
    t = q_ref.shape[0]
    ones_bd = ones_ref[...]
    lane = lax.broadcasted_iota(jnp.int32, (t, 128), 1)
    low_mask = (lane % 32) < 16
    gq = gq_ref[...]
    for c in range(Q_W // 128):
        xq = q_ref[:, 128 * c:128 * (c + 1)].astype(F32)
        yq = _head_norm_rope(xq, gq, ones_bd, cos, sin, low_mask) * Q_SCALE
        qo_ref[:, 128 * c:128 * (c + 1)] = yq.astype(BF16)
    xk = kv_ref[:, 0:128].astype(F32)
    yk = _head_norm_rope(xk, gk_ref[...], ones_bd, cos, sin, low_mask)
    kt = yk.T.astype(BF16)
    for h in range(N_KV_HEADS):
        kh = kt[64 * h:64 * (h + 1), :]
        kt_ref[0, h] = jnp.concatenate([kh] * Q_PER_KV, axis=0)
    v = kv_ref[:, 128:256].astype(F32)
    first = lane < 64
    v0 = jnp.where(first, v, 1.0).astype(BF16)
    v1 = jnp.where(first, pltpu.roll(v, 64, axis=1), 1.0).astype(BF16)
    v_ref[0, 0] = jnp.concatenate([v0, v0], axis=1)
    v_ref[0, 1] = jnp.concatenate([v1, v1], axis=1)


def qkv_prepare(proj, gq, gk, rope, *, batch, seq, tp):
    m = proj.shape[0]
    tps = seq // tp
    use_rope = rope is not None
    ones_bd = jnp.asarray(np.kron(np.eye(2, dtype=np.float32), np.ones((64, 64), np.float32)), BF16)
    gq2 = jnp.tile(gq, 2).reshape(1, 128)
    gk2 = jnp.tile(gk, 2).reshape(1, 128)
    in_specs = [
        pl.BlockSpec((tp, Q_W), lambda i: (i, P_OFF_Q // Q_W)),
        pl.BlockSpec((tp, 256), lambda i: (i, P_OFF_KV // 256)),
        pl.BlockSpec((1, 128), lambda i: (0, 0)),
        pl.BlockSpec((1, 128), lambda i: (0, 0)),
        pl.BlockSpec((128, 128), lambda i: (0, 0)),
    ]
    args = [proj, proj, gq2, gk2, ones_bd]
    if use_rope:
        in_specs += [pl.BlockSpec((tp, 128), lambda i: (i % tps, 0))] * 2
        args += list(rope)
    return pl.pallas_call(
        functools.partial(_prep_kernel, use_rope=use_rope),
        grid=(m // tp,),
        in_specs=in_specs,
        out_specs=[
            pl.BlockSpec((tp, Q_W), lambda i: (i, 0)),
            pl.BlockSpec((1, N_KV_HEADS, 256, tp), lambda i: (i // tps, 0, 0, i % tps)),
            pl.BlockSpec((1, N_KV_HEADS, tp, 256), lambda i: (i // tps, 0, i % tps, 0)),
        ],
        out_shape=[
            jax.ShapeDtypeStruct((m, Q_W), BF16),
            jax.ShapeDtypeStruct((batch, N_KV_HEADS, 256, seq), BF16),
            jax.ShapeDtypeStruct((batch, N_KV_HEADS, seq, 256), BF16),
        ],
        compiler_params=_cparams(1),
        name="qkv_prepare",
    )(*args)


def rope_tables(seq):
    n_freq = HEAD_DIM // 4
    freqs = ROPE_THETA ** (-jnp.arange(n_freq, dtype=F32) / n_freq)
    t = jnp.arange(seq)
    row = (t // GRID_W).astype(F32)
    col = (t % GRID_W).astype(F32)
    ang_r = row[:, None] * freqs
    ang_c = col[:, None] * freqs
    cos = jnp.concatenate([jnp.cos(ang_r)] * 2 + [jnp.cos(ang_c)] * 2, axis=1)
    sin = jnp.concatenate([-jnp.sin(ang_r), jnp.sin(ang_r), -jnp.sin(ang_c), jnp.sin(ang_c)], axis=1)
    return jnp.tile(cos, (1, 2)), jnp.tile(sin, (1, 2))


def _attn_kernel(q_ref, kt_ref, v_ref, o_ref, qs_ref, s0, s1, p0, p1, a0, a1, m_ref, acc_ref, *, tq, tk, nk):
    s_bufs, p_bufs, a_bufs = (s0, s1), (p0, p1), (a0, a1)
    group = lax.broadcasted_iota(jnp.int32, (tq, 256), 1) // HEAD_DIM
    q = q_ref[...]
    for g in range(Q_PER_KV):
        qs_ref[g * tq:(g + 1) * tq, :] = jnp.where(group == g, q, jnp.zeros_like(q))
    m_ref[...] = jnp.full(m_ref.shape, -jnp.inf, F32)
    acc_ref[...] = jnp.zeros(acc_ref.shape, F32)

    def scores(j, slot):
        off = pl.multiple_of(j * tk, tk)
        s_bufs[slot][...] = jnp.dot(qs_ref[...], kt_ref[0, 0, :, pl.ds(off, tk)], preferred_element_type=F32)

    def numerators(slot):
        s = s_bufs[slot][...]
        m_old = m_ref[...]
        m_new = jnp.maximum(m_old, jnp.max(s, axis=-1, keepdims=True))
        a_bufs[slot][...] = jnp.exp2(m_old - m_new)
        p_bufs[slot][...] = jnp.exp2(s - m_new).astype(BF16)
        m_ref[...] = m_new

    def weighted_sum(j, slot):
        off = pl.multiple_of(j * tk, tk)
        pv = jnp.dot(p_bufs[slot][...], v_ref[0, 0, pl.ds(off, tk), :], preferred_element_type=F32)
        acc_ref[...] = a_bufs[slot][...] * acc_ref[...] + pv

    def step(t, slot):
        scores(t, slot)
        numerators(1 - slot)
        weighted_sum(t - 2, slot)

    scores(0, 0)
    if nk > 1:
        scores(1, 1)
        numerators(0)
        n_pairs = (nk - 2) // 2

        def pair(i, carry):
            t = 2 + 2 * i
            step(t, 0)
            step(t + 1, 1)
            return carry

        lax.fori_loop(0, n_pairs, pair, 0)
        if (nk - 2) % 2:
            step(nk - 1, (nk - 1) % 2)
        numerators((nk - 1) % 2)
        weighted_sum(nk - 2, nk % 2)
    else:
        numerators(0)
    weighted_sum(nk - 1, (nk - 1) % 2)

    acc = acc_ref[...]
    low = lax.broadcasted_iota(jnp.int32, (tq, 128), 1) < HEAD_DIM
    for half in range(2):
        ae = acc[(2 * half) * tq:(2 * half + 1) * tq, 128 * half:128 * (half + 1)]
        ao = acc[(2 * half + 1) * tq:(2 * half + 2) * tq, 128 * half:128 * (half + 1)]
        even = ae / pltpu.roll(ae, 64, axis=1)
        odd = pltpu.roll(ao, 64, axis=1) / ao
        o_ref[:, 128 * half:128 * (half + 1)] = jnp.where(low, even, odd).astype(o_ref.dtype)


def attention(q, kt4, v4, *, batch, seq_q, tq, tk):
    lk = kt4.shape[3]
    nq = seq_q // tq
    nk = lk // tk
    rows = Q_PER_KV * tq
    return pl.pallas_call(
        functools.partial(_attn_kernel, tq=tq, tk=tk, nk=nk),
        grid=(batch, N_KV_HEADS, nq),
        in_specs=[
            pl.BlockSpec((tq, 256), lambda b, h, i: (b * nq + i, h)),
            pl.BlockSpec((1, 1, 256, lk), lambda b, h, i: (b, h, 0, 0)),
            pl.BlockSpec((1, 1, lk, 256), lambda b, h, i: (b, h, 0, 0)),
        ],
        out_specs=pl.BlockSpec((tq, 256), lambda b, h, i: (b * nq + i, h)),
        out_shape=jax.ShapeDtypeStruct((batch * seq_q, Q_W), BF16),
        scratch_shapes=[
            pltpu.VMEM((rows, 256), BF16),
            pltpu.VMEM((rows, tk), F32), pltpu.VMEM((rows, tk), F32),
            pltpu.VMEM((rows, tk), BF16), pltpu.VMEM((rows, tk), BF16),
            pltpu.VMEM((rows, 1), F32), pltpu.VMEM((rows, 1), F32),
            pltpu.VMEM((rows, 1), F32),
            pltpu.VMEM((rows, 256), F32),
        ],
        compiler_params=_cparams(3),
        name="attention",
    )(q, kt4, v4)


def _dft_cs(n):
    k = np.arange(n)
    ang = 2.0 * np.pi * ((k[:, None] * k[None, :]) % n) / n
    return np.cos(ang), np.sin(ang)


def _fft1_kernel(x_ref, f_ref, c_ref, s_ref, o_ref, *, n1):
    y = jnp.dot(f_ref[...], x_ref[0], preferred_element_type=F32)
    yr, yi = y[:n1], y[n1:]
    c, s = c_ref[...], s_ref[...]
    o_ref[0, 0] = (yr * c + yi * s).astype(o_ref.dtype)
    o_ref[0, 1] = (yi * c - yr * s).astype(o_ref.dtype)


def _fft2_kernel(y_ref, f_ref, bc_ref, bs_ref, o_ref, *, n2, kb):
    for j in range(kb):
        y2 = jnp.concatenate([y_ref[0, 0, j], y_ref[0, 1, j]], axis=0)
        x2 = jnp.dot(f_ref[...], y2, preferred_element_type=F32)
        xr = x2[:n2].astype(BF16)
        xi = x2[n2:].astype(BF16)
        z = (jnp.dot(xr, bc_ref[...], preferred_element_type=F32) + jnp.dot(xi, bs_ref[...], preferred_element_type=F32))
        o_ref[0, j] = z.astype(o_ref.dtype)


def fourier_mix(u, *, batch, seq, n1, n2):
    cw = u.shape[1]
    lanes = n2 * cw
    tl = min(lanes, 4096)
    c1, s1 = _dft_cs(n1)
    f1 = jnp.asarray(np.concatenate([c1, -s1], axis=0), BF16)
    k1 = np.arange(n1)[:, None]
    t2 = np.arange(n2)[None, :]
    ang = 2.0 * np.pi * ((k1 * t2) % seq) / seq
    twc = jnp.asarray(np.repeat(np.cos(ang), cw, axis=1), F32)
    tws = jnp.asarray(np.repeat(np.sin(ang), cw, axis=1), F32)
    x2 = u.reshape(batch, n1, lanes)
    yp = pl.pallas_call(
        functools.partial(_fft1_kernel, n1=n1),
        grid=(batch, lanes // tl),
        in_specs=[
            pl.BlockSpec((1, n1, tl), lambda b, j: (b, 0, j)),
            pl.BlockSpec((2 * n1, n1), lambda b, j: (0, 0)),
            pl.BlockSpec((n1, tl), lambda b, j: (0, j)),
            pl.BlockSpec((n1, tl), lambda b, j: (0, j)),
        ],
        out_specs=pl.BlockSpec((1, 2, n1, tl), lambda b, j: (b, 0, 0, j)),
        out_shape=jax.ShapeDtypeStruct((batch, 2, n1, lanes), BF16),
        compiler_params=_cparams(2),
        name="fft_stage1",
    )(x2, f1, twc, tws)

    c2, s2 = _dft_cs(n2)
    f2 = jnp.asarray(np.block([[c2, s2], [-s2, c2]]), BF16)
    cg, sg = _dft_cs(FOURIER_GW)
    norm = 1.0 / math.sqrt(seq * FOURIER_GW)
    bdc = jnp.asarray(np.kron(np.eye(cw // FOURIER_GW), cg) * norm, BF16)
    bds = jnp.asarray(np.kron(np.eye(cw // FOURIER_GW), sg) * norm, BF16)
    kb = min(n1, 16)
    y5 = yp.reshape(batch, 2, n1, n2, cw)
    z = pl.pallas_call(
        functools.partial(_fft2_kernel, n2=n2, kb=kb),
        grid=(batch, n1 // kb),
        in_specs=[
            pl.BlockSpec((1, 2, kb, n2, cw), lambda b, j: (b, 0, j, 0, 0)),
            pl.BlockSpec((2 * n2, 2 * n2), lambda b, j: (0, 0)),
            pl.BlockSpec((cw, cw), lambda b, j: (0, 0)),
            pl.BlockSpec((cw, cw), lambda b, j: (0, 0)),
        ],
        out_specs=pl.BlockSpec((1, kb, n2, cw), lambda b, j: (b, j, 0, 0)),
        out_shape=jax.ShapeDtypeStruct((batch, n1, n2, cw), BF16),
        compiler_params=_cparams(2),
        name="fft_stage2",
    )(y5, f2, bdc, bds)
    return z.transpose(0, 2, 1, 3).reshape(batch * seq, cw)


def _merge_kernel(x_ref, mod_ref, gate_ref, cp_ref, cpp_ref, cpn_ref, yf_ref, at_ref,
                  wf_ref, wc_ref, wp_ref, wa_ref, wo_ref, dw_ref, cb_ref, cg_ref, pw_ref, ps_ref,
                  o_ref, ybuf, xbuf, *, t, tps, seq):
    i = pl.program_id(0)
    pos_tile = i % tps
    keep_prev = jnp.where(pos_tile != 0, 1.0, 0.0).astype(F32)
    keep_next = jnp.where(pos_tile != tps - 1, 1.0, 0.0).astype(F32)

    def glu(blk):
        return blk[:, 0:CONV_W].astype(F32) * _sigmoid(blk[:, CONV_W:2 * CONV_W].astype(F32))

    cp, cpp, cpn = cp_ref[...], cpp_ref[...], cpn_ref[...]
    ybuf[0:HALO, :] = glu(cpp) * keep_prev
    ybuf[HALO:HALO + t, :] = glu(cp)
    ybuf[HALO + t:HALO + t + HALO, :] = glu(cpn) * keep_next
    xbuf[0:HALO, :] = cpp[:, 2 * CONV_W:].astype(F32) * keep_prev
    xbuf[HALO:HALO + t, :] = cp[:, 2 * CONV_W:].astype(F32)
    xbuf[HALO + t:HALO + t + HALO, :] = cpn[:, 2 * CONV_W:].astype(F32) * keep_next

    acc = jnp.zeros((t, CONV_W), F32)
    for k in range(CONV_K):
        acc = acc + dw_ref[k:k + 1, :] * ybuf[pl.ds(HALO - CONV_HALF + k, t), :]
    acc = acc + cb_ref[...]
    ms = jnp.mean(acc * acc, axis=-1, keepdims=True)
    conv_out = _silu(acc * lax.rsqrt(ms + EPS) * cg_ref[...]).astype(BF16)

    def xs(d):
        return xbuf[pl.ds(HALO + d, t), :]

    x0 = xs(0)
    s2 = xs(-1) + x0
    s4 = s2 + xs(-2) + xs(1)
    s8 = s4 + xs(-4) + xs(-3) + xs(2) + xs(3)
    s16 = s8 + xs(-8) + xs(-7) + xs(-6) + xs(-5) + xs(4) + xs(5) + xs(6) + xs(7)
    grp = lax.broadcasted_iota(jnp.int32, (t, POOL_W), 1) // POOL_GW
    pos = pos_tile * t + lax.broadcasted_iota(jnp.int32, (t, POOL_W), 0)
    half = jnp.where(grp == 0, 1, jnp.where(grp == 1, 2, jnp.where(grp == 2, 4, 8)))
    cnt = (jnp.minimum(pos + half, seq) - jnp.maximum(pos - half, 0)).astype(F32)
    wsum = jnp.where(grp == 0, s2, jnp.where(grp == 1, s4, jnp.where(grp == 2, s8, s16)))
    pool_in = (wsum / cnt - x0).astype(BF16)
    pool_out = (jnp.dot(pool_in, pw_ref[...], preferred_element_type=F32) * ps_ref[...]).astype(BF16)

    def gate(b):
        return _sigmoid(gate_ref[:, b * D_MODEL:(b + 1) * D_MODEL].astype(F32))

    merged = gate(0) * jnp.dot(yf_ref[...], wf_ref[...], preferred_element_type=F32)
    merged = merged + gate(1) * jnp.dot(conv_out, wc_ref[...], preferred_element_type=F32)
    merged = merged + gate(2) * jnp.dot(pool_out, wp_ref[...], preferred_element_type=F32)
    merged = merged + gate(3) * jnp.dot(at_ref[...], wa_ref[...], preferred_element_type=F32)
    out = jnp.dot(merged.astype(BF16), wo_ref[...], preferred_element_type=F32)
    o_ref[...] = x_ref[...] + mod_ref[0, 2:3, :] * out


def merge_branches(x2d, mod, proj, yf, attn, lw, *, seq, t):
    m, d = x2d.shape
    tps = seq // t
    hb = t // HALO
    n_halo = m // HALO
    n_mod = mod.shape[0]
    mod_idx = (lambda i: (i // tps, 0, 0)) if n_mod > 1 else (lambda i: (0, 0, 0))
    const = lambda i: (0, 0)
    cp_blk = P_OFF_CP // CP_W
    return pl.pallas_call(
        functools.partial(_merge_kernel, t=t, tps=tps, seq=seq),
        grid=(m // t,),
        in_specs=[
            pl.BlockSpec((t, d), lambda i: (i, 0)),
            pl.BlockSpec((1, 6, d), mod_idx),
            pl.BlockSpec((t, 4 * d), lambda i: (i, 0)),
            pl.BlockSpec((t, CP_W), lambda i: (i, cp_blk)),
            pl.BlockSpec((HALO, CP_W), lambda i: (jnp.maximum(i * hb - 1, 0), cp_blk)),
            pl.BlockSpec((HALO, CP_W), lambda i: (jnp.minimum((i + 1) * hb, n_halo - 1), cp_blk)),
            pl.BlockSpec((t, FOURIER_W), lambda i: (i, 0)),
            pl.BlockSpec((t, Q_W), lambda i: (i, 0)),
            pl.BlockSpec((FOURIER_W, d), const),
            pl.BlockSpec((CONV_W, d), const),
            pl.BlockSpec((POOL_W, d), const),
            pl.BlockSpec((Q_W, d), const),
            pl.BlockSpec((d, d), const),
            pl.BlockSpec((CONV_K, CONV_W), const),
            pl.BlockSpec((1, CONV_W), const),
            pl.BlockSpec((1, CONV_W), const),
            pl.BlockSpec((POOL_W, POOL_W), const),
            pl.BlockSpec((1, POOL_W), const),
        ],
        out_specs=pl.BlockSpec((t, d), lambda i: (i, 0)),
        out_shape=jax.ShapeDtypeStruct((m, d), F32),
        scratch_shapes=[pltpu.VMEM((t + 2 * HALO, CONV_W), F32), pltpu.VMEM((t + 2 * HALO, POOL_W), F32)],
        compiler_params=_cparams(1),
        name="merge_branches",
    )(x2d, mod, proj, proj, proj, proj, yf, attn,
      lw["wf"], lw["wc"], lw["wp"], lw["wa"], lw["wo"], lw["dw"], lw["cb"], lw["cg"], lw["pw"], lw["ps"])


def _ffn_kernel(x_ref, mod_ref, g_ref, w1_ref, w3_ref, w2_ref, o_ref, h_ref, acc_ref):
    j = pl.program_id(1)

    @pl.when(j == 0)
    def _():
        h_ref[...] = _norm_mod(x_ref[...], g_ref[...], mod_ref[0, 3:4, :], mod_ref[0, 4:5, :]).astype(BF16)
        acc_ref[...] = jnp.zeros(acc_ref.shape, F32)

    h = h_ref[...]
    a = jnp.dot(h, w1_ref[...], preferred_element_type=F32)
    b = jnp.dot(h, w3_ref[...], preferred_element_type=F32)
    acc_ref[...] += jnp.dot((_silu(a) * b).astype(BF16), w2_ref[...], preferred_element_type=F32)

    @pl.when(j == pl.num_programs(1) - 1)
    def _():
        o_ref[...] = x_ref[...] + mod_ref[0, 5:6, :] * acc_ref[...]


def ffn_dense(x2d, mod, g, w1, w3, w2, *, tm, tf, tiles_per_mod):
    m, d = x2d.shape
    dff = w1.shape[1]
    n_mod = mod.shape[0]
    mod_idx = (lambda i, j: (i // tiles_per_mod, 0, 0)) if n_mod > 1 else (lambda i, j: (0, 0, 0))
    return pl.pallas_call(
        _ffn_kernel,
        grid=(m // tm, dff // tf),
        in_specs=[
            pl.BlockSpec((tm, d), lambda i, j: (i, 0)),
            pl.BlockSpec((1, 6, d), mod_idx),
            pl.BlockSpec((1, d), lambda i, j: (0, 0)),
            pl.BlockSpec((d, tf), lambda i, j: (0, j)),
            pl.BlockSpec((d, tf), lambda i, j: (0, j)),
            pl.BlockSpec((tf, d), lambda i, j: (j, 0)),
        ],
        out_specs=pl.BlockSpec((tm, d), lambda i, j: (i, 0)),
        out_shape=jax.ShapeDtypeStruct((m, d), F32),
        scratch_shapes=[pltpu.VMEM((tm, d), BF16), pltpu.VMEM((tm, d), F32)],
        compiler_params=_cparams(2),
        name="ffn_dense",
    )(x2d, mod, g.reshape(1, d), w1, w3, w2)


def _top2(logits):
    t = logits.shape[0]
    lane = lax.broadcasted_iota(jnp.int32, (t, 128), 1).astype(F32)
    neg = jnp.float32(-jnp.inf)
    lg = jnp.where(lane < N_EXPERTS, logits, neg)
    v1 = jnp.max(lg, axis=-1, keepdims=True)
    i1 = jnp.min(jnp.where(lg == v1, lane, 128.0), axis=-1, keepdims=True)
    lg2 = jnp.where(lane == i1, neg, lg)
    v2 = jnp.max(lg2, axis=-1, keepdims=True)
    i2 = jnp.min(jnp.where(lg2 == v2, lane, 128.0), axis=-1, keepdims=True)
    e2 = jnp.exp(v2 - v1)
    return i1, i2, 1.0 / (1.0 + e2), e2 / (1.0 + e2)


R_E1, R_E2, R_W1, R_W2, R_RANK1, R_RANK2 = range(6)


def _route_kernel(x_ref, mod_ref, g_ref, r_ref, tri_ref, route_ref, cnt_ref, carry_ref):
    @pl.when(pl.program_id(0) == 0)
    def _():
        carry_ref[...] = jnp.zeros(carry_ref.shape, F32)

    t = x_ref.shape[0]
    h = _norm_mod(x_ref[...], g_ref[...], mod_ref[0, 3:4, :], mod_ref[0, 4:5, :])
    logits = jnp.dot(h, r_ref[...], preferred_element_type=F32, precision=lax.Precision.HIGHEST)
    i1, i2, w1, w2 = _top2(logits)
    lane = lax.broadcasted_iota(jnp.int32, (t, 128), 1).astype(F32)
    oh1 = jnp.where(lane == i1, 1.0, 0.0)
    oh2 = jnp.where(lane == i2, 1.0, 0.0)
    both = oh1 + oh2
    before = carry_ref[...] + jnp.dot(tri_ref[...], both.astype(BF16), preferred_element_type=F32)
    rank1 = jnp.sum(oh1 * before, axis=-1, keepdims=True)
    rank2 = jnp.sum(oh2 * before, axis=-1, keepdims=True)
    carry_ref[...] += jnp.sum(both, axis=0, keepdims=True)
    rec = jnp.zeros((t, 128), F32)
    for col, val in ((R_E1, i1), (R_E2, i2), (R_W1, w1), (R_W2, w2), (R_RANK1, rank1), (R_RANK2, rank2)):
        rec = jnp.where(lane == col, val, rec)
    route_ref[...] = rec
    cnt_ref[...] = carry_ref[...]


def moe_route(x2d, mod, g, router_pad, *, tm, tiles_per_mod):
    m, d = x2d.shape
    n_mod = mod.shape[0]
    mod_idx = (lambda i: (i // tiles_per_mod, 0, 0)) if n_mod > 1 else (lambda i: (0, 0, 0))
    tri = jnp.asarray(np.tril(np.ones((tm, tm), np.float32), -1), BF16)
    return pl.pallas_call(
        _route_kernel,
        grid=(m // tm,),
        in_specs=[
            pl.BlockSpec((tm, d), lambda i: (i, 0)),
            pl.BlockSpec((1, 6, d), mod_idx),
            pl.BlockSpec((1, d), lambda i: (0, 0)),
            pl.BlockSpec((d, 128), lambda i: (0, 0)),
            pl.BlockSpec((tm, tm), lambda i: (0, 0)),
        ],
        out_specs=[pl.BlockSpec((tm, 128), lambda i: (i, 0)), pl.BlockSpec((1, 128), lambda i: (0, 0))],
        out_shape=[jax.ShapeDtypeStruct((m, 128), F32), jax.ShapeDtypeStruct((1, 128), F32)],
        scratch_shapes=[pltpu.VMEM((1, 128), F32)],
        compiler_params=_cparams(1),
        name="moe_route",
    )(x2d, mod, g.reshape(1, d), router_pad, tri)


def _dispatch_kernel(pos_ref, x_ref, mod_ref, g_ref, xs_in_ref, xs_ref, h_ref, sem):
    del xs_in_ref
    t = x_ref.shape[0]
    h_ref[...] = _norm_mod(x_ref[...], g_ref[...], mod_ref[0, 3:4, :], mod_ref[0, 4:5, :])

    def row_copy(r, dst_row):
        return pltpu.make_async_copy(h_ref.at[pl.ds(r, 1), :], xs_ref.at[pl.ds(dst_row, 1), :], sem)

    def issue(r, carry):
        row_copy(r, pos_ref[0, 0, r]).start()
        row_copy(r, pos_ref[0, 0, t + r]).start()
        return carry

    lax.fori_loop(0, t, issue, 0)

    def drain(r, carry):
        row_copy(r, 0).wait()
        row_copy(r, 0).wait()
        return carry

    lax.fori_loop(0, t, drain, 0)


def moe_dispatch(x2d, mod, g, pos_tiles, n_rows, *, tm, tiles_per_mod):
    m, d = x2d.shape
    n_mod = mod.shape[0]
    mod_idx = (lambda i: (i // tiles_per_mod, 0, 0)) if n_mod > 1 else (lambda i: (0, 0, 0))
    return pl.pallas_call(
        _dispatch_kernel,
        grid=(m // tm,),
        in_specs=[
            pl.BlockSpec((1, 1, 2 * tm), lambda i: (i, 0, 0), memory_space=pltpu.SMEM),
            pl.BlockSpec((tm, d), lambda i: (i, 0)),
            pl.BlockSpec((1, 6, d), mod_idx),
            pl.BlockSpec((1, d), lambda i: (0, 0)),
            pl.BlockSpec(memory_space=pl.ANY),
        ],
        out_specs=pl.BlockSpec(memory_space=pl.ANY),
        out_shape=jax.ShapeDtypeStruct((n_rows, d), F32),
        scratch_shapes=[pltpu.VMEM((tm, d), F32), pltpu.SemaphoreType.DMA(())],
        input_output_aliases={4: 0},
        compiler_params=_cparams(1),
        name="moe_dispatch",
    )(pos_tiles, x2d, mod, g.reshape(1, d), jnp.zeros((n_rows, d), F32))


def _experts_kernel(te_ref, nv_ref, xs_ref, w1_ref, w3_ref, w2_ref, ys_ref, xb_ref, acc_ref):
    i = pl.program_id(0)
    j = pl.program_id(1)
    valid = i < nv_ref[0]

    @pl.when(jnp.logical_and(valid, j == 0))
    def _():
        xb_ref[...] = xs_ref[...].astype(BF16)
        acc_ref[...] = jnp.zeros(acc_ref.shape, F32)

    @pl.when(valid)
    def _():
        h = xb_ref[...]
        a = jnp.dot(h, w1_ref[0], preferred_element_type=F32)
        b = jnp.dot(h, w3_ref[0], preferred_element_type=F32)
        acc_ref[...] += jnp.dot((_silu(a) * b).astype(BF16), w2_ref[0], preferred_element_type=F32)

    @pl.when(jnp.logical_and(valid, j == pl.num_programs(1) - 1))
    def _():
        ys_ref[...] = acc_ref[...]

    @pl.when(jnp.logical_and(jnp.logical_not(valid), j == pl.num_programs(1) - 1))
    def _():
        ys_ref[...] = jnp.zeros(ys_ref.shape, F32)


def moe_experts_grouped(xs, tile_expert, n_valid, w1, w3, w2, *, tm, tf):
    n_rows, d = xs.shape
    dff = w1.shape[2]
    nf = dff // tf

    def w13_idx(i, j, te, nv):
        return (te[i], 0, jnp.where(i < nv[0], j, nf - 1))

    def w2_idx(i, j, te, nv):
        return (te[i], jnp.where(i < nv[0], j, nf - 1), 0)

    grid_spec = pltpu.PrefetchScalarGridSpec(
        num_scalar_prefetch=2,
        grid=(n_rows // tm, nf),
        in_specs=[
            pl.BlockSpec((tm, d), lambda i, j, te, nv: (jnp.minimum(i, nv[0] - 1), 0)),
            pl.BlockSpec((1, d, tf), w13_idx),
            pl.BlockSpec((1, d, tf), w13_idx),
            pl.BlockSpec((1, tf, d), w2_idx),
        ],
        out_specs=pl.BlockSpec((tm, d), lambda i, j, te, nv: (i, 0)),
        scratch_shapes=[pltpu.VMEM((tm, d), BF16), pltpu.VMEM((tm, d), F32)],
    )
    return pl.pallas_call(
        _experts_kernel,
        grid_spec=grid_spec,
        out_shape=jax.ShapeDtypeStruct((n_rows, d), F32),
        compiler_params=_cparams(2),
        name="moe_experts_grouped",
    )(tile_expert, n_valid, xs, w1, w3, w2)


def _combine_kernel(pos_ref, x_ref, mod_ref, rt_ref, ys_ref, o_ref, y1_ref, y2_ref, sem):
    t = x_ref.shape[0]

    def row_copy(src_row, dst_ref, r):
        return pltpu.make_async_copy(ys_ref.at[pl.ds(src_row, 1), :], dst_ref.at[pl.ds(r, 1), :], sem)

    def issue(r, carry):
        row_copy(pos_ref[0, 0, r], y1_ref, r).start()
        row_copy(pos_ref[0, 0, t + r], y2_ref, r).start()
        return carry

    lax.fori_loop(0, t, issue, 0)

    def drain(r, carry):
        row_copy(0, y1_ref, r).wait()
        row_copy(0, y2_ref, r).wait()
        return carry

    lax.fori_loop(0, t, drain, 0)
    rt = rt_ref[...]
    mix = rt[:, R_W1:R_W1 + 1] * y1_ref[...] + rt[:, R_W2:R_W2 + 1] * y2_ref[...]
    o_ref[...] = x_ref[...] + mod_ref[0, 5:6, :] * mix


def moe_combine(x2d, mod, route, pos_tiles, ys, *, tm, tiles_per_mod):
    m, d = x2d.shape
    n_mod = mod.shape[0]
    mod_idx = (lambda i: (i // tiles_per_mod, 0, 0)) if n_mod > 1 else (lambda i: (0, 0, 0))
    return pl.pallas_call(
        _combine_kernel,
        grid=(m // tm,),
        in_specs=[
            pl.BlockSpec((1, 1, 2 * tm), lambda i: (i, 0, 0), memory_space=pltpu.SMEM),
            pl.BlockSpec((tm, d), lambda i: (i, 0)),
            pl.BlockSpec((1, 6, d), mod_idx),
            pl.BlockSpec((tm, 128), lambda i: (i, 0)),
            pl.BlockSpec(memory_space=pl.ANY),
        ],
        out_specs=pl.BlockSpec((tm, d), lambda i: (i, 0)),
        out_shape=jax.ShapeDtypeStruct((m, d), F32),
        scratch_shapes=[pltpu.VMEM((tm, d), F32), pltpu.VMEM((tm, d), F32), pltpu.SemaphoreType.DMA(())],
        compiler_params=_cparams(1),
        name="moe_combine",
    )(pos_tiles, x2d, mod, route, ys)


def _pos_tiles(pos1, pos2, tm):
    n = pos1.shape[0] // tm
    return jnp.concatenate([pos1.reshape(n, 1, tm), pos2.reshape(n, 1, tm)], axis=2)


def moe_sparse(x2d, mod, g, router_pad, w1, w3, w2, *, rows_per_mod):
    m, d = x2d.shape
    tr, td, tc, te = MOE_ROUTE_TM, MOE_DISPATCH_TM, MOE_COMBINE_TM, MOE_EXPERT_TM
    tiles_per_seq_row = rows_per_mod
    route, cnt = moe_route(x2d, mod, g, router_pad, tm=tr, tiles_per_mod=tiles_per_seq_row // tr)
    counts = cnt[0, 0:N_EXPERTS].astype(jnp.int32)
    group = ((counts + te - 1) // te) * te
    ends = jnp.cumsum(group)
    starts = ends - group
    e1 = route[:, R_E1].astype(jnp.int32)
    e2 = route[:, R_E2].astype(jnp.int32)
    pos1 = starts[e1] + route[:, R_RANK1].astype(jnp.int32)
    pos2 = starts[e2] + route[:, R_RANK2].astype(jnp.int32)
    n_rows = 2 * m + N_EXPERTS * te
    n_tiles = n_rows // te
    tile_expert = jnp.minimum(
        jnp.searchsorted(ends, jnp.arange(n_tiles, dtype=jnp.int32) * te, side="right"), N_EXPERTS - 1
    ).astype(jnp.int32)
    n_valid = (ends[-1:] // te).astype(jnp.int32)
    xs = moe_dispatch(x2d, mod, g, _pos_tiles(pos1, pos2, td), n_rows, tm=td, tiles_per_mod=tiles_per_seq_row // td)
    ys = moe_experts_grouped(xs, tile_expert, n_valid, w1, w3, w2, tm=te, tf=MOE_EXPERT_TF)
    return moe_combine(x2d, mod, route, _pos_tiles(pos1, pos2, tc), ys, tm=tc, tiles_per_mod=tiles_per_seq_row // tc)


def _permute_w_in(w):
    f, c, p, q, kv, gts = w[:, 0:256], w[:, 256:768], w[:, 768:1024], w[:, 1024:1536], w[:, 1536:1792], w[:, 1792:]
    return jnp.concatenate([gts, q, c, p, f, kv], axis=1).astype(BF16)


def _layer_weights(layer, w_br_fourier, conv_dw, conv_b, conv_norm_g, w_br_conv, pool_w, pool_scale, w_br_pool,
                   w_br_attn, w_out):
    pw = jax.scipy.linalg.block_diag(*[pool_w[layer, i] for i in range(len(POOL_WINDOWS))])
    return {
        "wf": w_br_fourier[layer].astype(BF16), "wc": w_br_conv[layer].astype(BF16),
        "wp": w_br_pool[layer].astype(BF16), "wa": w_br_attn[layer].astype(BF16), "wo": w_out[layer].astype(BF16),
        "dw": conv_dw[layer], "cb": conv_b[layer].reshape(1, CONV_W), "cg": conv_norm_g[layer].reshape(1, CONV_W),
        "pw": pw.astype(BF16), "ps": pool_scale[layer].reshape(1, POOL_W),
    }


def kernel(x, c, ctx, c_ctx, w_mod, b_mod, norm1_g, norm2_g, w_in, w_br_fourier, conv_dw, conv_b, conv_norm_g,
           w_br_conv, pool_w, pool_scale, w_br_pool, q_norm_g, k_norm_g, w_br_attn, w_out, ffn_w1, ffn_w3, ffn_w2,
           moe_router, moe_w1, moe_w3, moe_w2):
    batch, seq, d = x.shape
    ctx_len = ctx.shape[1]
    depth = w_in.shape[0]
    rope = rope_tables(seq)

    c_rows = jnp.zeros((8, d), F32).at[0:batch].set(c).at[batch].set(c_ctx)
    mods = modulation_all(c_rows, w_mod, b_mod).reshape(depth, 8, 6, d)

    xl = x.reshape(batch * seq, d)
    xc = ctx.reshape(batch * ctx_len, d)
    for layer in range(depth):
        is_last = layer == depth - 1
        mod_l = mods[layer, 0:batch]
        mod_c = mods[layer, batch:batch + 1]
        w_in_l = _permute_w_in(w_in[layer])
        lw = _layer_weights(layer, w_br_fourier, conv_dw, conv_b, conv_norm_g, w_br_conv, pool_w, pool_scale,
                            w_br_pool, w_br_attn, w_out)

        proj_c = input_projection(xc, mod_c, norm1_g[layer], w_in_l, tm=256, tiles_per_mod=1)
        qc, ktc, vc = qkv_prepare(proj_c, q_norm_g[layer], k_norm_g[layer], None, batch=batch, seq=ctx_len, tp=256)

        proj = input_projection(xl, mod_l, norm1_g[layer], w_in_l, tm=512, tiles_per_mod=seq // 512)
        q, kt, v = qkv_prepare(proj, q_norm_g[layer], k_norm_g[layer], rope, batch=batch, seq=seq, tp=512)
        kt_all = jnp.concatenate([kt, ktc], axis=3)
        v_all = jnp.concatenate([v, vc], axis=2)
        attn = attention(q, kt_all, v_all, batch=batch, seq_q=seq, tq=256, tk=768)
        yf = fourier_mix(proj[:, P_OFF_F:P_OFF_F + FOURIER_W], batch=batch, seq=seq, n1=64, n2=seq // 64)
        xl = merge_branches(xl, mod_l, proj, yf, attn, lw, seq=seq, t=512)

        if not is_last:
            attn_c = attention(qc, ktc, vc, batch=batch, seq_q=ctx_len, tq=256, tk=ctx_len)
            yf_c = fourier_mix(proj_c[:, P_OFF_F:P_OFF_F + FOURIER_W], batch=batch, seq=ctx_len, n1=16,
                               n2=ctx_len // 16)
            xc = merge_branches(xc, mod_c, proj_c, yf_c, attn_c, lw, seq=ctx_len, t=256)

        j = layer // 2
        if layer % 2 == 0:
            w1, w3, w2 = ffn_w1[j].astype(BF16), ffn_w3[j].astype(BF16), ffn_w2[j].astype(BF16)
            xl = ffn_dense(xl, mod_l, norm2_g[layer], w1, w3, w2, tm=512, tf=1408, tiles_per_mod=seq // 512)
            if not is_last:
                xc = ffn_dense(xc, mod_c, norm2_g[layer], w1, w3, w2, tm=256, tf=1408, tiles_per_mod=1)
        else:
            w1, w3, w2 = moe_w1[j].astype(BF16), moe_w3[j].astype(BF16), moe_w2[j].astype(BF16)
            router_pad = jnp.zeros((d, 128), F32).at[:, 0:N_EXPERTS].set(moe_router[j])
            xl = moe_sparse(xl, mod_l, norm2_g[layer], router_pad, w1, w3, w2, rows_per_mod=seq)
            if not is_last:
                xc = moe_sparse(xc, mod_c, norm2_g[layer], router_pad, w1, w3, w2, rows_per_mod=batch * ctx_len)
    return xl.reshape(batch, seq, d)
```

```python
import functools
import math

import numpy as np
import jax
import jax.numpy as jnp
from jax import lax
from jax.experimental import pallas as pl
from jax.experimental.pallas import tpu as pltpu

F32 = jnp.float32
BF16 = jnp.bfloat16

D_MODEL = 1024
GRID_W = 64
EPS = 1e-6
FOURIER_GW = 64
FOURIER_W = 256
CONV_W = 256
CONV_K = 31
CONV_HALF = CONV_K // 2
POOL_WINDOWS = (2, 4, 8, 16)
POOL_GW = 64
POOL_W = 256
HEAD_DIM = 64
N_HEADS = 8
N_KV_HEADS = 2
Q_PER_KV = 4
Q_W = 512
KV_W = 128
ROPE_THETA = 10000.0
N_EXPERTS = 8
IN_W = 5888

P_OFF_G = 0
P_OFF_Q = 4096
P_OFF_CP = 4608
P_OFF_F = 5376
P_OFF_KV = 5632
CP_W = 2 * CONV_W + POOL_W

Q_SCALE = (HEAD_DIM ** -0.5) * math.log2(math.e)

MOE_ROUTE_TM = 512
MOE_DISPATCH_TM = 512
MOE_COMBINE_TM = 256
MOE_EXPERT_TM = 512
MOE_EXPERT_TF = 896

HALO = 16
VMEM_LIMIT = 56 * 1024 * 1024


def _cparams(n_axes):
    return pltpu.CompilerParams(dimension_semantics=("arbitrary",) * n_axes, vmem_limit_bytes=VMEM_LIMIT)


def _sigmoid(v):
    return 1.0 / (1.0 + jnp.exp(-v))


def _silu(v):
    return v * _sigmoid(v)


def _norm_mod(x, g, shift, scale):
    ms = jnp.mean(x * x, axis=-1, keepdims=True)
    return x * lax.rsqrt(ms + EPS) * g * (1.0 + scale) + shift


def _mod_kernel(c_ref, w_ref, b_ref, o_ref):
    s = _silu(c_ref[...])
    o_ref[0] = jnp.dot(s, w_ref[0], preferred_element_type=F32, precision=lax.Precision.HIGHEST) + b_ref[0]


def modulation_all(c_rows, w_mod, b_mod):
    n_layers, d, n = w_mod.shape
    tn = 1536
    return pl.pallas_call(
        _mod_kernel,
        grid=(n_layers, n // tn),
        in_specs=[
            pl.BlockSpec((8, d), lambda l, j: (0, 0)),
            pl.BlockSpec((1, d, tn), lambda l, j: (l, 0, j)),
            pl.BlockSpec((1, 1, tn), lambda l, j: (l, 0, j)),
        ],
        out_specs=pl.BlockSpec((1, 8, tn), lambda l, j: (l, 0, j)),
        out_shape=jax.ShapeDtypeStruct((n_layers, 8, n), F32),
        compiler_params=_cparams(2),
        name="modulation",
    )(c_rows, w_mod, b_mod.reshape(n_layers, 1, n))


def _inproj_kernel(x_ref, mod_ref, g_ref, w_ref, o_ref, *, chunks):
    h = _norm_mod(x_ref[...], g_ref[...], mod_ref[0, 0:1, :], mod_ref[0, 1:2, :]).astype(BF16)
    for c0, cw in chunks:
        o_ref[:, c0:c0 + cw] = jnp.dot(h, w_ref[:, c0:c0 + cw], preferred_element_type=F32).astype(o_ref.dtype)


def input_projection(x2d, mod, g, w_bf16, *, tm, tiles_per_mod):
    m, d = x2d.shape
    n = w_bf16.shape[1]
    chunks = tuple((c0, min(512, n - c0)) for c0 in range(0, n, 512))
    n_mod = mod.shape[0]
    mod_idx = (lambda i: (i // tiles_per_mod, 0, 0)) if n_mod > 1 else (lambda i: (0, 0, 0))
    return pl.pallas_call(
        functools.partial(_inproj_kernel, chunks=chunks),
        grid=(m // tm,),
        in_specs=[
            pl.BlockSpec((tm, d), lambda i: (i, 0)),
            pl.BlockSpec((1, 6, d), mod_idx),
            pl.BlockSpec((1, d), lambda i: (0, 0)),
            pl.BlockSpec((d, n), lambda i: (0, 0), pipeline_mode=pl.Buffered(1)),
        ],
        out_specs=pl.BlockSpec((tm, n), lambda i: (i, 0)),
        out_shape=jax.ShapeDtypeStruct((m, n), BF16),
        compiler_params=_cparams(1),
        name="input_projection",
    )(x2d, mod, g.reshape(1, d), w_bf16)


def _seg_sum64(v, ones_bd):
    hi = v.astype(BF16)
    lo = (v - hi.astype(F32)).astype(BF16)
    return (jnp.dot(hi, ones_bd, preferred_element_type=F32) + jnp.dot(lo, ones_bd, preferred_element_type=F32))


def _head_norm_rope(x, g, ones_bd, cos, sin, low_mask):
    y = x * lax.rsqrt(_seg_sum64(x * x, ones_bd) * (1.0 / HEAD_DIM) + EPS) * g
    if cos is None:
        return y
    partner = jnp.where(low_mask, pltpu.roll(y, 128 - 16, axis=1), pltpu.roll(y, 16, axis=1))
    return y * cos + partner * sin


def _prep_kernel(*refs, use_rope):
    if use_rope:
        q_ref, kv_ref, gq_ref, gk_ref, ones_ref, cos_ref, sin_ref, qo_ref, kt_ref, v_ref = refs
        cos, sin = cos_ref[...], sin_ref[...]
    else:
        q_ref, kv_ref, gq_ref, gk_ref, ones_ref, qo_ref, kt_ref, v_ref = refs
        cos = sin = None
    t = q_ref.shape[0]
    ones_bd = ones_ref[...]
    lane = lax.broadcasted_iota(jnp.int32, (t, 128), 1)
    low_mask = (lane % 32) < 16
    gq = gq_ref[...]
    for c in range(Q_W // 128):
        xq = q_ref[:, 128 * c:128 * (c + 1)].astype(F32)
        yq = _head_norm_rope(xq, gq, ones_bd, cos, sin, low_mask) * Q_SCALE
        qo_ref[0, c // 2, 128 * (c % 2):128 * (c % 2 + 1), :] = yq.T.astype(BF16)
    xk = kv_ref[:, 0:128].astype(F32)
    yk = _head_norm_rope(xk, gk_ref[...], ones_bd, cos, sin, low_mask)
    ykr = pltpu.roll(yk, 64, axis=1)
    first = lane < 64
    k0 = jnp.where(first, yk, ykr).astype(BF16)
    k1 = jnp.where(first, ykr, yk).astype(BF16)
    kt_ref[0, 0] = jnp.concatenate([k0, k0], axis=1)
    kt_ref[0, 1] = jnp.concatenate([k1, k1], axis=1)
    vt = kv_ref[:, 128:256].astype(F32).T
    ones = jnp.ones((HEAD_DIM, t), F32)
    for h in range(N_KV_HEADS):
        v_ref[0, h] = jnp.concatenate([vt[64 * h:64 * (h + 1), :], ones], axis=0).astype(BF16)


def qkv_prepare(proj, gq, gk, rope, *, batch, seq, tp):
    m = proj.shape[0]
    tps = seq // tp
    use_rope = rope is not None
    ones_bd = jnp.asarray(np.kron(np.eye(2, dtype=np.float32), np.ones((64, 64), np.float32)), BF16)
    gq2 = jnp.tile(gq, 2).reshape(1, 128)
    gk2 = jnp.tile(gk, 2).reshape(1, 128)
    in_specs = [
        pl.BlockSpec((tp, Q_W), lambda i: (i, P_OFF_Q // Q_W)),
        pl.BlockSpec((tp, 256), lambda i: (i, P_OFF_KV // 256)),
        pl.BlockSpec((1, 128), lambda i: (0, 0)),
        pl.BlockSpec((1, 128), lambda i: (0, 0)),
        pl.BlockSpec((128, 128), lambda i: (0, 0)),
    ]
    args = [proj, proj, gq2, gk2, ones_bd]
    if use_rope:
        in_specs += [pl.BlockSpec((tp, 128), lambda i: (i % tps, 0))] * 2
        args += list(rope)
    return pl.pallas_call(
        functools.partial(_prep_kernel, use_rope=use_rope),
        grid=(m // tp,),
        in_specs=in_specs,
        out_specs=[
            pl.BlockSpec((1, N_KV_HEADS, 256, tp), lambda i: (i // tps, 0, 0, i % tps)),
            pl.BlockSpec((1, N_KV_HEADS, tp, 256), lambda i: (i // tps, 0, i % tps, 0)),
            pl.BlockSpec((1, N_KV_HEADS, 128, tp), lambda i: (i // tps, 0, 0, i % tps)),
        ],
        out_shape=[
            jax.ShapeDtypeStruct((batch, N_KV_HEADS, 256, seq), BF16),
            jax.ShapeDtypeStruct((batch, N_KV_HEADS, seq, 256), BF16),
            jax.ShapeDtypeStruct((batch, N_KV_HEADS, 128, seq), BF16),
        ],
        compiler_params=_cparams(1),
        name="qkv_prepare",
    )(*args)


def rope_tables(seq):
    n_freq = HEAD_DIM // 4
    freqs = ROPE_THETA ** (-jnp.arange(n_freq, dtype=F32) / n_freq)
    t = jnp.arange(seq)
    row = (t // GRID_W).astype(F32)
    col = (t % GRID_W).astype(F32)
    ang_r = row[:, None] * freqs
    ang_c = col[:, None] * freqs
    cos = jnp.concatenate([jnp.cos(ang_r)] * 2 + [jnp.cos(ang_c)] * 2, axis=1)
    sin = jnp.concatenate([-jnp.sin(ang_r), jnp.sin(ang_r), -jnp.sin(ang_c), jnp.sin(ang_c)], axis=1)
    return jnp.tile(cos, (1, 2)), jnp.tile(sin, (1, 2))


def _attn_kernel(*refs, tq, tk, nk, tail):
    refs = list(refs)
    qt_ref = refs.pop(0)
    k_ref, vt_ref = (refs.pop(0), refs.pop(0)) if nk else (None, None)
    kc_ref, vtc_ref = (refs.pop(0), refs.pop(0)) if tail else (None, None)
    o_ref, qs_ref, s0, s1, p0, p1, a0, a1, m_ref, acc_ref = refs
    s_bufs, p_bufs, a_bufs = (s0, s1), (p0, p1), (a0, a1)
    n_blocks = nk + (1 if tail else 0)

    row_group = lax.broadcasted_iota(jnp.int32, (256, tq), 0) // HEAD_DIM
    qt = qt_ref[0, 0]
    for g in range(Q_PER_KV):
        qs_ref[:, g * tq:(g + 1) * tq] = jnp.where(row_group == g, qt, jnp.zeros_like(qt))
    m_ref[...] = jnp.full(m_ref.shape, -jnp.inf, F32)
    acc_ref[...] = jnp.zeros(acc_ref.shape, F32)

    def block(t):
        if isinstance(t, int) and t >= nk:
            return kc_ref[0, 0], vtc_ref[0, 0], tail
        off = t * tk if isinstance(t, int) else pl.multiple_of(t * tk, tk)
        return k_ref[0, 0, pl.ds(off, tk), :], vt_ref[0, 0, :, pl.ds(off, tk)], tk

    def scores(t, slot):
        k_rows, _, n = block(t)
        s_bufs[slot][0:n, :] = jnp.dot(k_rows, qs_ref[...], preferred_element_type=F32)

    def numerators(n, slot):
        s_ref, p_ref, a_ref = s_bufs[slot], p_bufs[slot], a_bufs[slot]
        for c0 in range(0, Q_PER_KV * tq, 128):
            cols = slice(c0, c0 + 128)
            m_old = m_ref[:, cols]
            m_new = jnp.maximum(m_old, jnp.max(s_ref[0:n, cols], axis=0, keepdims=True))
            a_ref[:, cols] = jnp.exp2(m_old - m_new)
            p_ref[0:n, cols] = jnp.exp2(s_ref[0:n, cols] - m_new).astype(BF16)
            m_ref[:, cols] = m_new

    def weighted_sum(t, slot):
        _, vt, n = block(t)
        pv = jnp.dot(vt, p_bufs[slot][0:n, :], preferred_element_type=F32)
        acc_ref[...] = a_bufs[slot][...] * acc_ref[...] + pv

    def rows_of(t):
        return tk if t < nk else tail

    def step(t, slot, n_mid):
        scores(t, slot)
        numerators(n_mid, 1 - slot)
        weighted_sum(t - 2, slot)

    scores(0, 0)
    if n_blocks > 1:
        scores(1, 1)
        numerators(rows_of(0), 0)
        n_pairs = max(nk - 2, 0) // 2

        def pair(i, carry):
            t = 2 + 2 * i
            step(t, 0, tk)
            step(t + 1, 1, tk)
            return carry

        if n_pairs:
            lax.fori_loop(0, n_pairs, pair, 0)
        for t in range(2 + 2 * n_pairs, n_blocks):
            step(t, t % 2, rows_of(t - 1))
        last = n_blocks - 1
        numerators(rows_of(last), last % 2)
        weighted_sum(last - 1, (last - 1) % 2)
        weighted_sum(last, last % 2)
    else:
        numerators(rows_of(0), 0)
        weighted_sum(0, 0)

    acc = acc_ref[...]
    ot = acc[0:HEAD_DIM, :] / acc[HEAD_DIM:2 * HEAD_DIM, :]
    for half in range(2):
        pair_t = jnp.concatenate([ot[:, (2 * half) * tq:(2 * half + 1) * tq],
                                  ot[:, (2 * half + 1) * tq:(2 * half + 2) * tq]], axis=0)
        o_ref[:, 128 * half:128 * (half + 1)] = pair_t.T.astype(o_ref.dtype)


def attention(qt, k4, vt1, k4_tail, vt1_tail, *, batch, seq_q, tq, tk):
    nq = seq_q // tq
    lanes = Q_PER_KV * tq
    nk = 0 if k4 is None else k4.shape[2] // tk
    tail = 0 if k4_tail is None else k4_tail.shape[2]
    buf_rows = max(tk if nk else 0, tail)
    in_specs = [pl.BlockSpec((1, 1, 256, tq), lambda b, h, i: (b, h, 0, i))]
    args = [qt]
    if nk:
        lk = k4.shape[2]
        in_specs += [pl.BlockSpec((1, 1, lk, 256), lambda b, h, i: (b, h, 0, 0)),
                     pl.BlockSpec((1, 1, 128, lk), lambda b, h, i: (b, h, 0, 0))]
        args += [k4, vt1]
    if tail:
        in_specs += [pl.BlockSpec((1, 1, tail, 256), lambda b, h, i: (b, h, 0, 0)),
                     pl.BlockSpec((1, 1, 128, tail), lambda b, h, i: (b, h, 0, 0))]
        args += [k4_tail, vt1_tail]
    return pl.pallas_call(
        functools.partial(_attn_kernel, tq=tq, tk=tk, nk=nk, tail=tail),
        grid=(batch, N_KV_HEADS, nq),
        in_specs=in_specs,
        out_specs=pl.BlockSpec((tq, 256), lambda b, h, i: (b * nq + i, h)),
        out_shape=jax.ShapeDtypeStruct((batch * seq_q, Q_W), BF16),
        scratch_shapes=[
            pltpu.VMEM((256, lanes), BF16),
            pltpu.VMEM((buf_rows, lanes), F32), pltpu.VMEM((buf_rows, lanes), F32),
            pltpu.VMEM((buf_rows, lanes), BF16), pltpu.VMEM((buf_rows, lanes), BF16),
            pltpu.VMEM((1, lanes), F32), pltpu.VMEM((1, lanes), F32),
            pltpu.VMEM((1, lanes), F32),
            pltpu.VMEM((2 * HEAD_DIM, lanes), F32),
        ],
        compiler_params=_cparams(3),
        name="attention",
    )(*args)


def _dft_cs(n):
    k = np.arange(n)
    ang = 2.0 * np.pi * ((k[:, None] * k[None, :]) % n) / n
    return np.cos(ang), np.sin(ang)


def _fft1_kernel(x_ref, f_ref, c_ref, s_ref, o_ref, *, n1):
    y = jnp.dot(f_ref[...], x_ref[0], preferred_element_type=F32)
    yr, yi = y[:n1], y[n1:]
    c, s = c_ref[...], s_ref[...]
    o_ref[0, 0] = (yr * c + yi * s).astype(o_ref.dtype)
    o_ref[0, 1] = (yi * c - yr * s).astype(o_ref.dtype)


def _fft2_kernel(y_ref, f_ref, bc_ref, bs_ref, o_ref, *, n2, kb):
    for j in range(kb):
        y2 = jnp.concatenate([y_ref[0, 0, j], y_ref[0, 1, j]], axis=0)
        x2 = jnp.dot(f_ref[...], y2, preferred_element_type=F32)
        xr = x2[:n2].astype(BF16)
        xi = x2[n2:].astype(BF16)
        z = (jnp.dot(xr, bc_ref[...], preferred_element_type=F32) + jnp.dot(xi, bs_ref[...], preferred_element_type=F32))
        o_ref[0, j] = z.astype(o_ref.dtype)


def fourier_mix(u, *, batch, seq, n1, n2):
    cw = u.shape[1]
    lanes = n2 * cw
    tl = min(lanes, 4096)
    c1, s1 = _dft_cs(n1)
    f1 = jnp.asarray(np.concatenate([c1, -s1], axis=0), BF16)
    k1 = np.arange(n1)[:, None]
    t2 = np.arange(n2)[None, :]
    ang = 2.0 * np.pi * ((k1 * t2) % seq) / seq
    twc = jnp.asarray(np.repeat(np.cos(ang), cw, axis=1), F32)
    tws = jnp.asarray(np.repeat(np.sin(ang), cw, axis=1), F32)
    x2 = u.reshape(batch, n1, lanes)
    yp = pl.pallas_call(
        functools.partial(_fft1_kernel, n1=n1),
        grid=(batch, lanes // tl),
        in_specs=[
            pl.BlockSpec((1, n1, tl), lambda b, j: (b, 0, j)),
            pl.BlockSpec((2 * n1, n1), lambda b, j: (0, 0)),
            pl.BlockSpec((n1, tl), lambda b, j: (0, j)),
            pl.BlockSpec((n1, tl), lambda b, j: (0, j)),
        ],
        out_specs=pl.BlockSpec((1, 2, n1, tl), lambda b, j: (b, 0, 0, j)),
        out_shape=jax.ShapeDtypeStruct((batch, 2, n1, lanes), BF16),
        compiler_params=_cparams(2),
        name="fft_stage1",
    )(x2, f1, twc, tws)

    c2, s2 = _dft_cs(n2)
    f2 = jnp.asarray(np.block([[c2, s2], [-s2, c2]]), BF16)
    cg, sg = _dft_cs(FOURIER_GW)
    norm = 1.0 / math.sqrt(seq * FOURIER_GW)
    bdc = jnp.asarray(np.kron(np.eye(cw // FOURIER_GW), cg) * norm, BF16)
    bds = jnp.asarray(np.kron(np.eye(cw // FOURIER_GW), sg) * norm, BF16)
    kb = min(n1, 16)
    y5 = yp.reshape(batch, 2, n1, n2, cw)
    z = pl.pallas_call(
        functools.partial(_fft2_kernel, n2=n2, kb=kb),
        grid=(batch, n1 // kb),
        in_specs=[
            pl.BlockSpec((1, 2, kb, n2, cw), lambda b, j: (b, 0, j, 0, 0)),
            pl.BlockSpec((2 * n2, 2 * n2), lambda b, j: (0, 0)),
            pl.BlockSpec((cw, cw), lambda b, j: (0, 0)),
            pl.BlockSpec((cw, cw), lambda b, j: (0, 0)),
        ],
        out_specs=pl.BlockSpec((1, kb, n2, cw), lambda b, j: (b, j, 0, 0)),
        out_shape=jax.ShapeDtypeStruct((batch, n1, n2, cw), BF16),
        compiler_params=_cparams(2),
        name="fft_stage2",
    )(y5, f2, bdc, bds)
    return z.transpose(0, 2, 1, 3).reshape(batch * seq, cw)


def _merge_kernel(x_ref, mod_ref, gate_ref, cp_ref, cpp_ref, cpn_ref, yf_ref, at_ref,
                  wf_ref, wc_ref, wp_ref, wa_ref, wo_ref, dw_ref, cb_ref, cg_ref, pw_ref, ps_ref,
                  o_ref, ybuf, xbuf, *, t, tps, seq):
    i = pl.program_id(0)
    pos_tile = i % tps
    keep_prev = jnp.where(pos_tile != 0, 1.0, 0.0).astype(F32)
    keep_next = jnp.where(pos_tile != tps - 1, 1.0, 0.0).astype(F32)

    def glu(blk):
        return blk[:, 0:CONV_W].astype(F32) * _sigmoid(blk[:, CONV_W:2 * CONV_W].astype(F32))

    cp, cpp, cpn = cp_ref[...], cpp_ref[...], cpn_ref[...]
    ybuf[0:HALO, :] = glu(cpp) * keep_prev
    ybuf[HALO:HALO + t, :] = glu(cp)
    ybuf[HALO + t:HALO + t + HALO, :] = glu(cpn) * keep_next
    xbuf[0:HALO, :] = cpp[:, 2 * CONV_W:].astype(F32) * keep_prev
    xbuf[HALO:HALO + t, :] = cp[:, 2 * CONV_W:].astype(F32)
    xbuf[HALO + t:HALO + t + HALO, :] = cpn[:, 2 * CONV_W:].astype(F32) * keep_next

    acc = jnp.zeros((t, CONV_W), F32)
    for k in range(CONV_K):
        acc = acc + dw_ref[k:k + 1, :] * ybuf[pl.ds(HALO - CONV_HALF + k, t), :]
    acc = acc + cb_ref[...]
    ms = jnp.mean(acc * acc, axis=-1, keepdims=True)
    conv_out = _silu(acc * lax.rsqrt(ms + EPS) * cg_ref[...]).astype(BF16)

    def xs(d):
        return xbuf[pl.ds(HALO + d, t), :]

    x0 = xs(0)
    s2 = xs(-1) + x0
    s4 = s2 + xs(-2) + xs(1)
    s8 = s4 + xs(-4) + xs(-3) + xs(2) + xs(3)
    s16 = s8 + xs(-8) + xs(-7) + xs(-6) + xs(-5) + xs(4) + xs(5) + xs(6) + xs(7)
    grp = lax.broadcasted_iota(jnp.int32, (t, POOL_W), 1) // POOL_GW
    pos = pos_tile * t + lax.broadcasted_iota(jnp.int32, (t, POOL_W), 0)
    half = jnp.where(grp == 0, 1, jnp.where(grp == 1, 2, jnp.where(grp == 2, 4, 8)))
    cnt = (jnp.minimum(pos + half, seq) - jnp.maximum(pos - half, 0)).astype(F32)
    wsum = jnp.where(grp == 0, s2, jnp.where(grp == 1, s4, jnp.where(grp == 2, s8, s16)))
    pool_in = (wsum / cnt - x0).astype(BF16)
    pool_out = (jnp.dot(pool_in, pw_ref[...], preferred_element_type=F32) * ps_ref[...]).astype(BF16)

    def gate(b):
        return _sigmoid(gate_ref[:, b * D_MODEL:(b + 1) * D_MODEL].astype(F32))

    merged = gate(0) * jnp.dot(yf_ref[...], wf_ref[...], preferred_element_type=F32)
    merged = merged + gate(1) * jnp.dot(conv_out, wc_ref[...], preferred_element_type=F32)
    merged = merged + gate(2) * jnp.dot(pool_out, wp_ref[...], preferred_element_type=F32)
    merged = merged + gate(3) * jnp.dot(at_ref[...], wa_ref[...], preferred_element_type=F32)
    out = jnp.dot(merged.astype(BF16), wo_ref[...], preferred_element_type=F32)
    o_ref[...] = x_ref[...] + mod_ref[0, 2:3, :] * out


def merge_branches(x2d, mod, proj, yf, attn, lw, *, seq, t):
    m, d = x2d.shape
    tps = seq // t
    hb = t // HALO
    n_halo = m // HALO
    n_mod = mod.shape[0]
    mod_idx = (lambda i: (i // tps, 0, 0)) if n_mod > 1 else (lambda i: (0, 0, 0))
    const = lambda i: (0, 0)
    cp_blk = P_OFF_CP // CP_W
    return pl.pallas_call(
        functools.partial(_merge_kernel, t=t, tps=tps, seq=seq),
        grid=(m // t,),
        in_specs=[
            pl.BlockSpec((t, d), lambda i: (i, 0)),
            pl.BlockSpec((1, 6, d), mod_idx),
            pl.BlockSpec((t, 4 * d), lambda i: (i, 0)),
            pl.BlockSpec((t, CP_W), lambda i: (i, cp_blk)),
            pl.BlockSpec((HALO, CP_W), lambda i: (jnp.maximum(i * hb - 1, 0), cp_blk)),
            pl.BlockSpec((HALO, CP_W), lambda i: (jnp.minimum((i + 1) * hb, n_halo - 1), cp_blk)),
            pl.BlockSpec((t, FOURIER_W), lambda i: (i, 0)),
            pl.BlockSpec((t, Q_W), lambda i: (i, 0)),
            pl.BlockSpec((FOURIER_W, d), const),
            pl.BlockSpec((CONV_W, d), const),
            pl.BlockSpec((POOL_W, d), const),
            pl.BlockSpec((Q_W, d), const),
            pl.BlockSpec((d, d), const),
            pl.BlockSpec((CONV_K, CONV_W), const),
            pl.BlockSpec((1, CONV_W), const),
            pl.BlockSpec((1, CONV_W), const),
            pl.BlockSpec((POOL_W, POOL_W), const),
            pl.BlockSpec((1, POOL_W), const),
        ],
        out_specs=pl.BlockSpec((t, d), lambda i: (i, 0)),
        out_shape=jax.ShapeDtypeStruct((m, d), F32),
        scratch_shapes=[pltpu.VMEM((t + 2 * HALO, CONV_W), F32), pltpu.VMEM((t + 2 * HALO, POOL_W), F32)],
        compiler_params=_cparams(1),
        name="merge_branches",
    )(x2d, mod, proj, proj, proj, proj, yf, attn,
      lw["wf"], lw["wc"], lw["wp"], lw["wa"], lw["wo"], lw["dw"], lw["cb"], lw["cg"], lw["pw"], lw["ps"])


def _ffn_kernel(x_ref, mod_ref, g_ref, w1_ref, w3_ref, w2_ref, o_ref, h_ref, acc_ref):
    j = pl.program_id(1)

    @pl.when(j == 0)
    def _():
        h_ref[...] = _norm_mod(x_ref[...], g_ref[...], mod_ref[0, 3:4, :], mod_ref[0, 4:5, :]).astype(BF16)
        acc_ref[...] = jnp.zeros(acc_ref.shape, F32)

    h = h_ref[...]
    a = jnp.dot(h, w1_ref[...], preferred_element_type=F32)
    b = jnp.dot(h, w3_ref[...], preferred_element_type=F32)
    acc_ref[...] += jnp.dot((_silu(a) * b).astype(BF16), w2_ref[...], preferred_element_type=F32)

    @pl.when(j == pl.num_programs(1) - 1)
    def _():
        o_ref[...] = x_ref[...] + mod_ref[0, 5:6, :] * acc_ref[...]


def ffn_dense(x2d, mod, g, w1, w3, w2, *, tm, tf, tiles_per_mod):
    m, d = x2d.shape
    dff = w1.shape[1]
    n_mod = mod.shape[0]
    mod_idx = (lambda i, j: (i // tiles_per_mod, 0, 0)) if n_mod > 1 else (lambda i, j: (0, 0, 0))
    return pl.pallas_call(
        _ffn_kernel,
        grid=(m // tm, dff // tf),
        in_specs=[
            pl.BlockSpec((tm, d), lambda i, j: (i, 0)),
            pl.BlockSpec((1, 6, d), mod_idx),
            pl.BlockSpec((1, d), lambda i, j: (0, 0)),
            pl.BlockSpec((d, tf), lambda i, j: (0, j)),
            pl.BlockSpec((d, tf), lambda i, j: (0, j)),
            pl.BlockSpec((tf, d), lambda i, j: (j, 0)),
        ],
        out_specs=pl.BlockSpec((tm, d), lambda i, j: (i, 0)),
        out_shape=jax.ShapeDtypeStruct((m, d), F32),
        scratch_shapes=[pltpu.VMEM((tm, d), BF16), pltpu.VMEM((tm, d), F32)],
        compiler_params=_cparams(2),
        name="ffn_dense",
    )(x2d, mod, g.reshape(1, d), w1, w3, w2)


def _top2(logits):
    t = logits.shape[0]
    lane = lax.broadcasted_iota(jnp.int32, (t, 128), 1).astype(F32)
    neg = jnp.float32(-jnp.inf)
    lg = jnp.where(lane < N_EXPERTS, logits, neg)
    v1 = jnp.max(lg, axis=-1, keepdims=True)
    i1 = jnp.min(jnp.where(lg == v1, lane, 128.0), axis=-1, keepdims=True)
    lg2 = jnp.where(lane == i1, neg, lg)
    v2 = jnp.max(lg2, axis=-1, keepdims=True)
    i2 = jnp.min(jnp.where(lg2 == v2, lane, 128.0), axis=-1, keepdims=True)
    e2 = jnp.exp(v2 - v1)
    return i1, i2, 1.0 / (1.0 + e2), e2 / (1.0 + e2)


R_E1, R_E2, R_W1, R_W2, R_RANK1, R_RANK2 = range(6)


def _route_kernel(x_ref, mod_ref, g_ref, r_ref, tri_ref, route_ref, cnt_ref, carry_ref):
    @pl.when(pl.program_id(0) == 0)
    def _():
        carry_ref[...] = jnp.zeros(carry_ref.shape, F32)

    t = x_ref.shape[0]
    h = _norm_mod(x_ref[...], g_ref[...], mod_ref[0, 3:4, :], mod_ref[0, 4:5, :])
    logits = jnp.dot(h, r_ref[...], preferred_element_type=F32, precision=lax.Precision.HIGHEST)
    i1, i2, w1, w2 = _top2(logits)
    lane = lax.broadcasted_iota(jnp.int32, (t, 128), 1).astype(F32)
    oh1 = jnp.where(lane == i1, 1.0, 0.0)
    oh2 = jnp.where(lane == i2, 1.0, 0.0)
    both = oh1 + oh2
    before = carry_ref[...] + jnp.dot(tri_ref[...], both.astype(BF16), preferred_element_type=F32)
    rank1 = jnp.sum(oh1 * before, axis=-1, keepdims=True)
    rank2 = jnp.sum(oh2 * before, axis=-1, keepdims=True)
    carry_ref[...] += jnp.sum(both, axis=0, keepdims=True)
    rec = jnp.zeros((t, 128), F32)
    for col, val in ((R_E1, i1), (R_E2, i2), (R_W1, w1), (R_W2, w2), (R_RANK1, rank1), (R_RANK2, rank2)):
        rec = jnp.where(lane == col, val, rec)
    route_ref[...] = rec
    cnt_ref[...] = carry_ref[...]


def moe_route(x2d, mod, g, router_pad, *, tm, tiles_per_mod):
    m, d = x2d.shape
    n_mod = mod.shape[0]
    mod_idx = (lambda i: (i // tiles_per_mod, 0, 0)) if n_mod > 1 else (lambda i: (0, 0, 0))
    tri = jnp.asarray(np.tril(np.ones((tm, tm), np.float32), -1), BF16)
    return pl.pallas_call(
        _route_kernel,
        grid=(m // tm,),
        in_specs=[
            pl.BlockSpec((tm, d), lambda i: (i, 0)),
            pl.BlockSpec((1, 6, d), mod_idx),
            pl.BlockSpec((1, d), lambda i: (0, 0)),
            pl.BlockSpec((d, 128), lambda i: (0, 0)),
            pl.BlockSpec((tm, tm), lambda i: (0, 0)),
        ],
        out_specs=[pl.BlockSpec((tm, 128), lambda i: (i, 0)), pl.BlockSpec((1, 128), lambda i: (0, 0))],
        out_shape=[jax.ShapeDtypeStruct((m, 128), F32), jax.ShapeDtypeStruct((1, 128), F32)],
        scratch_shapes=[pltpu.VMEM((1, 128), F32)],
        compiler_params=_cparams(1),
        name="moe_route",
    )(x2d, mod, g.reshape(1, d), router_pad, tri)


def _dispatch_kernel(pos_ref, x_ref, mod_ref, g_ref, xs_in_ref, xs_ref, h_ref, sem):
    del xs_in_ref
    t = x_ref.shape[0]
    h_ref[...] = _norm_mod(x_ref[...], g_ref[...], mod_ref[0, 3:4, :], mod_ref[0, 4:5, :])

    def row_copy(r, dst_row):
        return pltpu.make_async_copy(h_ref.at[pl.ds(r, 1), :], xs_ref.at[pl.ds(dst_row, 1), :], sem)

    def issue(r, carry):
        row_copy(r, pos_ref[0, 0, r]).start()
        row_copy(r, pos_ref[0, 0, t + r]).start()
        return carry

    lax.fori_loop(0, t, issue, 0)
    for _ in range(2):
        pltpu.make_async_copy(h_ref, xs_ref.at[pl.ds(0, t), :], sem).wait()


def moe_dispatch(x2d, mod, g, pos_tiles, n_rows, *, tm, tiles_per_mod):
    m, d = x2d.shape
    n_mod = mod.shape[0]
    mod_idx = (lambda i: (i // tiles_per_mod, 0, 0)) if n_mod > 1 else (lambda i: (0, 0, 0))
    return pl.pallas_call(
        _dispatch_kernel,
        grid=(m // tm,),
        in_specs=[
            pl.BlockSpec((1, 1, 2 * tm), lambda i: (i, 0, 0), memory_space=pltpu.SMEM),
            pl.BlockSpec((tm, d), lambda i: (i, 0)),
            pl.BlockSpec((1, 6, d), mod_idx),
            pl.BlockSpec((1, d), lambda i: (0, 0)),
            pl.BlockSpec(memory_space=pl.ANY),
        ],
        out_specs=pl.BlockSpec(memory_space=pl.ANY),
        out_shape=jax.ShapeDtypeStruct((n_rows, d), F32),
        scratch_shapes=[pltpu.VMEM((tm, d), F32), pltpu.SemaphoreType.DMA(())],
        input_output_aliases={4: 0},
        compiler_params=_cparams(1),
        name="moe_dispatch",
    )(pos_tiles, x2d, mod, g.reshape(1, d), jnp.zeros((n_rows, d), F32))


def _experts_kernel(te_ref, nv_ref, xs_ref, w1_ref, w3_ref, w2_ref, ys_ref, xb_ref, acc_ref):
    i = pl.program_id(0)
    j = pl.program_id(1)
    valid = i < nv_ref[0]

    @pl.when(jnp.logical_and(valid, j == 0))
    def _():
        xb_ref[...] = xs_ref[...].astype(BF16)
        acc_ref[...] = jnp.zeros(acc_ref.shape, F32)

    @pl.when(valid)
    def _():
        h = xb_ref[...]
        a = jnp.dot(h, w1_ref[0], preferred_element_type=F32)
        b = jnp.dot(h, w3_ref[0], preferred_element_type=F32)
        acc_ref[...] += jnp.dot((_silu(a) * b).astype(BF16), w2_ref[0], preferred_element_type=F32)

    @pl.when(jnp.logical_and(valid, j == pl.num_programs(1) - 1))
    def _():
        ys_ref[...] = acc_ref[...]

    @pl.when(jnp.logical_and(jnp.logical_not(valid), j == pl.num_programs(1) - 1))
    def _():
        ys_ref[...] = jnp.zeros(ys_ref.shape, F32)


def moe_experts_grouped(xs, tile_expert, n_valid, w1, w3, w2, *, tm, tf):
    n_rows, d = xs.shape
    dff = w1.shape[2]
    nf = dff // tf

    def w13_idx(i, j, te, nv):
        return (te[i], 0, jnp.where(i < nv[0], j, nf - 1))

    def w2_idx(i, j, te, nv):
        return (te[i], jnp.where(i < nv[0], j, nf - 1), 0)

    grid_spec = pltpu.PrefetchScalarGridSpec(
        num_scalar_prefetch=2,
        grid=(n_rows // tm, nf),
        in_specs=[
            pl.BlockSpec((tm, d), lambda i, j, te, nv: (jnp.minimum(i, nv[0] - 1), 0)),
            pl.BlockSpec((1, d, tf), w13_idx),
            pl.BlockSpec((1, d, tf), w13_idx),
            pl.BlockSpec((1, tf, d), w2_idx),
        ],
        out_specs=pl.BlockSpec((tm, d), lambda i, j, te, nv: (i, 0)),
        scratch_shapes=[pltpu.VMEM((tm, d), BF16), pltpu.VMEM((tm, d), F32)],
    )
    return pl.pallas_call(
        _experts_kernel,
        grid_spec=grid_spec,
        out_shape=jax.ShapeDtypeStruct((n_rows, d), F32),
        compiler_params=_cparams(2),
        name="moe_experts_grouped",
    )(tile_expert, n_valid, xs, w1, w3, w2)


def _combine_kernel(pos_ref, x_ref, mod_ref, rt_ref, ys_ref, o_ref, y1_ref, y2_ref, sem):
    t = x_ref.shape[0]

    def row_copy(src_row, dst_ref, r):
        return pltpu.make_async_copy(ys_ref.at[pl.ds(src_row, 1), :], dst_ref.at[pl.ds(r, 1), :], sem)

    def issue(r, carry):
        row_copy(pos_ref[0, 0, r], y1_ref, r).start()
        row_copy(pos_ref[0, 0, t + r], y2_ref, r).start()
        return carry

    lax.fori_loop(0, t, issue, 0)
    for dst_ref in (y1_ref, y2_ref):
        pltpu.make_async_copy(ys_ref.at[pl.ds(0, t), :], dst_ref, sem).wait()
    rt = rt_ref[...]
    mix = rt[:, R_W1:R_W1 + 1] * y1_ref[...] + rt[:, R_W2:R_W2 + 1] * y2_ref[...]
    o_ref[...] = x_ref[...] + mod_ref[0, 5:6, :] * mix


def moe_combine(x2d, mod, route, pos_tiles, ys, *, tm, tiles_per_mod):
    m, d = x2d.shape
    n_mod = mod.shape[0]
    mod_idx = (lambda i: (i // tiles_per_mod, 0, 0)) if n_mod > 1 else (lambda i: (0, 0, 0))
    return pl.pallas_call(
        _combine_kernel,
        grid=(m // tm,),
        in_specs=[
            pl.BlockSpec((1, 1, 2 * tm), lambda i: (i, 0, 0), memory_space=pltpu.SMEM),
            pl.BlockSpec((tm, d), lambda i: (i, 0)),
            pl.BlockSpec((1, 6, d), mod_idx),
            pl.BlockSpec((tm, 128), lambda i: (i, 0)),
            pl.BlockSpec(memory_space=pl.ANY),
        ],
        out_specs=pl.BlockSpec((tm, d), lambda i: (i, 0)),
        out_shape=jax.ShapeDtypeStruct((m, d), F32),
        scratch_shapes=[pltpu.VMEM((tm, d), F32), pltpu.VMEM((tm, d), F32), pltpu.SemaphoreType.DMA(())],
        compiler_params=_cparams(1),
        name="moe_combine",
    )(pos_tiles, x2d, mod, route, ys)


def _pos_tiles(pos1, pos2, tm):
    n = pos1.shape[0] // tm
    return jnp.concatenate([pos1.reshape(n, 1, tm), pos2.reshape(n, 1, tm)], axis=2)


def moe_sparse(x2d, mod, g, router_pad, w1, w3, w2, *, rows_per_mod):
    m, d = x2d.shape
    tr, td, tc, te = MOE_ROUTE_TM, MOE_DISPATCH_TM, MOE_COMBINE_TM, MOE_EXPERT_TM
    tiles_per_seq_row = rows_per_mod
    route, cnt = moe_route(x2d, mod, g, router_pad, tm=tr, tiles_per_mod=tiles_per_seq_row // tr)
    counts = cnt[0, 0:N_EXPERTS].astype(jnp.int32)
    group = ((counts + te - 1) // te) * te
    ends = jnp.cumsum(group)
    starts = ends - group
    e1 = route[:, R_E1].astype(jnp.int32)
    e2 = route[:, R_E2].astype(jnp.int32)
    pos1 = starts[e1] + route[:, R_RANK1].astype(jnp.int32)
    pos2 = starts[e2] + route[:, R_RANK2].astype(jnp.int32)
    n_rows = 2 * m + N_EXPERTS * te
    n_tiles = n_rows // te
    tile_expert = jnp.minimum(
        jnp.searchsorted(ends, jnp.arange(n_tiles, dtype=jnp.int32) * te, side="right"), N_EXPERTS - 1
    ).astype(jnp.int32)
    n_valid = (ends[-1:] // te).astype(jnp.int32)
    xs = moe_dispatch(x2d, mod, g, _pos_tiles(pos1, pos2, td), n_rows, tm=td, tiles_per_mod=tiles_per_seq_row // td)
    ys = moe_experts_grouped(xs, tile_expert, n_valid, w1, w3, w2, tm=te, tf=MOE_EXPERT_TF)
    return moe_combine(x2d, mod, route, _pos_tiles(pos1, pos2, tc), ys, tm=tc, tiles_per_mod=tiles_per_seq_row // tc)


def _permute_w_in(w):
    f, c, p, q, kv, gts = w[:, 0:256], w[:, 256:768], w[:, 768:1024], w[:, 1024:1536], w[:, 1536:1792], w[:, 1792:]
    return jnp.concatenate([gts, q, c, p, f, kv], axis=1).astype(BF16)


def _layer_weights(layer, w_br_fourier, conv_dw, conv_b, conv_norm_g, w_br_conv, pool_w, pool_scale, w_br_pool,
                   w_br_attn, w_out):
    pw = jax.scipy.linalg.block_diag(*[pool_w[layer, i] for i in range(len(POOL_WINDOWS))])
    return {
        "wf": w_br_fourier[layer].astype(BF16), "wc": w_br_conv[layer].astype(BF16),
        "wp": w_br_pool[layer].astype(BF16), "wa": w_br_attn[layer].astype(BF16), "wo": w_out[layer].astype(BF16),
        "dw": conv_dw[layer], "cb": conv_b[layer].reshape(1, CONV_W), "cg": conv_norm_g[layer].reshape(1, CONV_W),
        "pw": pw.astype(BF16), "ps": pool_scale[layer].reshape(1, POOL_W),
    }


def kernel(x, c, ctx, c_ctx, w_mod, b_mod, norm1_g, norm2_g, w_in, w_br_fourier, conv_dw, conv_b, conv_norm_g,
           w_br_conv, pool_w, pool_scale, w_br_pool, q_norm_g, k_norm_g, w_br_attn, w_out, ffn_w1, ffn_w3, ffn_w2,
           moe_router, moe_w1, moe_w3, moe_w2):
    batch, seq, d = x.shape
    ctx_len = ctx.shape[1]
    depth = w_in.shape[0]
    rope = rope_tables(seq)

    c_rows = jnp.zeros((8, d), F32).at[0:batch].set(c).at[batch].set(c_ctx)
    mods = modulation_all(c_rows, w_mod, b_mod).reshape(depth, 8, 6, d)

    xl = x.reshape(batch * seq, d)
    xc = ctx.reshape(batch * ctx_len, d)
    for layer in range(depth):
        is_last = layer == depth - 1
        mod_l = mods[layer, 0:batch]
        mod_c = mods[layer, batch:batch + 1]
        w_in_l = _permute_w_in(w_in[layer])
        lw = _layer_weights(layer, w_br_fourier, conv_dw, conv_b, conv_norm_g, w_br_conv, pool_w, pool_scale,
                            w_br_pool, w_br_attn, w_out)

        proj_c = input_projection(xc, mod_c, norm1_g[layer], w_in_l, tm=256, tiles_per_mod=1)
        qc, ktc, vc = qkv_prepare(proj_c, q_norm_g[layer], k_norm_g[layer], None, batch=batch, seq=ctx_len, tp=256)

        proj = input_projection(xl, mod_l, norm1_g[layer], w_in_l, tm=512, tiles_per_mod=seq // 512)
        q, kt, v = qkv_prepare(proj, q_norm_g[layer], k_norm_g[layer], rope, batch=batch, seq=seq, tp=512)
        attn = attention(q, kt, v, ktc, vc, batch=batch, seq_q=seq, tq=256, tk=1024)
        yf = fourier_mix(proj[:, P_OFF_F:P_OFF_F + FOURIER_W], batch=batch, seq=seq, n1=64, n2=seq // 64)
        xl = merge_branches(xl, mod_l, proj, yf, attn, lw, seq=seq, t=512)

        if not is_last:
            attn_c = attention(qc, None, None, ktc, vc, batch=batch, seq_q=ctx_len, tq=256, tk=ctx_len)
            yf_c = fourier_mix(proj_c[:, P_OFF_F:P_OFF_F + FOURIER_W], batch=batch, seq=ctx_len, n1=16,
                               n2=ctx_len // 16)
            xc = merge_branches(xc, mod_c, proj_c, yf_c, attn_c, lw, seq=ctx_len, t=256)

        j = layer // 2
        if layer % 2 == 0:
            w1, w3, w2 = ffn_w1[j].astype(BF16), ffn_w3[j].astype(BF16), ffn_w2[j].astype(BF16)
            xl = ffn_dense(xl, mod_l, norm2_g[layer], w1, w3, w2, tm=512, tf=1408, tiles_per_mod=seq // 512)
            if not is_last:
                xc = ffn_dense(xc, mod_c, norm2_g[layer], w1, w3, w2, tm=256, tf=1408, tiles_per_mod=1)
        else:
            w1, w3, w2 = moe_w1[j].astype(BF16), moe_w3[j].astype(BF16), moe_w2[j].astype(BF16)
            router_pad = jnp.zeros((d, 128), F32).at[:, 0:N_EXPERTS].set(moe_router[j])
            xl = moe_sparse(xl, mod_l, norm2_g[layer], router_pad, w1, w3, w2, rows_per_mod=seq)
            if not is_last:
                xc = moe_sparse(xc, mod_c, norm2_g[layer], router_pad, w1, w3, w2, rows_per_mod=batch * ctx_len)
    return xl.reshape(batch, seq, d)
```

```python
import functools
import math

import numpy as np
import jax
import jax.numpy as jnp
from jax import lax
from jax.experimental import pallas as pl
from jax.experimental.pallas import tpu as pltpu

F32 = jnp.float32
BF16 = jnp.bfloat16

D_MODEL = 1024
GRID_W = 64
EPS = 1e-6
FOURIER_GW = 64
FOURIER_W = 256
CONV_W = 256
CONV_K = 31
CONV_HALF = CONV_K // 2
POOL_WINDOWS = (2, 4, 8, 16)
POOL_GW = 64
POOL_W = 256
HEAD_DIM = 64
N_HEADS = 8
N_KV_HEADS = 2
Q_PER_KV = 4
Q_W = 512
KV_W = 128
ROPE_THETA = 10000.0
N_EXPERTS = 8
IN_W = 5888

P_OFF_G = 0
P_OFF_Q = 4096
P_OFF_CP = 4608
P_OFF_F = 5376
P_OFF_KV = 5632
CP_W = 2 * CONV_W + POOL_W

Q_SCALE = (HEAD_DIM ** -0.5) * math.log2(math.e)

STALE_MAX_EXP_LIMIT = 64.0

MOE_ROUTE_TM = 512
MOE_DISPATCH_TM = 512
MOE_COMBINE_TM = 256
MOE_EXPERT_TM = 512
MOE_EXPERT_TF = 896

CONV_ROWS = 64
HALO = 16
VMEM_LIMIT = 56 * 1024 * 1024


def _cparams(n_axes):
    return pltpu.CompilerParams(dimension_semantics=("arbitrary",) * n_axes, vmem_limit_bytes=VMEM_LIMIT)


def _sigmoid(v):
    return 0.5 * jnp.tanh(0.5 * v) + 0.5


def _silu(v):
    return v * _sigmoid(v)


def _norm_mod(x, g, shift, scale):
    ms = jnp.mean(x * x, axis=-1, keepdims=True)
    return x * lax.rsqrt(ms + EPS) * g * (1.0 + scale) + shift


def _mod_kernel(c_ref, w_ref, b_ref, o_ref):
    s = _silu(c_ref[...])
    o_ref[0] = jnp.dot(s, w_ref[0], preferred_element_type=F32, precision=lax.Precision.HIGHEST) + b_ref[0]


def modulation_all(c_rows, w_mod, b_mod):
    n_layers, d, n = w_mod.shape
    tn = 1536
    return pl.pallas_call(
        _mod_kernel,
        grid=(n_layers, n // tn),
        in_specs=[
            pl.BlockSpec((8, d), lambda l, j: (0, 0)),
            pl.BlockSpec((1, d, tn), lambda l, j: (l, 0, j)),
            pl.BlockSpec((1, 1, tn), lambda l, j: (l, 0, j)),
        ],
        out_specs=pl.BlockSpec((1, 8, tn), lambda l, j: (l, 0, j)),
        out_shape=jax.ShapeDtypeStruct((n_layers, 8, n), F32),
        compiler_params=_cparams(2),
        name="modulation",
    )(c_rows, w_mod, b_mod.reshape(n_layers, 1, n))


def _inproj_kernel(x_ref, mod_ref, g_ref, w_ref, o_ref, *, chunks):
    h = _norm_mod(x_ref[...], g_ref[...], mod_ref[0, 0:1, :], mod_ref[0, 1:2, :]).astype(BF16)
    for c0, cw in chunks:
        o_ref[:, c0:c0 + cw] = jnp.dot(h, w_ref[:, c0:c0 + cw], preferred_element_type=F32).astype(o_ref.dtype)


def input_projection(x2d, mod, g, w_bf16, *, tm, tiles_per_mod):
    m, d = x2d.shape
    n = w_bf16.shape[1]
    chunks = tuple((c0, min(512, n - c0)) for c0 in range(0, n, 512))
    n_mod = mod.shape[0]
    mod_idx = (lambda i: (i // tiles_per_mod, 0, 0)) if n_mod > 1 else (lambda i: (0, 0, 0))
    return pl.pallas_call(
        functools.partial(_inproj_kernel, chunks=chunks),
        grid=(m // tm,),
        in_specs=[
            pl.BlockSpec((tm, d), lambda i: (i, 0)),
            pl.BlockSpec((1, 6, d), mod_idx),
            pl.BlockSpec((1, d), lambda i: (0, 0)),
            pl.BlockSpec((d, n), lambda i: (0, 0), pipeline_mode=pl.Buffered(1)),
        ],
        out_specs=pl.BlockSpec((tm, n), lambda i: (i, 0)),
        out_shape=jax.ShapeDtypeStruct((m, n), BF16),
        compiler_params=_cparams(1),
        name="input_projection",
    )(x2d, mod, g.reshape(1, d), w_bf16)


def _seg_sum64(v, ones_bd):
    hi = v.astype(BF16)
    lo = (v - hi.astype(F32)).astype(BF16)
    return (jnp.dot(hi, ones_bd, preferred_element_type=F32) + jnp.dot(lo, ones_bd, preferred_element_type=F32))


def _head_norm_rope(x, g, ones_bd, cos, sin, low_mask):
    y = x * lax.rsqrt(_seg_sum64(x * x, ones_bd) * (1.0 / HEAD_DIM) + EPS) * g
    if cos is None:
        return y
    partner = jnp.where(low_mask, pltpu.roll(y, 128 - 16, axis=1), pltpu.roll(y, 16, axis=1))
    return y * cos + partner * sin


def _prep_kernel(*refs, use_rope):
    if use_rope:
        q_ref, kv_ref, gq_ref, gk_ref, ones_ref, cos_ref, sin_ref, qo_ref, kt_ref, v_ref = refs
        cos, sin = cos_ref[...], sin_ref[...]
    else:
        q_ref, kv_ref, gq_ref, gk_ref, ones_ref, qo_ref, kt_ref, v_ref = refs
        cos = sin = None
    t = q_ref.shape[0]
    ones_bd = ones_ref[...]
    lane = lax.broadcasted_iota(jnp.int32, (t, 128), 1)
    low_mask = (lane % 32) < 16
    gq = gq_ref[...]
    for c in range(Q_W // 128):
        xq = q_ref[:, 128 * c:128 * (c + 1)].astype(F32)
        yq = _head_norm_rope(xq, gq, ones_bd, cos, sin, low_mask) * Q_SCALE
        qo_ref[0, c // 2, 128 * (c % 2):128 * (c % 2 + 1), :] = yq.T.astype(BF16)
    xk = kv_ref[:, 0:128].astype(F32)
    yk = _head_norm_rope(xk, gk_ref[...], ones_bd, cos, sin, low_mask)
    ykr = pltpu.roll(yk, 64, axis=1)
    first = lane < 64
    k0 = jnp.where(first, yk, ykr).astype(BF16)
    k1 = jnp.where(first, ykr, yk).astype(BF16)
    kt_ref[0, 0] = jnp.concatenate([k0, k0], axis=1)
    kt_ref[0, 1] = jnp.concatenate([k1, k1], axis=1)
    vt = kv_ref[:, 128:256].astype(F32).T
    ones = jnp.ones((HEAD_DIM, t), F32)
    for h in range(N_KV_HEADS):
        v_ref[0, h] = jnp.concatenate([vt[64 * h:64 * (h + 1), :], ones], axis=0).astype(BF16)


def qkv_prepare(proj, gq, gk, rope, *, batch, seq, tp):
    m = proj.shape[0]
    tps = seq // tp
    use_rope = rope is not None
    ones_bd = jnp.asarray(np.kron(np.eye(2, dtype=np.float32), np.ones((64, 64), np.float32)), BF16)
    gq2 = jnp.tile(gq, 2).reshape(1, 128)
    gk2 = jnp.tile(gk, 2).reshape(1, 128)
    in_specs = [
        pl.BlockSpec((tp, Q_W), lambda i: (i, P_OFF_Q // Q_W)),
        pl.BlockSpec((tp, 256), lambda i: (i, P_OFF_KV // 256)),
        pl.BlockSpec((1, 128), lambda i: (0, 0)),
        pl.BlockSpec((1, 128), lambda i: (0, 0)),
        pl.BlockSpec((128, 128), lambda i: (0, 0)),
    ]
    args = [proj, proj, gq2, gk2, ones_bd]
    if use_rope:
        in_specs += [pl.BlockSpec((tp, 128), lambda i: (i % tps, 0))] * 2
        args += list(rope)
    return pl.pallas_call(
        functools.partial(_prep_kernel, use_rope=use_rope),
        grid=(m // tp,),
        in_specs=in_specs,
        out_specs=[
            pl.BlockSpec((1, N_KV_HEADS, 256, tp), lambda i: (i // tps, 0, 0, i % tps)),
            pl.BlockSpec((1, N_KV_HEADS, tp, 256), lambda i: (i // tps, 0, i % tps, 0)),
            pl.BlockSpec((1, N_KV_HEADS, 128, tp), lambda i: (i // tps, 0, 0, i % tps)),
        ],
        out_shape=[
            jax.ShapeDtypeStruct((batch, N_KV_HEADS, 256, seq), BF16),
            jax.ShapeDtypeStruct((batch, N_KV_HEADS, seq, 256), BF16),
            jax.ShapeDtypeStruct((batch, N_KV_HEADS, 128, seq), BF16),
        ],
        compiler_params=_cparams(1),
        name="qkv_prepare",
    )(*args)


def rope_tables(seq):
    n_freq = HEAD_DIM // 4
    freqs = ROPE_THETA ** (-jnp.arange(n_freq, dtype=F32) / n_freq)
    t = jnp.arange(seq)
    row = (t // GRID_W).astype(F32)
    col = (t % GRID_W).astype(F32)
    ang_r = row[:, None] * freqs
    ang_c = col[:, None] * freqs
    cos = jnp.concatenate([jnp.cos(ang_r)] * 2 + [jnp.cos(ang_c)] * 2, axis=1)
    sin = jnp.concatenate([-jnp.sin(ang_r), jnp.sin(ang_r), -jnp.sin(ang_c), jnp.sin(ang_c)], axis=1)
    return jnp.tile(cos, (1, 2)), jnp.tile(sin, (1, 2))


def _attn_kernel(*refs, tq, tk, nk, tail):
    refs = list(refs)
    qt_ref = refs.pop(0)
    k_ref, vt_ref = (refs.pop(0), refs.pop(0)) if nk else (None, None)
    kc_ref, vtc_ref = (refs.pop(0), refs.pop(0)) if tail else (None, None)
    o_ref, qs_ref, s0, s1, p0, p1, a0, a1, mx0, mx1, m_ref, acc_ref = refs
    s_bufs, p_bufs, a_bufs, mx_bufs = (s0, s1), (p0, p1), (a0, a1), (mx0, mx1)
    n_blocks = nk + (1 if tail else 0)

    _attn_stack_queries(qt_ref, qs_ref, tq)
    m_ref[...] = jnp.full(m_ref.shape, -jnp.inf, F32)
    acc_ref[...] = jnp.zeros(acc_ref.shape, F32)

    def block(t):
        if isinstance(t, int) and t >= nk:
            return kc_ref[0, 0], vtc_ref[0, 0], tail
        off = t * tk if isinstance(t, int) else pl.multiple_of(t * tk, tk)
        return k_ref[0, 0, pl.ds(off, tk), :], vt_ref[0, 0, :, pl.ds(off, tk)], tk

    def scores(t, slot):
        k_rows, _, n = block(t)
        s = jnp.dot(k_rows, qs_ref[...], preferred_element_type=F32)
        s_bufs[slot][0:n, :] = s
        mx_bufs[slot][...] = jnp.max(s, axis=0, keepdims=True)

    def numerators(n, slot):
        s_ref, p_ref, a_ref = s_bufs[slot], p_bufs[slot], a_bufs[slot]
        for c0 in range(0, Q_PER_KV * tq, 128):
            cols = slice(c0, c0 + 128)
            m_old = m_ref[:, cols]
            m_new = jnp.maximum(m_old, mx_bufs[slot][:, cols])
            a_ref[:, cols] = jnp.exp2(m_old - m_new)
            p_ref[0:n, cols] = jnp.exp2(s_ref[0:n, cols] - m_new).astype(BF16)
            m_ref[:, cols] = m_new

    def weighted_sum(t, slot):
        _, vt, n = block(t)
        pv = jnp.dot(vt, p_bufs[slot][0:n, :], preferred_element_type=F32)
        acc_ref[...] = a_bufs[slot][...] * acc_ref[...] + pv

    def rows_of(t):
        return tk if t < nk else tail

    def step(t, slot, n_mid):
        scores(t, slot)
        numerators(n_mid, 1 - slot)
        weighted_sum(t - 2, slot)

    scores(0, 0)
    if n_blocks > 1:
        scores(1, 1)
        numerators(rows_of(0), 0)
        n_pairs = max(nk - 2, 0) // 2

        def pair(i, carry):
            t = 2 + 2 * i
            step(t, 0, tk)
            step(t + 1, 1, tk)
            return carry

        if n_pairs:
            lax.fori_loop(0, n_pairs, pair, 0)
        for t in range(2 + 2 * n_pairs, n_blocks):
            step(t, t % 2, rows_of(t - 1))
        last = n_blocks - 1
        numerators(rows_of(last), last % 2)
        weighted_sum(last - 1, (last - 1) % 2)
        weighted_sum(last, last % 2)
    else:
        numerators(rows_of(0), 0)
        weighted_sum(0, 0)

    _attn_write_output(acc_ref, o_ref, tq)


def _attn_write_output(acc_ref, o_ref, tq):
    acc = acc_ref[...]
    ot = acc[0:HEAD_DIM, :] / acc[HEAD_DIM:2 * HEAD_DIM, :]
    for half in range(2):
        pair_t = jnp.concatenate([ot[:, (2 * half) * tq:(2 * half + 1) * tq],
                                  ot[:, (2 * half + 1) * tq:(2 * half + 2) * tq]], axis=0)
        o_ref[:, 128 * half:128 * (half + 1)] = pair_t.T.astype(o_ref.dtype)


def _attn_stack_queries(qt_ref, qs_ref, tq):
    row_group = lax.broadcasted_iota(jnp.int32, (256, tq), 0) // HEAD_DIM
    qt = qt_ref[0, 0]
    for g in range(Q_PER_KV):
        qs_ref[:, g * tq:(g + 1) * tq] = jnp.where(row_group == g, qt, jnp.zeros_like(qt))


def _attn_stale_max_kernel(qt_ref, k_ref, vt_ref, kc_ref, vtc_ref, o_ref, qs_ref, p0, p1, f0, f1, m_ref, acc_ref,
                           *, tq, tk, nk):
    p_bufs, f_bufs = (p0, p1), (f0, f1)
    _attn_stack_queries(qt_ref, qs_ref, tq)

    s = jnp.dot(kc_ref[0, 0], qs_ref[...], preferred_element_type=F32)
    m0 = jnp.max(s, axis=0, keepdims=True)
    m_ref[...] = m0
    acc_ref[...] = jnp.dot(vtc_ref[0, 0], jnp.exp2(s - m0).astype(BF16), preferred_element_type=F32)

    def numerators(t, slot):
        off = t * tk if isinstance(t, int) else pl.multiple_of(t * tk, tk)
        s = jnp.dot(k_ref[0, 0, pl.ds(off, tk), :], qs_ref[...], preferred_element_type=F32)
        m_old = m_ref[...]
        p_bufs[slot][...] = jnp.exp2(s - m_old).astype(BF16)
        m_new = jnp.maximum(m_old, jnp.max(s, axis=0, keepdims=True))
        f_bufs[slot][...] = jnp.exp2(m_old - m_new)
        m_ref[...] = m_new

    def weighted_sum(t, slot):
        off = t * tk if isinstance(t, int) else pl.multiple_of(t * tk, tk)
        pv = jnp.dot(vt_ref[0, 0, :, pl.ds(off, tk)], p_bufs[slot][...], preferred_element_type=F32)
        acc_ref[...] = (acc_ref[...] + pv) * f_bufs[slot][...]

    def step(t, slot):
        numerators(t, slot)
        weighted_sum(t - 1, 1 - slot)

    numerators(0, 0)
    n_pairs = (nk - 1) // 2

    def pair(i, carry):
        t = 1 + 2 * i
        step(t, 1)
        step(t + 1, 0)
        return carry

    if n_pairs:
        lax.fori_loop(0, n_pairs, pair, 0)
    for t in range(1 + 2 * n_pairs, nk):
        step(t, t % 2)
    weighted_sum(nk - 1, (nk - 1) % 2)
    _attn_write_output(acc_ref, o_ref, tq)


def attention_stale_max(qt, k4, vt1, k4_tail, vt1_tail, *, batch, seq_q, tq, tk):
    nq = seq_q // tq
    lanes = Q_PER_KV * tq
    lk = k4.shape[2]
    tail = k4_tail.shape[2]
    return pl.pallas_call(
        functools.partial(_attn_stale_max_kernel, tq=tq, tk=tk, nk=lk // tk),
        grid=(batch, N_KV_HEADS, nq),
        in_specs=[
            pl.BlockSpec((1, 1, 256, tq), lambda b, h, i: (b, h, 0, i)),
            pl.BlockSpec((1, 1, lk, 256), lambda b, h, i: (b, h, 0, 0)),
            pl.BlockSpec((1, 1, 128, lk), lambda b, h, i: (b, h, 0, 0)),
            pl.BlockSpec((1, 1, tail, 256), lambda b, h, i: (b, h, 0, 0)),
            pl.BlockSpec((1, 1, 128, tail), lambda b, h, i: (b, h, 0, 0)),
        ],
        out_specs=pl.BlockSpec((tq, 256), lambda b, h, i: (b * nq + i, h)),
        out_shape=jax.ShapeDtypeStruct((batch * seq_q, Q_W), BF16),
        scratch_shapes=[
            pltpu.VMEM((256, lanes), BF16),
            pltpu.VMEM((tk, lanes), BF16), pltpu.VMEM((tk, lanes), BF16),
            pltpu.VMEM((1, lanes), F32), pltpu.VMEM((1, lanes), F32),
            pltpu.VMEM((1, lanes), F32),
            pltpu.VMEM((2 * HEAD_DIM, lanes), F32),
        ],
        compiler_params=_cparams(3),
        name="attention_stale_max",
    )(qt, k4, vt1, k4_tail, vt1_tail)


def attention(qt, k4, vt1, k4_tail, vt1_tail, *, batch, seq_q, tq, tk):
    nq = seq_q // tq
    lanes = Q_PER_KV * tq
    nk = 0 if k4 is None else k4.shape[2] // tk
    tail = 0 if k4_tail is None else k4_tail.shape[2]
    buf_rows = max(tk if nk else 0, tail)
    in_specs = [pl.BlockSpec((1, 1, 256, tq), lambda b, h, i: (b, h, 0, i))]
    args = [qt]
    if nk:
        lk = k4.shape[2]
        in_specs += [pl.BlockSpec((1, 1, lk, 256), lambda b, h, i: (b, h, 0, 0)),
                     pl.BlockSpec((1, 1, 128, lk), lambda b, h, i: (b, h, 0, 0))]
        args += [k4, vt1]
    if tail:
        in_specs += [pl.BlockSpec((1, 1, tail, 256), lambda b, h, i: (b, h, 0, 0)),
                     pl.BlockSpec((1, 1, 128, tail), lambda b, h, i: (b, h, 0, 0))]
        args += [k4_tail, vt1_tail]
    return pl.pallas_call(
        functools.partial(_attn_kernel, tq=tq, tk=tk, nk=nk, tail=tail),
        grid=(batch, N_KV_HEADS, nq),
        in_specs=in_specs,
        out_specs=pl.BlockSpec((tq, 256), lambda b, h, i: (b * nq + i, h)),
        out_shape=jax.ShapeDtypeStruct((batch * seq_q, Q_W), BF16),
        scratch_shapes=[
            pltpu.VMEM((256, lanes), BF16),
            pltpu.VMEM((buf_rows, lanes), F32), pltpu.VMEM((buf_rows, lanes), F32),
            pltpu.VMEM((buf_rows, lanes), BF16), pltpu.VMEM((buf_rows, lanes), BF16),
            pltpu.VMEM((1, lanes), F32), pltpu.VMEM((1, lanes), F32),
            pltpu.VMEM((1, lanes), F32), pltpu.VMEM((1, lanes), F32),
            pltpu.VMEM((1, lanes), F32),
            pltpu.VMEM((2 * HEAD_DIM, lanes), F32),
        ],
        compiler_params=_cparams(3),
        name="attention",
    )(*args)


def _dft_cs(n):
    k = np.arange(n)
    ang = 2.0 * np.pi * ((k[:, None] * k[None, :]) % n) / n
    return np.cos(ang), np.sin(ang)


def _fft1_kernel(x_ref, f_ref, c_ref, s_ref, o_ref, *, n1):
    y = jnp.dot(f_ref[...], x_ref[0], preferred_element_type=F32)
    yr, yi = y[:n1], y[n1:]
    c, s = c_ref[...], s_ref[...]
    o_ref[0, 0] = (yr * c + yi * s).astype(o_ref.dtype)
    o_ref[0, 1] = (yi * c - yr * s).astype(o_ref.dtype)


def _fft2_kernel(y_ref, f_ref, bc_ref, bs_ref, o_ref, *, n2, kb):
    for j in range(kb):
        y2 = jnp.concatenate([y_ref[0, 0, j], y_ref[0, 1, j]], axis=0)
        x2 = jnp.dot(f_ref[...], y2, preferred_element_type=F32)
        xr = x2[:n2].astype(BF16)
        xi = x2[n2:].astype(BF16)
        z = (jnp.dot(xr, bc_ref[...], preferred_element_type=F32) + jnp.dot(xi, bs_ref[...], preferred_element_type=F32))
        o_ref[0, j] = z.astype(o_ref.dtype)


def fourier_mix(u, *, batch, seq, n1, n2):
    cw = u.shape[1]
    lanes = n2 * cw
    tl = min(lanes, 4096)
    c1, s1 = _dft_cs(n1)
    f1 = jnp.asarray(np.concatenate([c1, -s1], axis=0), BF16)
    k1 = np.arange(n1)[:, None]
    t2 = np.arange(n2)[None, :]
    ang = 2.0 * np.pi * ((k1 * t2) % seq) / seq
    twc = jnp.asarray(np.repeat(np.cos(ang), cw, axis=1), F32)
    tws = jnp.asarray(np.repeat(np.sin(ang), cw, axis=1), F32)
    x2 = u.reshape(batch, n1, lanes)
    yp = pl.pallas_call(
        functools.partial(_fft1_kernel, n1=n1),
        grid=(batch, lanes // tl),
        in_specs=[
            pl.BlockSpec((1, n1, tl), lambda b, j: (b, 0, j)),
            pl.BlockSpec((2 * n1, n1), lambda b, j: (0, 0)),
            pl.BlockSpec((n1, tl), lambda b, j: (0, j)),
            pl.BlockSpec((n1, tl), lambda b, j: (0, j)),
        ],
        out_specs=pl.BlockSpec((1, 2, n1, tl), lambda b, j: (b, 0, 0, j)),
        out_shape=jax.ShapeDtypeStruct((batch, 2, n1, lanes), BF16),
        compiler_params=_cparams(2),
        name="fft_stage1",
    )(x2, f1, twc, tws)

    c2, s2 = _dft_cs(n2)
    f2 = jnp.asarray(np.block([[c2, s2], [-s2, c2]]), BF16)
    cg, sg = _dft_cs(FOURIER_GW)
    norm = 1.0 / math.sqrt(seq * FOURIER_GW)
    bdc = jnp.asarray(np.kron(np.eye(cw // FOURIER_GW), cg) * norm, BF16)
    bds = jnp.asarray(np.kron(np.eye(cw // FOURIER_GW), sg) * norm, BF16)
    kb = min(n1, 16)
    y5 = yp.reshape(batch, 2, n1, n2, cw)
    z = pl.pallas_call(
        functools.partial(_fft2_kernel, n2=n2, kb=kb),
        grid=(batch, n1 // kb),
        in_specs=[
            pl.BlockSpec((1, 2, kb, n2, cw), lambda b, j: (b, 0, j, 0, 0)),
            pl.BlockSpec((2 * n2, 2 * n2), lambda b, j: (0, 0)),
            pl.BlockSpec((cw, cw), lambda b, j: (0, 0)),
            pl.BlockSpec((cw, cw), lambda b, j: (0, 0)),
        ],
        out_specs=pl.BlockSpec((1, kb, n2, cw), lambda b, j: (b, j, 0, 0)),
        out_shape=jax.ShapeDtypeStruct((batch, n1, n2, cw), BF16),
        compiler_params=_cparams(2),
        name="fft_stage2",
    )(y5, f2, bdc, bds)
    return z.transpose(0, 2, 1, 3).reshape(batch * seq, cw)


def _merge_kernel(x_ref, mod_ref, gate_ref, cp_ref, cpp_ref, cpn_ref, yf_ref, at_ref,
                  wf_ref, wc_ref, wp_ref, wa_ref, wo_ref, dw_ref, cb_ref, cg_ref, pw_ref, ps_ref,
                  o_ref, ybuf, xbuf, ysh, cacc, *, t, tps, seq):
    i = pl.program_id(0)
    pos_tile = i % tps
    keep_prev = jnp.where(pos_tile != 0, 1.0, 0.0).astype(F32)
    keep_next = jnp.where(pos_tile != tps - 1, 1.0, 0.0).astype(F32)

    def glu(blk):
        return blk[:, 0:CONV_W].astype(F32) * _sigmoid(blk[:, CONV_W:2 * CONV_W].astype(F32))

    cp, cpp, cpn = cp_ref[...], cpp_ref[...], cpn_ref[...]
    ybuf[0:HALO, :] = glu(cpp) * keep_prev
    ybuf[HALO:HALO + t, :] = glu(cp)
    ybuf[HALO + t:HALO + t + HALO, :] = glu(cpn) * keep_next
    xbuf[0:HALO, :] = cpp[:, 2 * CONV_W:].astype(F32) * keep_prev
    xbuf[HALO:HALO + t, :] = cp[:, 2 * CONV_W:].astype(F32)
    xbuf[HALO + t:HALO + t + HALO, :] = cpn[:, 2 * CONV_W:].astype(F32) * keep_next

    n_sh = t + 2 * HALO - 8
    for b in range(1, 8):
        ysh[b - 1, 0:n_sh, :] = ybuf[pl.ds(b, n_sh), :]
    for r0 in range(0, t, CONV_ROWS):
        part = jnp.zeros((CONV_ROWS, CONV_W), F32)
        for k in range(CONV_K):
            a, b = divmod(HALO - CONV_HALF + k, 8)
            src = ybuf if b == 0 else ysh.at[b - 1]
            part = part + dw_ref[k:k + 1, :] * src[8 * a + r0:8 * a + r0 + CONV_ROWS, :]
        cacc[r0:r0 + CONV_ROWS, :] = part + cb_ref[...]
    acc = cacc[...]
    ms = jnp.mean(acc * acc, axis=-1, keepdims=True)
    conv_out = _silu(acc * lax.rsqrt(ms + EPS) * cg_ref[...]).astype(BF16)

    def xs(d):
        return xbuf[pl.ds(HALO + d, t), :]

    x0 = xs(0)
    s2 = xs(-1) + x0
    s4 = s2 + xs(-2) + xs(1)
    s8 = s4 + xs(-4) + xs(-3) + xs(2) + xs(3)
    s16 = s8 + xs(-8) + xs(-7) + xs(-6) + xs(-5) + xs(4) + xs(5) + xs(6) + xs(7)
    grp = lax.broadcasted_iota(jnp.int32, (t, POOL_W), 1) // POOL_GW
    pos = pos_tile * t + lax.broadcasted_iota(jnp.int32, (t, POOL_W), 0)
    half = jnp.where(grp == 0, 1, jnp.where(grp == 1, 2, jnp.where(grp == 2, 4, 8)))
    cnt = (jnp.minimum(pos + half, seq) - jnp.maximum(pos - half, 0)).astype(F32)
    wsum = jnp.where(grp == 0, s2, jnp.where(grp == 1, s4, jnp.where(grp == 2, s8, s16)))
    pool_in = (wsum / cnt - x0).astype(BF16)
    pool_out = (jnp.dot(pool_in, pw_ref[...], preferred_element_type=F32) * ps_ref[...]).astype(BF16)

    def gate(b):
        return _sigmoid(gate_ref[:, b * D_MODEL:(b + 1) * D_MODEL].astype(F32))

    merged = gate(0) * jnp.dot(yf_ref[...], wf_ref[...], preferred_element_type=F32)
    merged = merged + gate(1) * jnp.dot(conv_out, wc_ref[...], preferred_element_type=F32)
    merged = merged + gate(2) * jnp.dot(pool_out, wp_ref[...], preferred_element_type=F32)
    merged = merged + gate(3) * jnp.dot(at_ref[...], wa_ref[...], preferred_element_type=F32)
    out = jnp.dot(merged.astype(BF16), wo_ref[...], preferred_element_type=F32)
    o_ref[...] = x_ref[...] + mod_ref[0, 2:3, :] * out


def merge_branches(x2d, mod, proj, yf, attn, lw, *, seq, t):
    m, d = x2d.shape
    tps = seq // t
    hb = t // HALO
    n_halo = m // HALO
    n_mod = mod.shape[0]
    mod_idx = (lambda i: (i // tps, 0, 0)) if n_mod > 1 else (lambda i: (0, 0, 0))
    const = lambda i: (0, 0)
    cp_blk = P_OFF_CP // CP_W
    return pl.pallas_call(
        functools.partial(_merge_kernel, t=t, tps=tps, seq=seq),
        grid=(m // t,),
        in_specs=[
            pl.BlockSpec((t, d), lambda i: (i, 0)),
            pl.BlockSpec((1, 6, d), mod_idx),
            pl.BlockSpec((t, 4 * d), lambda i: (i, 0)),
            pl.BlockSpec((t, CP_W), lambda i: (i, cp_blk)),
            pl.BlockSpec((HALO, CP_W), lambda i: (jnp.maximum(i * hb - 1, 0), cp_blk)),
            pl.BlockSpec((HALO, CP_W), lambda i: (jnp.minimum((i + 1) * hb, n_halo - 1), cp_blk)),
            pl.BlockSpec((t, FOURIER_W), lambda i: (i, 0)),
            pl.BlockSpec((t, Q_W), lambda i: (i, 0)),
            pl.BlockSpec((FOURIER_W, d), const),
            pl.BlockSpec((CONV_W, d), const),
            pl.BlockSpec((POOL_W, d), const),
            pl.BlockSpec((Q_W, d), const),
            pl.BlockSpec((d, d), const),
            pl.BlockSpec((CONV_K, CONV_W), const),
            pl.BlockSpec((1, CONV_W), const),
            pl.BlockSpec((1, CONV_W), const),
            pl.BlockSpec((POOL_W, POOL_W), const),
            pl.BlockSpec((1, POOL_W), const),
        ],
        out_specs=pl.BlockSpec((t, d), lambda i: (i, 0)),
        out_shape=jax.ShapeDtypeStruct((m, d), F32),
        scratch_shapes=[pltpu.VMEM((t + 2 * HALO, CONV_W), F32), pltpu.VMEM((t + 2 * HALO, POOL_W), F32),
                        pltpu.VMEM((7, t + 2 * HALO, CONV_W), F32), pltpu.VMEM((t, CONV_W), F32)],
        compiler_params=_cparams(1),
        name="merge_branches",
    )(x2d, mod, proj, proj, proj, proj, yf, attn,
      lw["wf"], lw["wc"], lw["wp"], lw["wa"], lw["wo"], lw["dw"], lw["cb"], lw["cg"], lw["pw"], lw["ps"])


def _ffn_kernel(x_ref, mod_ref, g_ref, w1_ref, w3_ref, w2_ref, o_ref, h_ref, acc_ref):
    j = pl.program_id(1)

    @pl.when(j == 0)
    def _():
        h_ref[...] = _norm_mod(x_ref[...], g_ref[...], mod_ref[0, 3:4, :], mod_ref[0, 4:5, :]).astype(BF16)
        acc_ref[...] = jnp.zeros(acc_ref.shape, F32)

    h = h_ref[...]
    a = jnp.dot(h, w1_ref[...], preferred_element_type=F32)
    b = jnp.dot(h, w3_ref[...], preferred_element_type=F32)
    acc_ref[...] += jnp.dot((_silu(a) * b).astype(BF16), w2_ref[...], preferred_element_type=F32)

    @pl.when(j == pl.num_programs(1) - 1)
    def _():
        o_ref[...] = x_ref[...] + mod_ref[0, 5:6, :] * acc_ref[...]


def ffn_dense(x2d, mod, g, w1, w3, w2, *, tm, tf, tiles_per_mod):
    m, d = x2d.shape
    dff = w1.shape[1]
    n_mod = mod.shape[0]
    mod_idx = (lambda i, j: (i // tiles_per_mod, 0, 0)) if n_mod > 1 else (lambda i, j: (0, 0, 0))
    return pl.pallas_call(
        _ffn_kernel,
        grid=(m // tm, dff // tf),
        in_specs=[
            pl.BlockSpec((tm, d), lambda i, j: (i, 0)),
            pl.BlockSpec((1, 6, d), mod_idx),
            pl.BlockSpec((1, d), lambda i, j: (0, 0)),
            pl.BlockSpec((d, tf), lambda i, j: (0, j)),
            pl.BlockSpec((d, tf), lambda i, j: (0, j)),
            pl.BlockSpec((tf, d), lambda i, j: (j, 0)),
        ],
        out_specs=pl.BlockSpec((tm, d), lambda i, j: (i, 0)),
        out_shape=jax.ShapeDtypeStruct((m, d), F32),
        scratch_shapes=[pltpu.VMEM((tm, d), BF16), pltpu.VMEM((tm, d), F32)],
        compiler_params=_cparams(2),
        name="ffn_dense",
    )(x2d, mod, g.reshape(1, d), w1, w3, w2)


def _top2(logits):
    t = logits.shape[0]
    lane = lax.broadcasted_iota(jnp.int32, (t, 128), 1).astype(F32)
    neg = jnp.float32(-jnp.inf)
    lg = jnp.where(lane < N_EXPERTS, logits, neg)
    v1 = jnp.max(lg, axis=-1, keepdims=True)
    i1 = jnp.min(jnp.where(lg == v1, lane, 128.0), axis=-1, keepdims=True)
    lg2 = jnp.where(lane == i1, neg, lg)
    v2 = jnp.max(lg2, axis=-1, keepdims=True)
    i2 = jnp.min(jnp.where(lg2 == v2, lane, 128.0), axis=-1, keepdims=True)
    e2 = jnp.exp(v2 - v1)
    return i1, i2, 1.0 / (1.0 + e2), e2 / (1.0 + e2)


R_E1, R_E2, R_W1, R_W2, R_RANK1, R_RANK2 = range(6)


def _route_kernel(x_ref, mod_ref, g_ref, r_ref, tri_ref, route_ref, cnt_ref, carry_ref):
    @pl.when(pl.program_id(0) == 0)
    def _():
        carry_ref[...] = jnp.zeros(carry_ref.shape, F32)

    t = x_ref.shape[0]
    h = _norm_mod(x_ref[...], g_ref[...], mod_ref[0, 3:4, :], mod_ref[0, 4:5, :])
    logits = jnp.dot(h, r_ref[...], preferred_element_type=F32, precision=lax.Precision.HIGHEST)
    i1, i2, w1, w2 = _top2(logits)
    lane = lax.broadcasted_iota(jnp.int32, (t, 128), 1).astype(F32)
    oh1 = jnp.where(lane == i1, 1.0, 0.0)
    oh2 = jnp.where(lane == i2, 1.0, 0.0)
    both = oh1 + oh2
    before = carry_ref[...] + jnp.dot(tri_ref[...], both.astype(BF16), preferred_element_type=F32)
    rank1 = jnp.sum(oh1 * before, axis=-1, keepdims=True)
    rank2 = jnp.sum(oh2 * before, axis=-1, keepdims=True)
    carry_ref[...] += jnp.sum(both, axis=0, keepdims=True)
    rec = jnp.zeros((t, 128), F32)
    for col, val in ((R_E1, i1), (R_E2, i2), (R_W1, w1), (R_W2, w2), (R_RANK1, rank1), (R_RANK2, rank2)):
        rec = jnp.where(lane == col, val, rec)
    route_ref[...] = rec
    cnt_ref[...] = carry_ref[...]


def moe_route(x2d, mod, g, router_pad, *, tm, tiles_per_mod):
    m, d = x2d.shape
    n_mod = mod.shape[0]
    mod_idx = (lambda i: (i // tiles_per_mod, 0, 0)) if n_mod > 1 else (lambda i: (0, 0, 0))
    tri = jnp.asarray(np.tril(np.ones((tm, tm), np.float32), -1), BF16)
    return pl.pallas_call(
        _route_kernel,
        grid=(m // tm,),
        in_specs=[
            pl.BlockSpec((tm, d), lambda i: (i, 0)),
            pl.BlockSpec((1, 6, d), mod_idx),
            pl.BlockSpec((1, d), lambda i: (0, 0)),
            pl.BlockSpec((d, 128), lambda i: (0, 0)),
            pl.BlockSpec((tm, tm), lambda i: (0, 0)),
        ],
        out_specs=[pl.BlockSpec((tm, 128), lambda i: (i, 0)), pl.BlockSpec((1, 128), lambda i: (0, 0))],
        out_shape=[jax.ShapeDtypeStruct((m, 128), F32), jax.ShapeDtypeStruct((1, 128), F32)],
        scratch_shapes=[pltpu.VMEM((1, 128), F32)],
        compiler_params=_cparams(1),
        name="moe_route",
    )(x2d, mod, g.reshape(1, d), router_pad, tri)


def _dispatch_kernel(pos_ref, x_ref, mod_ref, g_ref, xs_in_ref, xs_ref, h_ref, sem):
    del xs_in_ref
    t = x_ref.shape[0]
    h_ref[...] = _norm_mod(x_ref[...], g_ref[...], mod_ref[0, 3:4, :], mod_ref[0, 4:5, :])

    def row_copy(r, dst_row):
        return pltpu.make_async_copy(h_ref.at[pl.ds(r, 1), :], xs_ref.at[pl.ds(dst_row, 1), :], sem)

    def issue(r, carry):
        row_copy(r, pos_ref[0, 0, r]).start()
        row_copy(r, pos_ref[0, 0, t + r]).start()
        return carry

    lax.fori_loop(0, t, issue, 0)
    for _ in range(2):
        pltpu.make_async_copy(h_ref, xs_ref.at[pl.ds(0, t), :], sem).wait()


def moe_dispatch(x2d, mod, g, pos_tiles, n_rows, *, tm, tiles_per_mod):
    m, d = x2d.shape
    n_mod = mod.shape[0]
    mod_idx = (lambda i: (i // tiles_per_mod, 0, 0)) if n_mod > 1 else (lambda i: (0, 0, 0))
    return pl.pallas_call(
        _dispatch_kernel,
        grid=(m // tm,),
        in_specs=[
            pl.BlockSpec((1, 1, 2 * tm), lambda i: (i, 0, 0), memory_space=pltpu.SMEM),
            pl.BlockSpec((tm, d), lambda i: (i, 0)),
            pl.BlockSpec((1, 6, d), mod_idx),
            pl.BlockSpec((1, d), lambda i: (0, 0)),
            pl.BlockSpec(memory_space=pl.ANY),
        ],
        out_specs=pl.BlockSpec(memory_space=pl.ANY),
        out_shape=jax.ShapeDtypeStruct((n_rows, d), F32),
        scratch_shapes=[pltpu.VMEM((tm, d), F32), pltpu.SemaphoreType.DMA(())],
        input_output_aliases={4: 0},
        compiler_params=_cparams(1),
        name="moe_dispatch",
    )(pos_tiles, x2d, mod, g.reshape(1, d), jnp.zeros((n_rows, d), F32))


def _experts_kernel(te_ref, nv_ref, xs_ref, w1_ref, w3_ref, w2_ref, ys_ref, xb_ref, acc_ref):
    i = pl.program_id(0)
    j = pl.program_id(1)
    valid = i < nv_ref[0]

    @pl.when(jnp.logical_and(valid, j == 0))
    def _():
        xb_ref[...] = xs_ref[...].astype(BF16)
        acc_ref[...] = jnp.zeros(acc_ref.shape, F32)

    @pl.when(valid)
    def _():
        h = xb_ref[...]
        a = jnp.dot(h, w1_ref[0], preferred_element_type=F32)
        b = jnp.dot(h, w3_ref[0], preferred_element_type=F32)
        acc_ref[...] += jnp.dot((_silu(a) * b).astype(BF16), w2_ref[0], preferred_element_type=F32)

    @pl.when(jnp.logical_and(valid, j == pl.num_programs(1) - 1))
    def _():
        ys_ref[...] = acc_ref[...]

    @pl.when(jnp.logical_and(jnp.logical_not(valid), j == pl.num_programs(1) - 1))
    def _():
        ys_ref[...] = jnp.zeros(ys_ref.shape, F32)


def moe_experts_grouped(xs, tile_expert, n_valid, w1, w3, w2, *, tm, tf):
    n_rows, d = xs.shape
    dff = w1.shape[2]
    nf = dff // tf

    def w13_idx(i, j, te, nv):
        return (te[i], 0, jnp.where(i < nv[0], j, nf - 1))

    def w2_idx(i, j, te, nv):
        return (te[i], jnp.where(i < nv[0], j, nf - 1), 0)

    grid_spec = pltpu.PrefetchScalarGridSpec(
        num_scalar_prefetch=2,
        grid=(n_rows // tm, nf),
        in_specs=[
            pl.BlockSpec((tm, d), lambda i, j, te, nv: (jnp.minimum(i, nv[0] - 1), 0)),
            pl.BlockSpec((1, d, tf), w13_idx),
            pl.BlockSpec((1, d, tf), w13_idx),
            pl.BlockSpec((1, tf, d), w2_idx),
        ],
        out_specs=pl.BlockSpec((tm, d), lambda i, j, te, nv: (i, 0)),
        scratch_shapes=[pltpu.VMEM((tm, d), BF16), pltpu.VMEM((tm, d), F32)],
    )
    return pl.pallas_call(
        _experts_kernel,
        grid_spec=grid_spec,
        out_shape=jax.ShapeDtypeStruct((n_rows, d), F32),
        compiler_params=_cparams(2),
        name="moe_experts_grouped",
    )(tile_expert, n_valid, xs, w1, w3, w2)


def _combine_kernel(pos_ref, x_ref, mod_ref, rt_ref, ys_ref, o_ref, y1_ref, y2_ref, sem):
    t = x_ref.shape[0]

    def row_copy(src_row, dst_ref, r):
        return pltpu.make_async_copy(ys_ref.at[pl.ds(src_row, 1), :], dst_ref.at[pl.ds(r, 1), :], sem)

    def issue(r, carry):
        row_copy(pos_ref[0, 0, r], y1_ref, r).start()
        row_copy(pos_ref[0, 0, t + r], y2_ref, r).start()
        return carry

    lax.fori_loop(0, t, issue, 0)
    for dst_ref in (y1_ref, y2_ref):
        pltpu.make_async_copy(ys_ref.at[pl.ds(0, t), :], dst_ref, sem).wait()
    rt = rt_ref[...]
    mix = rt[:, R_W1:R_W1 + 1] * y1_ref[...] + rt[:, R_W2:R_W2 + 1] * y2_ref[...]
    o_ref[...] = x_ref[...] + mod_ref[0, 5:6, :] * mix


def moe_combine(x2d, mod, route, pos_tiles, ys, *, tm, tiles_per_mod):
    m, d = x2d.shape
    n_mod = mod.shape[0]
    mod_idx = (lambda i: (i // tiles_per_mod, 0, 0)) if n_mod > 1 else (lambda i: (0, 0, 0))
    return pl.pallas_call(
        _combine_kernel,
        grid=(m // tm,),
        in_specs=[
            pl.BlockSpec((1, 1, 2 * tm), lambda i: (i, 0, 0), memory_space=pltpu.SMEM),
            pl.BlockSpec((tm, d), lambda i: (i, 0)),
            pl.BlockSpec((1, 6, d), mod_idx),
            pl.BlockSpec((tm, 128), lambda i: (i, 0)),
            pl.BlockSpec(memory_space=pl.ANY),
        ],
        out_specs=pl.BlockSpec((tm, d), lambda i: (i, 0)),
        out_shape=jax.ShapeDtypeStruct((m, d), F32),
        scratch_shapes=[pltpu.VMEM((tm, d), F32), pltpu.VMEM((tm, d), F32), pltpu.SemaphoreType.DMA(())],
        compiler_params=_cparams(1),
        name="moe_combine",
    )(pos_tiles, x2d, mod, route, ys)


def _pos_tiles(pos1, pos2, tm):
    n = pos1.shape[0] // tm
    return jnp.concatenate([pos1.reshape(n, 1, tm), pos2.reshape(n, 1, tm)], axis=2)


def moe_sparse(x2d, mod, g, router_pad, w1, w3, w2, *, rows_per_mod):
    m, d = x2d.shape
    tr, td, tc, te = MOE_ROUTE_TM, MOE_DISPATCH_TM, MOE_COMBINE_TM, MOE_EXPERT_TM
    tiles_per_seq_row = rows_per_mod
    route, cnt = moe_route(x2d, mod, g, router_pad, tm=tr, tiles_per_mod=tiles_per_seq_row // tr)
    counts = cnt[0, 0:N_EXPERTS].astype(jnp.int32)
    group = ((counts + te - 1) // te) * te
    ends = jnp.cumsum(group)
    starts = ends - group
    e1 = route[:, R_E1].astype(jnp.int32)
    e2 = route[:, R_E2].astype(jnp.int32)
    pos1 = starts[e1] + route[:, R_RANK1].astype(jnp.int32)
    pos2 = starts[e2] + route[:, R_RANK2].astype(jnp.int32)
    n_rows = 2 * m + N_EXPERTS * te
    n_tiles = n_rows // te
    tile_expert = jnp.minimum(
        jnp.searchsorted(ends, jnp.arange(n_tiles, dtype=jnp.int32) * te, side="right"), N_EXPERTS - 1
    ).astype(jnp.int32)
    n_valid = (ends[-1:] // te).astype(jnp.int32)
    xs = moe_dispatch(x2d, mod, g, _pos_tiles(pos1, pos2, td), n_rows, tm=td, tiles_per_mod=tiles_per_seq_row // td)
    ys = moe_experts_grouped(xs, tile_expert, n_valid, w1, w3, w2, tm=te, tf=MOE_EXPERT_TF)
    return moe_combine(x2d, mod, route, _pos_tiles(pos1, pos2, tc), ys, tm=tc, tiles_per_mod=tiles_per_seq_row // tc)


def _permute_w_in(w):
    f, c, p, q, kv, gts = w[:, 0:256], w[:, 256:768], w[:, 768:1024], w[:, 1024:1536], w[:, 1536:1792], w[:, 1792:]
    return jnp.concatenate([gts, q, c, p, f, kv], axis=1).astype(BF16)


def _layer_weights(layer, w_br_fourier, conv_dw, conv_b, conv_norm_g, w_br_conv, pool_w, pool_scale, w_br_pool,
                   w_br_attn, w_out):
    pw = jax.scipy.linalg.block_diag(*[pool_w[layer, i] for i in range(len(POOL_WINDOWS))])
    return {
        "wf": w_br_fourier[layer].astype(BF16), "wc": w_br_conv[layer].astype(BF16),
        "wp": w_br_pool[layer].astype(BF16), "wa": w_br_attn[layer].astype(BF16), "wo": w_out[layer].astype(BF16),
        "dw": conv_dw[layer], "cb": conv_b[layer].reshape(1, CONV_W), "cg": conv_norm_g[layer].reshape(1, CONV_W),
        "pw": pw.astype(BF16), "ps": pool_scale[layer].reshape(1, POOL_W),
    }


def kernel(x, c, ctx, c_ctx, w_mod, b_mod, norm1_g, norm2_g, w_in, w_br_fourier, conv_dw, conv_b, conv_norm_g,
           w_br_conv, pool_w, pool_scale, w_br_pool, q_norm_g, k_norm_g, w_br_attn, w_out, ffn_w1, ffn_w3, ffn_w2,
           moe_router, moe_w1, moe_w3, moe_w2):
    batch, seq, d = x.shape
    ctx_len = ctx.shape[1]
    depth = w_in.shape[0]
    rope = rope_tables(seq)

    c_rows = jnp.zeros((8, d), F32).at[0:batch].set(c).at[batch].set(c_ctx)
    mods = modulation_all(c_rows, w_mod, b_mod).reshape(depth, 8, 6, d)

    xl = x.reshape(batch * seq, d)
    xc = ctx.reshape(batch * ctx_len, d)
    for layer in range(depth):
        is_last = layer == depth - 1
        mod_l = mods[layer, 0:batch]
        mod_c = mods[layer, batch:batch + 1]
        w_in_l = _permute_w_in(w_in[layer])
        lw = _layer_weights(layer, w_br_fourier, conv_dw, conv_b, conv_norm_g, w_br_conv, pool_w, pool_scale,
                            w_br_pool, w_br_attn, w_out)

        proj_c = input_projection(xc, mod_c, norm1_g[layer], w_in_l, tm=256, tiles_per_mod=1)
        qc, ktc, vc = qkv_prepare(proj_c, q_norm_g[layer], k_norm_g[layer], None, batch=batch, seq=ctx_len, tp=256)

        proj = input_projection(xl, mod_l, norm1_g[layer], w_in_l, tm=512, tiles_per_mod=seq // 512)
        q, kt, v = qkv_prepare(proj, q_norm_g[layer], k_norm_g[layer], rope, batch=batch, seq=seq, tp=512)
        spread = (2.0 * 1.02 * HEAD_DIM * Q_SCALE) * jnp.max(jnp.abs(q_norm_g[layer])) * jnp.max(
            jnp.abs(k_norm_g[layer]))
        attn = lax.cond(
            spread < STALE_MAX_EXP_LIMIT,
            lambda ops: attention_stale_max(*ops, batch=batch, seq_q=seq, tq=256, tk=1024),
            lambda ops: attention(*ops, batch=batch, seq_q=seq, tq=256, tk=1024),
            (q, kt, v, ktc, vc))
        yf = fourier_mix(proj[:, P_OFF_F:P_OFF_F + FOURIER_W], batch=batch, seq=seq, n1=64, n2=seq // 64)
        xl = merge_branches(xl, mod_l, proj, yf, attn, lw, seq=seq, t=512)

        if not is_last:
            attn_c = attention(qc, None, None, ktc, vc, batch=batch, seq_q=ctx_len, tq=256, tk=ctx_len)
            yf_c = fourier_mix(proj_c[:, P_OFF_F:P_OFF_F + FOURIER_W], batch=batch, seq=ctx_len, n1=16,
                               n2=ctx_len // 16)
            xc = merge_branches(xc, mod_c, proj_c, yf_c, attn_c, lw, seq=ctx_len, t=256)

        j = layer // 2
        if layer % 2 == 0:
            w1, w3, w2 = ffn_w1[j].astype(BF16), ffn_w3[j].astype(BF16), ffn_w2[j].astype(BF16)
            xl = ffn_dense(xl, mod_l, norm2_g[layer], w1, w3, w2, tm=512, tf=1408, tiles_per_mod=seq // 512)
            if not is_last:
                xc = ffn_dense(xc, mod_c, norm2_g[layer], w1, w3, w2, tm=256, tf=1408, tiles_per_mod=1)
        else:
            w1, w3, w2 = moe_w1[j].astype(BF16), moe_w3[j].astype(BF16), moe_w2[j].astype(BF16)
            router_pad = jnp.zeros((d, 128), F32).at[:, 0:N_EXPERTS].set(moe_router[j])
            xl = moe_sparse(xl, mod_l, norm2_g[layer], router_pad, w1, w3, w2, rows_per_mod=seq)
            if not is_last:
                xc = moe_sparse(xc, mod_c, norm2_g[layer], router_pad, w1, w3, w2, rows_per_mod=batch * ctx_len)
    return xl.reshape(batch, seq, d)
```

```python
import functools
import math

import numpy as np
import jax
import jax.numpy as jnp
from jax import lax
from jax.experimental import pallas as pl
from jax.experimental.pallas import tpu as pltpu

F32 = jnp.float32
BF16 = jnp.bfloat16

D_MODEL = 1024
GRID_W = 64
EPS = 1e-6
FOURIER_GW = 64
FOURIER_W = 256
CONV_W = 256
CONV_K = 31
CONV_HALF = CONV_K // 2
POOL_WINDOWS = (2, 4, 8, 16)
POOL_GW = 64
POOL_W = 256
HEAD_DIM = 64
N_HEADS = 8
N_KV_HEADS = 2
Q_PER_KV = 4
Q_W = 512
KV_W = 128
ROPE_THETA = 10000.0
N_EXPERTS = 8
IN_W = 5888

P_OFF_G = 0
P_OFF_Q = 4096
P_OFF_CP = 4608
P_OFF_F = 5376
P_OFF_KV = 5632
CP_W = 2 * CONV_W + POOL_W

Q_SCALE = (HEAD_DIM ** -0.5) * math.log2(math.e)

STALE_MAX_EXP_LIMIT = 64.0

MOE_ROUTE_TM = 512
MOE_DISPATCH_TM = 512
MOE_COMBINE_TM = 256
MOE_EXPERT_TM = 512
MOE_EXPERT_TF = 1792
DMA_ISSUE_UNROLL = 8

CONV_ROWS = 64
HALO = 16
VMEM_LIMIT = 56 * 1024 * 1024


def _cparams(n_axes):
    return pltpu.CompilerParams(dimension_semantics=("arbitrary",) * n_axes, vmem_limit_bytes=VMEM_LIMIT)


def _sigmoid(v):
    return 0.5 * jnp.tanh(0.5 * v) + 0.5


def _silu(v):
    return v * _sigmoid(v)


def _norm_mod(x, g, shift, scale):
    ms = jnp.mean(x * x, axis=-1, keepdims=True)
    return x * lax.rsqrt(ms + EPS) * g * (1.0 + scale) + shift


def _mod_kernel(c_ref, w_ref, b_ref, o_ref):
    s = _silu(c_ref[...])
    o_ref[0] = jnp.dot(s, w_ref[0], preferred_element_type=F32, precision=lax.Precision.HIGHEST) + b_ref[0]


def modulation_all(c_rows, w_mod, b_mod):
    n_layers, d, n = w_mod.shape
    tn = 1536
    return pl.pallas_call(
        _mod_kernel,
        grid=(n_layers, n // tn),
        in_specs=[
            pl.BlockSpec((8, d), lambda l, j: (0, 0)),
            pl.BlockSpec((1, d, tn), lambda l, j: (l, 0, j)),
            pl.BlockSpec((1, 1, tn), lambda l, j: (l, 0, j)),
        ],
        out_specs=pl.BlockSpec((1, 8, tn), lambda l, j: (l, 0, j)),
        out_shape=jax.ShapeDtypeStruct((n_layers, 8, n), F32),
        compiler_params=_cparams(2),
        name="modulation",
    )(c_rows, w_mod, b_mod.reshape(n_layers, 1, n))


def _inproj_kernel(x_ref, mod_ref, g_ref, w_ref, o_ref, *, chunks):
    h = _norm_mod(x_ref[...], g_ref[...], mod_ref[0, 0:1, :], mod_ref[0, 1:2, :]).astype(BF16)
    for c0, cw in chunks:
        o_ref[:, c0:c0 + cw] = jnp.dot(h, w_ref[:, c0:c0 + cw], preferred_element_type=F32).astype(o_ref.dtype)


def input_projection(x2d, mod, g, w_bf16, *, tm, tiles_per_mod):
    m, d = x2d.shape
    n = w_bf16.shape[1]
    chunks = tuple((c0, min(512, n - c0)) for c0 in range(0, n, 512))
    n_mod = mod.shape[0]
    mod_idx = (lambda i: (i // tiles_per_mod, 0, 0)) if n_mod > 1 else (lambda i: (0, 0, 0))
    return pl.pallas_call(
        functools.partial(_inproj_kernel, chunks=chunks),
        grid=(m // tm,),
        in_specs=[
            pl.BlockSpec((tm, d), lambda i: (i, 0)),
            pl.BlockSpec((1, 6, d), mod_idx),
            pl.BlockSpec((1, d), lambda i: (0, 0)),
            pl.BlockSpec((d, n), lambda i: (0, 0), pipeline_mode=pl.Buffered(1)),
        ],
        out_specs=pl.BlockSpec((tm, n), lambda i: (i, 0)),
        out_shape=jax.ShapeDtypeStruct((m, n), BF16),
        compiler_params=_cparams(1),
        name="input_projection",
    )(x2d, mod, g.reshape(1, d), w_bf16)


def _seg_sum64(v, ones_bd):
    hi = v.astype(BF16)
    lo = (v - hi.astype(F32)).astype(BF16)
    return (jnp.dot(hi, ones_bd, preferred_element_type=F32) + jnp.dot(lo, ones_bd, preferred_element_type=F32))


def _head_norm_rope(x, g, ones_bd, cos, sin, low_mask):
    y = x * lax.rsqrt(_seg_sum64(x * x, ones_bd) * (1.0 / HEAD_DIM) + EPS) * g
    if cos is None:
        return y
    partner = jnp.where(low_mask, pltpu.roll(y, 128 - 16, axis=1), pltpu.roll(y, 16, axis=1))
    return y * cos + partner * sin


def _prep_kernel(*refs, use_rope):
    if use_rope:
        q_ref, kv_ref, gq_ref, gk_ref, ones_ref, cos_ref, sin_ref, qo_ref, kt_ref, v_ref = refs
        cos, sin = cos_ref[...], sin_ref[...]
    else:
        q_ref, kv_ref, gq_ref, gk_ref, ones_ref, qo_ref, kt_ref, v_ref = refs
        cos = sin = None
    t = q_ref.shape[0]
    ones_bd = ones_ref[...]
    lane = lax.broadcasted_iota(jnp.int32, (t, 128), 1)
    low_mask = (lane % 32) < 16
    gq = gq_ref[...]
    for c in range(Q_W // 128):
        xq = q_ref[:, 128 * c:128 * (c + 1)].astype(F32)
        yq = _head_norm_rope(xq, gq, ones_bd, cos, sin, low_mask) * Q_SCALE
        qo_ref[0, c // 2, 128 * (c % 2):128 * (c % 2 + 1), :] = yq.T.astype(BF16)
    xk = kv_ref[:, 0:128].astype(F32)
    yk = _head_norm_rope(xk, gk_ref[...], ones_bd, cos, sin, low_mask)
    ykr = pltpu.roll(yk, 64, axis=1)
    first = lane < 64
    k0 = jnp.where(first, yk, ykr).astype(BF16)
    k1 = jnp.where(first, ykr, yk).astype(BF16)
    kt_ref[0, 0] = jnp.concatenate([k0, k0], axis=1)
    kt_ref[0, 1] = jnp.concatenate([k1, k1], axis=1)
    vt = kv_ref[:, 128:256].astype(F32).T
    ones = jnp.ones((HEAD_DIM, t), F32)
    for h in range(N_KV_HEADS):
        v_ref[0, h] = jnp.concatenate([vt[64 * h:64 * (h + 1), :], ones], axis=0).astype(BF16)


def qkv_prepare(proj, gq, gk, rope, *, batch, seq, tp):
    m = proj.shape[0]
    tps = seq // tp
    use_rope = rope is not None
    ones_bd = jnp.asarray(np.kron(np.eye(2, dtype=np.float32), np.ones((64, 64), np.float32)), BF16)
    gq2 = jnp.tile(gq, 2).reshape(1, 128)
    gk2 = jnp.tile(gk, 2).reshape(1, 128)
    in_specs = [
        pl.BlockSpec((tp, Q_W), lambda i: (i, P_OFF_Q // Q_W)),
        pl.BlockSpec((tp, 256), lambda i: (i, P_OFF_KV // 256)),
        pl.BlockSpec((1, 128), lambda i: (0, 0)),
        pl.BlockSpec((1, 128), lambda i: (0, 0)),
        pl.BlockSpec((128, 128), lambda i: (0, 0)),
    ]
    args = [proj, proj, gq2, gk2, ones_bd]
    if use_rope:
        in_specs += [pl.BlockSpec((tp, 128), lambda i: (i % tps, 0))] * 2
        args += list(rope)
    return pl.pallas_call(
        functools.partial(_prep_kernel, use_rope=use_rope),
        grid=(m // tp,),
        in_specs=in_specs,
        out_specs=[
            pl.BlockSpec((1, N_KV_HEADS, 256, tp), lambda i: (i // tps, 0, 0, i % tps)),
            pl.BlockSpec((1, N_KV_HEADS, tp, 256), lambda i: (i // tps, 0, i % tps, 0)),
            pl.BlockSpec((1, N_KV_HEADS, 128, tp), lambda i: (i // tps, 0, 0, i % tps)),
        ],
        out_shape=[
            jax.ShapeDtypeStruct((batch, N_KV_HEADS, 256, seq), BF16),
            jax.ShapeDtypeStruct((batch, N_KV_HEADS, seq, 256), BF16),
            jax.ShapeDtypeStruct((batch, N_KV_HEADS, 128, seq), BF16),
        ],
        compiler_params=_cparams(1),
        name="qkv_prepare",
    )(*args)


def rope_tables(seq):
    n_freq = HEAD_DIM // 4
    freqs = ROPE_THETA ** (-jnp.arange(n_freq, dtype=F32) / n_freq)
    t = jnp.arange(seq)
    row = (t // GRID_W).astype(F32)
    col = (t % GRID_W).astype(F32)
    ang_r = row[:, None] * freqs
    ang_c = col[:, None] * freqs
    cos = jnp.concatenate([jnp.cos(ang_r)] * 2 + [jnp.cos(ang_c)] * 2, axis=1)
    sin = jnp.concatenate([-jnp.sin(ang_r), jnp.sin(ang_r), -jnp.sin(ang_c), jnp.sin(ang_c)], axis=1)
    return jnp.tile(cos, (1, 2)), jnp.tile(sin, (1, 2))


def _attn_kernel(*refs, tq, tk, nk, tail):
    refs = list(refs)
    qt_ref = refs.pop(0)
    k_ref, vt_ref = (refs.pop(0), refs.pop(0)) if nk else (None, None)
    kc_ref, vtc_ref = (refs.pop(0), refs.pop(0)) if tail else (None, None)
    o_ref, qs_ref, s0, s1, p0, p1, a0, a1, mx0, mx1, m_ref, acc_ref = refs
    s_bufs, p_bufs, a_bufs, mx_bufs = (s0, s1), (p0, p1), (a0, a1), (mx0, mx1)
    n_blocks = nk + (1 if tail else 0)

    _attn_stack_queries(qt_ref, qs_ref, tq)
    m_ref[...] = jnp.full(m_ref.shape, -jnp.inf, F32)
    acc_ref[...] = jnp.zeros(acc_ref.shape, F32)

    def block(t):
        if isinstance(t, int) and t >= nk:
            return kc_ref[0, 0], vtc_ref[0, 0], tail
        off = t * tk if isinstance(t, int) else pl.multiple_of(t * tk, tk)
        return k_ref[0, 0, pl.ds(off, tk), :], vt_ref[0, 0, :, pl.ds(off, tk)], tk

    def scores(t, slot):
        k_rows, _, n = block(t)
        s = jnp.dot(k_rows, qs_ref[...], preferred_element_type=F32)
        s_bufs[slot][0:n, :] = s
        mx_bufs[slot][...] = jnp.max(s, axis=0, keepdims=True)

    def numerators(n, slot):
        s_ref, p_ref, a_ref = s_bufs[slot], p_bufs[slot], a_bufs[slot]
        for c0 in range(0, Q_PER_KV * tq, 128):
            cols = slice(c0, c0 + 128)
            m_old = m_ref[:, cols]
            m_new = jnp.maximum(m_old, mx_bufs[slot][:, cols])
            a_ref[:, cols] = jnp.exp2(m_old - m_new)
            p_ref[0:n, cols] = jnp.exp2(s_ref[0:n, cols] - m_new).astype(BF16)
            m_ref[:, cols] = m_new

    def weighted_sum(t, slot):
        _, vt, n = block(t)
        pv = jnp.dot(vt, p_bufs[slot][0:n, :], preferred_element_type=F32)
        acc_ref[...] = a_bufs[slot][...] * acc_ref[...] + pv

    def rows_of(t):
        return tk if t < nk else tail

    def step(t, slot, n_mid):
        scores(t, slot)
        numerators(n_mid, 1 - slot)
        weighted_sum(t - 2, slot)

    scores(0, 0)
    if n_blocks > 1:
        scores(1, 1)
        numerators(rows_of(0), 0)
        n_pairs = max(nk - 2, 0) // 2

        def pair(i, carry):
            t = 2 + 2 * i
            step(t, 0, tk)
            step(t + 1, 1, tk)
            return carry

        if n_pairs:
            lax.fori_loop(0, n_pairs, pair, 0)
        for t in range(2 + 2 * n_pairs, n_blocks):
            step(t, t % 2, rows_of(t - 1))
        last = n_blocks - 1
        numerators(rows_of(last), last % 2)
        weighted_sum(last - 1, (last - 1) % 2)
        weighted_sum(last, last % 2)
    else:
        numerators(rows_of(0), 0)
        weighted_sum(0, 0)

    _attn_write_output(acc_ref, o_ref, tq)


def _attn_write_output(acc_ref, o_ref, tq):
    acc = acc_ref[...]
    ot = acc[0:HEAD_DIM, :] / acc[HEAD_DIM:2 * HEAD_DIM, :]
    for half in range(2):
        pair_t = jnp.concatenate([ot[:, (2 * half) * tq:(2 * half + 1) * tq],
                                  ot[:, (2 * half + 1) * tq:(2 * half + 2) * tq]], axis=0)
        o_ref[:, 128 * half:128 * (half + 1)] = pair_t.T.astype(o_ref.dtype)


def _attn_stack_queries(qt_ref, qs_ref, tq):
    row_group = lax.broadcasted_iota(jnp.int32, (256, tq), 0) // HEAD_DIM
    qt = qt_ref[0, 0]
    for g in range(Q_PER_KV):
        qs_ref[:, g * tq:(g + 1) * tq] = jnp.where(row_group == g, qt, jnp.zeros_like(qt))


def _attn_stale_max_kernel(qt_ref, k_ref, vt_ref, kc_ref, vtc_ref, o_ref, qs_ref, p0, p1, f0, f1, m_ref, acc_ref,
                           *, tq, tk, nk):
    p_bufs, f_bufs = (p0, p1), (f0, f1)
    _attn_stack_queries(qt_ref, qs_ref, tq)

    s = jnp.dot(kc_ref[0, 0], qs_ref[...], preferred_element_type=F32)
    m0 = jnp.max(s, axis=0, keepdims=True)
    m_ref[...] = m0
    acc_ref[...] = jnp.dot(vtc_ref[0, 0], jnp.exp2(s - m0).astype(BF16), preferred_element_type=F32)

    def numerators(t, slot):
        off = t * tk if isinstance(t, int) else pl.multiple_of(t * tk, tk)
        s = jnp.dot(k_ref[0, 0, pl.ds(off, tk), :], qs_ref[...], preferred_element_type=F32)
        m_old = m_ref[...]
        p_bufs[slot][...] = jnp.exp2(s - m_old).astype(BF16)
        m_new = jnp.maximum(m_old, jnp.max(s, axis=0, keepdims=True))
        f_bufs[slot][...] = jnp.exp2(m_old - m_new)
        m_ref[...] = m_new

    def weighted_sum(t, slot):
        off = t * tk if isinstance(t, int) else pl.multiple_of(t * tk, tk)
        pv = jnp.dot(vt_ref[0, 0, :, pl.ds(off, tk)], p_bufs[slot][...], preferred_element_type=F32)
        acc_ref[...] = (acc_ref[...] + pv) * f_bufs[slot][...]

    def step(t, slot):
        numerators(t, slot)
        weighted_sum(t - 1, 1 - slot)

    numerators(0, 0)
    n_pairs = (nk - 1) // 2

    def pair(i, carry):
        t = 1 + 2 * i
        step(t, 1)
        step(t + 1, 0)
        return carry

    if n_pairs:
        lax.fori_loop(0, n_pairs, pair, 0)
    for t in range(1 + 2 * n_pairs, nk):
        step(t, t % 2)
    weighted_sum(nk - 1, (nk - 1) % 2)
    _attn_write_output(acc_ref, o_ref, tq)


def attention_stale_max(qt, k4, vt1, k4_tail, vt1_tail, *, batch, seq_q, tq, tk):
    nq = seq_q // tq
    lanes = Q_PER_KV * tq
    lk = k4.shape[2]
    tail = k4_tail.shape[2]
    return pl.pallas_call(
        functools.partial(_attn_stale_max_kernel, tq=tq, tk=tk, nk=lk // tk),
        grid=(batch, N_KV_HEADS, nq),
        in_specs=[
            pl.BlockSpec((1, 1, 256, tq), lambda b, h, i: (b, h, 0, i)),
            pl.BlockSpec((1, 1, lk, 256), lambda b, h, i: (b, h, 0, 0)),
            pl.BlockSpec((1, 1, 128, lk), lambda b, h, i: (b, h, 0, 0)),
            pl.BlockSpec((1, 1, tail, 256), lambda b, h, i: (b, h, 0, 0)),
            pl.BlockSpec((1, 1, 128, tail), lambda b, h, i: (b, h, 0, 0)),
        ],
        out_specs=pl.BlockSpec((tq, 256), lambda b, h, i: (b * nq + i, h)),
        out_shape=jax.ShapeDtypeStruct((batch * seq_q, Q_W), BF16),
        scratch_shapes=[
            pltpu.VMEM((256, lanes), BF16),
            pltpu.VMEM((tk, lanes), BF16), pltpu.VMEM((tk, lanes), BF16),
            pltpu.VMEM((1, lanes), F32), pltpu.VMEM((1, lanes), F32),
            pltpu.VMEM((1, lanes), F32),
            pltpu.VMEM((2 * HEAD_DIM, lanes), F32),
        ],
        compiler_params=_cparams(3),
        name="attention_stale_max",
    )(qt, k4, vt1, k4_tail, vt1_tail)


def attention(qt, k4, vt1, k4_tail, vt1_tail, *, batch, seq_q, tq, tk):
    nq = seq_q // tq
    lanes = Q_PER_KV * tq
    nk = 0 if k4 is None else k4.shape[2] // tk
    tail = 0 if k4_tail is None else k4_tail.shape[2]
    buf_rows = max(tk if nk else 0, tail)
    in_specs = [pl.BlockSpec((1, 1, 256, tq), lambda b, h, i: (b, h, 0, i))]
    args = [qt]
    if nk:
        lk = k4.shape[2]
        in_specs += [pl.BlockSpec((1, 1, lk, 256), lambda b, h, i: (b, h, 0, 0)),
                     pl.BlockSpec((1, 1, 128, lk), lambda b, h, i: (b, h, 0, 0))]
        args += [k4, vt1]
    if tail:
        in_specs += [pl.BlockSpec((1, 1, tail, 256), lambda b, h, i: (b, h, 0, 0)),
                     pl.BlockSpec((1, 1, 128, tail), lambda b, h, i: (b, h, 0, 0))]
        args += [k4_tail, vt1_tail]
    return pl.pallas_call(
        functools.partial(_attn_kernel, tq=tq, tk=tk, nk=nk, tail=tail),
        grid=(batch, N_KV_HEADS, nq),
        in_specs=in_specs,
        out_specs=pl.BlockSpec((tq, 256), lambda b, h, i: (b * nq + i, h)),
        out_shape=jax.ShapeDtypeStruct((batch * seq_q, Q_W), BF16),
        scratch_shapes=[
            pltpu.VMEM((256, lanes), BF16),
            pltpu.VMEM((buf_rows, lanes), F32), pltpu.VMEM((buf_rows, lanes), F32),
            pltpu.VMEM((buf_rows, lanes), BF16), pltpu.VMEM((buf_rows, lanes), BF16),
            pltpu.VMEM((1, lanes), F32), pltpu.VMEM((1, lanes), F32),
            pltpu.VMEM((1, lanes), F32), pltpu.VMEM((1, lanes), F32),
            pltpu.VMEM((1, lanes), F32),
            pltpu.VMEM((2 * HEAD_DIM, lanes), F32),
        ],
        compiler_params=_cparams(3),
        name="attention",
    )(*args)


def _dft_cs(n):
    k = np.arange(n)
    ang = 2.0 * np.pi * ((k[:, None] * k[None, :]) % n) / n
    return np.cos(ang), np.sin(ang)


def _fft1_kernel(x_ref, f_ref, c_ref, s_ref, o_ref, *, n1):
    y = jnp.dot(f_ref[...], x_ref[0], preferred_element_type=F32)
    yr, yi = y[:n1], y[n1:]
    c, s = c_ref[...], s_ref[...]
    o_ref[0, 0] = (yr * c + yi * s).astype(o_ref.dtype)
    o_ref[0, 1] = (yi * c - yr * s).astype(o_ref.dtype)


def _fft2_kernel(y_ref, f_ref, bc_ref, bs_ref, o_ref, *, n2, kb):
    for j in range(kb):
        y2 = jnp.concatenate([y_ref[0, 0, j], y_ref[0, 1, j]], axis=0)
        x2 = jnp.dot(f_ref[...], y2, preferred_element_type=F32)
        xr = x2[:n2].astype(BF16)
        xi = x2[n2:].astype(BF16)
        z = (jnp.dot(xr, bc_ref[...], preferred_element_type=F32) + jnp.dot(xi, bs_ref[...], preferred_element_type=F32))
        o_ref[0, j] = z.astype(o_ref.dtype)


def fourier_mix(u, *, batch, seq, n1, n2):
    cw = u.shape[1]
    lanes = n2 * cw
    tl = min(lanes, 4096)
    c1, s1 = _dft_cs(n1)
    f1 = jnp.asarray(np.concatenate([c1, -s1], axis=0), BF16)
    k1 = np.arange(n1)[:, None]
    t2 = np.arange(n2)[None, :]
    ang = 2.0 * np.pi * ((k1 * t2) % seq) / seq
    twc = jnp.asarray(np.repeat(np.cos(ang), cw, axis=1), F32)
    tws = jnp.asarray(np.repeat(np.sin(ang), cw, axis=1), F32)
    x2 = u.reshape(batch, n1, lanes)
    yp = pl.pallas_call(
        functools.partial(_fft1_kernel, n1=n1),
        grid=(batch, lanes // tl),
        in_specs=[
            pl.BlockSpec((1, n1, tl), lambda b, j: (b, 0, j)),
            pl.BlockSpec((2 * n1, n1), lambda b, j: (0, 0)),
            pl.BlockSpec((n1, tl), lambda b, j: (0, j)),
            pl.BlockSpec((n1, tl), lambda b, j: (0, j)),
        ],
        out_specs=pl.BlockSpec((1, 2, n1, tl), lambda b, j: (b, 0, 0, j)),
        out_shape=jax.ShapeDtypeStruct((batch, 2, n1, lanes), BF16),
        compiler_params=_cparams(2),
        name="fft_stage1",
    )(x2, f1, twc, tws)

    c2, s2 = _dft_cs(n2)
    f2 = jnp.asarray(np.block([[c2, s2], [-s2, c2]]), BF16)
    cg, sg = _dft_cs(FOURIER_GW)
    norm = 1.0 / math.sqrt(seq * FOURIER_GW)
    bdc = jnp.asarray(np.kron(np.eye(cw // FOURIER_GW), cg) * norm, BF16)
    bds = jnp.asarray(np.kron(np.eye(cw // FOURIER_GW), sg) * norm, BF16)
    kb = min(n1, 16)
    y5 = yp.reshape(batch, 2, n1, n2, cw)
    z = pl.pallas_call(
        functools.partial(_fft2_kernel, n2=n2, kb=kb),
        grid=(batch, n1 // kb),
        in_specs=[
            pl.BlockSpec((1, 2, kb, n2, cw), lambda b, j: (b, 0, j, 0, 0)),
            pl.BlockSpec((2 * n2, 2 * n2), lambda b, j: (0, 0)),
            pl.BlockSpec((cw, cw), lambda b, j: (0, 0)),
            pl.BlockSpec((cw, cw), lambda b, j: (0, 0)),
        ],
        out_specs=pl.BlockSpec((1, kb, n2, cw), lambda b, j: (b, j, 0, 0)),
        out_shape=jax.ShapeDtypeStruct((batch, n1, n2, cw), BF16),
        compiler_params=_cparams(2),
        name="fft_stage2",
    )(y5, f2, bdc, bds)
    return z.transpose(0, 2, 1, 3).reshape(batch * seq, cw)


def _merge_kernel(x_ref, mod_ref, gate_ref, cp_ref, cpp_ref, cpn_ref, yf_ref, at_ref,
                  wf_ref, wc_ref, wp_ref, wa_ref, wo_ref, dw_ref, cb_ref, cg_ref, pw_ref, ps_ref,
                  o_ref, ybuf, xbuf, ysh, cacc, *, t, tps, seq):
    i = pl.program_id(0)
    pos_tile = i % tps
    keep_prev = jnp.where(pos_tile != 0, 1.0, 0.0).astype(F32)
    keep_next = jnp.where(pos_tile != tps - 1, 1.0, 0.0).astype(F32)

    def glu(blk):
        return blk[:, 0:CONV_W].astype(F32) * _sigmoid(blk[:, CONV_W:2 * CONV_W].astype(F32))

    cp, cpp, cpn = cp_ref[...], cpp_ref[...], cpn_ref[...]
    ybuf[0:HALO, :] = glu(cpp) * keep_prev
    ybuf[HALO:HALO + t, :] = glu(cp)
    ybuf[HALO + t:HALO + t + HALO, :] = glu(cpn) * keep_next
    xbuf[0:HALO, :] = cpp[:, 2 * CONV_W:].astype(F32) * keep_prev
    xbuf[HALO:HALO + t, :] = cp[:, 2 * CONV_W:].astype(F32)
    xbuf[HALO + t:HALO + t + HALO, :] = cpn[:, 2 * CONV_W:].astype(F32) * keep_next

    n_sh = t + 2 * HALO - 8
    for b in range(1, 8):
        ysh[b - 1, 0:n_sh, :] = ybuf[pl.ds(b, n_sh), :]
    for r0 in range(0, t, CONV_ROWS):
        part = jnp.zeros((CONV_ROWS, CONV_W), F32)
        for k in range(CONV_K):
            a, b = divmod(HALO - CONV_HALF + k, 8)
            src = ybuf if b == 0 else ysh.at[b - 1]
            part = part + dw_ref[k:k + 1, :] * src[8 * a + r0:8 * a + r0 + CONV_ROWS, :]
        cacc[r0:r0 + CONV_ROWS, :] = part + cb_ref[...]
    acc = cacc[...]
    ms = jnp.mean(acc * acc, axis=-1, keepdims=True)
    conv_out = _silu(acc * lax.rsqrt(ms + EPS) * cg_ref[...]).astype(BF16)

    def xs(d):
        return xbuf[pl.ds(HALO + d, t), :]

    x0 = xs(0)
    s2 = xs(-1) + x0
    s4 = s2 + xs(-2) + xs(1)
    s8 = s4 + xs(-4) + xs(-3) + xs(2) + xs(3)
    s16 = s8 + xs(-8) + xs(-7) + xs(-6) + xs(-5) + xs(4) + xs(5) + xs(6) + xs(7)
    grp = lax.broadcasted_iota(jnp.int32, (t, POOL_W), 1) // POOL_GW
    pos = pos_tile * t + lax.broadcasted_iota(jnp.int32, (t, POOL_W), 0)
    half = jnp.where(grp == 0, 1, jnp.where(grp == 1, 2, jnp.where(grp == 2, 4, 8)))
    cnt = (jnp.minimum(pos + half, seq) - jnp.maximum(pos - half, 0)).astype(F32)
    wsum = jnp.where(grp == 0, s2, jnp.where(grp == 1, s4, jnp.where(grp == 2, s8, s16)))
    pool_in = (wsum / cnt - x0).astype(BF16)
    pool_out = (jnp.dot(pool_in, pw_ref[...], preferred_element_type=F32) * ps_ref[...]).astype(BF16)

    def gate(b):
        return _sigmoid(gate_ref[:, b * D_MODEL:(b + 1) * D_MODEL].astype(F32))

    merged = gate(0) * jnp.dot(yf_ref[...], wf_ref[...], preferred_element_type=F32)
    merged = merged + gate(1) * jnp.dot(conv_out, wc_ref[...], preferred_element_type=F32)
    merged = merged + gate(2) * jnp.dot(pool_out, wp_ref[...], preferred_element_type=F32)
    merged = merged + gate(3) * jnp.dot(at_ref[...], wa_ref[...], preferred_element_type=F32)
    out = jnp.dot(merged.astype(BF16), wo_ref[...], preferred_element_type=F32)
    o_ref[...] = x_ref[...] + mod_ref[0, 2:3, :] * out


def merge_branches(x2d, mod, proj, yf, attn, lw, *, seq, t):
    m, d = x2d.shape
    tps = seq // t
    hb = t // HALO
    n_halo = m // HALO
    n_mod = mod.shape[0]
    mod_idx = (lambda i: (i // tps, 0, 0)) if n_mod > 1 else (lambda i: (0, 0, 0))
    const = lambda i: (0, 0)
    cp_blk = P_OFF_CP // CP_W
    return pl.pallas_call(
        functools.partial(_merge_kernel, t=t, tps=tps, seq=seq),
        grid=(m // t,),
        in_specs=[
            pl.BlockSpec((t, d), lambda i: (i, 0)),
            pl.BlockSpec((1, 6, d), mod_idx),
            pl.BlockSpec((t, 4 * d), lambda i: (i, 0)),
            pl.BlockSpec((t, CP_W), lambda i: (i, cp_blk)),
            pl.BlockSpec((HALO, CP_W), lambda i: (jnp.maximum(i * hb - 1, 0), cp_blk)),
            pl.BlockSpec((HALO, CP_W), lambda i: (jnp.minimum((i + 1) * hb, n_halo - 1), cp_blk)),
            pl.BlockSpec((t, FOURIER_W), lambda i: (i, 0)),
            pl.BlockSpec((t, Q_W), lambda i: (i, 0)),
            pl.BlockSpec((FOURIER_W, d), const),
            pl.BlockSpec((CONV_W, d), const),
            pl.BlockSpec((POOL_W, d), const),
            pl.BlockSpec((Q_W, d), const),
            pl.BlockSpec((d, d), const),
            pl.BlockSpec((CONV_K, CONV_W), const),
            pl.BlockSpec((1, CONV_W), const),
            pl.BlockSpec((1, CONV_W), const),
            pl.BlockSpec((POOL_W, POOL_W), const),
            pl.BlockSpec((1, POOL_W), const),
        ],
        out_specs=pl.BlockSpec((t, d), lambda i: (i, 0)),
        out_shape=jax.ShapeDtypeStruct((m, d), F32),
        scratch_shapes=[pltpu.VMEM((t + 2 * HALO, CONV_W), F32), pltpu.VMEM((t + 2 * HALO, POOL_W), F32),
                        pltpu.VMEM((7, t + 2 * HALO, CONV_W), F32), pltpu.VMEM((t, CONV_W), F32)],
        compiler_params=_cparams(1),
        name="merge_branches",
    )(x2d, mod, proj, proj, proj, proj, yf, attn,
      lw["wf"], lw["wc"], lw["wp"], lw["wa"], lw["wo"], lw["dw"], lw["cb"], lw["cg"], lw["pw"], lw["ps"])


def _ffn_kernel(x_ref, mod_ref, g_ref, w1_ref, w3_ref, w2_ref, o_ref, *, chunks):
    x = x_ref[...]
    h = _norm_mod(x, g_ref[...], mod_ref[0, 3:4, :], mod_ref[0, 4:5, :]).astype(BF16)
    acc = jnp.zeros(x.shape, F32)
    for c0, cw in chunks:
        a = jnp.dot(h, w1_ref[:, c0:c0 + cw], preferred_element_type=F32)
        b = jnp.dot(h, w3_ref[:, c0:c0 + cw], preferred_element_type=F32)
        acc = acc + jnp.dot((_silu(a) * b).astype(BF16), w2_ref[c0:c0 + cw, :], preferred_element_type=F32)
    o_ref[...] = x + mod_ref[0, 5:6, :] * acc


def ffn_dense(x2d, mod, g, w1, w3, w2, *, tm, tiles_per_mod):
    m, d = x2d.shape
    dff = w1.shape[1]
    chunks = tuple((c0, min(1024, dff - c0)) for c0 in range(0, dff, 1024))
    n_mod = mod.shape[0]
    mod_idx = (lambda i: (i // tiles_per_mod, 0, 0)) if n_mod > 1 else (lambda i: (0, 0, 0))
    const = lambda i: (0, 0)
    return pl.pallas_call(
        functools.partial(_ffn_kernel, chunks=chunks),
        grid=(m // tm,),
        in_specs=[
            pl.BlockSpec((tm, d), lambda i: (i, 0)),
            pl.BlockSpec((1, 6, d), mod_idx),
            pl.BlockSpec((1, d), const),
            pl.BlockSpec((d, dff), const, pipeline_mode=pl.Buffered(1)),
            pl.BlockSpec((d, dff), const, pipeline_mode=pl.Buffered(1)),
            pl.BlockSpec((dff, d), const, pipeline_mode=pl.Buffered(1)),
        ],
        out_specs=pl.BlockSpec((tm, d), lambda i: (i, 0)),
        out_shape=jax.ShapeDtypeStruct((m, d), F32),
        compiler_params=_cparams(1),
        name="ffn_dense",
    )(x2d, mod, g.reshape(1, d), w1, w3, w2)


def _top2(logits):
    t = logits.shape[0]
    lane = lax.broadcasted_iota(jnp.int32, (t, 128), 1).astype(F32)
    neg = jnp.float32(-jnp.inf)
    lg = jnp.where(lane < N_EXPERTS, logits, neg)
    v1 = jnp.max(lg, axis=-1, keepdims=True)
    i1 = jnp.min(jnp.where(lg == v1, lane, 128.0), axis=-1, keepdims=True)
    lg2 = jnp.where(lane == i1, neg, lg)
    v2 = jnp.max(lg2, axis=-1, keepdims=True)
    i2 = jnp.min(jnp.where(lg2 == v2, lane, 128.0), axis=-1, keepdims=True)
    e2 = jnp.exp(v2 - v1)
    return i1, i2, 1.0 / (1.0 + e2), e2 / (1.0 + e2)


R_E1, R_E2, R_W1, R_W2, R_RANK1, R_RANK2 = range(6)


def _route_kernel(x_ref, mod_ref, g_ref, r_ref, tri_ref, route_ref, cnt_ref, carry_ref):
    @pl.when(pl.program_id(0) == 0)
    def _():
        carry_ref[...] = jnp.zeros(carry_ref.shape, F32)

    t = x_ref.shape[0]
    h = _norm_mod(x_ref[...], g_ref[...], mod_ref[0, 3:4, :], mod_ref[0, 4:5, :])
    logits = jnp.dot(h, r_ref[...], preferred_element_type=F32, precision=lax.Precision.HIGHEST)
    i1, i2, w1, w2 = _top2(logits)
    lane = lax.broadcasted_iota(jnp.int32, (t, 128), 1).astype(F32)
    oh1 = jnp.where(lane == i1, 1.0, 0.0)
    oh2 = jnp.where(lane == i2, 1.0, 0.0)
    both = oh1 + oh2
    before = carry_ref[...] + jnp.dot(tri_ref[...], both.astype(BF16), preferred_element_type=F32)
    rank1 = jnp.sum(oh1 * before, axis=-1, keepdims=True)
    rank2 = jnp.sum(oh2 * before, axis=-1, keepdims=True)
    carry_ref[...] += jnp.sum(both, axis=0, keepdims=True)
    rec = jnp.zeros((t, 128), F32)
    for col, val in ((R_E1, i1), (R_E2, i2), (R_W1, w1), (R_W2, w2), (R_RANK1, rank1), (R_RANK2, rank2)):
        rec = jnp.where(lane == col, val, rec)
    route_ref[...] = rec
    cnt_ref[...] = carry_ref[...]


def moe_route(x2d, mod, g, router_pad, *, tm, tiles_per_mod):
    m, d = x2d.shape
    n_mod = mod.shape[0]
    mod_idx = (lambda i: (i // tiles_per_mod, 0, 0)) if n_mod > 1 else (lambda i: (0, 0, 0))
    tri = jnp.asarray(np.tril(np.ones((tm, tm), np.float32), -1), BF16)
    return pl.pallas_call(
        _route_kernel,
        grid=(m // tm,),
        in_specs=[
            pl.BlockSpec((tm, d), lambda i: (i, 0)),
            pl.BlockSpec((1, 6, d), mod_idx),
            pl.BlockSpec((1, d), lambda i: (0, 0)),
            pl.BlockSpec((d, 128), lambda i: (0, 0)),
            pl.BlockSpec((tm, tm), lambda i: (0, 0)),
        ],
        out_specs=[pl.BlockSpec((tm, 128), lambda i: (i, 0)), pl.BlockSpec((1, 128), lambda i: (0, 0))],
        out_shape=[jax.ShapeDtypeStruct((m, 128), F32), jax.ShapeDtypeStruct((1, 128), F32)],
        scratch_shapes=[pltpu.VMEM((1, 128), F32)],
        compiler_params=_cparams(1),
        name="moe_route",
    )(x2d, mod, g.reshape(1, d), router_pad, tri)


def _dispatch_kernel(pos_ref, x_ref, mod_ref, g_ref, xs_in_ref, xs_ref, h_ref, sem):
    del xs_in_ref
    i = pl.program_id(0)
    t = x_ref.shape[0]
    slot = i % 2
    h_ref[slot] = _norm_mod(x_ref[...], g_ref[...], mod_ref[0, 3:4, :], mod_ref[0, 4:5, :])

    def row_copy(r, dst_row):
        return pltpu.make_async_copy(h_ref.at[slot, pl.ds(r, 1), :], xs_ref.at[pl.ds(dst_row, 1), :], sem.at[slot])

    def issue(r, carry):
        row_copy(r, pos_ref[0, 0, r]).start()
        row_copy(r, pos_ref[0, 0, t + r]).start()
        return carry

    lax.fori_loop(0, t, issue, 0, unroll=DMA_ISSUE_UNROLL)

    def drain(s):
        for _ in range(2):
            pltpu.make_async_copy(h_ref.at[s], xs_ref.at[pl.ds(0, t), :], sem.at[s]).wait()

    @pl.when(i > 0)
    def _():
        drain(1 - slot)

    @pl.when(i == pl.num_programs(0) - 1)
    def _():
        drain(slot)


def moe_dispatch(x2d, mod, g, pos_tiles, n_rows, *, tm, tiles_per_mod):
    m, d = x2d.shape
    n_mod = mod.shape[0]
    mod_idx = (lambda i: (i // tiles_per_mod, 0, 0)) if n_mod > 1 else (lambda i: (0, 0, 0))
    return pl.pallas_call(
        _dispatch_kernel,
        grid=(m // tm,),
        in_specs=[
            pl.BlockSpec((1, 1, 2 * tm), lambda i: (i, 0, 0), memory_space=pltpu.SMEM),
            pl.BlockSpec((tm, d), lambda i: (i, 0)),
            pl.BlockSpec((1, 6, d), mod_idx),
            pl.BlockSpec((1, d), lambda i: (0, 0)),
            pl.BlockSpec(memory_space=pl.ANY),
        ],
        out_specs=pl.BlockSpec(memory_space=pl.ANY),
        out_shape=jax.ShapeDtypeStruct((n_rows, d), F32),
        scratch_shapes=[pltpu.VMEM((2, tm, d), F32), pltpu.SemaphoreType.DMA((2,))],
        input_output_aliases={4: 0},
        compiler_params=_cparams(1),
        name="moe_dispatch",
    )(pos_tiles, x2d, mod, g.reshape(1, d), jnp.zeros((n_rows, d), F32))


def _experts_kernel(te_ref, nv_ref, xs_ref, w1_ref, w3_ref, w2_ref, ys_ref, xb_ref, acc_ref):
    i = pl.program_id(0)
    j = pl.program_id(1)
    valid = i < nv_ref[0]

    @pl.when(jnp.logical_and(valid, j == 0))
    def _():
        xb_ref[...] = xs_ref[...].astype(BF16)
        acc_ref[...] = jnp.zeros(acc_ref.shape, F32)

    @pl.when(valid)
    def _():
        h = xb_ref[...]
        a = jnp.dot(h, w1_ref[0], preferred_element_type=F32)
        b = jnp.dot(h, w3_ref[0], preferred_element_type=F32)
        acc_ref[...] += jnp.dot((_silu(a) * b).astype(BF16), w2_ref[0], preferred_element_type=F32)

    @pl.when(jnp.logical_and(valid, j == pl.num_programs(1) - 1))
    def _():
        ys_ref[...] = acc_ref[...]

    @pl.when(jnp.logical_and(jnp.logical_not(valid), j == pl.num_programs(1) - 1))
    def _():
        ys_ref[...] = jnp.zeros(ys_ref.shape, F32)


def moe_experts_grouped(xs, tile_expert, n_valid, w1, w3, w2, *, tm, tf):
    n_rows, d = xs.shape
    dff = w1.shape[2]
    nf = dff // tf

    def w13_idx(i, j, te, nv):
        return (te[i], 0, jnp.where(i < nv[0], j, nf - 1))

    def w2_idx(i, j, te, nv):
        return (te[i], jnp.where(i < nv[0], j, nf - 1), 0)

    grid_spec = pltpu.PrefetchScalarGridSpec(
        num_scalar_prefetch=2,
        grid=(n_rows // tm, nf),
        in_specs=[
            pl.BlockSpec((tm, d), lambda i, j, te, nv: (jnp.minimum(i, nv[0] - 1), 0)),
            pl.BlockSpec((1, d, tf), w13_idx),
            pl.BlockSpec((1, d, tf), w13_idx),
            pl.BlockSpec((1, tf, d), w2_idx),
        ],
        out_specs=pl.BlockSpec((tm, d), lambda i, j, te, nv: (i, 0)),
        scratch_shapes=[pltpu.VMEM((tm, d), BF16), pltpu.VMEM((tm, d), F32)],
    )
    return pl.pallas_call(
        _experts_kernel,
        grid_spec=grid_spec,
        out_shape=jax.ShapeDtypeStruct((n_rows, d), F32),
        compiler_params=_cparams(2),
        name="moe_experts_grouped",
    )(tile_expert, n_valid, xs, w1, w3, w2)


def _combine_kernel(pos_ref, pos_next_ref, x_ref, mod_ref, rt_ref, ys_ref, o_ref, y_ref, sem):
    i = pl.program_id(0)
    t = x_ref.shape[0]
    slot = i % 2

    def issue_tile(p_ref, s):
        def issue(r, carry):
            for k in range(2):
                pltpu.make_async_copy(ys_ref.at[pl.ds(p_ref[0, 0, k * t + r], 1), :],
                                      y_ref.at[s, k, pl.ds(r, 1), :], sem.at[s]).start()
            return carry

        lax.fori_loop(0, t, issue, 0, unroll=DMA_ISSUE_UNROLL)

    @pl.when(i == 0)
    def _():
        issue_tile(pos_ref, 0)

    @pl.when(i + 1 < pl.num_programs(0))
    def _():
        issue_tile(pos_next_ref, 1 - slot)

    for k in range(2):
        pltpu.make_async_copy(ys_ref.at[pl.ds(0, t), :], y_ref.at[slot, k], sem.at[slot]).wait()
    rt = rt_ref[...]
    mix = rt[:, R_W1:R_W1 + 1] * y_ref[slot, 0] + rt[:, R_W2:R_W2 + 1] * y_ref[slot, 1]
    o_ref[...] = x_ref[...] + mod_ref[0, 5:6, :] * mix


def moe_combine(x2d, mod, route, pos_tiles, ys, *, tm, tiles_per_mod):
    m, d = x2d.shape
    n_mod = mod.shape[0]
    mod_idx = (lambda i: (i // tiles_per_mod, 0, 0)) if n_mod > 1 else (lambda i: (0, 0, 0))
    n_tiles = m // tm
    return pl.pallas_call(
        _combine_kernel,
        grid=(n_tiles,),
        in_specs=[
            pl.BlockSpec((1, 1, 2 * tm), lambda i: (i, 0, 0), memory_space=pltpu.SMEM),
            pl.BlockSpec((1, 1, 2 * tm), lambda i: (jnp.minimum(i + 1, n_tiles - 1), 0, 0), memory_space=pltpu.SMEM),
            pl.BlockSpec((tm, d), lambda i: (i, 0)),
            pl.BlockSpec((1, 6, d), mod_idx),
            pl.BlockSpec((tm, 128), lambda i: (i, 0)),
            pl.BlockSpec(memory_space=pl.ANY),
        ],
        out_specs=pl.BlockSpec((tm, d), lambda i: (i, 0)),
        out_shape=jax.ShapeDtypeStruct((m, d), F32),
        scratch_shapes=[pltpu.VMEM((2, 2, tm, d), F32), pltpu.SemaphoreType.DMA((2,))],
        compiler_params=_cparams(1),
        name="moe_combine",
    )(pos_tiles, pos_tiles, x2d, mod, route, ys)


def _pos_tiles(pos1, pos2, tm):
    n = pos1.shape[0] // tm
    return jnp.concatenate([pos1.reshape(n, 1, tm), pos2.reshape(n, 1, tm)], axis=2)


def moe_sparse(x2d, mod, g, router_pad, w1, w3, w2, *, rows_per_mod):
    m, d = x2d.shape
    tr, td, tc, te = MOE_ROUTE_TM, MOE_DISPATCH_TM, MOE_COMBINE_TM, MOE_EXPERT_TM
    tiles_per_seq_row = rows_per_mod
    route, cnt = moe_route(x2d, mod, g, router_pad, tm=tr, tiles_per_mod=tiles_per_seq_row // tr)
    counts = cnt[0, 0:N_EXPERTS].astype(jnp.int32)
    group = ((counts + te - 1) // te) * te
    ends = jnp.cumsum(group)
    starts = ends - group
    e1 = route[:, R_E1].astype(jnp.int32)
    e2 = route[:, R_E2].astype(jnp.int32)
    pos1 = starts[e1] + route[:, R_RANK1].astype(jnp.int32)
    pos2 = starts[e2] + route[:, R_RANK2].astype(jnp.int32)
    n_rows = 2 * m + N_EXPERTS * te
    n_tiles = n_rows // te
    tile_start = jnp.arange(n_tiles, dtype=jnp.int32)[:, None] * te
    tile_expert = jnp.minimum(jnp.sum((tile_start >= ends[None, :]).astype(jnp.int32), axis=1), N_EXPERTS - 1)
    n_valid = (ends[-1:] // te).astype(jnp.int32)
    xs = moe_dispatch(x2d, mod, g, _pos_tiles(pos1, pos2, td), n_rows, tm=td, tiles_per_mod=tiles_per_seq_row // td)
    ys = moe_experts_grouped(xs, tile_expert, n_valid, w1, w3, w2, tm=te, tf=MOE_EXPERT_TF)
    return moe_combine(x2d, mod, route, _pos_tiles(pos1, pos2, tc), ys, tm=tc, tiles_per_mod=tiles_per_seq_row // tc)


def _permute_w_in(w):
    f, c, p, q, kv, gts = w[:, 0:256], w[:, 256:768], w[:, 768:1024], w[:, 1024:1536], w[:, 1536:1792], w[:, 1792:]
    return jnp.concatenate([gts, q, c, p, f, kv], axis=1).astype(BF16)


def _layer_weights(layer, w_br_fourier, conv_dw, conv_b, conv_norm_g, w_br_conv, pool_w, pool_scale, w_br_pool,
                   w_br_attn, w_out):
    pw = jax.scipy.linalg.block_diag(*[pool_w[layer, i] for i in range(len(POOL_WINDOWS))])
    return {
        "wf": w_br_fourier[layer].astype(BF16), "wc": w_br_conv[layer].astype(BF16),
        "wp": w_br_pool[layer].astype(BF16), "wa": w_br_attn[layer].astype(BF16), "wo": w_out[layer].astype(BF16),
        "dw": conv_dw[layer], "cb": conv_b[layer].reshape(1, CONV_W), "cg": conv_norm_g[layer].reshape(1, CONV_W),
        "pw": pw.astype(BF16), "ps": pool_scale[layer].reshape(1, POOL_W),
    }


def kernel(x, c, ctx, c_ctx, w_mod, b_mod, norm1_g, norm2_g, w_in, w_br_fourier, conv_dw, conv_b, conv_norm_g,
           w_br_conv, pool_w, pool_scale, w_br_pool, q_norm_g, k_norm_g, w_br_attn, w_out, ffn_w1, ffn_w3, ffn_w2,
           moe_router, moe_w1, moe_w3, moe_w2):
    batch, seq, d = x.shape
    ctx_len = ctx.shape[1]
    depth = w_in.shape[0]
    rope = rope_tables(seq)

    c_rows = jnp.zeros((8, d), F32).at[0:batch].set(c).at[batch].set(c_ctx)
    mods = modulation_all(c_rows, w_mod, b_mod).reshape(depth, 8, 6, d)

    xl = x.reshape(batch * seq, d)
    xc = ctx.reshape(batch * ctx_len, d)
    for layer in range(depth):
        is_last = layer == depth - 1
        mod_l = mods[layer, 0:batch]
        mod_c = mods[layer, batch:batch + 1]
        w_in_l = _permute_w_in(w_in[layer])
        lw = _layer_weights(layer, w_br_fourier, conv_dw, conv_b, conv_norm_g, w_br_conv, pool_w, pool_scale,
                            w_br_pool, w_br_attn, w_out)

        proj_c = input_projection(xc, mod_c, norm1_g[layer], w_in_l, tm=256, tiles_per_mod=1)
        qc, ktc, vc = qkv_prepare(proj_c, q_norm_g[layer], k_norm_g[layer], None, batch=batch, seq=ctx_len, tp=256)

        proj = input_projection(xl, mod_l, norm1_g[layer], w_in_l, tm=512, tiles_per_mod=seq // 512)
        q, kt, v = qkv_prepare(proj, q_norm_g[layer], k_norm_g[layer], rope, batch=batch, seq=seq, tp=512)
        spread = (2.0 * 1.02 * HEAD_DIM * Q_SCALE) * jnp.max(jnp.abs(q_norm_g[layer])) * jnp.max(
            jnp.abs(k_norm_g[layer]))
        attn = lax.cond(
            spread < STALE_MAX_EXP_LIMIT,
            lambda ops: attention_stale_max(*ops, batch=batch, seq_q=seq, tq=256, tk=1024),
            lambda ops: attention(*ops, batch=batch, seq_q=seq, tq=256, tk=1024),
            (q, kt, v, ktc, vc))
        yf = fourier_mix(proj[:, P_OFF_F:P_OFF_F + FOURIER_W], batch=batch, seq=seq, n1=64, n2=seq // 64)
        xl = merge_branches(xl, mod_l, proj, yf, attn, lw, seq=seq, t=512)

        if not is_last:
            attn_c = attention(qc, None, None, ktc, vc, batch=batch, seq_q=ctx_len, tq=256, tk=ctx_len)
            yf_c = fourier_mix(proj_c[:, P_OFF_F:P_OFF_F + FOURIER_W], batch=batch, seq=ctx_len, n1=16,
                               n2=ctx_len // 16)
            xc = merge_branches(xc, mod_c, proj_c, yf_c, attn_c, lw, seq=ctx_len, t=256)

        j = layer // 2
        if layer % 2 == 0:
            w1, w3, w2 = ffn_w1[j].astype(BF16), ffn_w3[j].astype(BF16), ffn_w2[j].astype(BF16)
            xl = ffn_dense(xl, mod_l, norm2_g[layer], w1, w3, w2, tm=512, tiles_per_mod=seq // 512)
            if not is_last:
                xc = ffn_dense(xc, mod_c, norm2_g[layer], w1, w3, w2, tm=256, tiles_per_mod=1)
        else:
            w1, w3, w2 = moe_w1[j].astype(BF16), moe_w3[j].astype(BF16), moe_w2[j].astype(BF16)
            router_pad = jnp.zeros((d, 128), F32).at[:, 0:N_EXPERTS].set(moe_router[j])
            xl = moe_sparse(xl, mod_l, norm2_g[layer], router_pad, w1, w3, w2, rows_per_mod=seq)
            if not is_last:
                xc = moe_sparse(xc, mod_c, norm2_g[layer], router_pad, w1, w3, w2, rows_per_mod=batch * ctx_len)
    return xl.reshape(batch, seq, d)
```

```python
import functools
import math

import numpy as np
import jax
import jax.numpy as jnp
from jax import lax
from jax.experimental import pallas as pl
from jax.experimental.pallas import tpu as pltpu

F32 = jnp.float32
BF16 = jnp.bfloat16

D_MODEL = 1024
GRID_W = 64
EPS = 1e-6
FOURIER_GW = 64
FOURIER_W = 256
CONV_W = 256
CONV_K = 31
CONV_HALF = CONV_K // 2
POOL_WINDOWS = (2, 4, 8, 16)
POOL_GW = 64
POOL_W = 256
HEAD_DIM = 64
N_HEADS = 8
N_KV_HEADS = 2
Q_PER_KV = 4
Q_W = 512
KV_W = 128
ROPE_THETA = 10000.0
N_EXPERTS = 8
IN_W = 5888

P_OFF_G = 0
P_OFF_Q = 4096
P_OFF_CP = 4608
P_OFF_F = 5376
P_OFF_KV = 5632
CP_W = 2 * CONV_W + POOL_W

Q_SCALE = (HEAD_DIM ** -0.5) * math.log2(math.e)

STALE_MAX_EXP_LIMIT = 64.0

MOE_ROUTE_TM = 512
MOE_DISPATCH_TM = 512
MOE_COMBINE_TM = 256
MOE_EXPERT_TM = 512
MOE_EXPERT_TF = 1792
DMA_ISSUE_UNROLL = 8

CONV_ROWS = 64
HALO = 16
VMEM_LIMIT = 56 * 1024 * 1024


def _cparams(n_axes):
    return pltpu.CompilerParams(dimension_semantics=("arbitrary",) * n_axes, vmem_limit_bytes=VMEM_LIMIT)


def _sigmoid(v):
    return 0.5 * jnp.tanh(0.5 * v) + 0.5


def _silu(v):
    return v * _sigmoid(v)


def _norm_mod(x, g, shift, scale):
    ms = jnp.mean(x * x, axis=-1, keepdims=True)
    return x * lax.rsqrt(ms + EPS) * g * (1.0 + scale) + shift


def _mod_kernel(c_ref, w_ref, b_ref, o_ref):
    s = _silu(c_ref[...])
    o_ref[0] = jnp.dot(s, w_ref[0], preferred_element_type=F32, precision=lax.Precision.HIGHEST) + b_ref[0]


def modulation_all(c_rows, w_mod, b_mod):
    n_layers, d, n = w_mod.shape
    tn = 1536
    return pl.pallas_call(
        _mod_kernel,
        grid=(n_layers, n // tn),
        in_specs=[
            pl.BlockSpec((8, d), lambda l, j: (0, 0)),
            pl.BlockSpec((1, d, tn), lambda l, j: (l, 0, j)),
            pl.BlockSpec((1, 1, tn), lambda l, j: (l, 0, j)),
        ],
        out_specs=pl.BlockSpec((1, 8, tn), lambda l, j: (l, 0, j)),
        out_shape=jax.ShapeDtypeStruct((n_layers, 8, n), F32),
        compiler_params=_cparams(2),
        name="modulation",
    )(c_rows, w_mod, b_mod.reshape(n_layers, 1, n))


def _inproj_kernel(x_ref, mod_ref, g_ref, w_ref, o_ref, *, chunks):
    h = _norm_mod(x_ref[...], g_ref[...], mod_ref[0, 0:1, :], mod_ref[0, 1:2, :]).astype(BF16)
    for c0, cw in chunks:
        r = jnp.dot(h, w_ref[:, c0:c0 + cw], preferred_element_type=F32)
        if c0 + cw <= P_OFF_Q:
            r = _sigmoid(r)
        elif c0 == P_OFF_CP:
            sg = _sigmoid(r[:, CONV_W:2 * CONV_W])
            r = jnp.concatenate([r[:, 0:CONV_W] * sg, sg], axis=1)
        o_ref[:, c0:c0 + cw] = r.astype(o_ref.dtype)


def input_projection(x2d, mod, g, w_bf16, *, tm, tiles_per_mod):
    m, d = x2d.shape
    n = w_bf16.shape[1]
    chunks = tuple((c0, min(512, n - c0)) for c0 in range(0, n, 512))
    assert P_OFF_Q % 512 == 0 and (P_OFF_CP, 2 * CONV_W) in chunks
    n_mod = mod.shape[0]
    mod_idx = (lambda i: (i // tiles_per_mod, 0, 0)) if n_mod > 1 else (lambda i: (0, 0, 0))
    return pl.pallas_call(
        functools.partial(_inproj_kernel, chunks=chunks),
        grid=(m // tm,),
        in_specs=[
            pl.BlockSpec((tm, d), lambda i: (i, 0)),
            pl.BlockSpec((1, 6, d), mod_idx),
            pl.BlockSpec((1, d), lambda i: (0, 0)),
            pl.BlockSpec((d, n), lambda i: (0, 0), pipeline_mode=pl.Buffered(1)),
        ],
        out_specs=pl.BlockSpec((tm, n), lambda i: (i, 0)),
        out_shape=jax.ShapeDtypeStruct((m, n), BF16),
        compiler_params=_cparams(1),
        name="input_projection",
    )(x2d, mod, g.reshape(1, d), w_bf16)


def _seg_sum64(v, ones_bd):
    hi = v.astype(BF16)
    lo = (v - hi.astype(F32)).astype(BF16)
    return (jnp.dot(hi, ones_bd, preferred_element_type=F32) + jnp.dot(lo, ones_bd, preferred_element_type=F32))


def _head_norm_rope(x, g, ones_bd, cos, sin, low_mask):
    y = x * lax.rsqrt(_seg_sum64(x * x, ones_bd) * (1.0 / HEAD_DIM) + EPS) * g
    if cos is None:
        return y
    partner = jnp.where(low_mask, pltpu.roll(y, 128 - 16, axis=1), pltpu.roll(y, 16, axis=1))
    return y * cos + partner * sin


def _prep_kernel(*refs, use_rope):
    if use_rope:
        q_ref, kv_ref, gq_ref, gk_ref, ones_ref, cos_ref, sin_ref, qo_ref, kt_ref, v_ref = refs
        cos, sin = cos_ref[...], sin_ref[...]
    else:
        q_ref, kv_ref, gq_ref, gk_ref, ones_ref, qo_ref, kt_ref, v_ref = refs
        cos = sin = None
    t = q_ref.shape[0]
    ones_bd = ones_ref[...]
    lane = lax.broadcasted_iota(jnp.int32, (t, 128), 1)
    low_mask = (lane % 32) < 16
    gq = gq_ref[...]
    for c in range(Q_W // 128):
        xq = q_ref[:, 128 * c:128 * (c + 1)].astype(F32)
        yq = _head_norm_rope(xq, gq, ones_bd, cos, sin, low_mask) * Q_SCALE
        qo_ref[0, c // 2, 128 * (c % 2):128 * (c % 2 + 1), :] = yq.T.astype(BF16)
    xk = kv_ref[:, 0:128].astype(F32)
    yk = _head_norm_rope(xk, gk_ref[...], ones_bd, cos, sin, low_mask)
    ykr = pltpu.roll(yk, 64, axis=1)
    first = lane < 64
    k0 = jnp.where(first, yk, ykr).astype(BF16)
    k1 = jnp.where(first, ykr, yk).astype(BF16)
    kt_ref[0, 0] = jnp.concatenate([k0, k0], axis=1)
    kt_ref[0, 1] = jnp.concatenate([k1, k1], axis=1)
    vt = kv_ref[:, 128:256].astype(F32).T
    ones = jnp.ones((HEAD_DIM, t), F32)
    for h in range(N_KV_HEADS):
        v_ref[0, h] = jnp.concatenate([vt[64 * h:64 * (h + 1), :], ones], axis=0).astype(BF16)


def qkv_prepare(proj, gq, gk, rope, *, batch, seq, tp):
    m = proj.shape[0]
    tps = seq // tp
    use_rope = rope is not None
    ones_bd = jnp.asarray(np.kron(np.eye(2, dtype=np.float32), np.ones((64, 64), np.float32)), BF16)
    gq2 = jnp.tile(gq, 2).reshape(1, 128)
    gk2 = jnp.tile(gk, 2).reshape(1, 128)
    in_specs = [
        pl.BlockSpec((tp, Q_W), lambda i: (i, P_OFF_Q // Q_W)),
        pl.BlockSpec((tp, 256), lambda i: (i, P_OFF_KV // 256)),
        pl.BlockSpec((1, 128), lambda i: (0, 0)),
        pl.BlockSpec((1, 128), lambda i: (0, 0)),
        pl.BlockSpec((128, 128), lambda i: (0, 0)),
    ]
    args = [proj, proj, gq2, gk2, ones_bd]
    if use_rope:
        in_specs += [pl.BlockSpec((tp, 128), lambda i: (i % tps, 0))] * 2
        args += list(rope)
    return pl.pallas_call(
        functools.partial(_prep_kernel, use_rope=use_rope),
        grid=(m // tp,),
        in_specs=in_specs,
        out_specs=[
            pl.BlockSpec((1, N_KV_HEADS, 256, tp), lambda i: (i // tps, 0, 0, i % tps)),
            pl.BlockSpec((1, N_KV_HEADS, tp, 256), lambda i: (i // tps, 0, i % tps, 0)),
            pl.BlockSpec((1, N_KV_HEADS, 128, tp), lambda i: (i // tps, 0, 0, i % tps)),
        ],
        out_shape=[
            jax.ShapeDtypeStruct((batch, N_KV_HEADS, 256, seq), BF16),
            jax.ShapeDtypeStruct((batch, N_KV_HEADS, seq, 256), BF16),
            jax.ShapeDtypeStruct((batch, N_KV_HEADS, 128, seq), BF16),
        ],
        compiler_params=_cparams(1),
        name="qkv_prepare",
    )(*args)


def rope_tables(seq):
    n_freq = HEAD_DIM // 4
    freqs = ROPE_THETA ** (-jnp.arange(n_freq, dtype=F32) / n_freq)
    t = jnp.arange(seq)
    row = (t // GRID_W).astype(F32)
    col = (t % GRID_W).astype(F32)
    ang_r = row[:, None] * freqs
    ang_c = col[:, None] * freqs
    cos = jnp.concatenate([jnp.cos(ang_r)] * 2 + [jnp.cos(ang_c)] * 2, axis=1)
    sin = jnp.concatenate([-jnp.sin(ang_r), jnp.sin(ang_r), -jnp.sin(ang_c), jnp.sin(ang_c)], axis=1)
    return jnp.tile(cos, (1, 2)), jnp.tile(sin, (1, 2))


def _attn_kernel(*refs, tq, tk, nk, tail):
    refs = list(refs)
    qt_ref = refs.pop(0)
    k_ref, vt_ref = (refs.pop(0), refs.pop(0)) if nk else (None, None)
    kc_ref, vtc_ref = (refs.pop(0), refs.pop(0)) if tail else (None, None)
    o_ref, qs_ref, s0, s1, p0, p1, a0, a1, mx0, mx1, m_ref, acc_ref = refs
    s_bufs, p_bufs, a_bufs, mx_bufs = (s0, s1), (p0, p1), (a0, a1), (mx0, mx1)
    n_blocks = nk + (1 if tail else 0)

    _attn_stack_queries(qt_ref, qs_ref, tq)
    m_ref[...] = jnp.full(m_ref.shape, -jnp.inf, F32)
    acc_ref[...] = jnp.zeros(acc_ref.shape, F32)

    def block(t):
        if isinstance(t, int) and t >= nk:
            return kc_ref[0, 0], vtc_ref[0, 0], tail
        off = t * tk if isinstance(t, int) else pl.multiple_of(t * tk, tk)
        return k_ref[0, 0, pl.ds(off, tk), :], vt_ref[0, 0, :, pl.ds(off, tk)], tk

    def scores(t, slot):
        k_rows, _, n = block(t)
        s = jnp.dot(k_rows, qs_ref[...], preferred_element_type=F32)
        s_bufs[slot][0:n, :] = s
        mx_bufs[slot][...] = jnp.max(s, axis=0, keepdims=True)

    def numerators(n, slot):
        s_ref, p_ref, a_ref = s_bufs[slot], p_bufs[slot], a_bufs[slot]
        for c0 in range(0, Q_PER_KV * tq, 128):
            cols = slice(c0, c0 + 128)
            m_old = m_ref[:, cols]
            m_new = jnp.maximum(m_old, mx_bufs[slot][:, cols])
            a_ref[:, cols] = jnp.exp2(m_old - m_new)
            p_ref[0:n, cols] = jnp.exp2(s_ref[0:n, cols] - m_new).astype(BF16)
            m_ref[:, cols] = m_new

    def weighted_sum(t, slot):
        _, vt, n = block(t)
        pv = jnp.dot(vt, p_bufs[slot][0:n, :], preferred_element_type=F32)
        acc_ref[...] = a_bufs[slot][...] * acc_ref[...] + pv

    def rows_of(t):
        return tk if t < nk else tail

    def step(t, slot, n_mid):
        scores(t, slot)
        numerators(n_mid, 1 - slot)
        weighted_sum(t - 2, slot)

    scores(0, 0)
    if n_blocks > 1:
        scores(1, 1)
        numerators(rows_of(0), 0)
        n_pairs = max(nk - 2, 0) // 2

        def pair(i, carry):
            t = 2 + 2 * i
            step(t, 0, tk)
            step(t + 1, 1, tk)
            return carry

        if n_pairs:
            lax.fori_loop(0, n_pairs, pair, 0)
        for t in range(2 + 2 * n_pairs, n_blocks):
            step(t, t % 2, rows_of(t - 1))
        last = n_blocks - 1
        numerators(rows_of(last), last % 2)
        weighted_sum(last - 1, (last - 1) % 2)
        weighted_sum(last, last % 2)
    else:
        numerators(rows_of(0), 0)
        weighted_sum(0, 0)

    _attn_write_output(acc_ref, o_ref, tq)


def _attn_write_output(acc_ref, o_ref, tq):
    acc = acc_ref[...]
    ot = acc[0:HEAD_DIM, :] / acc[HEAD_DIM:2 * HEAD_DIM, :]
    for half in range(2):
        pair_t = jnp.concatenate([ot[:, (2 * half) * tq:(2 * half + 1) * tq],
                                  ot[:, (2 * half + 1) * tq:(2 * half + 2) * tq]], axis=0)
        o_ref[:, 128 * half:128 * (half + 1)] = pair_t.T.astype(o_ref.dtype)


def _attn_stack_queries(qt_ref, qs_ref, tq):
    row_group = lax.broadcasted_iota(jnp.int32, (256, tq), 0) // HEAD_DIM
    qt = qt_ref[0, 0]
    for g in range(Q_PER_KV):
        qs_ref[:, g * tq:(g + 1) * tq] = jnp.where(row_group == g, qt, jnp.zeros_like(qt))


def _attn_stale_max_kernel(qt_ref, k_ref, vt_ref, kc_ref, vtc_ref, o_ref, qs_ref, p0, p1, f0, f1, m_ref, acc_ref,
                           *, tq, tk, nk):
    p_bufs, f_bufs = (p0, p1), (f0, f1)
    _attn_stack_queries(qt_ref, qs_ref, tq)

    s = jnp.dot(kc_ref[0, 0], qs_ref[...], preferred_element_type=F32)
    m0 = jnp.max(s, axis=0, keepdims=True)
    m_ref[...] = m0
    acc_ref[...] = jnp.dot(vtc_ref[0, 0], jnp.exp2(s - m0).astype(BF16), preferred_element_type=F32)

    def numerators(t, slot):
        off = t * tk if isinstance(t, int) else pl.multiple_of(t * tk, tk)
        s = jnp.dot(k_ref[0, 0, pl.ds(off, tk), :], qs_ref[...], preferred_element_type=F32)
        m_old = m_ref[...]
        p_bufs[slot][...] = jnp.exp2(s - m_old).astype(BF16)
        m_new = jnp.maximum(m_old, jnp.max(s, axis=0, keepdims=True))
        f_bufs[slot][...] = jnp.exp2(m_old - m_new)
        m_ref[...] = m_new

    def weighted_sum(t, slot):
        off = t * tk if isinstance(t, int) else pl.multiple_of(t * tk, tk)
        pv = jnp.dot(vt_ref[0, 0, :, pl.ds(off, tk)], p_bufs[slot][...], preferred_element_type=F32)
        acc_ref[...] = (acc_ref[...] + pv) * f_bufs[slot][...]

    def step(t, slot):
        numerators(t, slot)
        weighted_sum(t - 1, 1 - slot)

    numerators(0, 0)
    n_pairs = (nk - 1) // 2

    def pair(i, carry):
        t = 1 + 2 * i
        step(t, 1)
        step(t + 1, 0)
        return carry

    if n_pairs:
        lax.fori_loop(0, n_pairs, pair, 0)
    for t in range(1 + 2 * n_pairs, nk):
        step(t, t % 2)
    weighted_sum(nk - 1, (nk - 1) % 2)
    _attn_write_output(acc_ref, o_ref, tq)


def attention_stale_max(qt, k4, vt1, k4_tail, vt1_tail, *, batch, seq_q, tq, tk):
    nq = seq_q // tq
    lanes = Q_PER_KV * tq
    lk = k4.shape[2]
    tail = k4_tail.shape[2]
    return pl.pallas_call(
        functools.partial(_attn_stale_max_kernel, tq=tq, tk=tk, nk=lk // tk),
        grid=(batch, N_KV_HEADS, nq),
        in_specs=[
            pl.BlockSpec((1, 1, 256, tq), lambda b, h, i: (b, h, 0, i)),
            pl.BlockSpec((1, 1, lk, 256), lambda b, h, i: (b, h, 0, 0)),
            pl.BlockSpec((1, 1, 128, lk), lambda b, h, i: (b, h, 0, 0)),
            pl.BlockSpec((1, 1, tail, 256), lambda b, h, i: (b, h, 0, 0)),
            pl.BlockSpec((1, 1, 128, tail), lambda b, h, i: (b, h, 0, 0)),
        ],
        out_specs=pl.BlockSpec((tq, 256), lambda b, h, i: (b * nq + i, h)),
        out_shape=jax.ShapeDtypeStruct((batch * seq_q, Q_W), BF16),
        scratch_shapes=[
            pltpu.VMEM((256, lanes), BF16),
            pltpu.VMEM((tk, lanes), BF16), pltpu.VMEM((tk, lanes), BF16),
            pltpu.VMEM((1, lanes), F32), pltpu.VMEM((1, lanes), F32),
            pltpu.VMEM((1, lanes), F32),
            pltpu.VMEM((2 * HEAD_DIM, lanes), F32),
        ],
        compiler_params=_cparams(3),
        name="attention_stale_max",
    )(qt, k4, vt1, k4_tail, vt1_tail)


def attention(qt, k4, vt1, k4_tail, vt1_tail, *, batch, seq_q, tq, tk):
    nq = seq_q // tq
    lanes = Q_PER_KV * tq
    nk = 0 if k4 is None else k4.shape[2] // tk
    tail = 0 if k4_tail is None else k4_tail.shape[2]
    buf_rows = max(tk if nk else 0, tail)
    in_specs = [pl.BlockSpec((1, 1, 256, tq), lambda b, h, i: (b, h, 0, i))]
    args = [qt]
    if nk:
        lk = k4.shape[2]
        in_specs += [pl.BlockSpec((1, 1, lk, 256), lambda b, h, i: (b, h, 0, 0)),
                     pl.BlockSpec((1, 1, 128, lk), lambda b, h, i: (b, h, 0, 0))]
        args += [k4, vt1]
    if tail:
        in_specs += [pl.BlockSpec((1, 1, tail, 256), lambda b, h, i: (b, h, 0, 0)),
                     pl.BlockSpec((1, 1, 128, tail), lambda b, h, i: (b, h, 0, 0))]
        args += [k4_tail, vt1_tail]
    return pl.pallas_call(
        functools.partial(_attn_kernel, tq=tq, tk=tk, nk=nk, tail=tail),
        grid=(batch, N_KV_HEADS, nq),
        in_specs=in_specs,
        out_specs=pl.BlockSpec((tq, 256), lambda b, h, i: (b * nq + i, h)),
        out_shape=jax.ShapeDtypeStruct((batch * seq_q, Q_W), BF16),
        scratch_shapes=[
            pltpu.VMEM((256, lanes), BF16),
            pltpu.VMEM((buf_rows, lanes), F32), pltpu.VMEM((buf_rows, lanes), F32),
            pltpu.VMEM((buf_rows, lanes), BF16), pltpu.VMEM((buf_rows, lanes), BF16),
            pltpu.VMEM((1, lanes), F32), pltpu.VMEM((1, lanes), F32),
            pltpu.VMEM((1, lanes), F32), pltpu.VMEM((1, lanes), F32),
            pltpu.VMEM((1, lanes), F32),
            pltpu.VMEM((2 * HEAD_DIM, lanes), F32),
        ],
        compiler_params=_cparams(3),
        name="attention",
    )(*args)


def _dft_cs(n):
    k = np.arange(n)
    ang = 2.0 * np.pi * ((k[:, None] * k[None, :]) % n) / n
    return np.cos(ang), np.sin(ang)


def _fft1_kernel(x_ref, f_ref, c_ref, s_ref, o_ref, *, n1):
    y = jnp.dot(f_ref[...], x_ref[0], preferred_element_type=F32)
    yr, yi = y[:n1], y[n1:]
    c, s = c_ref[...], s_ref[...]
    o_ref[0, 0] = (yr * c + yi * s).astype(o_ref.dtype)
    o_ref[0, 1] = (yi * c - yr * s).astype(o_ref.dtype)


def _fft2_kernel(y_ref, f_ref, bc_ref, bs_ref, o_ref, *, n2, kb):
    for j in range(kb):
        y2 = jnp.concatenate([y_ref[0, 0, j], y_ref[0, 1, j]], axis=0)
        x2 = jnp.dot(f_ref[...], y2, preferred_element_type=F32)
        xr = x2[:n2].astype(BF16)
        xi = x2[n2:].astype(BF16)
        z = (jnp.dot(xr, bc_ref[...], preferred_element_type=F32) + jnp.dot(xi, bs_ref[...], preferred_element_type=F32))
        o_ref[0, j] = z.astype(o_ref.dtype)


def fourier_mix(u, *, batch, seq, n1, n2):
    cw = u.shape[1]
    lanes = n2 * cw
    tl = min(lanes, 4096)
    c1, s1 = _dft_cs(n1)
    f1 = jnp.asarray(np.concatenate([c1, -s1], axis=0), BF16)
    k1 = np.arange(n1)[:, None]
    t2 = np.arange(n2)[None, :]
    ang = 2.0 * np.pi * ((k1 * t2) % seq) / seq
    twc = jnp.asarray(np.repeat(np.cos(ang), cw, axis=1), F32)
    tws = jnp.asarray(np.repeat(np.sin(ang), cw, axis=1), F32)
    x2 = u.reshape(batch, n1, lanes)
    yp = pl.pallas_call(
        functools.partial(_fft1_kernel, n1=n1),
        grid=(batch, lanes // tl),
        in_specs=[
            pl.BlockSpec((1, n1, tl), lambda b, j: (b, 0, j)),
            pl.BlockSpec((2 * n1, n1), lambda b, j: (0, 0)),
            pl.BlockSpec((n1, tl), lambda b, j: (0, j)),
            pl.BlockSpec((n1, tl), lambda b, j: (0, j)),
        ],
        out_specs=pl.BlockSpec((1, 2, n1, tl), lambda b, j: (b, 0, 0, j)),
        out_shape=jax.ShapeDtypeStruct((batch, 2, n1, lanes), BF16),
        compiler_params=_cparams(2),
        name="fft_stage1",
    )(x2, f1, twc, tws)

    c2, s2 = _dft_cs(n2)
    f2 = jnp.asarray(np.block([[c2, s2], [-s2, c2]]), BF16)
    cg, sg = _dft_cs(FOURIER_GW)
    norm = 1.0 / math.sqrt(seq * FOURIER_GW)
    bdc = jnp.asarray(np.kron(np.eye(cw // FOURIER_GW), cg) * norm, BF16)
    bds = jnp.asarray(np.kron(np.eye(cw // FOURIER_GW), sg) * norm, BF16)
    kb = min(n1, 16)
    y5 = yp.reshape(batch, 2, n1, n2, cw)
    z = pl.pallas_call(
        functools.partial(_fft2_kernel, n2=n2, kb=kb),
        grid=(batch, n1 // kb),
        in_specs=[
            pl.BlockSpec((1, 2, kb, n2, cw), lambda b, j: (b, 0, j, 0, 0)),
            pl.BlockSpec((2 * n2, 2 * n2), lambda b, j: (0, 0)),
            pl.BlockSpec((cw, cw), lambda b, j: (0, 0)),
            pl.BlockSpec((cw, cw), lambda b, j: (0, 0)),
        ],
        out_specs=pl.BlockSpec((1, kb, n2, cw), lambda b, j: (b, j, 0, 0)),
        out_shape=jax.ShapeDtypeStruct((batch, n1, n2, cw), BF16),
        compiler_params=_cparams(2),
        name="fft_stage2",
    )(y5, f2, bdc, bds)
    return z.transpose(0, 2, 1, 3).reshape(batch * seq, cw)


def _merge_kernel(x_ref, mod_ref, gate_ref, cp_ref, cpp_ref, cpn_ref, yf_ref, at_ref,
                  wf_ref, wc_ref, wp_ref, wa_ref, wo_ref, dw_ref, cb_ref, cg_ref, pw_ref, ps_ref,
                  o_ref, ybuf, xbuf, ysh, cacc, *, t, tps, seq):
    i = pl.program_id(0)
    pos_tile = i % tps
    keep_prev = jnp.where(pos_tile != 0, 1.0, 0.0).astype(F32)
    keep_next = jnp.where(pos_tile != tps - 1, 1.0, 0.0).astype(F32)

    def glu(blk):
        return blk[:, 0:CONV_W].astype(F32)

    cp, cpp, cpn = cp_ref[...], cpp_ref[...], cpn_ref[...]
    ybuf[0:HALO, :] = glu(cpp) * keep_prev
    ybuf[HALO:HALO + t, :] = glu(cp)
    ybuf[HALO + t:HALO + t + HALO, :] = glu(cpn) * keep_next
    xbuf[0:HALO, :] = cpp[:, 2 * CONV_W:].astype(F32) * keep_prev
    xbuf[HALO:HALO + t, :] = cp[:, 2 * CONV_W:].astype(F32)
    xbuf[HALO + t:HALO + t + HALO, :] = cpn[:, 2 * CONV_W:].astype(F32) * keep_next

    n_sh = t + 2 * HALO - 8
    for b in range(1, 8):
        ysh[b - 1, 0:n_sh, :] = ybuf[pl.ds(b, n_sh), :]
    for r0 in range(0, t, CONV_ROWS):
        part = jnp.zeros((CONV_ROWS, CONV_W), F32)
        for k in range(CONV_K):
            a, b = divmod(HALO - CONV_HALF + k, 8)
            src = ybuf if b == 0 else ysh.at[b - 1]
            part = part + dw_ref[k:k + 1, :] * src[8 * a + r0:8 * a + r0 + CONV_ROWS, :]
        cacc[r0:r0 + CONV_ROWS, :] = part + cb_ref[...]
    acc = cacc[...]
    ms = jnp.mean(acc * acc, axis=-1, keepdims=True)
    conv_out = _silu(acc * lax.rsqrt(ms + EPS) * cg_ref[...]).astype(BF16)

    def xs(d):
        return xbuf[pl.ds(HALO + d, t), :]

    x0 = xs(0)
    s2 = xs(-1) + x0
    s4 = s2 + xs(-2) + xs(1)
    s8 = s4 + xs(-4) + xs(-3) + xs(2) + xs(3)
    s16 = s8 + xs(-8) + xs(-7) + xs(-6) + xs(-5) + xs(4) + xs(5) + xs(6) + xs(7)
    grp = lax.broadcasted_iota(jnp.int32, (t, POOL_W), 1) // POOL_GW
    pos = pos_tile * t + lax.broadcasted_iota(jnp.int32, (t, POOL_W), 0)
    half = jnp.where(grp == 0, 1, jnp.where(grp == 1, 2, jnp.where(grp == 2, 4, 8)))
    cnt = (jnp.minimum(pos + half, seq) - jnp.maximum(pos - half, 0)).astype(F32)
    wsum = jnp.where(grp == 0, s2, jnp.where(grp == 1, s4, jnp.where(grp == 2, s8, s16)))
    pool_in = (wsum / cnt - x0).astype(BF16)
    pool_out = (jnp.dot(pool_in, pw_ref[...], preferred_element_type=F32) * ps_ref[...]).astype(BF16)

    def gate(b):
        return gate_ref[:, b * D_MODEL:(b + 1) * D_MODEL].astype(F32)

    merged = gate(0) * jnp.dot(yf_ref[...], wf_ref[...], preferred_element_type=F32)
    merged = merged + gate(1) * jnp.dot(conv_out, wc_ref[...], preferred_element_type=F32)
    merged = merged + gate(2) * jnp.dot(pool_out, wp_ref[...], preferred_element_type=F32)
    merged = merged + gate(3) * jnp.dot(at_ref[...], wa_ref[...], preferred_element_type=F32)
    out = jnp.dot(merged.astype(BF16), wo_ref[...], preferred_element_type=F32)
    o_ref[...] = x_ref[...] + mod_ref[0, 2:3, :] * out


def merge_branches(x2d, mod, proj, yf, attn, lw, *, seq, t):
    m, d = x2d.shape
    tps = seq // t
    hb = t // HALO
    n_halo = m // HALO
    n_mod = mod.shape[0]
    mod_idx = (lambda i: (i // tps, 0, 0)) if n_mod > 1 else (lambda i: (0, 0, 0))
    const = lambda i: (0, 0)
    cp_blk = P_OFF_CP // CP_W
    return pl.pallas_call(
        functools.partial(_merge_kernel, t=t, tps=tps, seq=seq),
        grid=(m // t,),
        in_specs=[
            pl.BlockSpec((t, d), lambda i: (i, 0)),
            pl.BlockSpec((1, 6, d), mod_idx),
            pl.BlockSpec((t, 4 * d), lambda i: (i, 0)),
            pl.BlockSpec((t, CP_W), lambda i: (i, cp_blk)),
            pl.BlockSpec((HALO, CP_W), lambda i: (jnp.maximum(i * hb - 1, 0), cp_blk)),
            pl.BlockSpec((HALO, CP_W), lambda i: (jnp.minimum((i + 1) * hb, n_halo - 1), cp_blk)),
            pl.BlockSpec((t, FOURIER_W), lambda i: (i, 0)),
            pl.BlockSpec((t, Q_W), lambda i: (i, 0)),
            pl.BlockSpec((FOURIER_W, d), const),
            pl.BlockSpec((CONV_W, d), const),
            pl.BlockSpec((POOL_W, d), const),
            pl.BlockSpec((Q_W, d), const),
            pl.BlockSpec((d, d), const),
            pl.BlockSpec((CONV_K, CONV_W), const),
            pl.BlockSpec((1, CONV_W), const),
            pl.BlockSpec((1, CONV_W), const),
            pl.BlockSpec((POOL_W, POOL_W), const),
            pl.BlockSpec((1, POOL_W), const),
        ],
        out_specs=pl.BlockSpec((t, d), lambda i: (i, 0)),
        out_shape=jax.ShapeDtypeStruct((m, d), F32),
        scratch_shapes=[pltpu.VMEM((t + 2 * HALO, CONV_W), F32), pltpu.VMEM((t + 2 * HALO, POOL_W), F32),
                        pltpu.VMEM((7, t + 2 * HALO, CONV_W), F32), pltpu.VMEM((t, CONV_W), F32)],
        compiler_params=_cparams(1),
        name="merge_branches",
    )(x2d, mod, proj, proj, proj, proj, yf, attn,
      lw["wf"], lw["wc"], lw["wp"], lw["wa"], lw["wo"], lw["dw"], lw["cb"], lw["cg"], lw["pw"], lw["ps"])


def _ffn_kernel(x_ref, mod_ref, g_ref, w1_ref, w3_ref, w2_ref, o_ref, *, chunks):
    x = x_ref[...]
    h = _norm_mod(x, g_ref[...], mod_ref[0, 3:4, :], mod_ref[0, 4:5, :]).astype(BF16)
    acc = jnp.zeros(x.shape, F32)
    for c0, cw in chunks:
        a = jnp.dot(h, w1_ref[:, c0:c0 + cw], preferred_element_type=F32)
        b = jnp.dot(h, w3_ref[:, c0:c0 + cw], preferred_element_type=F32)
        acc = acc + jnp.dot((_silu(a) * b).astype(BF16), w2_ref[c0:c0 + cw, :], preferred_element_type=F32)
    o_ref[...] = x + mod_ref[0, 5:6, :] * acc


def ffn_dense(x2d, mod, g, w1, w3, w2, *, tm, tiles_per_mod):
    m, d = x2d.shape
    dff = w1.shape[1]
    chunks = tuple((c0, min(1024, dff - c0)) for c0 in range(0, dff, 1024))
    n_mod = mod.shape[0]
    mod_idx = (lambda i: (i // tiles_per_mod, 0, 0)) if n_mod > 1 else (lambda i: (0, 0, 0))
    const = lambda i: (0, 0)
    return pl.pallas_call(
        functools.partial(_ffn_kernel, chunks=chunks),
        grid=(m // tm,),
        in_specs=[
            pl.BlockSpec((tm, d), lambda i: (i, 0)),
            pl.BlockSpec((1, 6, d), mod_idx),
            pl.BlockSpec((1, d), const),
            pl.BlockSpec((d, dff), const, pipeline_mode=pl.Buffered(1)),
            pl.BlockSpec((d, dff), const, pipeline_mode=pl.Buffered(1)),
            pl.BlockSpec((dff, d), const, pipeline_mode=pl.Buffered(1)),
        ],
        out_specs=pl.BlockSpec((tm, d), lambda i: (i, 0)),
        out_shape=jax.ShapeDtypeStruct((m, d), F32),
        compiler_params=_cparams(1),
        name="ffn_dense",
    )(x2d, mod, g.reshape(1, d), w1, w3, w2)


def _top2(logits):
    t = logits.shape[0]
    lane = lax.broadcasted_iota(jnp.int32, (t, 128), 1).astype(F32)
    neg = jnp.float32(-jnp.inf)
    lg = jnp.where(lane < N_EXPERTS, logits, neg)
    v1 = jnp.max(lg, axis=-1, keepdims=True)
    i1 = jnp.min(jnp.where(lg == v1, lane, 128.0), axis=-1, keepdims=True)
    lg2 = jnp.where(lane == i1, neg, lg)
    v2 = jnp.max(lg2, axis=-1, keepdims=True)
    i2 = jnp.min(jnp.where(lg2 == v2, lane, 128.0), axis=-1, keepdims=True)
    e2 = jnp.exp(v2 - v1)
    return i1, i2, 1.0 / (1.0 + e2), e2 / (1.0 + e2)


R_E1, R_E2, R_W1, R_W2, R_RANK1, R_RANK2 = range(6)


def _route_kernel(x_ref, mod_ref, g_ref, r_ref, tri_ref, route_ref, cnt_ref, carry_ref):
    @pl.when(pl.program_id(0) == 0)
    def _():
        carry_ref[...] = jnp.zeros(carry_ref.shape, F32)

    t = x_ref.shape[0]
    h = _norm_mod(x_ref[...], g_ref[...], mod_ref[0, 3:4, :], mod_ref[0, 4:5, :])
    logits = jnp.dot(h, r_ref[...], preferred_element_type=F32, precision=lax.Precision.HIGHEST)
    i1, i2, w1, w2 = _top2(logits)
    lane = lax.broadcasted_iota(jnp.int32, (t, 128), 1).astype(F32)
    oh1 = jnp.where(lane == i1, 1.0, 0.0)
    oh2 = jnp.where(lane == i2, 1.0, 0.0)
    both = oh1 + oh2
    before = carry_ref[...] + jnp.dot(tri_ref[...], both.astype(BF16), preferred_element_type=F32)
    rank1 = jnp.sum(oh1 * before, axis=-1, keepdims=True)
    rank2 = jnp.sum(oh2 * before, axis=-1, keepdims=True)
    carry_ref[...] += jnp.sum(both, axis=0, keepdims=True)
    rec = jnp.zeros((t, 128), F32)
    for col, val in ((R_E1, i1), (R_E2, i2), (R_W1, w1), (R_W2, w2), (R_RANK1, rank1), (R_RANK2, rank2)):
        rec = jnp.where(lane == col, val, rec)
    route_ref[...] = rec
    cnt_ref[...] = carry_ref[...]


def moe_route(x2d, mod, g, router_pad, *, tm, tiles_per_mod):
    m, d = x2d.shape
    n_mod = mod.shape[0]
    mod_idx = (lambda i: (i // tiles_per_mod, 0, 0)) if n_mod > 1 else (lambda i: (0, 0, 0))
    tri = jnp.asarray(np.tril(np.ones((tm, tm), np.float32), -1), BF16)
    return pl.pallas_call(
        _route_kernel,
        grid=(m // tm,),
        in_specs=[
            pl.BlockSpec((tm, d), lambda i: (i, 0)),
            pl.BlockSpec((1, 6, d), mod_idx),
            pl.BlockSpec((1, d), lambda i: (0, 0)),
            pl.BlockSpec((d, 128), lambda i: (0, 0)),
            pl.BlockSpec((tm, tm), lambda i: (0, 0)),
        ],
        out_specs=[pl.BlockSpec((tm, 128), lambda i: (i, 0)), pl.BlockSpec((1, 128), lambda i: (0, 0))],
        out_shape=[jax.ShapeDtypeStruct((m, 128), F32), jax.ShapeDtypeStruct((1, 128), F32)],
        scratch_shapes=[pltpu.VMEM((1, 128), F32)],
        compiler_params=_cparams(1),
        name="moe_route",
    )(x2d, mod, g.reshape(1, d), router_pad, tri)


def _dispatch_kernel(pos_ref, pad_ref, x_ref, mod_ref, g_ref, xs_ref, h_ref, zrow_ref, sem, zsem):
    i = pl.program_id(0)
    t = x_ref.shape[0]
    slot = i % 2
    h_ref[slot] = _norm_mod(x_ref[...], g_ref[...], mod_ref[0, 3:4, :], mod_ref[0, 4:5, :])

    def row_copy(r, dst_row):
        return pltpu.make_async_copy(h_ref.at[slot, pl.ds(r, 1), :], xs_ref.at[pl.ds(dst_row, 1), :], sem.at[slot])

    def issue(r, carry):
        row_copy(r, pos_ref[0, 0, r]).start()
        row_copy(r, pos_ref[0, 0, t + r]).start()
        return carry

    lax.fori_loop(0, t, issue, 0, unroll=DMA_ISSUE_UNROLL)

    def drain(s):
        for _ in range(2):
            pltpu.make_async_copy(h_ref.at[s], xs_ref.at[pl.ds(0, t), :], sem.at[s]).wait()

    @pl.when(i > 0)
    def _():
        drain(1 - slot)

    @pl.when(i == pl.num_programs(0) - 1)
    def _():
        drain(slot)
        zrow_ref[...] = jnp.zeros(zrow_ref.shape, F32)

        def zero_copy(r):
            return pltpu.make_async_copy(zrow_ref.at[pl.ds(0, 1), :], xs_ref.at[pl.ds(r, 1), :], zsem)

        def start_one(r, carry):
            zero_copy(r).start()
            return carry

        def wait_one(r, carry):
            zero_copy(r).wait()
            return carry

        for e in range(N_EXPERTS):
            lax.fori_loop(pad_ref[0, e], pad_ref[0, N_EXPERTS + e], start_one, 0)
        for e in range(N_EXPERTS):
            lax.fori_loop(pad_ref[0, e], pad_ref[0, N_EXPERTS + e], wait_one, 0)

        h_ref[1 - slot] = jnp.zeros((t, h_ref.shape[2]), F32)

        def tile_copy(k):
            return pltpu.make_async_copy(h_ref.at[1 - slot], xs_ref.at[pl.ds(pl.multiple_of(k * t, t), t), :], zsem)

        def start_tile(k, carry):
            tile_copy(k).start()
            return carry

        def wait_tile(k, carry):
            tile_copy(k).wait()
            return carry

        first_unused, n_tiles = pad_ref[0, 2 * N_EXPERTS], xs_ref.shape[0] // t
        lax.fori_loop(first_unused, n_tiles, start_tile, 0)
        lax.fori_loop(first_unused, n_tiles, wait_tile, 0)


def moe_dispatch(x2d, mod, g, pos_tiles, pad_rows, n_rows, *, tm, tiles_per_mod):
    m, d = x2d.shape
    n_mod = mod.shape[0]
    mod_idx = (lambda i: (i // tiles_per_mod, 0, 0)) if n_mod > 1 else (lambda i: (0, 0, 0))
    return pl.pallas_call(
        _dispatch_kernel,
        grid=(m // tm,),
        in_specs=[
            pl.BlockSpec((1, 1, 2 * tm), lambda i: (i, 0, 0), memory_space=pltpu.SMEM),
            pl.BlockSpec((1, 2 * N_EXPERTS + 1), lambda i: (0, 0), memory_space=pltpu.SMEM),
            pl.BlockSpec((tm, d), lambda i: (i, 0)),
            pl.BlockSpec((1, 6, d), mod_idx),
            pl.BlockSpec((1, d), lambda i: (0, 0)),
        ],
        out_specs=pl.BlockSpec(memory_space=pl.ANY),
        out_shape=jax.ShapeDtypeStruct((n_rows, d), F32),
        scratch_shapes=[pltpu.VMEM((2, tm, d), F32), pltpu.VMEM((8, d), F32),
                        pltpu.SemaphoreType.DMA((2,)), pltpu.SemaphoreType.DMA(())],
        compiler_params=_cparams(1),
        name="moe_dispatch",
    )(pos_tiles, pad_rows, x2d, mod, g.reshape(1, d))


def _experts_kernel(te_ref, nv_ref, xs_ref, w1_ref, w3_ref, w2_ref, ys_ref, xb_ref, acc_ref):
    i = pl.program_id(0)
    j = pl.program_id(1)
    valid = i < nv_ref[0]

    @pl.when(jnp.logical_and(valid, j == 0))
    def _():
        xb_ref[...] = xs_ref[...].astype(BF16)
        acc_ref[...] = jnp.zeros(acc_ref.shape, F32)

    @pl.when(valid)
    def _():
        h = xb_ref[...]
        a = jnp.dot(h, w1_ref[0], preferred_element_type=F32)
        b = jnp.dot(h, w3_ref[0], preferred_element_type=F32)
        acc_ref[...] += jnp.dot((_silu(a) * b).astype(BF16), w2_ref[0], preferred_element_type=F32)

    @pl.when(jnp.logical_and(valid, j == pl.num_programs(1) - 1))
    def _():
        ys_ref[...] = acc_ref[...]

    @pl.when(jnp.logical_and(jnp.logical_not(valid), j == pl.num_programs(1) - 1))
    def _():
        ys_ref[...] = jnp.zeros(ys_ref.shape, F32)


def moe_experts_grouped(xs, tile_expert, n_valid, w1, w3, w2, *, tm, tf):
    n_rows, d = xs.shape
    dff = w1.shape[2]
    nf = dff // tf

    def w13_idx(i, j, te, nv):
        return (te[i], 0, jnp.where(i < nv[0], j, nf - 1))

    def w2_idx(i, j, te, nv):
        return (te[i], jnp.where(i < nv[0], j, nf - 1), 0)

    grid_spec = pltpu.PrefetchScalarGridSpec(
        num_scalar_prefetch=2,
        grid=(n_rows // tm, nf),
        in_specs=[
            pl.BlockSpec((tm, d), lambda i, j, te, nv: (jnp.minimum(i, nv[0] - 1), 0)),
            pl.BlockSpec((1, d, tf), w13_idx),
            pl.BlockSpec((1, d, tf), w13_idx),
            pl.BlockSpec((1, tf, d), w2_idx),
        ],
        out_specs=pl.BlockSpec((tm, d), lambda i, j, te, nv: (i, 0)),
        scratch_shapes=[pltpu.VMEM((tm, d), BF16), pltpu.VMEM((tm, d), F32)],
    )
    return pl.pallas_call(
        _experts_kernel,
        grid_spec=grid_spec,
        out_shape=jax.ShapeDtypeStruct((n_rows, d), F32),
        compiler_params=_cparams(2),
        name="moe_experts_grouped",
    )(tile_expert, n_valid, xs, w1, w3, w2)


def _combine_kernel(pos_ref, pos_next_ref, x_ref, mod_ref, rt_ref, ys_ref, o_ref, y_ref, sem):
    i = pl.program_id(0)
    t = x_ref.shape[0]
    slot = i % 2

    def issue_tile(p_ref, s):
        def issue(r, carry):
            for k in range(2):
                pltpu.make_async_copy(ys_ref.at[pl.ds(p_ref[0, 0, k * t + r], 1), :],
                                      y_ref.at[s, k, pl.ds(r, 1), :], sem.at[s]).start()
            return carry

        lax.fori_loop(0, t, issue, 0, unroll=DMA_ISSUE_UNROLL)

    @pl.when(i == 0)
    def _():
        issue_tile(pos_ref, 0)

    @pl.when(i + 1 < pl.num_programs(0))
    def _():
        issue_tile(pos_next_ref, 1 - slot)

    for k in range(2):
        pltpu.make_async_copy(ys_ref.at[pl.ds(0, t), :], y_ref.at[slot, k], sem.at[slot]).wait()
    rt = rt_ref[...]
    mix = rt[:, R_W1:R_W1 + 1] * y_ref[slot, 0] + rt[:, R_W2:R_W2 + 1] * y_ref[slot, 1]
    o_ref[...] = x_ref[...] + mod_ref[0, 5:6, :] * mix


def moe_combine(x2d, mod, route, pos_tiles, ys, *, tm, tiles_per_mod):
    m, d = x2d.shape
    n_mod = mod.shape[0]
    mod_idx = (lambda i: (i // tiles_per_mod, 0, 0)) if n_mod > 1 else (lambda i: (0, 0, 0))
    n_tiles = m // tm
    return pl.pallas_call(
        _combine_kernel,
        grid=(n_tiles,),
        in_specs=[
            pl.BlockSpec((1, 1, 2 * tm), lambda i: (i, 0, 0), memory_space=pltpu.SMEM),
            pl.BlockSpec((1, 1, 2 * tm), lambda i: (jnp.minimum(i + 1, n_tiles - 1), 0, 0), memory_space=pltpu.SMEM),
            pl.BlockSpec((tm, d), lambda i: (i, 0)),
            pl.BlockSpec((1, 6, d), mod_idx),
            pl.BlockSpec((tm, 128), lambda i: (i, 0)),
            pl.BlockSpec(memory_space=pl.ANY),
        ],
        out_specs=pl.BlockSpec((tm, d), lambda i: (i, 0)),
        out_shape=jax.ShapeDtypeStruct((m, d), F32),
        scratch_shapes=[pltpu.VMEM((2, 2, tm, d), F32), pltpu.SemaphoreType.DMA((2,))],
        compiler_params=_cparams(1),
        name="moe_combine",
    )(pos_tiles, pos_tiles, x2d, mod, route, ys)


def _pos_tiles(pos1, pos2, tm):
    n = pos1.shape[0] // tm
    return jnp.concatenate([pos1.reshape(n, 1, tm), pos2.reshape(n, 1, tm)], axis=2)


def moe_sparse(x2d, mod, g, router_pad, w1, w3, w2, *, rows_per_mod):
    m, d = x2d.shape
    tr, td, tc, te = MOE_ROUTE_TM, MOE_DISPATCH_TM, MOE_COMBINE_TM, MOE_EXPERT_TM
    tiles_per_seq_row = rows_per_mod
    route, cnt = moe_route(x2d, mod, g, router_pad, tm=tr, tiles_per_mod=tiles_per_seq_row // tr)
    counts = cnt[0, 0:N_EXPERTS].astype(jnp.int32)
    group = ((counts + te - 1) // te) * te
    ends = jnp.cumsum(group)
    starts = ends - group
    e1 = route[:, R_E1].astype(jnp.int32)
    e2 = route[:, R_E2].astype(jnp.int32)
    pos1 = starts[e1] + route[:, R_RANK1].astype(jnp.int32)
    pos2 = starts[e2] + route[:, R_RANK2].astype(jnp.int32)
    n_rows = 2 * m + N_EXPERTS * te
    n_tiles = n_rows // te
    tile_start = jnp.arange(n_tiles, dtype=jnp.int32)[:, None] * te
    tile_expert = jnp.minimum(jnp.sum((tile_start >= ends[None, :]).astype(jnp.int32), axis=1), N_EXPERTS - 1)
    n_valid = (ends[-1:] // te).astype(jnp.int32)
    assert td == te
    pad_rows = jnp.concatenate([starts + counts, ends, n_valid]).astype(jnp.int32).reshape(1, 2 * N_EXPERTS + 1)
    xs = moe_dispatch(x2d, mod, g, _pos_tiles(pos1, pos2, td), pad_rows, n_rows, tm=td,
                      tiles_per_mod=tiles_per_seq_row // td)
    ys = moe_experts_grouped(xs, tile_expert, n_valid, w1, w3, w2, tm=te, tf=MOE_EXPERT_TF)
    return moe_combine(x2d, mod, route, _pos_tiles(pos1, pos2, tc), ys, tm=tc, tiles_per_mod=tiles_per_seq_row // tc)


def _permute_w_in(w):
    f, c, p, q, kv, gts = w[:, 0:256], w[:, 256:768], w[:, 768:1024], w[:, 1024:1536], w[:, 1536:1792], w[:, 1792:]
    return jnp.concatenate([gts, q, c, p, f, kv], axis=1).astype(BF16)


def _layer_weights(layer, w_br_fourier, conv_dw, conv_b, conv_norm_g, w_br_conv, pool_w, pool_scale, w_br_pool,
                   w_br_attn, w_out):
    pw = jax.scipy.linalg.block_diag(*[pool_w[layer, i] for i in range(len(POOL_WINDOWS))])
    return {
        "wf": w_br_fourier[layer].astype(BF16), "wc": w_br_conv[layer].astype(BF16),
        "wp": w_br_pool[layer].astype(BF16), "wa": w_br_attn[layer].astype(BF16), "wo": w_out[layer].astype(BF16),
        "dw": conv_dw[layer], "cb": conv_b[layer].reshape(1, CONV_W), "cg": conv_norm_g[layer].reshape(1, CONV_W),
        "pw": pw.astype(BF16), "ps": pool_scale[layer].reshape(1, POOL_W),
    }


def kernel(x, c, ctx, c_ctx, w_mod, b_mod, norm1_g, norm2_g, w_in, w_br_fourier, conv_dw, conv_b, conv_norm_g,
           w_br_conv, pool_w, pool_scale, w_br_pool, q_norm_g, k_norm_g, w_br_attn, w_out, ffn_w1, ffn_w3, ffn_w2,
           moe_router, moe_w1, moe_w3, moe_w2):
    batch, seq, d = x.shape
    ctx_len = ctx.shape[1]
    depth = w_in.shape[0]
    rope = rope_tables(seq)

    c_rows = jnp.zeros((8, d), F32).at[0:batch].set(c).at[batch].set(c_ctx)
    mods = modulation_all(c_rows, w_mod, b_mod).reshape(depth, 8, 6, d)

    xl = x.reshape(batch * seq, d)
    xc = ctx.reshape(batch * ctx_len, d)
    for layer in range(depth):
        is_last = layer == depth - 1
        mod_l = mods[layer, 0:batch]
        mod_c = mods[layer, batch:batch + 1]
        w_in_l = _permute_w_in(w_in[layer])
        lw = _layer_weights(layer, w_br_fourier, conv_dw, conv_b, conv_norm_g, w_br_conv, pool_w, pool_scale,
                            w_br_pool, w_br_attn, w_out)

        proj_c = input_projection(xc, mod_c, norm1_g[layer], w_in_l, tm=256, tiles_per_mod=1)
        qc, ktc, vc = qkv_prepare(proj_c, q_norm_g[layer], k_norm_g[layer], None, batch=batch, seq=ctx_len, tp=256)

        proj = input_projection(xl, mod_l, norm1_g[layer], w_in_l, tm=512, tiles_per_mod=seq // 512)
        q, kt, v = qkv_prepare(proj, q_norm_g[layer], k_norm_g[layer], rope, batch=batch, seq=seq, tp=512)
        spread = (2.0 * 1.02 * HEAD_DIM * Q_SCALE) * jnp.max(jnp.abs(q_norm_g[layer])) * jnp.max(
            jnp.abs(k_norm_g[layer]))
        attn = lax.cond(
            spread < STALE_MAX_EXP_LIMIT,
            lambda ops: attention_stale_max(*ops, batch=batch, seq_q=seq, tq=256, tk=1024),
            lambda ops: attention(*ops, batch=batch, seq_q=seq, tq=256, tk=1024),
            (q, kt, v, ktc, vc))
        yf = fourier_mix(proj[:, P_OFF_F:P_OFF_F + FOURIER_W], batch=batch, seq=seq, n1=64, n2=seq // 64)
        xl = merge_branches(xl, mod_l, proj, yf, attn, lw, seq=seq, t=512)

        if not is_last:
            attn_c = attention(qc, None, None, ktc, vc, batch=batch, seq_q=ctx_len, tq=256, tk=ctx_len)
            yf_c = fourier_mix(proj_c[:, P_OFF_F:P_OFF_F + FOURIER_W], batch=batch, seq=ctx_len, n1=16,
                               n2=ctx_len // 16)
            xc = merge_branches(xc, mod_c, proj_c, yf_c, attn_c, lw, seq=ctx_len, t=256)

        j = layer // 2
        if layer % 2 == 0:
            w1, w3, w2 = ffn_w1[j].astype(BF16), ffn_w3[j].astype(BF16), ffn_w2[j].astype(BF16)
            xl = ffn_dense(xl, mod_l, norm2_g[layer], w1, w3, w2, tm=512, tiles_per_mod=seq // 512)
            if not is_last:
                xc = ffn_dense(xc, mod_c, norm2_g[layer], w1, w3, w2, tm=256, tiles_per_mod=1)
        else:
            w1, w3, w2 = moe_w1[j].astype(BF16), moe_w3[j].astype(BF16), moe_w2[j].astype(BF16)
            router_pad = jnp.zeros((d, 128), F32).at[:, 0:N_EXPERTS].set(moe_router[j])
            xl = moe_sparse(xl, mod_l, norm2_g[layer], router_pad, w1, w3, w2, rows_per_mod=seq)
            if not is_last:
                xc = moe_sparse(xc, mod_c, norm2_g[layer], router_pad, w1, w3, w2, rows_per_mod=batch * ctx_len)
    return xl.reshape(batch, seq, d)
```

```python
import functools
import math

import numpy as np
import jax
import jax.numpy as jnp
from jax import lax
from jax.experimental import pallas as pl
from jax.experimental.pallas import tpu as pltpu

F32 = jnp.float32
BF16 = jnp.bfloat16

D_MODEL = 1024
GRID_W = 64
EPS = 1e-6
FOURIER_GW = 64
FOURIER_W = 256
CONV_W = 256
CONV_K = 31
CONV_HALF = CONV_K // 2
POOL_WINDOWS = (2, 4, 8, 16)
POOL_GW = 64
POOL_W = 256
HEAD_DIM = 64
N_HEADS = 8
N_KV_HEADS = 2
Q_PER_KV = 4
Q_W = 512
KV_W = 128
ROPE_THETA = 10000.0
N_EXPERTS = 8
IN_W = 5888

P_OFF_G = 0
P_OFF_Q = 4096
P_OFF_CP = 4608
P_OFF_F = 5376
P_OFF_KV = 5632
CP_W = 2 * CONV_W + POOL_W

Q_SCALE = (HEAD_DIM ** -0.5) * math.log2(math.e)

STALE_MAX_EXP_LIMIT = 64.0

MOE_ROUTE_TM = 512
MOE_DISPATCH_TM = 512
MOE_COMBINE_TM = 256
MOE_EXPERT_TM = 512
MOE_EXPERT_TF = 1792
DMA_ISSUE_UNROLL = 8

CONV_ROWS = 64
HALO = 16
VMEM_LIMIT = 56 * 1024 * 1024


def _cparams(n_axes):
    return pltpu.CompilerParams(dimension_semantics=("arbitrary",) * n_axes, vmem_limit_bytes=VMEM_LIMIT)


def _sigmoid(v):
    return 0.5 * jnp.tanh(0.5 * v) + 0.5


def _silu(v):
    return v * _sigmoid(v)


def _norm_mod(x, g, shift, scale):
    ms = jnp.mean(x * x, axis=-1, keepdims=True)
    return x * lax.rsqrt(ms + EPS) * g * (1.0 + scale) + shift


def _mod_kernel(c_ref, w_ref, b_ref, o_ref):
    s = _silu(c_ref[...])
    o_ref[0] = jnp.dot(s, w_ref[0], preferred_element_type=F32, precision=lax.Precision.HIGHEST) + b_ref[0]


def modulation_all(c_rows, w_mod, b_mod):
    n_layers, d, n = w_mod.shape
    tn = 1536
    return pl.pallas_call(
        _mod_kernel,
        grid=(n_layers, n // tn),
        in_specs=[
            pl.BlockSpec((8, d), lambda l, j: (0, 0)),
            pl.BlockSpec((1, d, tn), lambda l, j: (l, 0, j)),
            pl.BlockSpec((1, 1, tn), lambda l, j: (l, 0, j)),
        ],
        out_specs=pl.BlockSpec((1, 8, tn), lambda l, j: (l, 0, j)),
        out_shape=jax.ShapeDtypeStruct((n_layers, 8, n), F32),
        compiler_params=_cparams(2),
        name="modulation",
    )(c_rows, w_mod, b_mod.reshape(n_layers, 1, n))


def _inproj_kernel(*refs, chunks, use_rope):
    if use_rope:
        x_ref, mod_ref, g_ref, w_ref, gq_ref, gk_ref, ones_ref, cos_ref, sin_ref, o_ref, qo_ref, kt_ref, v_ref = refs
        cos, sin = cos_ref[...], sin_ref[...]
    else:
        x_ref, mod_ref, g_ref, w_ref, gq_ref, gk_ref, ones_ref, o_ref, qo_ref, kt_ref, v_ref = refs
        cos = sin = None
    h = _norm_mod(x_ref[...], g_ref[...], mod_ref[0, 0:1, :], mod_ref[0, 1:2, :]).astype(BF16)
    for c0, cw in chunks:
        r = jnp.dot(h, w_ref[:, c0:c0 + cw], preferred_element_type=F32)
        if c0 + cw <= P_OFF_Q:
            r = _sigmoid(r)
        elif c0 == P_OFF_CP:
            sg = _sigmoid(r[:, CONV_W:2 * CONV_W])
            r = jnp.concatenate([r[:, 0:CONV_W] * sg, sg], axis=1)
        elif c0 == P_OFF_Q:
            _q_epilogue(r, gq_ref[...], ones_ref[...], cos, sin, qo_ref)
        elif c0 == P_OFF_KV:
            _kv_epilogue(r, gk_ref[...], ones_ref[...], cos, sin, kt_ref, v_ref)
        o_ref[:, c0:c0 + cw] = r.astype(o_ref.dtype)


def input_projection(x2d, mod, g, w_bf16, gq, gk, rope, *, batch, seq, tm):
    m, d = x2d.shape
    n = w_bf16.shape[1]
    tps = seq // tm
    chunks = tuple((c0, min(512, n - c0)) for c0 in range(0, n, 512))
    assert P_OFF_Q % 512 == 0 and {(P_OFF_CP, 2 * CONV_W), (P_OFF_Q, Q_W), (P_OFF_KV, 2 * KV_W)} <= set(chunks)
    use_rope = rope is not None
    n_mod = mod.shape[0]
    mod_idx = (lambda i: (i // tps, 0, 0)) if n_mod > 1 else (lambda i: (0, 0, 0))
    const = lambda i: (0, 0)
    ones_bd = jnp.asarray(np.kron(np.eye(2, dtype=np.float32), np.ones((64, 64), np.float32)), BF16)
    in_specs = [
        pl.BlockSpec((tm, d), lambda i: (i, 0)),
        pl.BlockSpec((1, 6, d), mod_idx),
        pl.BlockSpec((1, d), const),
        pl.BlockSpec((d, n), const, pipeline_mode=pl.Buffered(1)),
        pl.BlockSpec((1, 128), const),
        pl.BlockSpec((1, 128), const),
        pl.BlockSpec((128, 128), const),
    ]
    args = [x2d, mod, g.reshape(1, d), w_bf16, jnp.tile(gq, 2).reshape(1, 128), jnp.tile(gk, 2).reshape(1, 128), ones_bd]
    if use_rope:
        in_specs += [pl.BlockSpec((tm, 128), lambda i: (i % tps, 0))] * 2
        args += list(rope)
    return pl.pallas_call(
        functools.partial(_inproj_kernel, chunks=chunks, use_rope=use_rope),
        grid=(m // tm,),
        in_specs=in_specs,
        out_specs=[
            pl.BlockSpec((tm, n), lambda i: (i, 0)),
            pl.BlockSpec((1, N_KV_HEADS, 256, tm), lambda i: (i // tps, 0, 0, i % tps)),
            pl.BlockSpec((1, N_KV_HEADS, tm, 256), lambda i: (i // tps, 0, i % tps, 0)),
            pl.BlockSpec((1, N_KV_HEADS, 128, tm), lambda i: (i // tps, 0, 0, i % tps)),
        ],
        out_shape=[
            jax.ShapeDtypeStruct((m, n), BF16),
            jax.ShapeDtypeStruct((batch, N_KV_HEADS, 256, seq), BF16),
            jax.ShapeDtypeStruct((batch, N_KV_HEADS, seq, 256), BF16),
            jax.ShapeDtypeStruct((batch, N_KV_HEADS, 128, seq), BF16),
        ],
        compiler_params=_cparams(1),
        name="input_projection",
    )(*args)


def _seg_sum64(v, ones_bd):
    hi = v.astype(BF16)
    lo = (v - hi.astype(F32)).astype(BF16)
    return (jnp.dot(hi, ones_bd, preferred_element_type=F32) + jnp.dot(lo, ones_bd, preferred_element_type=F32))


def _head_norm_rope(x, g, ones_bd, cos, sin, low_mask):
    y = x * lax.rsqrt(_seg_sum64(x * x, ones_bd) * (1.0 / HEAD_DIM) + EPS) * g
    if cos is None:
        return y
    partner = jnp.where(low_mask, pltpu.roll(y, 128 - 16, axis=1), pltpu.roll(y, 16, axis=1))
    return y * cos + partner * sin


def _rope_low_mask(t):
    return (lax.broadcasted_iota(jnp.int32, (t, 128), 1) % 32) < 16


def _q_epilogue(rq, gq, ones_bd, cos, sin, qo_ref):
    low_mask = _rope_low_mask(rq.shape[0])
    for c in range(Q_W // 128):
        yq = _head_norm_rope(rq[:, 128 * c:128 * (c + 1)], gq, ones_bd, cos, sin, low_mask) * Q_SCALE
        qo_ref[0, c // 2, 128 * (c % 2):128 * (c % 2 + 1), :] = yq.T.astype(BF16)


def _kv_epilogue(rkv, gk, ones_bd, cos, sin, kt_ref, v_ref):
    t = rkv.shape[0]
    yk = _head_norm_rope(rkv[:, 0:128], gk, ones_bd, cos, sin, _rope_low_mask(t))
    ykr = pltpu.roll(yk, 64, axis=1)
    first = lax.broadcasted_iota(jnp.int32, (t, 128), 1) < 64
    k0 = jnp.where(first, yk, ykr).astype(BF16)
    k1 = jnp.where(first, ykr, yk).astype(BF16)
    kt_ref[0, 0] = jnp.concatenate([k0, k0], axis=1)
    kt_ref[0, 1] = jnp.concatenate([k1, k1], axis=1)
    vt = rkv[:, 128:256].T
    ones = jnp.ones((HEAD_DIM, t), F32)
    for h in range(N_KV_HEADS):
        v_ref[0, h] = jnp.concatenate([vt[64 * h:64 * (h + 1), :], ones], axis=0).astype(BF16)


def rope_tables(seq):
    n_freq = HEAD_DIM // 4
    freqs = ROPE_THETA ** (-jnp.arange(n_freq, dtype=F32) / n_freq)
    t = jnp.arange(seq)
    row = (t // GRID_W).astype(F32)
    col = (t % GRID_W).astype(F32)
    ang_r = row[:, None] * freqs
    ang_c = col[:, None] * freqs
    cos = jnp.concatenate([jnp.cos(ang_r)] * 2 + [jnp.cos(ang_c)] * 2, axis=1)
    sin = jnp.concatenate([-jnp.sin(ang_r), jnp.sin(ang_r), -jnp.sin(ang_c), jnp.sin(ang_c)], axis=1)
    return jnp.tile(cos, (1, 2)), jnp.tile(sin, (1, 2))


def _attn_kernel(*refs, tq, tk, nk, tail):
    refs = list(refs)
    qt_ref = refs.pop(0)
    k_ref, vt_ref = (refs.pop(0), refs.pop(0)) if nk else (None, None)
    kc_ref, vtc_ref = (refs.pop(0), refs.pop(0)) if tail else (None, None)
    o_ref, qs_ref, s0, s1, p0, p1, a0, a1, mx0, mx1, m_ref, acc_ref = refs
    s_bufs, p_bufs, a_bufs, mx_bufs = (s0, s1), (p0, p1), (a0, a1), (mx0, mx1)
    n_blocks = nk + (1 if tail else 0)

    _attn_stack_queries(qt_ref, qs_ref, tq)
    m_ref[...] = jnp.full(m_ref.shape, -jnp.inf, F32)
    acc_ref[...] = jnp.zeros(acc_ref.shape, F32)

    def block(t):
        if isinstance(t, int) and t >= nk:
            return kc_ref[0, 0], vtc_ref[0, 0], tail
        off = t * tk if isinstance(t, int) else pl.multiple_of(t * tk, tk)
        return k_ref[0, 0, pl.ds(off, tk), :], vt_ref[0, 0, :, pl.ds(off, tk)], tk

    def scores(t, slot):
        k_rows, _, n = block(t)
        s = jnp.dot(k_rows, qs_ref[...], preferred_element_type=F32)
        s_bufs[slot][0:n, :] = s
        mx_bufs[slot][...] = jnp.max(s, axis=0, keepdims=True)

    def numerators(n, slot):
        s_ref, p_ref, a_ref = s_bufs[slot], p_bufs[slot], a_bufs[slot]
        for c0 in range(0, Q_PER_KV * tq, 128):
            cols = slice(c0, c0 + 128)
            m_old = m_ref[:, cols]
            m_new = jnp.maximum(m_old, mx_bufs[slot][:, cols])
            a_ref[:, cols] = jnp.exp2(m_old - m_new)
            p_ref[0:n, cols] = jnp.exp2(s_ref[0:n, cols] - m_new).astype(BF16)
            m_ref[:, cols] = m_new

    def weighted_sum(t, slot):
        _, vt, n = block(t)
        pv = jnp.dot(vt, p_bufs[slot][0:n, :], preferred_element_type=F32)
        acc_ref[...] = a_bufs[slot][...] * acc_ref[...] + pv

    def rows_of(t):
        return tk if t < nk else tail

    def step(t, slot, n_mid):
        scores(t, slot)
        numerators(n_mid, 1 - slot)
        weighted_sum(t - 2, slot)

    scores(0, 0)
    if n_blocks > 1:
        scores(1, 1)
        numerators(rows_of(0), 0)
        n_pairs = max(nk - 2, 0) // 2

        def pair(i, carry):
            t = 2 + 2 * i
            step(t, 0, tk)
            step(t + 1, 1, tk)
            return carry

        if n_pairs:
            lax.fori_loop(0, n_pairs, pair, 0)
        for t in range(2 + 2 * n_pairs, n_blocks):
            step(t, t % 2, rows_of(t - 1))
        last = n_blocks - 1
        numerators(rows_of(last), last % 2)
        weighted_sum(last - 1, (last - 1) % 2)
        weighted_sum(last, last % 2)
    else:
        numerators(rows_of(0), 0)
        weighted_sum(0, 0)

    _attn_write_output(acc_ref, o_ref, tq)


def _attn_write_output(acc_ref, o_ref, tq):
    acc = acc_ref[...]
    ot = acc[0:HEAD_DIM, :] / acc[HEAD_DIM:2 * HEAD_DIM, :]
    for half in range(2):
        pair_t = jnp.concatenate([ot[:, (2 * half) * tq:(2 * half + 1) * tq],
                                  ot[:, (2 * half + 1) * tq:(2 * half + 2) * tq]], axis=0)
        o_ref[:, 128 * half:128 * (half + 1)] = pair_t.T.astype(o_ref.dtype)


def _attn_stack_queries(qt_ref, qs_ref, tq):
    row_group = lax.broadcasted_iota(jnp.int32, (256, tq), 0) // HEAD_DIM
    qt = qt_ref[0, 0]
    for g in range(Q_PER_KV):
        qs_ref[:, g * tq:(g + 1) * tq] = jnp.where(row_group == g, qt, jnp.zeros_like(qt))


def _attn_stale_max_kernel(qt_ref, k_ref, vt_ref, kc_ref, vtc_ref, o_ref, qs_ref, p0, p1, f0, f1, m_ref, acc_ref,
                           *, tq, tk, nk):
    p_bufs, f_bufs = (p0, p1), (f0, f1)
    _attn_stack_queries(qt_ref, qs_ref, tq)

    s = jnp.dot(kc_ref[0, 0], qs_ref[...], preferred_element_type=F32)
    m0 = jnp.max(s, axis=0, keepdims=True)
    m_ref[...] = m0
    acc_ref[...] = jnp.dot(vtc_ref[0, 0], jnp.exp2(s - m0).astype(BF16), preferred_element_type=F32)

    def numerators(t, slot):
        off = t * tk if isinstance(t, int) else pl.multiple_of(t * tk, tk)
        s = jnp.dot(k_ref[0, 0, pl.ds(off, tk), :], qs_ref[...], preferred_element_type=F32)
        m_old = m_ref[...]
        p_bufs[slot][...] = jnp.exp2(s - m_old).astype(BF16)
        m_new = jnp.maximum(m_old, jnp.max(s, axis=0, keepdims=True))
        f_bufs[slot][...] = jnp.exp2(m_old - m_new)
        m_ref[...] = m_new

    def weighted_sum(t, slot):
        off = t * tk if isinstance(t, int) else pl.multiple_of(t * tk, tk)
        pv = jnp.dot(vt_ref[0, 0, :, pl.ds(off, tk)], p_bufs[slot][...], preferred_element_type=F32)
        acc_ref[...] = (acc_ref[...] + pv) * f_bufs[slot][...]

    def step(t, slot):
        numerators(t, slot)
        weighted_sum(t - 1, 1 - slot)

    numerators(0, 0)
    n_pairs = (nk - 1) // 2

    def pair(i, carry):
        t = 1 + 2 * i
        step(t, 1)
        step(t + 1, 0)
        return carry

    if n_pairs:
        lax.fori_loop(0, n_pairs, pair, 0)
    for t in range(1 + 2 * n_pairs, nk):
        step(t, t % 2)
    weighted_sum(nk - 1, (nk - 1) % 2)
    _attn_write_output(acc_ref, o_ref, tq)


def attention_stale_max(qt, k4, vt1, k4_tail, vt1_tail, *, batch, seq_q, tq, tk):
    nq = seq_q // tq
    lanes = Q_PER_KV * tq
    lk = k4.shape[2]
    tail = k4_tail.shape[2]
    return pl.pallas_call(
        functools.partial(_attn_stale_max_kernel, tq=tq, tk=tk, nk=lk // tk),
        grid=(batch, N_KV_HEADS, nq),
        in_specs=[
            pl.BlockSpec((1, 1, 256, tq), lambda b, h, i: (b, h, 0, i)),
            pl.BlockSpec((1, 1, lk, 256), lambda b, h, i: (b, h, 0, 0)),
            pl.BlockSpec((1, 1, 128, lk), lambda b, h, i: (b, h, 0, 0)),
            pl.BlockSpec((1, 1, tail, 256), lambda b, h, i: (b, h, 0, 0)),
            pl.BlockSpec((1, 1, 128, tail), lambda b, h, i: (b, h, 0, 0)),
        ],
        out_specs=pl.BlockSpec((tq, 256), lambda b, h, i: (b * nq + i, h)),
        out_shape=jax.ShapeDtypeStruct((batch * seq_q, Q_W), BF16),
        scratch_shapes=[
            pltpu.VMEM((256, lanes), BF16),
            pltpu.VMEM((tk, lanes), BF16), pltpu.VMEM((tk, lanes), BF16),
            pltpu.VMEM((1, lanes), F32), pltpu.VMEM((1, lanes), F32),
            pltpu.VMEM((1, lanes), F32),
            pltpu.VMEM((2 * HEAD_DIM, lanes), F32),
        ],
        compiler_params=_cparams(3),
        name="attention_stale_max",
    )(qt, k4, vt1, k4_tail, vt1_tail)


def attention(qt, k4, vt1, k4_tail, vt1_tail, *, batch, seq_q, tq, tk):
    nq = seq_q // tq
    lanes = Q_PER_KV * tq
    nk = 0 if k4 is None else k4.shape[2] // tk
    tail = 0 if k4_tail is None else k4_tail.shape[2]
    buf_rows = max(tk if nk else 0, tail)
    in_specs = [pl.BlockSpec((1, 1, 256, tq), lambda b, h, i: (b, h, 0, i))]
    args = [qt]
    if nk:
        lk = k4.shape[2]
        in_specs += [pl.BlockSpec((1, 1, lk, 256), lambda b, h, i: (b, h, 0, 0)),
                     pl.BlockSpec((1, 1, 128, lk), lambda b, h, i: (b, h, 0, 0))]
        args += [k4, vt1]
    if tail:
        in_specs += [pl.BlockSpec((1, 1, tail, 256), lambda b, h, i: (b, h, 0, 0)),
                     pl.BlockSpec((1, 1, 128, tail), lambda b, h, i: (b, h, 0, 0))]
        args += [k4_tail, vt1_tail]
    return pl.pallas_call(
        functools.partial(_attn_kernel, tq=tq, tk=tk, nk=nk, tail=tail),
        grid=(batch, N_KV_HEADS, nq),
        in_specs=in_specs,
        out_specs=pl.BlockSpec((tq, 256), lambda b, h, i: (b * nq + i, h)),
        out_shape=jax.ShapeDtypeStruct((batch * seq_q, Q_W), BF16),
        scratch_shapes=[
            pltpu.VMEM((256, lanes), BF16),
            pltpu.VMEM((buf_rows, lanes), F32), pltpu.VMEM((buf_rows, lanes), F32),
            pltpu.VMEM((buf_rows, lanes), BF16), pltpu.VMEM((buf_rows, lanes), BF16),
            pltpu.VMEM((1, lanes), F32), pltpu.VMEM((1, lanes), F32),
            pltpu.VMEM((1, lanes), F32), pltpu.VMEM((1, lanes), F32),
            pltpu.VMEM((1, lanes), F32),
            pltpu.VMEM((2 * HEAD_DIM, lanes), F32),
        ],
        compiler_params=_cparams(3),
        name="attention",
    )(*args)


def _dft_cs(n):
    k = np.arange(n)
    ang = 2.0 * np.pi * ((k[:, None] * k[None, :]) % n) / n
    return np.cos(ang), np.sin(ang)


def _fft1_kernel(x_ref, f_ref, c_ref, s_ref, o_ref, *, n1):
    y = jnp.dot(f_ref[...], x_ref[0], preferred_element_type=F32)
    yr, yi = y[:n1], y[n1:]
    c, s = c_ref[...], s_ref[...]
    o_ref[0, 0] = (yr * c + yi * s).astype(o_ref.dtype)
    o_ref[0, 1] = (yi * c - yr * s).astype(o_ref.dtype)


def _fft2_kernel(y_ref, f_ref, bc_ref, bs_ref, o_ref, *, n2, kb):
    for j in range(kb):
        y2 = jnp.concatenate([y_ref[0, 0, j], y_ref[0, 1, j]], axis=0)
        x2 = jnp.dot(f_ref[...], y2, preferred_element_type=F32)
        xr = x2[:n2].astype(BF16)
        xi = x2[n2:].astype(BF16)
        z = (jnp.dot(xr, bc_ref[...], preferred_element_type=F32) + jnp.dot(xi, bs_ref[...], preferred_element_type=F32))
        o_ref[0, j] = z.astype(o_ref.dtype)


def fourier_mix(u, *, batch, seq, n1, n2):
    cw = u.shape[1]
    lanes = n2 * cw
    tl = min(lanes, 4096)
    c1, s1 = _dft_cs(n1)
    f1 = jnp.asarray(np.concatenate([c1, -s1], axis=0), BF16)
    k1 = np.arange(n1)[:, None]
    t2 = np.arange(n2)[None, :]
    ang = 2.0 * np.pi * ((k1 * t2) % seq) / seq
    twc = jnp.asarray(np.repeat(np.cos(ang), cw, axis=1), F32)
    tws = jnp.asarray(np.repeat(np.sin(ang), cw, axis=1), F32)
    x2 = u.reshape(batch, n1, lanes)
    yp = pl.pallas_call(
        functools.partial(_fft1_kernel, n1=n1),
        grid=(batch, lanes // tl),
        in_specs=[
            pl.BlockSpec((1, n1, tl), lambda b, j: (b, 0, j)),
            pl.BlockSpec((2 * n1, n1), lambda b, j: (0, 0)),
            pl.BlockSpec((n1, tl), lambda b, j: (0, j)),
            pl.BlockSpec((n1, tl), lambda b, j: (0, j)),
        ],
        out_specs=pl.BlockSpec((1, 2, n1, tl), lambda b, j: (b, 0, 0, j)),
        out_shape=jax.ShapeDtypeStruct((batch, 2, n1, lanes), BF16),
        compiler_params=_cparams(2),
        name="fft_stage1",
    )(x2, f1, twc, tws)

    c2, s2 = _dft_cs(n2)
    f2 = jnp.asarray(np.block([[c2, s2], [-s2, c2]]), BF16)
    cg, sg = _dft_cs(FOURIER_GW)
    norm = 1.0 / math.sqrt(seq * FOURIER_GW)
    bdc = jnp.asarray(np.kron(np.eye(cw // FOURIER_GW), cg) * norm, BF16)
    bds = jnp.asarray(np.kron(np.eye(cw // FOURIER_GW), sg) * norm, BF16)
    kb = min(n1, 16)
    y5 = yp.reshape(batch, 2, n1, n2, cw)
    z = pl.pallas_call(
        functools.partial(_fft2_kernel, n2=n2, kb=kb),
        grid=(batch, n1 // kb),
        in_specs=[
            pl.BlockSpec((1, 2, kb, n2, cw), lambda b, j: (b, 0, j, 0, 0)),
            pl.BlockSpec((2 * n2, 2 * n2), lambda b, j: (0, 0)),
            pl.BlockSpec((cw, cw), lambda b, j: (0, 0)),
            pl.BlockSpec((cw, cw), lambda b, j: (0, 0)),
        ],
        out_specs=pl.BlockSpec((1, kb, n2, cw), lambda b, j: (b, j, 0, 0)),
        out_shape=jax.ShapeDtypeStruct((batch, n1, n2, cw), BF16),
        compiler_params=_cparams(2),
        name="fft_stage2",
    )(y5, f2, bdc, bds)
    return z.transpose(0, 2, 1, 3).reshape(batch * seq, cw)


def _merge_kernel(x_ref, mod_ref, gate_ref, cp_ref, cpp_ref, cpn_ref, yf_ref, at_ref,
                  wf_ref, wc_ref, wp_ref, wa_ref, wo_ref, dw_ref, cb_ref, cg_ref, pw_ref, ps_ref,
                  o_ref, ybuf, xbuf, ysh, cacc, *, t, tps, seq):
    i = pl.program_id(0)
    pos_tile = i % tps
    keep_prev = jnp.where(pos_tile != 0, 1.0, 0.0).astype(F32)
    keep_next = jnp.where(pos_tile != tps - 1, 1.0, 0.0).astype(F32)

    def glu(blk):
        return blk[:, 0:CONV_W].astype(F32)

    cp, cpp, cpn = cp_ref[...], cpp_ref[...], cpn_ref[...]
    ybuf[0:HALO, :] = glu(cpp) * keep_prev
    ybuf[HALO:HALO + t, :] = glu(cp)
    ybuf[HALO + t:HALO + t + HALO, :] = glu(cpn) * keep_next
    xbuf[0:HALO, :] = cpp[:, 2 * CONV_W:].astype(F32) * keep_prev
    xbuf[HALO:HALO + t, :] = cp[:, 2 * CONV_W:].astype(F32)
    xbuf[HALO + t:HALO + t + HALO, :] = cpn[:, 2 * CONV_W:].astype(F32) * keep_next

    n_sh = t + 2 * HALO - 8
    for b in range(1, 8):
        ysh[b - 1, 0:n_sh, :] = ybuf[pl.ds(b, n_sh), :]
    for r0 in range(0, t, CONV_ROWS):
        part = jnp.zeros((CONV_ROWS, CONV_W), F32)
        for k in range(CONV_K):
            a, b = divmod(HALO - CONV_HALF + k, 8)
            src = ybuf if b == 0 else ysh.at[b - 1]
            part = part + dw_ref[k:k + 1, :] * src[8 * a + r0:8 * a + r0 + CONV_ROWS, :]
        cacc[r0:r0 + CONV_ROWS, :] = part + cb_ref[...]
    acc = cacc[...]
    ms = jnp.mean(acc * acc, axis=-1, keepdims=True)
    conv_out = _silu(acc * lax.rsqrt(ms + EPS) * cg_ref[...]).astype(BF16)

    def xs(d):
        return xbuf[pl.ds(HALO + d, t), :]

    x0 = xs(0)
    s2 = xs(-1) + x0
    s4 = s2 + xs(-2) + xs(1)
    s8 = s4 + xs(-4) + xs(-3) + xs(2) + xs(3)
    s16 = s8 + xs(-8) + xs(-7) + xs(-6) + xs(-5) + xs(4) + xs(5) + xs(6) + xs(7)
    grp = lax.broadcasted_iota(jnp.int32, (t, POOL_W), 1) // POOL_GW
    pos = pos_tile * t + lax.broadcasted_iota(jnp.int32, (t, POOL_W), 0)
    half = jnp.where(grp == 0, 1, jnp.where(grp == 1, 2, jnp.where(grp == 2, 4, 8)))
    cnt = (jnp.minimum(pos + half, seq) - jnp.maximum(pos - half, 0)).astype(F32)
    wsum = jnp.where(grp == 0, s2, jnp.where(grp == 1, s4, jnp.where(grp == 2, s8, s16)))
    pool_in = (wsum / cnt - x0).astype(BF16)
    pool_out = (jnp.dot(pool_in, pw_ref[...], preferred_element_type=F32) * ps_ref[...]).astype(BF16)

    def gate(b):
        return gate_ref[:, b * D_MODEL:(b + 1) * D_MODEL].astype(F32)

    merged = gate(0) * jnp.dot(yf_ref[...], wf_ref[...], preferred_element_type=F32)
    merged = merged + gate(1) * jnp.dot(conv_out, wc_ref[...], preferred_element_type=F32)
    merged = merged + gate(2) * jnp.dot(pool_out, wp_ref[...], preferred_element_type=F32)
    merged = merged + gate(3) * jnp.dot(at_ref[...], wa_ref[...], preferred_element_type=F32)
    out = jnp.dot(merged.astype(BF16), wo_ref[...], preferred_element_type=F32)
    o_ref[...] = x_ref[...] + mod_ref[0, 2:3, :] * out


def merge_branches(x2d, mod, proj, yf, attn, lw, *, seq, t):
    m, d = x2d.shape
    tps = seq // t
    hb = t // HALO
    n_halo = m // HALO
    n_mod = mod.shape[0]
    mod_idx = (lambda i: (i // tps, 0, 0)) if n_mod > 1 else (lambda i: (0, 0, 0))
    const = lambda i: (0, 0)
    cp_blk = P_OFF_CP // CP_W
    return pl.pallas_call(
        functools.partial(_merge_kernel, t=t, tps=tps, seq=seq),
        grid=(m // t,),
        in_specs=[
            pl.BlockSpec((t, d), lambda i: (i, 0)),
            pl.BlockSpec((1, 6, d), mod_idx),
            pl.BlockSpec((t, 4 * d), lambda i: (i, 0)),
            pl.BlockSpec((t, CP_W), lambda i: (i, cp_blk)),
            pl.BlockSpec((HALO, CP_W), lambda i: (jnp.maximum(i * hb - 1, 0), cp_blk)),
            pl.BlockSpec((HALO, CP_W), lambda i: (jnp.minimum((i + 1) * hb, n_halo - 1), cp_blk)),
            pl.BlockSpec((t, FOURIER_W), lambda i: (i, 0)),
            pl.BlockSpec((t, Q_W), lambda i: (i, 0)),
            pl.BlockSpec((FOURIER_W, d), const),
            pl.BlockSpec((CONV_W, d), const),
            pl.BlockSpec((POOL_W, d), const),
            pl.BlockSpec((Q_W, d), const),
            pl.BlockSpec((d, d), const),
            pl.BlockSpec((CONV_K, CONV_W), const),
            pl.BlockSpec((1, CONV_W), const),
            pl.BlockSpec((1, CONV_W), const),
            pl.BlockSpec((POOL_W, POOL_W), const),
            pl.BlockSpec((1, POOL_W), const),
        ],
        out_specs=pl.BlockSpec((t, d), lambda i: (i, 0)),
        out_shape=jax.ShapeDtypeStruct((m, d), F32),
        scratch_shapes=[pltpu.VMEM((t + 2 * HALO, CONV_W), F32), pltpu.VMEM((t + 2 * HALO, POOL_W), F32),
                        pltpu.VMEM((7, t + 2 * HALO, CONV_W), F32), pltpu.VMEM((t, CONV_W), F32)],
        compiler_params=_cparams(1),
        name="merge_branches",
    )(x2d, mod, proj, proj, proj, proj, yf, attn,
      lw["wf"], lw["wc"], lw["wp"], lw["wa"], lw["wo"], lw["dw"], lw["cb"], lw["cg"], lw["pw"], lw["ps"])


def _ffn_kernel(x_ref, mod_ref, g_ref, w1_ref, w3_ref, w2_ref, o_ref, *, chunks):
    x = x_ref[...]
    h = _norm_mod(x, g_ref[...], mod_ref[0, 3:4, :], mod_ref[0, 4:5, :]).astype(BF16)
    acc = jnp.zeros(x.shape, F32)
    for c0, cw in chunks:
        a = jnp.dot(h, w1_ref[:, c0:c0 + cw], preferred_element_type=F32)
        b = jnp.dot(h, w3_ref[:, c0:c0 + cw], preferred_element_type=F32)
        acc = acc + jnp.dot((_silu(a) * b).astype(BF16), w2_ref[c0:c0 + cw, :], preferred_element_type=F32)
    o_ref[...] = x + mod_ref[0, 5:6, :] * acc


def ffn_dense(x2d, mod, g, w1, w3, w2, *, tm, tiles_per_mod):
    m, d = x2d.shape
    dff = w1.shape[1]
    chunks = tuple((c0, min(1024, dff - c0)) for c0 in range(0, dff, 1024))
    n_mod = mod.shape[0]
    mod_idx = (lambda i: (i // tiles_per_mod, 0, 0)) if n_mod > 1 else (lambda i: (0, 0, 0))
    const = lambda i: (0, 0)
    return pl.pallas_call(
        functools.partial(_ffn_kernel, chunks=chunks),
        grid=(m // tm,),
        in_specs=[
            pl.BlockSpec((tm, d), lambda i: (i, 0)),
            pl.BlockSpec((1, 6, d), mod_idx),
            pl.BlockSpec((1, d), const),
            pl.BlockSpec((d, dff), const, pipeline_mode=pl.Buffered(1)),
            pl.BlockSpec((d, dff), const, pipeline_mode=pl.Buffered(1)),
            pl.BlockSpec((dff, d), const, pipeline_mode=pl.Buffered(1)),
        ],
        out_specs=pl.BlockSpec((tm, d), lambda i: (i, 0)),
        out_shape=jax.ShapeDtypeStruct((m, d), F32),
        compiler_params=_cparams(1),
        name="ffn_dense",
    )(x2d, mod, g.reshape(1, d), w1, w3, w2)


def _top2(logits):
    t = logits.shape[0]
    lane = lax.broadcasted_iota(jnp.int32, (t, 128), 1).astype(F32)
    neg = jnp.float32(-jnp.inf)
    lg = jnp.where(lane < N_EXPERTS, logits, neg)
    v1 = jnp.max(lg, axis=-1, keepdims=True)
    i1 = jnp.min(jnp.where(lg == v1, lane, 128.0), axis=-1, keepdims=True)
    lg2 = jnp.where(lane == i1, neg, lg)
    v2 = jnp.max(lg2, axis=-1, keepdims=True)
    i2 = jnp.min(jnp.where(lg2 == v2, lane, 128.0), axis=-1, keepdims=True)
    e2 = jnp.exp(v2 - v1)
    return i1, i2, 1.0 / (1.0 + e2), e2 / (1.0 + e2)


R_E1, R_E2, R_W1, R_W2, R_RANK1, R_RANK2 = range(6)


def _route_kernel(x_ref, mod_ref, g_ref, r_ref, tri_ref, route_ref, cnt_ref, carry_ref):
    @pl.when(pl.program_id(0) == 0)
    def _():
        carry_ref[...] = jnp.zeros(carry_ref.shape, F32)

    t = x_ref.shape[0]
    h = _norm_mod(x_ref[...], g_ref[...], mod_ref[0, 3:4, :], mod_ref[0, 4:5, :])
    logits = jnp.dot(h, r_ref[...], preferred_element_type=F32, precision=lax.Precision.HIGHEST)
    i1, i2, w1, w2 = _top2(logits)
    lane = lax.broadcasted_iota(jnp.int32, (t, 128), 1).astype(F32)
    oh1 = jnp.where(lane == i1, 1.0, 0.0)
    oh2 = jnp.where(lane == i2, 1.0, 0.0)
    both = oh1 + oh2
    before = carry_ref[...] + jnp.dot(tri_ref[...], both.astype(BF16), preferred_element_type=F32)
    rank1 = jnp.sum(oh1 * before, axis=-1, keepdims=True)
    rank2 = jnp.sum(oh2 * before, axis=-1, keepdims=True)
    carry_ref[...] += jnp.sum(both, axis=0, keepdims=True)
    rec = jnp.zeros((t, 128), F32)
    for col, val in ((R_E1, i1), (R_E2, i2), (R_W1, w1), (R_W2, w2), (R_RANK1, rank1), (R_RANK2, rank2)):
        rec = jnp.where(lane == col, val, rec)
    route_ref[...] = rec
    cnt_ref[...] = carry_ref[...]


def moe_route(x2d, mod, g, router_pad, *, tm, tiles_per_mod):
    m, d = x2d.shape
    n_mod = mod.shape[0]
    mod_idx = (lambda i: (i // tiles_per_mod, 0, 0)) if n_mod > 1 else (lambda i: (0, 0, 0))
    tri = jnp.asarray(np.tril(np.ones((tm, tm), np.float32), -1), BF16)
    return pl.pallas_call(
        _route_kernel,
        grid=(m // tm,),
        in_specs=[
            pl.BlockSpec((tm, d), lambda i: (i, 0)),
            pl.BlockSpec((1, 6, d), mod_idx),
            pl.BlockSpec((1, d), lambda i: (0, 0)),
            pl.BlockSpec((d, 128), lambda i: (0, 0)),
            pl.BlockSpec((tm, tm), lambda i: (0, 0)),
        ],
        out_specs=[pl.BlockSpec((tm, 128), lambda i: (i, 0)), pl.BlockSpec((1, 128), lambda i: (0, 0))],
        out_shape=[jax.ShapeDtypeStruct((m, 128), F32), jax.ShapeDtypeStruct((1, 128), F32)],
        scratch_shapes=[pltpu.VMEM((1, 128), F32)],
        compiler_params=_cparams(1),
        name="moe_route",
    )(x2d, mod, g.reshape(1, d), router_pad, tri)


def _dispatch_kernel(pos_ref, pad_ref, x_ref, mod_ref, g_ref, xs_ref, h_ref, zrow_ref, sem, zsem):
    i = pl.program_id(0)
    t = x_ref.shape[0]
    slot = i % 2
    h_ref[slot] = _norm_mod(x_ref[...], g_ref[...], mod_ref[0, 3:4, :], mod_ref[0, 4:5, :])

    def row_copy(r, dst_row):
        return pltpu.make_async_copy(h_ref.at[slot, pl.ds(r, 1), :], xs_ref.at[pl.ds(dst_row, 1), :], sem.at[slot])

    def issue(r, carry):
        row_copy(r, pos_ref[0, 0, r]).start()
        row_copy(r, pos_ref[0, 0, t + r]).start()
        return carry

    lax.fori_loop(0, t, issue, 0, unroll=DMA_ISSUE_UNROLL)

    def drain(s):
        for _ in range(2):
            pltpu.make_async_copy(h_ref.at[s], xs_ref.at[pl.ds(0, t), :], sem.at[s]).wait()

    @pl.when(i > 0)
    def _():
        drain(1 - slot)

    @pl.when(i == pl.num_programs(0) - 1)
    def _():
        drain(slot)
        zrow_ref[...] = jnp.zeros(zrow_ref.shape, F32)

        def zero_copy(r):
            return pltpu.make_async_copy(zrow_ref.at[pl.ds(0, 1), :], xs_ref.at[pl.ds(r, 1), :], zsem)

        def start_one(r, carry):
            zero_copy(r).start()
            return carry

        def wait_one(r, carry):
            zero_copy(r).wait()
            return carry

        for e in range(N_EXPERTS):
            lax.fori_loop(pad_ref[0, e], pad_ref[0, N_EXPERTS + e], start_one, 0)
        for e in range(N_EXPERTS):
            lax.fori_loop(pad_ref[0, e], pad_ref[0, N_EXPERTS + e], wait_one, 0)

        h_ref[1 - slot] = jnp.zeros((t, h_ref.shape[2]), F32)

        def tile_copy(k):
            return pltpu.make_async_copy(h_ref.at[1 - slot], xs_ref.at[pl.ds(pl.multiple_of(k * t, t), t), :], zsem)

        def start_tile(k, carry):
            tile_copy(k).start()
            return carry

        def wait_tile(k, carry):
            tile_copy(k).wait()
            return carry

        first_unused, n_tiles = pad_ref[0, 2 * N_EXPERTS], xs_ref.shape[0] // t
        lax.fori_loop(first_unused, n_tiles, start_tile, 0)
        lax.fori_loop(first_unused, n_tiles, wait_tile, 0)


def moe_dispatch(x2d, mod, g, pos_tiles, pad_rows, n_rows, *, tm, tiles_per_mod):
    m, d = x2d.shape
    n_mod = mod.shape[0]
    mod_idx = (lambda i: (i // tiles_per_mod, 0, 0)) if n_mod > 1 else (lambda i: (0, 0, 0))
    return pl.pallas_call(
        _dispatch_kernel,
        grid=(m // tm,),
        in_specs=[
            pl.BlockSpec((1, 1, 2 * tm), lambda i: (i, 0, 0), memory_space=pltpu.SMEM),
            pl.BlockSpec((1, 2 * N_EXPERTS + 1), lambda i: (0, 0), memory_space=pltpu.SMEM),
            pl.BlockSpec((tm, d), lambda i: (i, 0)),
            pl.BlockSpec((1, 6, d), mod_idx),
            pl.BlockSpec((1, d), lambda i: (0, 0)),
        ],
        out_specs=pl.BlockSpec(memory_space=pl.ANY),
        out_shape=jax.ShapeDtypeStruct((n_rows, d), F32),
        scratch_shapes=[pltpu.VMEM((2, tm, d), F32), pltpu.VMEM((8, d), F32),
                        pltpu.SemaphoreType.DMA((2,)), pltpu.SemaphoreType.DMA(())],
        compiler_params=_cparams(1),
        name="moe_dispatch",
    )(pos_tiles, pad_rows, x2d, mod, g.reshape(1, d))


def _experts_kernel(te_ref, nv_ref, xs_ref, w1_ref, w3_ref, w2_ref, ys_ref, xb_ref, acc_ref):
    i = pl.program_id(0)
    j = pl.program_id(1)
    valid = i < nv_ref[0]

    @pl.when(jnp.logical_and(valid, j == 0))
    def _():
        xb_ref[...] = xs_ref[...].astype(BF16)
        acc_ref[...] = jnp.zeros(acc_ref.shape, F32)

    @pl.when(valid)
    def _():
        h = xb_ref[...]
        a = jnp.dot(h, w1_ref[0], preferred_element_type=F32)
        b = jnp.dot(h, w3_ref[0], preferred_element_type=F32)
        acc_ref[...] += jnp.dot((_silu(a) * b).astype(BF16), w2_ref[0], preferred_element_type=F32)

    @pl.when(jnp.logical_and(valid, j == pl.num_programs(1) - 1))
    def _():
        ys_ref[...] = acc_ref[...]

    @pl.when(jnp.logical_and(jnp.logical_not(valid), j == pl.num_programs(1) - 1))
    def _():
        ys_ref[...] = jnp.zeros(ys_ref.shape, F32)


def moe_experts_grouped(xs, tile_expert, n_valid, w1, w3, w2, *, tm, tf):
    n_rows, d = xs.shape
    dff = w1.shape[2]
    nf = dff // tf

    def w13_idx(i, j, te, nv):
        return (te[i], 0, jnp.where(i < nv[0], j, nf - 1))

    def w2_idx(i, j, te, nv):
        return (te[i], jnp.where(i < nv[0], j, nf - 1), 0)

    grid_spec = pltpu.PrefetchScalarGridSpec(
        num_scalar_prefetch=2,
        grid=(n_rows // tm, nf),
        in_specs=[
            pl.BlockSpec((tm, d), lambda i, j, te, nv: (jnp.minimum(i, nv[0] - 1), 0)),
            pl.BlockSpec((1, d, tf), w13_idx),
            pl.BlockSpec((1, d, tf), w13_idx),
            pl.BlockSpec((1, tf, d), w2_idx),
        ],
        out_specs=pl.BlockSpec((tm, d), lambda i, j, te, nv: (i, 0)),
        scratch_shapes=[pltpu.VMEM((tm, d), BF16), pltpu.VMEM((tm, d), F32)],
    )
    return pl.pallas_call(
        _experts_kernel,
        grid_spec=grid_spec,
        out_shape=jax.ShapeDtypeStruct((n_rows, d), F32),
        compiler_params=_cparams(2),
        name="moe_experts_grouped",
    )(tile_expert, n_valid, xs, w1, w3, w2)


def _combine_kernel(pos_ref, pos_next_ref, x_ref, mod_ref, rt_ref, ys_ref, o_ref, y_ref, sem):
    i = pl.program_id(0)
    t = x_ref.shape[0]
    slot = i % 2

    def issue_tile(p_ref, s):
        def issue(r, carry):
            for k in range(2):
                pltpu.make_async_copy(ys_ref.at[pl.ds(p_ref[0, 0, k * t + r], 1), :],
                                      y_ref.at[s, k, pl.ds(r, 1), :], sem.at[s]).start()
            return carry

        lax.fori_loop(0, t, issue, 0, unroll=DMA_ISSUE_UNROLL)

    @pl.when(i == 0)
    def _():
        issue_tile(pos_ref, 0)

    @pl.when(i + 1 < pl.num_programs(0))
    def _():
        issue_tile(pos_next_ref, 1 - slot)

    for k in range(2):
        pltpu.make_async_copy(ys_ref.at[pl.ds(0, t), :], y_ref.at[slot, k], sem.at[slot]).wait()
    rt = rt_ref[...]
    mix = rt[:, R_W1:R_W1 + 1] * y_ref[slot, 0] + rt[:, R_W2:R_W2 + 1] * y_ref[slot, 1]
    o_ref[...] = x_ref[...] + mod_ref[0, 5:6, :] * mix


def moe_combine(x2d, mod, route, pos_tiles, ys, *, tm, tiles_per_mod):
    m, d = x2d.shape
    n_mod = mod.shape[0]
    mod_idx = (lambda i: (i // tiles_per_mod, 0, 0)) if n_mod > 1 else (lambda i: (0, 0, 0))
    n_tiles = m // tm
    return pl.pallas_call(
        _combine_kernel,
        grid=(n_tiles,),
        in_specs=[
            pl.BlockSpec((1, 1, 2 * tm), lambda i: (i, 0, 0), memory_space=pltpu.SMEM),
            pl.BlockSpec((1, 1, 2 * tm), lambda i: (jnp.minimum(i + 1, n_tiles - 1), 0, 0), memory_space=pltpu.SMEM),
            pl.BlockSpec((tm, d), lambda i: (i, 0)),
            pl.BlockSpec((1, 6, d), mod_idx),
            pl.BlockSpec((tm, 128), lambda i: (i, 0)),
            pl.BlockSpec(memory_space=pl.ANY),
        ],
        out_specs=pl.BlockSpec((tm, d), lambda i: (i, 0)),
        out_shape=jax.ShapeDtypeStruct((m, d), F32),
        scratch_shapes=[pltpu.VMEM((2, 2, tm, d), F32), pltpu.SemaphoreType.DMA((2,))],
        compiler_params=_cparams(1),
        name="moe_combine",
    )(pos_tiles, pos_tiles, x2d, mod, route, ys)


def _pos_tiles(pos1, pos2, tm):
    n = pos1.shape[0] // tm
    return jnp.concatenate([pos1.reshape(n, 1, tm), pos2.reshape(n, 1, tm)], axis=2)


def moe_sparse(x2d, mod, g, router_pad, w1, w3, w2, *, rows_per_mod):
    m, d = x2d.shape
    tr, td, tc, te = MOE_ROUTE_TM, MOE_DISPATCH_TM, MOE_COMBINE_TM, MOE_EXPERT_TM
    tiles_per_seq_row = rows_per_mod
    route, cnt = moe_route(x2d, mod, g, router_pad, tm=tr, tiles_per_mod=tiles_per_seq_row // tr)
    counts = cnt[0, 0:N_EXPERTS].astype(jnp.int32)
    group = ((counts + te - 1) // te) * te
    ends = jnp.cumsum(group)
    starts = ends - group
    e1 = route[:, R_E1].astype(jnp.int32)
    e2 = route[:, R_E2].astype(jnp.int32)
    pos1 = starts[e1] + route[:, R_RANK1].astype(jnp.int32)
    pos2 = starts[e2] + route[:, R_RANK2].astype(jnp.int32)
    n_rows = 2 * m + N_EXPERTS * te
    n_tiles = n_rows // te
    tile_start = jnp.arange(n_tiles, dtype=jnp.int32)[:, None] * te
    tile_expert = jnp.minimum(jnp.sum((tile_start >= ends[None, :]).astype(jnp.int32), axis=1), N_EXPERTS - 1)
    n_valid = (ends[-1:] // te).astype(jnp.int32)
    assert td == te
    pad_rows = jnp.concatenate([starts + counts, ends, n_valid]).astype(jnp.int32).reshape(1, 2 * N_EXPERTS + 1)
    xs = moe_dispatch(x2d, mod, g, _pos_tiles(pos1, pos2, td), pad_rows, n_rows, tm=td,
                      tiles_per_mod=tiles_per_seq_row // td)
    ys = moe_experts_grouped(xs, tile_expert, n_valid, w1, w3, w2, tm=te, tf=MOE_EXPERT_TF)
    return moe_combine(x2d, mod, route, _pos_tiles(pos1, pos2, tc), ys, tm=tc, tiles_per_mod=tiles_per_seq_row // tc)


def _permute_w_in(w):
    f, c, p, q, kv, gts = w[:, 0:256], w[:, 256:768], w[:, 768:1024], w[:, 1024:1536], w[:, 1536:1792], w[:, 1792:]
    return jnp.concatenate([gts, q, c, p, f, kv], axis=1).astype(BF16)


def _layer_weights(layer, w_br_fourier, conv_dw, conv_b, conv_norm_g, w_br_conv, pool_w, pool_scale, w_br_pool,
                   w_br_attn, w_out):
    pw = jax.scipy.linalg.block_diag(*[pool_w[layer, i] for i in range(len(POOL_WINDOWS))])
    return {
        "wf": w_br_fourier[layer].astype(BF16), "wc": w_br_conv[layer].astype(BF16),
        "wp": w_br_pool[layer].astype(BF16), "wa": w_br_attn[layer].astype(BF16), "wo": w_out[layer].astype(BF16),
        "dw": conv_dw[layer], "cb": conv_b[layer].reshape(1, CONV_W), "cg": conv_norm_g[layer].reshape(1, CONV_W),
        "pw": pw.astype(BF16), "ps": pool_scale[layer].reshape(1, POOL_W),
    }


def kernel(x, c, ctx, c_ctx, w_mod, b_mod, norm1_g, norm2_g, w_in, w_br_fourier, conv_dw, conv_b, conv_norm_g,
           w_br_conv, pool_w, pool_scale, w_br_pool, q_norm_g, k_norm_g, w_br_attn, w_out, ffn_w1, ffn_w3, ffn_w2,
           moe_router, moe_w1, moe_w3, moe_w2):
    batch, seq, d = x.shape
    ctx_len = ctx.shape[1]
    depth = w_in.shape[0]
    rope = rope_tables(seq)

    c_rows = jnp.zeros((8, d), F32).at[0:batch].set(c).at[batch].set(c_ctx)
    mods = modulation_all(c_rows, w_mod, b_mod).reshape(depth, 8, 6, d)

    xl = x.reshape(batch * seq, d)
    xc = ctx.reshape(batch * ctx_len, d)
    for layer in range(depth):
        is_last = layer == depth - 1
        mod_l = mods[layer, 0:batch]
        mod_c = mods[layer, batch:batch + 1]
        w_in_l = _permute_w_in(w_in[layer])
        lw = _layer_weights(layer, w_br_fourier, conv_dw, conv_b, conv_norm_g, w_br_conv, pool_w, pool_scale,
                            w_br_pool, w_br_attn, w_out)

        proj_c, qc, ktc, vc = input_projection(xc, mod_c, norm1_g[layer], w_in_l, q_norm_g[layer], k_norm_g[layer],
                                               None, batch=batch, seq=ctx_len, tm=256)

        proj, q, kt, v = input_projection(xl, mod_l, norm1_g[layer], w_in_l, q_norm_g[layer], k_norm_g[layer], rope,
                                          batch=batch, seq=seq, tm=512)
        spread = (2.0 * 1.02 * HEAD_DIM * Q_SCALE) * jnp.max(jnp.abs(q_norm_g[layer])) * jnp.max(
            jnp.abs(k_norm_g[layer]))
        attn = lax.cond(
            spread < STALE_MAX_EXP_LIMIT,
            lambda ops: attention_stale_max(*ops, batch=batch, seq_q=seq, tq=256, tk=1024),
            lambda ops: attention(*ops, batch=batch, seq_q=seq, tq=256, tk=1024),
            (q, kt, v, ktc, vc))
        yf = fourier_mix(proj[:, P_OFF_F:P_OFF_F + FOURIER_W], batch=batch, seq=seq, n1=64, n2=seq // 64)
        xl = merge_branches(xl, mod_l, proj, yf, attn, lw, seq=seq, t=512)

        if not is_last:
            attn_c = attention(qc, None, None, ktc, vc, batch=batch, seq_q=ctx_len, tq=256, tk=ctx_len)
            yf_c = fourier_mix(proj_c[:, P_OFF_F:P_OFF_F + FOURIER_W], batch=batch, seq=ctx_len, n1=16,
                               n2=ctx_len // 16)
            xc = merge_branches(xc, mod_c, proj_c, yf_c, attn_c, lw, seq=ctx_len, t=256)

        j = layer // 2
        if layer % 2 == 0:
            w1, w3, w2 = ffn_w1[j].astype(BF16), ffn_w3[j].astype(BF16), ffn_w2[j].astype(BF16)
            xl = ffn_dense(xl, mod_l, norm2_g[layer], w1, w3, w2, tm=512, tiles_per_mod=seq // 512)
            if not is_last:
                xc = ffn_dense(xc, mod_c, norm2_g[layer], w1, w3, w2, tm=256, tiles_per_mod=1)
        else:
            w1, w3, w2 = moe_w1[j].astype(BF16), moe_w3[j].astype(BF16), moe_w2[j].astype(BF16)
            router_pad = jnp.zeros((d, 128), F32).at[:, 0:N_EXPERTS].set(moe_router[j])
            xl = moe_sparse(xl, mod_l, norm2_g[layer], router_pad, w1, w3, w2, rows_per_mod=seq)
            if not is_last:
                xc = moe_sparse(xc, mod_c, norm2_g[layer], router_pad, w1, w3, w2, rows_per_mod=batch * ctx_len)
    return xl.reshape(batch, seq, d)
```

```python
import functools
import math

import numpy as np
import jax
import jax.numpy as jnp
from jax import lax
from jax.experimental import pallas as pl
from jax.experimental.pallas import tpu as pltpu

F32 = jnp.float32
BF16 = jnp.bfloat16

D_MODEL = 1024
GRID_W = 64
EPS = 1e-6
FOURIER_GW = 64
FOURIER_W = 256
CONV_W = 256
CONV_K = 31
CONV_HALF = CONV_K // 2
POOL_WINDOWS = (2, 4, 8, 16)
POOL_GW = 64
POOL_W = 256
HEAD_DIM = 64
N_HEADS = 8
N_KV_HEADS = 2
Q_PER_KV = 4
Q_W = 512
KV_W = 128
ROPE_THETA = 10000.0
N_EXPERTS = 8
IN_W = 5888

P_OFF_G = 0
P_OFF_Q = 4096
P_OFF_CP = 4608
P_OFF_F = 5376
P_OFF_KV = 5632
CP_W = 2 * CONV_W + POOL_W

Q_SCALE = (HEAD_DIM ** -0.5) * math.log2(math.e)

ATTN_TQ = 256
STALE_MAX_EXP_LIMIT = 64.0

MOE_ROUTE_TM = 512
MOE_DISPATCH_TM = 512
MOE_COMBINE_TM = 256
MOE_EXPERT_TM = 512
MOE_EXPERT_TF = 1792
DMA_ISSUE_UNROLL = 8

CONV_ROWS = 64
HALO = 16
VMEM_LIMIT = 56 * 1024 * 1024


def _cparams(n_axes):
    return pltpu.CompilerParams(dimension_semantics=("arbitrary",) * n_axes, vmem_limit_bytes=VMEM_LIMIT)


def _sigmoid(v):
    return 0.5 * jnp.tanh(0.5 * v) + 0.5


def _silu(v):
    return v * _sigmoid(v)


def _norm_mod(x, g, shift, scale):
    ms = jnp.mean(x * x, axis=-1, keepdims=True)
    return x * lax.rsqrt(ms + EPS) * g * (1.0 + scale) + shift


def _mod_kernel(c_ref, w_ref, b_ref, o_ref):
    s = _silu(c_ref[...])
    o_ref[0] = jnp.dot(s, w_ref[0], preferred_element_type=F32, precision=lax.Precision.HIGHEST) + b_ref[0]


def modulation_all(c_rows, w_mod, b_mod):
    n_layers, d, n = w_mod.shape
    tn = 1536
    return pl.pallas_call(
        _mod_kernel,
        grid=(n_layers, n // tn),
        in_specs=[
            pl.BlockSpec((8, d), lambda l, j: (0, 0)),
            pl.BlockSpec((1, d, tn), lambda l, j: (l, 0, j)),
            pl.BlockSpec((1, 1, tn), lambda l, j: (l, 0, j)),
        ],
        out_specs=pl.BlockSpec((1, 8, tn), lambda l, j: (l, 0, j)),
        out_shape=jax.ShapeDtypeStruct((n_layers, 8, n), F32),
        compiler_params=_cparams(2),
        name="modulation",
    )(c_rows, w_mod, b_mod.reshape(n_layers, 1, n))


def _inproj_kernel(*refs, chunks, use_rope):
    if use_rope:
        x_ref, mod_ref, g_ref, w_ref, gq_ref, gk_ref, ones_ref, cos_ref, sin_ref, o_ref, qo_ref, kt_ref, v_ref = refs
        cos, sin = cos_ref[...], sin_ref[...]
    else:
        x_ref, mod_ref, g_ref, w_ref, gq_ref, gk_ref, ones_ref, o_ref, qo_ref, kt_ref, v_ref = refs
        cos = sin = None
    h = _norm_mod(x_ref[...], g_ref[...], mod_ref[0, 0:1, :], mod_ref[0, 1:2, :]).astype(BF16)
    for c0, cw in chunks:
        r = jnp.dot(h, w_ref[:, c0:c0 + cw], preferred_element_type=F32)
        if c0 + cw <= P_OFF_Q:
            r = _sigmoid(r)
        elif c0 == P_OFF_CP:
            sg = _sigmoid(r[:, CONV_W:2 * CONV_W])
            r = jnp.concatenate([r[:, 0:CONV_W] * sg, sg], axis=1)
        elif c0 == P_OFF_Q:
            _q_epilogue(r, gq_ref[...], ones_ref[...], cos, sin, qo_ref)
        elif c0 == P_OFF_KV:
            _kv_epilogue(r, gk_ref[...], ones_ref[...], cos, sin, kt_ref, v_ref)
        o_ref[:, c0:c0 + cw] = r.astype(o_ref.dtype)


def input_projection(x2d, mod, g, w_bf16, gq, gk, rope, *, batch, seq, tm):
    m, d = x2d.shape
    n = w_bf16.shape[1]
    tps = seq // tm
    chunks = tuple((c0, min(512, n - c0)) for c0 in range(0, n, 512))
    assert P_OFF_Q % 512 == 0 and {(P_OFF_CP, 2 * CONV_W), (P_OFF_Q, Q_W), (P_OFF_KV, 2 * KV_W)} <= set(chunks)
    use_rope = rope is not None
    n_mod = mod.shape[0]
    mod_idx = (lambda i: (i // tps, 0, 0)) if n_mod > 1 else (lambda i: (0, 0, 0))
    const = lambda i: (0, 0)
    ones_bd = jnp.asarray(np.kron(np.eye(2, dtype=np.float32), np.ones((64, 64), np.float32)), BF16)
    in_specs = [
        pl.BlockSpec((tm, d), lambda i: (i, 0)),
        pl.BlockSpec((1, 6, d), mod_idx),
        pl.BlockSpec((1, d), const),
        pl.BlockSpec((d, n), const, pipeline_mode=pl.Buffered(1)),
        pl.BlockSpec((1, 128), const),
        pl.BlockSpec((1, 128), const),
        pl.BlockSpec((128, 128), const),
    ]
    args = [x2d, mod, g.reshape(1, d), w_bf16, jnp.tile(gq, 2).reshape(1, 128), jnp.tile(gk, 2).reshape(1, 128), ones_bd]
    if use_rope:
        in_specs += [pl.BlockSpec((tm, 128), lambda i: (i % tps, 0))] * 2
        args += list(rope)
    return pl.pallas_call(
        functools.partial(_inproj_kernel, chunks=chunks, use_rope=use_rope),
        grid=(m // tm,),
        in_specs=in_specs,
        out_specs=[
            pl.BlockSpec((tm, n), lambda i: (i, 0)),
            pl.BlockSpec((1, N_KV_HEADS, 256, tm), lambda i: (i // tps, 0, 0, i % tps)),
            pl.BlockSpec((1, N_KV_HEADS, tm, 256), lambda i: (i // tps, 0, i % tps, 0)),
            pl.BlockSpec((1, N_KV_HEADS, 128, tm), lambda i: (i // tps, 0, 0, i % tps)),
        ],
        out_shape=[
            jax.ShapeDtypeStruct((m, n), BF16),
            jax.ShapeDtypeStruct((batch, N_KV_HEADS, 256, seq), BF16),
            jax.ShapeDtypeStruct((batch, N_KV_HEADS, seq, 256), BF16),
            jax.ShapeDtypeStruct((batch, N_KV_HEADS, 128, seq), BF16),
        ],
        compiler_params=_cparams(1),
        name="input_projection",
    )(*args)


def _seg_sum64(v, ones_bd):
    hi = v.astype(BF16)
    lo = (v - hi.astype(F32)).astype(BF16)
    return (jnp.dot(hi, ones_bd, preferred_element_type=F32) + jnp.dot(lo, ones_bd, preferred_element_type=F32))


def _head_norm_rope(x, g, ones_bd, cos, sin, low_mask):
    y = x * lax.rsqrt(_seg_sum64(x * x, ones_bd) * (1.0 / HEAD_DIM) + EPS) * g
    if cos is None:
        return y
    partner = jnp.where(low_mask, pltpu.roll(y, 128 - 16, axis=1), pltpu.roll(y, 16, axis=1))
    return y * cos + partner * sin


def _rope_low_mask(t):
    return (lax.broadcasted_iota(jnp.int32, (t, 128), 1) % 32) < 16


def _q_epilogue(rq, gq, ones_bd, cos, sin, qo_ref):
    low_mask = _rope_low_mask(rq.shape[0])
    for c in range(Q_W // 128):
        yq = _head_norm_rope(rq[:, 128 * c:128 * (c + 1)], gq, ones_bd, cos, sin, low_mask) * Q_SCALE
        qo_ref[0, c // 2, 128 * (c % 2):128 * (c % 2 + 1), :] = yq.T.astype(BF16)


def _kv_epilogue(rkv, gk, ones_bd, cos, sin, kt_ref, v_ref):
    t = rkv.shape[0]
    yk = _head_norm_rope(rkv[:, 0:128], gk, ones_bd, cos, sin, _rope_low_mask(t))
    ykr = pltpu.roll(yk, 64, axis=1)
    first = lax.broadcasted_iota(jnp.int32, (t, 128), 1) < 64
    k0 = jnp.where(first, yk, ykr).astype(BF16)
    k1 = jnp.where(first, ykr, yk).astype(BF16)
    kt_ref[0, 0] = jnp.concatenate([k0, k0], axis=1)
    kt_ref[0, 1] = jnp.concatenate([k1, k1], axis=1)
    vt = rkv[:, 128:256].T
    ones = jnp.ones((HEAD_DIM, t), F32)
    for h in range(N_KV_HEADS):
        v_ref[0, h] = jnp.concatenate([vt[64 * h:64 * (h + 1), :], ones], axis=0).astype(BF16)


def rope_tables(seq):
    n_freq = HEAD_DIM // 4
    freqs = ROPE_THETA ** (-jnp.arange(n_freq, dtype=F32) / n_freq)
    t = jnp.arange(seq)
    row = (t // GRID_W).astype(F32)
    col = (t % GRID_W).astype(F32)
    ang_r = row[:, None] * freqs
    ang_c = col[:, None] * freqs
    cos = jnp.concatenate([jnp.cos(ang_r)] * 2 + [jnp.cos(ang_c)] * 2, axis=1)
    sin = jnp.concatenate([-jnp.sin(ang_r), jnp.sin(ang_r), -jnp.sin(ang_c), jnp.sin(ang_c)], axis=1)
    return jnp.tile(cos, (1, 2)), jnp.tile(sin, (1, 2))


def _attn_kernel(*refs, tq, tk, nk, tail):
    refs = list(refs)
    qt_ref = refs.pop(0)
    k_ref, vt_ref = (refs.pop(0), refs.pop(0)) if nk else (None, None)
    kc_ref, vtc_ref = (refs.pop(0), refs.pop(0)) if tail else (None, None)
    o_ref, qs_ref, s0, s1, p0, p1, a0, a1, mx0, mx1, m_ref, acc_ref = refs
    s_bufs, p_bufs, a_bufs, mx_bufs = (s0, s1), (p0, p1), (a0, a1), (mx0, mx1)
    n_blocks = nk + (1 if tail else 0)

    _attn_stack_queries(qt_ref, qs_ref, tq)
    m_ref[...] = jnp.full(m_ref.shape, -jnp.inf, F32)
    acc_ref[...] = jnp.zeros(acc_ref.shape, F32)

    def block(t):
        if isinstance(t, int) and t >= nk:
            return kc_ref[0, 0], vtc_ref[0, 0], tail
        off = t * tk if isinstance(t, int) else pl.multiple_of(t * tk, tk)
        return k_ref[0, 0, pl.ds(off, tk), :], vt_ref[0, 0, :, pl.ds(off, tk)], tk

    def scores(t, slot):
        k_rows, _, n = block(t)
        s = jnp.dot(k_rows, qs_ref[...], preferred_element_type=F32)
        s_bufs[slot][0:n, :] = s
        mx_bufs[slot][...] = jnp.max(s, axis=0, keepdims=True)

    def numerators(n, slot):
        s_ref, p_ref, a_ref = s_bufs[slot], p_bufs[slot], a_bufs[slot]
        for c0 in range(0, Q_PER_KV * tq, 128):
            cols = slice(c0, c0 + 128)
            m_old = m_ref[:, cols]
            m_new = jnp.maximum(m_old, mx_bufs[slot][:, cols])
            a_ref[:, cols] = jnp.exp2(m_old - m_new)
            p_ref[0:n, cols] = jnp.exp2(s_ref[0:n, cols] - m_new).astype(BF16)
            m_ref[:, cols] = m_new

    def weighted_sum(t, slot):
        _, vt, n = block(t)
        pv = jnp.dot(vt, p_bufs[slot][0:n, :], preferred_element_type=F32)
        acc_ref[...] = a_bufs[slot][...] * acc_ref[...] + pv

    def rows_of(t):
        return tk if t < nk else tail

    def step(t, slot, n_mid):
        scores(t, slot)
        numerators(n_mid, 1 - slot)
        weighted_sum(t - 2, slot)

    scores(0, 0)
    if n_blocks > 1:
        scores(1, 1)
        numerators(rows_of(0), 0)
        n_pairs = max(nk - 2, 0) // 2

        def pair(i, carry):
            t = 2 + 2 * i
            step(t, 0, tk)
            step(t + 1, 1, tk)
            return carry

        if n_pairs:
            lax.fori_loop(0, n_pairs, pair, 0)
        for t in range(2 + 2 * n_pairs, n_blocks):
            step(t, t % 2, rows_of(t - 1))
        last = n_blocks - 1
        numerators(rows_of(last), last % 2)
        weighted_sum(last - 1, (last - 1) % 2)
        weighted_sum(last, last % 2)
    else:
        numerators(rows_of(0), 0)
        weighted_sum(0, 0)

    _attn_write_output(acc_ref, o_ref, tq)


def _attn_write_output(acc_ref, o_ref, tq):
    acc = acc_ref[...]
    ot = acc[0:HEAD_DIM, :] / acc[HEAD_DIM:2 * HEAD_DIM, :]
    for half in range(2):
        pair_t = jnp.concatenate([ot[:, (2 * half) * tq:(2 * half + 1) * tq],
                                  ot[:, (2 * half + 1) * tq:(2 * half + 2) * tq]], axis=0)
        o_ref[:, 128 * half:128 * (half + 1)] = pair_t.T.astype(o_ref.dtype)


def _attn_stack_queries(qt_ref, qs_ref, tq):
    row_group = lax.broadcasted_iota(jnp.int32, (256, tq), 0) // HEAD_DIM
    qt = qt_ref[0, 0]
    for g in range(Q_PER_KV):
        qs_ref[:, g * tq:(g + 1) * tq] = jnp.where(row_group == g, qt, jnp.zeros_like(qt))


def _attn_stale_max_kernel(*refs, tq, tk, nk, n_side):
    qt_ref, k_ref, vt_ref, kc_ref, vtc_ref = refs[:5]
    side_in = refs[5:5 + n_side]
    o_ref = refs[5 + n_side]
    side_out = refs[6 + n_side:6 + 2 * n_side]
    qs_ref, p0, p1, f0, f1, m_ref, acc_ref = refs[6 + 2 * n_side:]
    for src, dst in zip(side_in, side_out):
        dst[...] = src[...].astype(dst.dtype)
    p_bufs, f_bufs = (p0, p1), (f0, f1)
    _attn_stack_queries(qt_ref, qs_ref, tq)

    s = jnp.dot(kc_ref[0, 0], qs_ref[...], preferred_element_type=F32)
    m0 = jnp.max(s, axis=0, keepdims=True)
    m_ref[...] = m0
    acc_ref[...] = jnp.dot(vtc_ref[0, 0], jnp.exp2(s - m0).astype(BF16), preferred_element_type=F32)

    def numerators(t, slot):
        off = t * tk if isinstance(t, int) else pl.multiple_of(t * tk, tk)
        s = jnp.dot(k_ref[0, 0, pl.ds(off, tk), :], qs_ref[...], preferred_element_type=F32)
        m_old = m_ref[...]
        p_bufs[slot][...] = jnp.exp2(s - m_old).astype(BF16)
        m_new = jnp.maximum(m_old, jnp.max(s, axis=0, keepdims=True))
        f_bufs[slot][...] = jnp.exp2(m_old - m_new)
        m_ref[...] = m_new

    def weighted_sum(t, slot):
        off = t * tk if isinstance(t, int) else pl.multiple_of(t * tk, tk)
        pv = jnp.dot(vt_ref[0, 0, :, pl.ds(off, tk)], p_bufs[slot][...], preferred_element_type=F32)
        acc_ref[...] = (acc_ref[...] + pv) * f_bufs[slot][...]

    def step(t, slot):
        numerators(t, slot)
        weighted_sum(t - 1, 1 - slot)

    numerators(0, 0)
    n_pairs = (nk - 1) // 2

    def pair(i, carry):
        t = 1 + 2 * i
        step(t, 1)
        step(t + 1, 0)
        return carry

    if n_pairs:
        lax.fori_loop(0, n_pairs, pair, 0)
    for t in range(1 + 2 * n_pairs, nk):
        step(t, t % 2)
    weighted_sum(nk - 1, (nk - 1) % 2)
    _attn_write_output(acc_ref, o_ref, tq)


def attention_stale_max(qt, k4, vt1, k4_tail, vt1_tail, side=(), *, batch, seq_q, tq, tk):
    nq = seq_q // tq
    lanes = Q_PER_KV * tq
    lk = k4.shape[2]
    tail = k4_tail.shape[2]
    n_steps = batch * N_KV_HEADS * nq
    assert all(n_steps % w.shape[0] == 0 for w in side)

    def side_spec(w):
        repeat = n_steps // w.shape[0]
        return pl.BlockSpec((1,) + w.shape[1:], lambda b, h, i: (((b * N_KV_HEADS + h) * nq + i) // repeat, 0, 0))

    side_specs = [side_spec(w) for w in side]
    outs = pl.pallas_call(
        functools.partial(_attn_stale_max_kernel, tq=tq, tk=tk, nk=lk // tk, n_side=len(side)),
        grid=(batch, N_KV_HEADS, nq),
        in_specs=[
            pl.BlockSpec((1, 1, 256, tq), lambda b, h, i: (b, h, 0, i)),
            pl.BlockSpec((1, 1, lk, 256), lambda b, h, i: (b, h, 0, 0)),
            pl.BlockSpec((1, 1, 128, lk), lambda b, h, i: (b, h, 0, 0)),
            pl.BlockSpec((1, 1, tail, 256), lambda b, h, i: (b, h, 0, 0)),
            pl.BlockSpec((1, 1, 128, tail), lambda b, h, i: (b, h, 0, 0)),
        ] + side_specs,
        out_specs=[pl.BlockSpec((tq, 256), lambda b, h, i: (b * nq + i, h))] + side_specs,
        out_shape=[jax.ShapeDtypeStruct((batch * seq_q, Q_W), BF16)]
        + [jax.ShapeDtypeStruct(w.shape, BF16) for w in side],
        scratch_shapes=[
            pltpu.VMEM((256, lanes), BF16),
            pltpu.VMEM((tk, lanes), BF16), pltpu.VMEM((tk, lanes), BF16),
            pltpu.VMEM((1, lanes), F32), pltpu.VMEM((1, lanes), F32),
            pltpu.VMEM((1, lanes), F32),
            pltpu.VMEM((2 * HEAD_DIM, lanes), F32),
        ],
        compiler_params=_cparams(3),
        name="attention_stale_max",
    )(qt, k4, vt1, k4_tail, vt1_tail, *side)
    return tuple(outs)


def attention(qt, k4, vt1, k4_tail, vt1_tail, *, batch, seq_q, tq, tk):
    nq = seq_q // tq
    lanes = Q_PER_KV * tq
    nk = 0 if k4 is None else k4.shape[2] // tk
    tail = 0 if k4_tail is None else k4_tail.shape[2]
    buf_rows = max(tk if nk else 0, tail)
    in_specs = [pl.BlockSpec((1, 1, 256, tq), lambda b, h, i: (b, h, 0, i))]
    args = [qt]
    if nk:
        lk = k4.shape[2]
        in_specs += [pl.BlockSpec((1, 1, lk, 256), lambda b, h, i: (b, h, 0, 0)),
                     pl.BlockSpec((1, 1, 128, lk), lambda b, h, i: (b, h, 0, 0))]
        args += [k4, vt1]
    if tail:
        in_specs += [pl.BlockSpec((1, 1, tail, 256), lambda b, h, i: (b, h, 0, 0)),
                     pl.BlockSpec((1, 1, 128, tail), lambda b, h, i: (b, h, 0, 0))]
        args += [k4_tail, vt1_tail]
    return pl.pallas_call(
        functools.partial(_attn_kernel, tq=tq, tk=tk, nk=nk, tail=tail),
        grid=(batch, N_KV_HEADS, nq),
        in_specs=in_specs,
        out_specs=pl.BlockSpec((tq, 256), lambda b, h, i: (b * nq + i, h)),
        out_shape=jax.ShapeDtypeStruct((batch * seq_q, Q_W), BF16),
        scratch_shapes=[
            pltpu.VMEM((256, lanes), BF16),
            pltpu.VMEM((buf_rows, lanes), F32), pltpu.VMEM((buf_rows, lanes), F32),
            pltpu.VMEM((buf_rows, lanes), BF16), pltpu.VMEM((buf_rows, lanes), BF16),
            pltpu.VMEM((1, lanes), F32), pltpu.VMEM((1, lanes), F32),
            pltpu.VMEM((1, lanes), F32), pltpu.VMEM((1, lanes), F32),
            pltpu.VMEM((1, lanes), F32),
            pltpu.VMEM((2 * HEAD_DIM, lanes), F32),
        ],
        compiler_params=_cparams(3),
        name="attention",
    )(*args)


def _dft_cs(n):
    k = np.arange(n)
    ang = 2.0 * np.pi * ((k[:, None] * k[None, :]) % n) / n
    return np.cos(ang), np.sin(ang)


def _fft1_kernel(x_ref, f_ref, c_ref, s_ref, o_ref, *, n1):
    y = jnp.dot(f_ref[...], x_ref[0], preferred_element_type=F32)
    yr, yi = y[:n1], y[n1:]
    c, s = c_ref[...], s_ref[...]
    o_ref[0, 0] = (yr * c + yi * s).astype(o_ref.dtype)
    o_ref[0, 1] = (yi * c - yr * s).astype(o_ref.dtype)


def _fft2_kernel(y_ref, f_ref, bc_ref, bs_ref, o_ref, *, n2, kb):
    for j in range(kb):
        y2 = jnp.concatenate([y_ref[0, 0, j], y_ref[0, 1, j]], axis=0)
        x2 = jnp.dot(f_ref[...], y2, preferred_element_type=F32)
        xr = x2[:n2].astype(BF16)
        xi = x2[n2:].astype(BF16)
        z = (jnp.dot(xr, bc_ref[...], preferred_element_type=F32) + jnp.dot(xi, bs_ref[...], preferred_element_type=F32))
        o_ref[0, j] = z.astype(o_ref.dtype)


def fourier_mix(u, *, batch, seq, n1, n2):
    cw = u.shape[1]
    lanes = n2 * cw
    tl = min(lanes, 4096)
    c1, s1 = _dft_cs(n1)
    f1 = jnp.asarray(np.concatenate([c1, -s1], axis=0), BF16)
    k1 = np.arange(n1)[:, None]
    t2 = np.arange(n2)[None, :]
    ang = 2.0 * np.pi * ((k1 * t2) % seq) / seq
    twc = jnp.asarray(np.repeat(np.cos(ang), cw, axis=1), F32)
    tws = jnp.asarray(np.repeat(np.sin(ang), cw, axis=1), F32)
    x2 = u.reshape(batch, n1, lanes)
    yp = pl.pallas_call(
        functools.partial(_fft1_kernel, n1=n1),
        grid=(batch, lanes // tl),
        in_specs=[
            pl.BlockSpec((1, n1, tl), lambda b, j: (b, 0, j)),
            pl.BlockSpec((2 * n1, n1), lambda b, j: (0, 0)),
            pl.BlockSpec((n1, tl), lambda b, j: (0, j)),
            pl.BlockSpec((n1, tl), lambda b, j: (0, j)),
        ],
        out_specs=pl.BlockSpec((1, 2, n1, tl), lambda b, j: (b, 0, 0, j)),
        out_shape=jax.ShapeDtypeStruct((batch, 2, n1, lanes), BF16),
        compiler_params=_cparams(2),
        name="fft_stage1",
    )(x2, f1, twc, tws)

    c2, s2 = _dft_cs(n2)
    f2 = jnp.asarray(np.block([[c2, s2], [-s2, c2]]), BF16)
    cg, sg = _dft_cs(FOURIER_GW)
    norm = 1.0 / math.sqrt(seq * FOURIER_GW)
    bdc = jnp.asarray(np.kron(np.eye(cw // FOURIER_GW), cg) * norm, BF16)
    bds = jnp.asarray(np.kron(np.eye(cw // FOURIER_GW), sg) * norm, BF16)
    kb = min(n1, 16)
    y5 = yp.reshape(batch, 2, n1, n2, cw)
    z = pl.pallas_call(
        functools.partial(_fft2_kernel, n2=n2, kb=kb),
        grid=(batch, n1 // kb),
        in_specs=[
            pl.BlockSpec((1, 2, kb, n2, cw), lambda b, j: (b, 0, j, 0, 0)),
            pl.BlockSpec((2 * n2, 2 * n2), lambda b, j: (0, 0)),
            pl.BlockSpec((cw, cw), lambda b, j: (0, 0)),
            pl.BlockSpec((cw, cw), lambda b, j: (0, 0)),
        ],
        out_specs=pl.BlockSpec((1, kb, n2, cw), lambda b, j: (b, j, 0, 0)),
        out_shape=jax.ShapeDtypeStruct((batch, n1, n2, cw), BF16),
        compiler_params=_cparams(2),
        name="fft_stage2",
    )(y5, f2, bdc, bds)
    return z.transpose(0, 2, 1, 3).reshape(batch * seq, cw)


def _merge_kernel(x_ref, mod_ref, gate_ref, cp_ref, cpp_ref, cpn_ref, yf_ref, at_ref,
                  wf_ref, wc_ref, wp_ref, wa_ref, wo_ref, dw_ref, cb_ref, cg_ref, pw_ref, ps_ref,
                  o_ref, ybuf, xbuf, ysh, cacc, *, t, tps, seq):
    i = pl.program_id(0)
    pos_tile = i % tps
    keep_prev = jnp.where(pos_tile != 0, 1.0, 0.0).astype(F32)
    keep_next = jnp.where(pos_tile != tps - 1, 1.0, 0.0).astype(F32)

    def glu(blk):
        return blk[:, 0:CONV_W].astype(F32)

    cp, cpp, cpn = cp_ref[...], cpp_ref[...], cpn_ref[...]
    ybuf[0:HALO, :] = glu(cpp) * keep_prev
    ybuf[HALO:HALO + t, :] = glu(cp)
    ybuf[HALO + t:HALO + t + HALO, :] = glu(cpn) * keep_next
    xbuf[0:HALO, :] = cpp[:, 2 * CONV_W:].astype(F32) * keep_prev
    xbuf[HALO:HALO + t, :] = cp[:, 2 * CONV_W:].astype(F32)
    xbuf[HALO + t:HALO + t + HALO, :] = cpn[:, 2 * CONV_W:].astype(F32) * keep_next

    n_sh = t + 2 * HALO - 8
    for b in range(1, 8):
        ysh[b - 1, 0:n_sh, :] = ybuf[pl.ds(b, n_sh), :]
    for r0 in range(0, t, CONV_ROWS):
        part = jnp.zeros((CONV_ROWS, CONV_W), F32)
        for k in range(CONV_K):
            a, b = divmod(HALO - CONV_HALF + k, 8)
            src = ybuf if b == 0 else ysh.at[b - 1]
            part = part + dw_ref[k:k + 1, :] * src[8 * a + r0:8 * a + r0 + CONV_ROWS, :]
        cacc[r0:r0 + CONV_ROWS, :] = part + cb_ref[...]
    acc = cacc[...]
    ms = jnp.mean(acc * acc, axis=-1, keepdims=True)
    conv_out = _silu(acc * lax.rsqrt(ms + EPS) * cg_ref[...]).astype(BF16)

    def xs(d):
        return xbuf[pl.ds(HALO + d, t), :]

    x0 = xs(0)
    s2 = xs(-1) + x0
    s4 = s2 + xs(-2) + xs(1)
    s8 = s4 + xs(-4) + xs(-3) + xs(2) + xs(3)
    s16 = s8 + xs(-8) + xs(-7) + xs(-6) + xs(-5) + xs(4) + xs(5) + xs(6) + xs(7)
    grp = lax.broadcasted_iota(jnp.int32, (t, POOL_W), 1) // POOL_GW
    pos = pos_tile * t + lax.broadcasted_iota(jnp.int32, (t, POOL_W), 0)
    half = jnp.where(grp == 0, 1, jnp.where(grp == 1, 2, jnp.where(grp == 2, 4, 8)))
    cnt = (jnp.minimum(pos + half, seq) - jnp.maximum(pos - half, 0)).astype(F32)
    wsum = jnp.where(grp == 0, s2, jnp.where(grp == 1, s4, jnp.where(grp == 2, s8, s16)))
    pool_in = (wsum / cnt - x0).astype(BF16)
    pool_out = (jnp.dot(pool_in, pw_ref[...], preferred_element_type=F32) * ps_ref[...]).astype(BF16)

    def gate(b):
        return gate_ref[:, b * D_MODEL:(b + 1) * D_MODEL].astype(F32)

    merged = gate(0) * jnp.dot(yf_ref[...], wf_ref[...], preferred_element_type=F32)
    merged = merged + gate(1) * jnp.dot(conv_out, wc_ref[...], preferred_element_type=F32)
    merged = merged + gate(2) * jnp.dot(pool_out, wp_ref[...], preferred_element_type=F32)
    merged = merged + gate(3) * jnp.dot(at_ref[...], wa_ref[...], preferred_element_type=F32)
    out = jnp.dot(merged.astype(BF16), wo_ref[...], preferred_element_type=F32)
    o_ref[...] = x_ref[...] + mod_ref[0, 2:3, :] * out


def merge_branches(x2d, mod, proj, yf, attn, lw, *, seq, t):
    m, d = x2d.shape
    tps = seq // t
    hb = t // HALO
    n_halo = m // HALO
    n_mod = mod.shape[0]
    mod_idx = (lambda i: (i // tps, 0, 0)) if n_mod > 1 else (lambda i: (0, 0, 0))
    const = lambda i: (0, 0)
    cp_blk = P_OFF_CP // CP_W
    return pl.pallas_call(
        functools.partial(_merge_kernel, t=t, tps=tps, seq=seq),
        grid=(m // t,),
        in_specs=[
            pl.BlockSpec((t, d), lambda i: (i, 0)),
            pl.BlockSpec((1, 6, d), mod_idx),
            pl.BlockSpec((t, 4 * d), lambda i: (i, 0)),
            pl.BlockSpec((t, CP_W), lambda i: (i, cp_blk)),
            pl.BlockSpec((HALO, CP_W), lambda i: (jnp.maximum(i * hb - 1, 0), cp_blk)),
            pl.BlockSpec((HALO, CP_W), lambda i: (jnp.minimum((i + 1) * hb, n_halo - 1), cp_blk)),
            pl.BlockSpec((t, FOURIER_W), lambda i: (i, 0)),
            pl.BlockSpec((t, Q_W), lambda i: (i, 0)),
            pl.BlockSpec((FOURIER_W, d), const),
            pl.BlockSpec((CONV_W, d), const),
            pl.BlockSpec((POOL_W, d), const),
            pl.BlockSpec((Q_W, d), const),
            pl.BlockSpec((d, d), const),
            pl.BlockSpec((CONV_K, CONV_W), const),
            pl.BlockSpec((1, CONV_W), const),
            pl.BlockSpec((1, CONV_W), const),
            pl.BlockSpec((POOL_W, POOL_W), const),
            pl.BlockSpec((1, POOL_W), const),
        ],
        out_specs=pl.BlockSpec((t, d), lambda i: (i, 0)),
        out_shape=jax.ShapeDtypeStruct((m, d), F32),
        scratch_shapes=[pltpu.VMEM((t + 2 * HALO, CONV_W), F32), pltpu.VMEM((t + 2 * HALO, POOL_W), F32),
                        pltpu.VMEM((7, t + 2 * HALO, CONV_W), F32), pltpu.VMEM((t, CONV_W), F32)],
        compiler_params=_cparams(1),
        name="merge_branches",
    )(x2d, mod, proj, proj, proj, proj, yf, attn,
      lw["wf"], lw["wc"], lw["wp"], lw["wa"], lw["wo"], lw["dw"], lw["cb"], lw["cg"], lw["pw"], lw["ps"])


def _ffn_kernel(x_ref, mod_ref, g_ref, w1_ref, w3_ref, w2_ref, o_ref, *, chunks):
    x = x_ref[...]
    h = _norm_mod(x, g_ref[...], mod_ref[0, 3:4, :], mod_ref[0, 4:5, :]).astype(BF16)
    acc = jnp.zeros(x.shape, F32)
    for c0, cw in chunks:
        a = jnp.dot(h, w1_ref[:, c0:c0 + cw], preferred_element_type=F32)
        b = jnp.dot(h, w3_ref[:, c0:c0 + cw], preferred_element_type=F32)
        acc = acc + jnp.dot((_silu(a) * b).astype(BF16), w2_ref[c0:c0 + cw, :], preferred_element_type=F32)
    o_ref[...] = x + mod_ref[0, 5:6, :] * acc


def ffn_dense(x2d, mod, g, w1, w3, w2, *, tm, tiles_per_mod):
    m, d = x2d.shape
    dff = w1.shape[1]
    chunks = tuple((c0, min(1024, dff - c0)) for c0 in range(0, dff, 1024))
    n_mod = mod.shape[0]
    mod_idx = (lambda i: (i // tiles_per_mod, 0, 0)) if n_mod > 1 else (lambda i: (0, 0, 0))
    const = lambda i: (0, 0)
    return pl.pallas_call(
        functools.partial(_ffn_kernel, chunks=chunks),
        grid=(m // tm,),
        in_specs=[
            pl.BlockSpec((tm, d), lambda i: (i, 0)),
            pl.BlockSpec((1, 6, d), mod_idx),
            pl.BlockSpec((1, d), const),
            pl.BlockSpec((d, dff), const, pipeline_mode=pl.Buffered(1)),
            pl.BlockSpec((d, dff), const, pipeline_mode=pl.Buffered(1)),
            pl.BlockSpec((dff, d), const, pipeline_mode=pl.Buffered(1)),
        ],
        out_specs=pl.BlockSpec((tm, d), lambda i: (i, 0)),
        out_shape=jax.ShapeDtypeStruct((m, d), F32),
        compiler_params=_cparams(1),
        name="ffn_dense",
    )(x2d, mod, g.reshape(1, d), w1, w3, w2)


def _top2(logits):
    t = logits.shape[0]
    lane = lax.broadcasted_iota(jnp.int32, (t, 128), 1).astype(F32)
    neg = jnp.float32(-jnp.inf)
    lg = jnp.where(lane < N_EXPERTS, logits, neg)
    v1 = jnp.max(lg, axis=-1, keepdims=True)
    i1 = jnp.min(jnp.where(lg == v1, lane, 128.0), axis=-1, keepdims=True)
    lg2 = jnp.where(lane == i1, neg, lg)
    v2 = jnp.max(lg2, axis=-1, keepdims=True)
    i2 = jnp.min(jnp.where(lg2 == v2, lane, 128.0), axis=-1, keepdims=True)
    e2 = jnp.exp(v2 - v1)
    return i1, i2, 1.0 / (1.0 + e2), e2 / (1.0 + e2)


R_E1, R_E2, R_W1, R_W2, R_RANK1, R_RANK2 = range(6)


def _route_kernel(x_ref, mod_ref, g_ref, r_ref, tri_ref, route_ref, cnt_ref, carry_ref):
    @pl.when(pl.program_id(0) == 0)
    def _():
        carry_ref[...] = jnp.zeros(carry_ref.shape, F32)

    t = x_ref.shape[0]
    h = _norm_mod(x_ref[...], g_ref[...], mod_ref[0, 3:4, :], mod_ref[0, 4:5, :])
    r = r_ref[...]
    h_hi, r_hi = h.astype(BF16), r.astype(BF16)
    h_lo, r_lo = (h - h_hi.astype(F32)).astype(BF16), (r - r_hi.astype(F32)).astype(BF16)
    logits = (jnp.dot(h_hi, r_hi, preferred_element_type=F32) + jnp.dot(h_hi, r_lo, preferred_element_type=F32)
              + jnp.dot(h_lo, r_hi, preferred_element_type=F32))
    i1, i2, w1, w2 = _top2(logits)
    lane = lax.broadcasted_iota(jnp.int32, (t, 128), 1).astype(F32)
    oh1 = jnp.where(lane == i1, 1.0, 0.0)
    oh2 = jnp.where(lane == i2, 1.0, 0.0)
    both = oh1 + oh2
    before = carry_ref[...] + jnp.dot(tri_ref[...], both.astype(BF16), preferred_element_type=F32)
    rank1 = jnp.sum(oh1 * before, axis=-1, keepdims=True)
    rank2 = jnp.sum(oh2 * before, axis=-1, keepdims=True)
    carry_ref[...] += jnp.sum(both, axis=0, keepdims=True)
    rec = jnp.zeros((t, 128), F32)
    for col, val in ((R_E1, i1), (R_E2, i2), (R_W1, w1), (R_W2, w2), (R_RANK1, rank1), (R_RANK2, rank2)):
        rec = jnp.where(lane == col, val, rec)
    route_ref[...] = rec
    cnt_ref[...] = carry_ref[...]


def moe_route(x2d, mod, g, router_pad, *, tm, tiles_per_mod):
    m, d = x2d.shape
    n_mod = mod.shape[0]
    mod_idx = (lambda i: (i // tiles_per_mod, 0, 0)) if n_mod > 1 else (lambda i: (0, 0, 0))
    tri = jnp.asarray(np.tril(np.ones((tm, tm), np.float32), -1), BF16)
    return pl.pallas_call(
        _route_kernel,
        grid=(m // tm,),
        in_specs=[
            pl.BlockSpec((tm, d), lambda i: (i, 0)),
            pl.BlockSpec((1, 6, d), mod_idx),
            pl.BlockSpec((1, d), lambda i: (0, 0)),
            pl.BlockSpec((d, 128), lambda i: (0, 0)),
            pl.BlockSpec((tm, tm), lambda i: (0, 0)),
        ],
        out_specs=[pl.BlockSpec((tm, 128), lambda i: (i, 0)), pl.BlockSpec((1, 128), lambda i: (0, 0))],
        out_shape=[jax.ShapeDtypeStruct((m, 128), F32), jax.ShapeDtypeStruct((1, 128), F32)],
        scratch_shapes=[pltpu.VMEM((1, 128), F32)],
        compiler_params=_cparams(1),
        name="moe_route",
    )(x2d, mod, g.reshape(1, d), router_pad, tri)


def _dispatch_kernel(pos_ref, pad_ref, x_ref, mod_ref, g_ref, xs_ref, h_ref, zrow_ref, sem, zsem):
    i = pl.program_id(0)
    t = x_ref.shape[0]
    slot = i % 2
    h_ref[slot] = _norm_mod(x_ref[...], g_ref[...], mod_ref[0, 3:4, :], mod_ref[0, 4:5, :])

    def row_copy(r, dst_row):
        return pltpu.make_async_copy(h_ref.at[slot, pl.ds(r, 1), :], xs_ref.at[pl.ds(dst_row, 1), :], sem.at[slot])

    def issue(r, carry):
        row_copy(r, pos_ref[0, 0, r]).start()
        row_copy(r, pos_ref[0, 0, t + r]).start()
        return carry

    lax.fori_loop(0, t, issue, 0, unroll=DMA_ISSUE_UNROLL)

    def drain(s):
        for _ in range(2):
            pltpu.make_async_copy(h_ref.at[s], xs_ref.at[pl.ds(0, t), :], sem.at[s]).wait()

    @pl.when(i > 0)
    def _():
        drain(1 - slot)

    @pl.when(i == pl.num_programs(0) - 1)
    def _():
        drain(slot)
        zrow_ref[...] = jnp.zeros(zrow_ref.shape, F32)

        def zero_copy(r):
            return pltpu.make_async_copy(zrow_ref.at[pl.ds(0, 1), :], xs_ref.at[pl.ds(r, 1), :], zsem)

        def start_one(r, carry):
            zero_copy(r).start()
            return carry

        def wait_one(r, carry):
            zero_copy(r).wait()
            return carry

        for e in range(N_EXPERTS):
            lax.fori_loop(pad_ref[0, e], pad_ref[0, N_EXPERTS + e], start_one, 0)
        for e in range(N_EXPERTS):
            lax.fori_loop(pad_ref[0, e], pad_ref[0, N_EXPERTS + e], wait_one, 0)

        h_ref[1 - slot] = jnp.zeros((t, h_ref.shape[2]), F32)

        def tile_copy(k):
            return pltpu.make_async_copy(h_ref.at[1 - slot], xs_ref.at[pl.ds(pl.multiple_of(k * t, t), t), :], zsem)

        def start_tile(k, carry):
            tile_copy(k).start()
            return carry

        def wait_tile(k, carry):
            tile_copy(k).wait()
            return carry

        first_unused, n_tiles = pad_ref[0, 2 * N_EXPERTS], xs_ref.shape[0] // t
        lax.fori_loop(first_unused, n_tiles, start_tile, 0)
        lax.fori_loop(first_unused, n_tiles, wait_tile, 0)


def moe_dispatch(x2d, mod, g, pos_tiles, pad_rows, n_rows, *, tm, tiles_per_mod):
    m, d = x2d.shape
    n_mod = mod.shape[0]
    mod_idx = (lambda i: (i // tiles_per_mod, 0, 0)) if n_mod > 1 else (lambda i: (0, 0, 0))
    return pl.pallas_call(
        _dispatch_kernel,
        grid=(m // tm,),
        in_specs=[
            pl.BlockSpec((1, 1, 2 * tm), lambda i: (i, 0, 0), memory_space=pltpu.SMEM),
            pl.BlockSpec((1, 2 * N_EXPERTS + 1), lambda i: (0, 0), memory_space=pltpu.SMEM),
            pl.BlockSpec((tm, d), lambda i: (i, 0)),
            pl.BlockSpec((1, 6, d), mod_idx),
            pl.BlockSpec((1, d), lambda i: (0, 0)),
        ],
        out_specs=pl.BlockSpec(memory_space=pl.ANY),
        out_shape=jax.ShapeDtypeStruct((n_rows, d), F32),
        scratch_shapes=[pltpu.VMEM((2, tm, d), F32), pltpu.VMEM((8, d), F32),
                        pltpu.SemaphoreType.DMA((2,)), pltpu.SemaphoreType.DMA(())],
        compiler_params=_cparams(1),
        name="moe_dispatch",
    )(pos_tiles, pad_rows, x2d, mod, g.reshape(1, d))


def _experts_kernel(te_ref, nv_ref, xs_ref, w1_ref, w3_ref, w2_ref, ys_ref, xb_ref, acc_ref):
    i = pl.program_id(0)
    j = pl.program_id(1)
    valid = i < nv_ref[0]

    @pl.when(jnp.logical_and(valid, j == 0))
    def _():
        xb_ref[...] = xs_ref[...].astype(BF16)
        acc_ref[...] = jnp.zeros(acc_ref.shape, F32)

    @pl.when(valid)
    def _():
        h = xb_ref[...]
        a = jnp.dot(h, w1_ref[0], preferred_element_type=F32)
        b = jnp.dot(h, w3_ref[0], preferred_element_type=F32)
        acc_ref[...] += jnp.dot((_silu(a) * b).astype(BF16), w2_ref[0], preferred_element_type=F32)

    @pl.when(jnp.logical_and(valid, j == pl.num_programs(1) - 1))
    def _():
        ys_ref[...] = acc_ref[...]

    @pl.when(jnp.logical_and(jnp.logical_not(valid), j == pl.num_programs(1) - 1))
    def _():
        ys_ref[...] = jnp.zeros(ys_ref.shape, F32)


def moe_experts_grouped(xs, tile_expert, n_valid, w1, w3, w2, *, tm, tf):
    n_rows, d = xs.shape
    dff = w1.shape[2]
    nf = dff // tf

    def w13_idx(i, j, te, nv):
        return (te[i], 0, jnp.where(i < nv[0], j, nf - 1))

    def w2_idx(i, j, te, nv):
        return (te[i], jnp.where(i < nv[0], j, nf - 1), 0)

    grid_spec = pltpu.PrefetchScalarGridSpec(
        num_scalar_prefetch=2,
        grid=(n_rows // tm, nf),
        in_specs=[
            pl.BlockSpec((tm, d), lambda i, j, te, nv: (jnp.minimum(i, nv[0] - 1), 0)),
            pl.BlockSpec((1, d, tf), w13_idx),
            pl.BlockSpec((1, d, tf), w13_idx),
            pl.BlockSpec((1, tf, d), w2_idx),
        ],
        out_specs=pl.BlockSpec((tm, d), lambda i, j, te, nv: (i, 0)),
        scratch_shapes=[pltpu.VMEM((tm, d), BF16), pltpu.VMEM((tm, d), F32)],
    )
    return pl.pallas_call(
        _experts_kernel,
        grid_spec=grid_spec,
        out_shape=jax.ShapeDtypeStruct((n_rows, d), F32),
        compiler_params=_cparams(2),
        name="moe_experts_grouped",
    )(tile_expert, n_valid, xs, w1, w3, w2)


def _combine_kernel(pos_ref, pos_next_ref, x_ref, mod_ref, rt_ref, ys_ref, o_ref, y_ref, sem):
    i = pl.program_id(0)
    t = x_ref.shape[0]
    slot = i % 2

    def issue_tile(p_ref, s):
        def issue(r, carry):
            for k in range(2):
                pltpu.make_async_copy(ys_ref.at[pl.ds(p_ref[0, 0, k * t + r], 1), :],
                                      y_ref.at[s, k, pl.ds(r, 1), :], sem.at[s]).start()
            return carry

        lax.fori_loop(0, t, issue, 0, unroll=DMA_ISSUE_UNROLL)

    @pl.when(i == 0)
    def _():
        issue_tile(pos_ref, 0)

    @pl.when(i + 1 < pl.num_programs(0))
    def _():
        issue_tile(pos_next_ref, 1 - slot)

    for k in range(2):
        pltpu.make_async_copy(ys_ref.at[pl.ds(0, t), :], y_ref.at[slot, k], sem.at[slot]).wait()
    rt = rt_ref[...]
    mix = rt[:, R_W1:R_W1 + 1] * y_ref[slot, 0] + rt[:, R_W2:R_W2 + 1] * y_ref[slot, 1]
    o_ref[...] = x_ref[...] + mod_ref[0, 5:6, :] * mix


def moe_combine(x2d, mod, route, pos_tiles, ys, *, tm, tiles_per_mod):
    m, d = x2d.shape
    n_mod = mod.shape[0]
    mod_idx = (lambda i: (i // tiles_per_mod, 0, 0)) if n_mod > 1 else (lambda i: (0, 0, 0))
    n_tiles = m // tm
    return pl.pallas_call(
        _combine_kernel,
        grid=(n_tiles,),
        in_specs=[
            pl.BlockSpec((1, 1, 2 * tm), lambda i: (i, 0, 0), memory_space=pltpu.SMEM),
            pl.BlockSpec((1, 1, 2 * tm), lambda i: (jnp.minimum(i + 1, n_tiles - 1), 0, 0), memory_space=pltpu.SMEM),
            pl.BlockSpec((tm, d), lambda i: (i, 0)),
            pl.BlockSpec((1, 6, d), mod_idx),
            pl.BlockSpec((tm, 128), lambda i: (i, 0)),
            pl.BlockSpec(memory_space=pl.ANY),
        ],
        out_specs=pl.BlockSpec((tm, d), lambda i: (i, 0)),
        out_shape=jax.ShapeDtypeStruct((m, d), F32),
        scratch_shapes=[pltpu.VMEM((2, 2, tm, d), F32), pltpu.SemaphoreType.DMA((2,))],
        compiler_params=_cparams(1),
        name="moe_combine",
    )(pos_tiles, pos_tiles, x2d, mod, route, ys)


def _pos_tiles(pos1, pos2, tm):
    n = pos1.shape[0] // tm
    return jnp.concatenate([pos1.reshape(n, 1, tm), pos2.reshape(n, 1, tm)], axis=2)


def moe_sparse(x2d, mod, g, router_pad, w1, w3, w2, *, rows_per_mod):
    m, d = x2d.shape
    tr, td, tc, te = MOE_ROUTE_TM, MOE_DISPATCH_TM, MOE_COMBINE_TM, MOE_EXPERT_TM
    tiles_per_seq_row = rows_per_mod
    route, cnt = moe_route(x2d, mod, g, router_pad, tm=tr, tiles_per_mod=tiles_per_seq_row // tr)
    counts = cnt[0, 0:N_EXPERTS].astype(jnp.int32)
    group = ((counts + te - 1) // te) * te
    ends = jnp.cumsum(group)
    starts = ends - group
    e1 = route[:, R_E1].astype(jnp.int32)
    e2 = route[:, R_E2].astype(jnp.int32)
    pos1 = starts[e1] + route[:, R_RANK1].astype(jnp.int32)
    pos2 = starts[e2] + route[:, R_RANK2].astype(jnp.int32)
    n_rows = 2 * m + N_EXPERTS * te
    n_tiles = n_rows // te
    tile_start = jnp.arange(n_tiles, dtype=jnp.int32)[:, None] * te
    tile_expert = jnp.minimum(jnp.sum((tile_start >= ends[None, :]).astype(jnp.int32), axis=1), N_EXPERTS - 1)
    n_valid = (ends[-1:] // te).astype(jnp.int32)
    assert td == te
    pad_rows = jnp.concatenate([starts + counts, ends, n_valid]).astype(jnp.int32).reshape(1, 2 * N_EXPERTS + 1)
    xs = moe_dispatch(x2d, mod, g, _pos_tiles(pos1, pos2, td), pad_rows, n_rows, tm=td,
                      tiles_per_mod=tiles_per_seq_row // td)
    ys = moe_experts_grouped(xs, tile_expert, n_valid, w1, w3, w2, tm=te, tf=MOE_EXPERT_TF)
    return moe_combine(x2d, mod, route, _pos_tiles(pos1, pos2, tc), ys, tm=tc, tiles_per_mod=tiles_per_seq_row // tc)


def _permute_w_in(w):
    f, c, p, q, kv, gts = w[:, 0:256], w[:, 256:768], w[:, 768:1024], w[:, 1024:1536], w[:, 1536:1792], w[:, 1792:]
    return jnp.concatenate([gts, q, c, p, f, kv], axis=1).astype(BF16)


def _layer_weights(layer, w_br_fourier, conv_dw, conv_b, conv_norm_g, w_br_conv, pool_w, pool_scale, w_br_pool,
                   w_br_attn, w_out):
    pw = jax.scipy.linalg.block_diag(*[pool_w[layer, i] for i in range(len(POOL_WINDOWS))])
    return {
        "wf": w_br_fourier[layer].astype(BF16), "wc": w_br_conv[layer].astype(BF16),
        "wp": w_br_pool[layer].astype(BF16), "wa": w_br_attn[layer].astype(BF16), "wo": w_out[layer].astype(BF16),
        "dw": conv_dw[layer], "cb": conv_b[layer].reshape(1, CONV_W), "cg": conv_norm_g[layer].reshape(1, CONV_W),
        "pw": pw.astype(BF16), "ps": pool_scale[layer].reshape(1, POOL_W),
    }


def _cast_slices(w, max_slices):
    w2 = w.reshape(-1, w.shape[-1])
    rows = w2.shape[0]
    n = 1
    while 2 * n <= max_slices and rows % (32 * n) == 0:
        n *= 2
    return w2.reshape(n, rows // n, w2.shape[1])


def kernel(x, c, ctx, c_ctx, w_mod, b_mod, norm1_g, norm2_g, w_in, w_br_fourier, conv_dw, conv_b, conv_norm_g,
           w_br_conv, pool_w, pool_scale, w_br_pool, q_norm_g, k_norm_g, w_br_attn, w_out, ffn_w1, ffn_w3, ffn_w2,
           moe_router, moe_w1, moe_w3, moe_w2):
    batch, seq, d = x.shape
    ctx_len = ctx.shape[1]
    depth = w_in.shape[0]
    rope = rope_tables(seq)

    c_rows = jnp.zeros((8, d), F32).at[0:batch].set(c).at[batch].set(c_ctx)
    mods = modulation_all(c_rows, w_mod, b_mod).reshape(depth, 8, 6, d)

    xl = x.reshape(batch * seq, d)
    xc = ctx.reshape(batch * ctx_len, d)
    for layer in range(depth):
        is_last = layer == depth - 1
        mod_l = mods[layer, 0:batch]
        mod_c = mods[layer, batch:batch + 1]
        w_in_l = _permute_w_in(w_in[layer])
        lw = _layer_weights(layer, w_br_fourier, conv_dw, conv_b, conv_norm_g, w_br_conv, pool_w, pool_scale,
                            w_br_pool, w_br_attn, w_out)

        proj_c, qc, ktc, vc = input_projection(xc, mod_c, norm1_g[layer], w_in_l, q_norm_g[layer], k_norm_g[layer],
                                               None, batch=batch, seq=ctx_len, tm=256)

        proj, q, kt, v = input_projection(xl, mod_l, norm1_g[layer], w_in_l, q_norm_g[layer], k_norm_g[layer], rope,
                                          batch=batch, seq=seq, tm=512)
        spread = (2.0 * 1.02 * HEAD_DIM * Q_SCALE) * jnp.max(jnp.abs(q_norm_g[layer])) * jnp.max(
            jnp.abs(k_norm_g[layer]))
        n_steps = batch * N_KV_HEADS * (seq // ATTN_TQ)
        mixer_w = ((ffn_w1, ffn_w3, ffn_w2) if layer % 2 == 0 else (moe_w1, moe_w3, moe_w2))
        mixer_w = tuple(w[layer // 2] for w in mixer_w)
        side = tuple(_cast_slices(w, n_steps) for w in mixer_w)
        attn, *side_bf16 = lax.cond(
            spread < STALE_MAX_EXP_LIMIT,
            lambda ops, sd: attention_stale_max(*ops, sd, batch=batch, seq_q=seq, tq=ATTN_TQ, tk=1024),
            lambda ops, sd: (attention(*ops, batch=batch, seq_q=seq, tq=ATTN_TQ, tk=1024),)
            + tuple(w.astype(BF16) for w in sd),
            (q, kt, v, ktc, vc), side)
        yf = fourier_mix(proj[:, P_OFF_F:P_OFF_F + FOURIER_W], batch=batch, seq=seq, n1=64, n2=seq // 64)
        xl = merge_branches(xl, mod_l, proj, yf, attn, lw, seq=seq, t=512)

        if not is_last:
            attn_c = attention(qc, None, None, ktc, vc, batch=batch, seq_q=ctx_len, tq=256, tk=ctx_len)
            yf_c = fourier_mix(proj_c[:, P_OFF_F:P_OFF_F + FOURIER_W], batch=batch, seq=ctx_len, n1=16,
                               n2=ctx_len // 16)
            xc = merge_branches(xc, mod_c, proj_c, yf_c, attn_c, lw, seq=ctx_len, t=256)

        j = layer // 2
        w1, w3, w2 = (wb.reshape(w.shape) for wb, w in zip(side_bf16, mixer_w))
        if layer % 2 == 0:
            xl = ffn_dense(xl, mod_l, norm2_g[layer], w1, w3, w2, tm=512, tiles_per_mod=seq // 512)
            if not is_last:
                xc = ffn_dense(xc, mod_c, norm2_g[layer], w1, w3, w2, tm=256, tiles_per_mod=1)
        else:
            router_pad = jnp.zeros((d, 128), F32).at[:, 0:N_EXPERTS].set(moe_router[j])
            xl = moe_sparse(xl, mod_l, norm2_g[layer], router_pad, w1, w3, w2, rows_per_mod=seq)
            if not is_last:
                xc = moe_sparse(xc, mod_c, norm2_g[layer], router_pad, w1, w3, w2, rows_per_mod=batch * ctx_len)
    return xl.reshape(batch, seq, d)
```

```python
import functools
import math

import numpy as np
import jax
import jax.numpy as jnp
from jax import lax
from jax.experimental import pallas as pl
from jax.experimental.pallas import tpu as pltpu

F32 = jnp.float32
BF16 = jnp.bfloat16

D_MODEL = 1024
GRID_W = 64
EPS = 1e-6
FOURIER_GW = 64
FOURIER_W = 256
CONV_W = 256
CONV_K = 31
CONV_HALF = CONV_K // 2
POOL_WINDOWS = (2, 4, 8, 16)
POOL_GW = 64
POOL_W = 256
HEAD_DIM = 64
N_HEADS = 8
N_KV_HEADS = 2
Q_PER_KV = 4
Q_W = 512
KV_W = 128
ROPE_THETA = 10000.0
N_EXPERTS = 8
IN_W = 5888

P_OFF_G = 0
P_OFF_Q = 4096
P_OFF_CP = 4608
P_OFF_F = 5376
P_OFF_KV = 5632
CP_W = 2 * CONV_W + POOL_W

Q_SCALE = (HEAD_DIM ** -0.5) * math.log2(math.e)

ATTN_TQ = 512
ATTN_TK = 1024
STALE_MAX_EXP_LIMIT = 64.0

MOE_ROUTE_TM = 512
MOE_DISPATCH_TM = 512
MOE_COMBINE_TM = 256
MOE_EXPERT_TM = 512
MOE_EXPERT_TF = 1792
DMA_ISSUE_UNROLL = 8

CONV_ROWS = 64
HALO = 16
VMEM_LIMIT = 56 * 1024 * 1024


def _cparams(n_axes):
    return pltpu.CompilerParams(dimension_semantics=("arbitrary",) * n_axes, vmem_limit_bytes=VMEM_LIMIT)


def _sigmoid(v):
    return 0.5 * jnp.tanh(0.5 * v) + 0.5


def _silu(v):
    return v * _sigmoid(v)


def _norm_mod(x, g, shift, scale):
    ms = jnp.mean(x * x, axis=-1, keepdims=True)
    return x * lax.rsqrt(ms + EPS) * g * (1.0 + scale) + shift


def _mod_kernel(c_ref, w_ref, b_ref, o_ref):
    s = _silu(c_ref[...])
    o_ref[0] = jnp.dot(s, w_ref[0], preferred_element_type=F32, precision=lax.Precision.HIGHEST) + b_ref[0]


def modulation_all(c_rows, w_mod, b_mod):
    n_layers, d, n = w_mod.shape
    tn = 1536
    return pl.pallas_call(
        _mod_kernel,
        grid=(n_layers, n // tn),
        in_specs=[
            pl.BlockSpec((8, d), lambda l, j: (0, 0)),
            pl.BlockSpec((1, d, tn), lambda l, j: (l, 0, j)),
            pl.BlockSpec((1, 1, tn), lambda l, j: (l, 0, j)),
        ],
        out_specs=pl.BlockSpec((1, 8, tn), lambda l, j: (l, 0, j)),
        out_shape=jax.ShapeDtypeStruct((n_layers, 8, n), F32),
        compiler_params=_cparams(2),
        name="modulation",
    )(c_rows, w_mod, b_mod.reshape(n_layers, 1, n))


def _inproj_kernel(*refs, chunks, use_rope):
    if use_rope:
        x_ref, mod_ref, g_ref, w_ref, gq_ref, gk_ref, ones_ref, cos_ref, sin_ref, o_ref, qo_ref, kt_ref, v_ref = refs
        cos, sin = cos_ref[...], sin_ref[...]
    else:
        x_ref, mod_ref, g_ref, w_ref, gq_ref, gk_ref, ones_ref, o_ref, qo_ref, kt_ref, v_ref = refs
        cos = sin = None
    h = _norm_mod(x_ref[...], g_ref[...], mod_ref[0, 0:1, :], mod_ref[0, 1:2, :]).astype(BF16)
    for c0, cw in chunks:
        r = jnp.dot(h, w_ref[:, c0:c0 + cw], preferred_element_type=F32)
        if c0 + cw <= P_OFF_Q:
            r = _sigmoid(r)
        elif c0 == P_OFF_CP:
            sg = _sigmoid(r[:, CONV_W:2 * CONV_W])
            r = jnp.concatenate([r[:, 0:CONV_W] * sg, sg], axis=1)
        elif c0 == P_OFF_Q:
            _q_epilogue(r, gq_ref[...], ones_ref[...], cos, sin, qo_ref)
        elif c0 == P_OFF_KV:
            _kv_epilogue(r, gk_ref[...], ones_ref[...], cos, sin, kt_ref, v_ref)
        o_ref[:, c0:c0 + cw] = r.astype(o_ref.dtype)


def input_projection(x2d, mod, g, w_bf16, gq, gk, rope, *, batch, seq, tm):
    m, d = x2d.shape
    n = w_bf16.shape[1]
    tps = seq // tm
    chunks = tuple((c0, min(512, n - c0)) for c0 in range(0, n, 512))
    assert P_OFF_Q % 512 == 0 and {(P_OFF_CP, 2 * CONV_W), (P_OFF_Q, Q_W), (P_OFF_KV, 2 * KV_W)} <= set(chunks)
    use_rope = rope is not None
    n_mod = mod.shape[0]
    mod_idx = (lambda i: (i // tps, 0, 0)) if n_mod > 1 else (lambda i: (0, 0, 0))
    const = lambda i: (0, 0)
    ones_bd = jnp.asarray(np.kron(np.eye(2, dtype=np.float32), np.ones((64, 64), np.float32)), BF16)
    in_specs = [
        pl.BlockSpec((tm, d), lambda i: (i, 0)),
        pl.BlockSpec((1, 6, d), mod_idx),
        pl.BlockSpec((1, d), const),
        pl.BlockSpec((d, n), const, pipeline_mode=pl.Buffered(1)),
        pl.BlockSpec((1, 128), const),
        pl.BlockSpec((1, 128), const),
        pl.BlockSpec((128, 128), const),
    ]
    args = [x2d, mod, g.reshape(1, d), w_bf16, jnp.tile(gq, 2).reshape(1, 128), jnp.tile(gk, 2).reshape(1, 128), ones_bd]
    if use_rope:
        in_specs += [pl.BlockSpec((tm, 128), lambda i: (i % tps, 0))] * 2
        args += list(rope)
    return pl.pallas_call(
        functools.partial(_inproj_kernel, chunks=chunks, use_rope=use_rope),
        grid=(m // tm,),
        in_specs=in_specs,
        out_specs=[
            pl.BlockSpec((tm, n), lambda i: (i, 0)),
            pl.BlockSpec((1, N_KV_HEADS, 256, tm), lambda i: (i // tps, 0, 0, i % tps)),
            pl.BlockSpec((1, N_KV_HEADS, tm, 256), lambda i: (i // tps, 0, i % tps, 0)),
            pl.BlockSpec((1, N_KV_HEADS, 128, tm), lambda i: (i // tps, 0, 0, i % tps)),
        ],
        out_shape=[
            jax.ShapeDtypeStruct((m, n), BF16),
            jax.ShapeDtypeStruct((batch, N_KV_HEADS, 256, seq), BF16),
            jax.ShapeDtypeStruct((batch, N_KV_HEADS, seq, 256), BF16),
            jax.ShapeDtypeStruct((batch, N_KV_HEADS, 128, seq), BF16),
        ],
        compiler_params=_cparams(1),
        name="input_projection",
    )(*args)


def _seg_sum64(v, ones_bd):
    hi = v.astype(BF16)
    lo = (v - hi.astype(F32)).astype(BF16)
    return (jnp.dot(hi, ones_bd, preferred_element_type=F32) + jnp.dot(lo, ones_bd, preferred_element_type=F32))


def _head_norm_rope(x, g, ones_bd, cos, sin, low_mask):
    y = x * lax.rsqrt(_seg_sum64(x * x, ones_bd) * (1.0 / HEAD_DIM) + EPS) * g
    if cos is None:
        return y
    partner = jnp.where(low_mask, pltpu.roll(y, 128 - 16, axis=1), pltpu.roll(y, 16, axis=1))
    return y * cos + partner * sin


def _rope_low_mask(t):
    return (lax.broadcasted_iota(jnp.int32, (t, 128), 1) % 32) < 16


def _q_epilogue(rq, gq, ones_bd, cos, sin, qo_ref):
    low_mask = _rope_low_mask(rq.shape[0])
    for c in range(Q_W // 128):
        yq = _head_norm_rope(rq[:, 128 * c:128 * (c + 1)], gq, ones_bd, cos, sin, low_mask) * Q_SCALE
        qo_ref[0, c // 2, 128 * (c % 2):128 * (c % 2 + 1), :] = yq.T.astype(BF16)


def _kv_epilogue(rkv, gk, ones_bd, cos, sin, kt_ref, v_ref):
    t = rkv.shape[0]
    yk = _head_norm_rope(rkv[:, 0:128], gk, ones_bd, cos, sin, _rope_low_mask(t))
    ykr = pltpu.roll(yk, 64, axis=1)
    first = lax.broadcasted_iota(jnp.int32, (t, 128), 1) < 64
    k0 = jnp.where(first, yk, ykr).astype(BF16)
    k1 = jnp.where(first, ykr, yk).astype(BF16)
    kt_ref[0, 0] = jnp.concatenate([k0, k0], axis=1)
    kt_ref[0, 1] = jnp.concatenate([k1, k1], axis=1)
    vt = rkv[:, 128:256].T
    ones = jnp.ones((HEAD_DIM, t), F32)
    for h in range(N_KV_HEADS):
        v_ref[0, h] = jnp.concatenate([vt[64 * h:64 * (h + 1), :], ones], axis=0).astype(BF16)


def rope_tables(seq):
    n_freq = HEAD_DIM // 4
    freqs = ROPE_THETA ** (-jnp.arange(n_freq, dtype=F32) / n_freq)
    t = jnp.arange(seq)
    row = (t // GRID_W).astype(F32)
    col = (t % GRID_W).astype(F32)
    ang_r = row[:, None] * freqs
    ang_c = col[:, None] * freqs
    cos = jnp.concatenate([jnp.cos(ang_r)] * 2 + [jnp.cos(ang_c)] * 2, axis=1)
    sin = jnp.concatenate([-jnp.sin(ang_r), jnp.sin(ang_r), -jnp.sin(ang_c), jnp.sin(ang_c)], axis=1)
    return jnp.tile(cos, (1, 2)), jnp.tile(sin, (1, 2))


def _attn_kernel(*refs, tq, tk, nk, tail):
    refs = list(refs)
    qt_ref = refs.pop(0)
    k_ref, vt_ref = (refs.pop(0), refs.pop(0)) if nk else (None, None)
    kc_ref, vtc_ref = (refs.pop(0), refs.pop(0)) if tail else (None, None)
    o_ref, qs_ref, s0, s1, p0, p1, a0, a1, mx0, mx1, m_ref, acc_ref = refs
    s_bufs, p_bufs, a_bufs, mx_bufs = (s0, s1), (p0, p1), (a0, a1), (mx0, mx1)
    n_blocks = nk + (1 if tail else 0)

    _attn_stack_queries(qt_ref, qs_ref, tq)
    m_ref[...] = jnp.full(m_ref.shape, -jnp.inf, F32)
    acc_ref[...] = jnp.zeros(acc_ref.shape, F32)

    def block(t):
        if isinstance(t, int) and t >= nk:
            return kc_ref[0, 0], vtc_ref[0, 0], tail
        off = t * tk if isinstance(t, int) else pl.multiple_of(t * tk, tk)
        return k_ref[0, 0, pl.ds(off, tk), :], vt_ref[0, 0, :, pl.ds(off, tk)], tk

    def scores(t, slot):
        k_rows, _, n = block(t)
        s = jnp.dot(k_rows, qs_ref[...], preferred_element_type=F32)
        s_bufs[slot][0:n, :] = s
        mx_bufs[slot][...] = jnp.max(s, axis=0, keepdims=True)

    def numerators(n, slot):
        s_ref, p_ref, a_ref = s_bufs[slot], p_bufs[slot], a_bufs[slot]
        for c0 in range(0, Q_PER_KV * tq, 128):
            cols = slice(c0, c0 + 128)
            m_old = m_ref[:, cols]
            m_new = jnp.maximum(m_old, mx_bufs[slot][:, cols])
            a_ref[:, cols] = jnp.exp2(m_old - m_new)
            p_ref[0:n, cols] = jnp.exp2(s_ref[0:n, cols] - m_new).astype(BF16)
            m_ref[:, cols] = m_new

    def weighted_sum(t, slot):
        _, vt, n = block(t)
        pv = jnp.dot(vt, p_bufs[slot][0:n, :], preferred_element_type=F32)
        acc_ref[...] = a_bufs[slot][...] * acc_ref[...] + pv

    def rows_of(t):
        return tk if t < nk else tail

    def step(t, slot, n_mid):
        scores(t, slot)
        numerators(n_mid, 1 - slot)
        weighted_sum(t - 2, slot)

    scores(0, 0)
    if n_blocks > 1:
        scores(1, 1)
        numerators(rows_of(0), 0)
        n_pairs = max(nk - 2, 0) // 2

        def pair(i, carry):
            t = 2 + 2 * i
            step(t, 0, tk)
            step(t + 1, 1, tk)
            return carry

        if n_pairs:
            lax.fori_loop(0, n_pairs, pair, 0)
        for t in range(2 + 2 * n_pairs, n_blocks):
            step(t, t % 2, rows_of(t - 1))
        last = n_blocks - 1
        numerators(rows_of(last), last % 2)
        weighted_sum(last - 1, (last - 1) % 2)
        weighted_sum(last, last % 2)
    else:
        numerators(rows_of(0), 0)
        weighted_sum(0, 0)

    _attn_write_output(acc_ref, o_ref, tq)


def _attn_write_output(acc_ref, o_ref, tq):
    acc = acc_ref[...]
    ot = acc[0:HEAD_DIM, :] / acc[HEAD_DIM:2 * HEAD_DIM, :]
    for half in range(2):
        pair_t = jnp.concatenate([ot[:, (2 * half) * tq:(2 * half + 1) * tq],
                                  ot[:, (2 * half + 1) * tq:(2 * half + 2) * tq]], axis=0)
        o_ref[:, 128 * half:128 * (half + 1)] = pair_t.T.astype(o_ref.dtype)


def _attn_stack_queries(qt_ref, qs_ref, tq):
    row_group = lax.broadcasted_iota(jnp.int32, (256, tq), 0) // HEAD_DIM
    qt = qt_ref[0, 0]
    for g in range(Q_PER_KV):
        qs_ref[:, g * tq:(g + 1) * tq] = jnp.where(row_group == g, qt, jnp.zeros_like(qt))


def _attn_stale_max_kernel(*refs, tq, tk, nk, n_side):
    qt_ref, k_ref, vt_ref, kc_ref, vtc_ref = refs[:5]
    side_in = refs[5:5 + n_side]
    o_ref = refs[5 + n_side]
    side_out = refs[6 + n_side:6 + 2 * n_side]
    qs_ref, p0, p1, f0, f1, m_ref, acc_ref = refs[6 + 2 * n_side:]
    for src, dst in zip(side_in, side_out):
        dst[...] = src[...].astype(dst.dtype)
    p_bufs, f_bufs = (p0, p1), (f0, f1)
    _attn_stack_queries(qt_ref, qs_ref, tq)

    s = jnp.dot(kc_ref[0, 0], qs_ref[...], preferred_element_type=F32)
    m0 = jnp.max(s, axis=0, keepdims=True)
    m_ref[...] = m0
    acc_ref[...] = jnp.dot(vtc_ref[0, 0], jnp.exp2(s - m0).astype(BF16), preferred_element_type=F32)

    def numerators(t, slot):
        off = t * tk if isinstance(t, int) else pl.multiple_of(t * tk, tk)
        s = jnp.dot(k_ref[0, 0, pl.ds(off, tk), :], qs_ref[...], preferred_element_type=F32)
        m_old = m_ref[...]
        p_bufs[slot][...] = jnp.exp2(s - m_old).astype(BF16)
        m_new = jnp.maximum(m_old, jnp.max(s, axis=0, keepdims=True))
        f_bufs[slot][...] = jnp.exp2(m_old - m_new)
        m_ref[...] = m_new

    def weighted_sum(t, slot):
        off = t * tk if isinstance(t, int) else pl.multiple_of(t * tk, tk)
        pv = jnp.dot(vt_ref[0, 0, :, pl.ds(off, tk)], p_bufs[slot][...], preferred_element_type=F32)
        acc_ref[...] = (acc_ref[...] + pv) * f_bufs[slot][...]

    def step(t, slot):
        numerators(t, slot)
        weighted_sum(t - 1, 1 - slot)

    numerators(0, 0)
    n_pairs = (nk - 1) // 2

    def pair(i, carry):
        t = 1 + 2 * i
        step(t, 1)
        step(t + 1, 0)
        return carry

    if n_pairs:
        lax.fori_loop(0, n_pairs, pair, 0)
    for t in range(1 + 2 * n_pairs, nk):
        step(t, t % 2)
    weighted_sum(nk - 1, (nk - 1) % 2)
    _attn_write_output(acc_ref, o_ref, tq)


def attention_stale_max(qt, k4, vt1, k4_tail, vt1_tail, side=(), *, batch, seq_q, tq, tk):
    nq = seq_q // tq
    lanes = Q_PER_KV * tq
    lk = k4.shape[2]
    tail = k4_tail.shape[2]
    n_steps = batch * N_KV_HEADS * nq
    assert all(n_steps % w.shape[0] == 0 for w in side)

    def side_spec(w):
        repeat = n_steps // w.shape[0]
        return pl.BlockSpec((1,) + w.shape[1:], lambda b, h, i: (((b * N_KV_HEADS + h) * nq + i) // repeat, 0, 0))

    side_specs = [side_spec(w) for w in side]
    outs = pl.pallas_call(
        functools.partial(_attn_stale_max_kernel, tq=tq, tk=tk, nk=lk // tk, n_side=len(side)),
        grid=(batch, N_KV_HEADS, nq),
        in_specs=[
            pl.BlockSpec((1, 1, 256, tq), lambda b, h, i: (b, h, 0, i)),
            pl.BlockSpec((1, 1, lk, 256), lambda b, h, i: (b, h, 0, 0)),
            pl.BlockSpec((1, 1, 128, lk), lambda b, h, i: (b, h, 0, 0)),
            pl.BlockSpec((1, 1, tail, 256), lambda b, h, i: (b, h, 0, 0)),
            pl.BlockSpec((1, 1, 128, tail), lambda b, h, i: (b, h, 0, 0)),
        ] + side_specs,
        out_specs=[pl.BlockSpec((tq, 256), lambda b, h, i: (b * nq + i, h))] + side_specs,
        out_shape=[jax.ShapeDtypeStruct((batch * seq_q, Q_W), BF16)]
        + [jax.ShapeDtypeStruct(w.shape, BF16) for w in side],
        scratch_shapes=[
            pltpu.VMEM((256, lanes), BF16),
            pltpu.VMEM((tk, lanes), BF16), pltpu.VMEM((tk, lanes), BF16),
            pltpu.VMEM((1, lanes), F32), pltpu.VMEM((1, lanes), F32),
            pltpu.VMEM((1, lanes), F32),
            pltpu.VMEM((2 * HEAD_DIM, lanes), F32),
        ],
        compiler_params=_cparams(3),
        name="attention_stale_max",
    )(qt, k4, vt1, k4_tail, vt1_tail, *side)
    return tuple(outs)


def attention(qt, k4, vt1, k4_tail, vt1_tail, *, batch, seq_q, tq, tk):
    nq = seq_q // tq
    lanes = Q_PER_KV * tq
    nk = 0 if k4 is None else k4.shape[2] // tk
    tail = 0 if k4_tail is None else k4_tail.shape[2]
    buf_rows = max(tk if nk else 0, tail)
    in_specs = [pl.BlockSpec((1, 1, 256, tq), lambda b, h, i: (b, h, 0, i))]
    args = [qt]
    if nk:
        lk = k4.shape[2]
        in_specs += [pl.BlockSpec((1, 1, lk, 256), lambda b, h, i: (b, h, 0, 0)),
                     pl.BlockSpec((1, 1, 128, lk), lambda b, h, i: (b, h, 0, 0))]
        args += [k4, vt1]
    if tail:
        in_specs += [pl.BlockSpec((1, 1, tail, 256), lambda b, h, i: (b, h, 0, 0)),
                     pl.BlockSpec((1, 1, 128, tail), lambda b, h, i: (b, h, 0, 0))]
        args += [k4_tail, vt1_tail]
    return pl.pallas_call(
        functools.partial(_attn_kernel, tq=tq, tk=tk, nk=nk, tail=tail),
        grid=(batch, N_KV_HEADS, nq),
        in_specs=in_specs,
        out_specs=pl.BlockSpec((tq, 256), lambda b, h, i: (b * nq + i, h)),
        out_shape=jax.ShapeDtypeStruct((batch * seq_q, Q_W), BF16),
        scratch_shapes=[
            pltpu.VMEM((256, lanes), BF16),
            pltpu.VMEM((buf_rows, lanes), F32), pltpu.VMEM((buf_rows, lanes), F32),
            pltpu.VMEM((buf_rows, lanes), BF16), pltpu.VMEM((buf_rows, lanes), BF16),
            pltpu.VMEM((1, lanes), F32), pltpu.VMEM((1, lanes), F32),
            pltpu.VMEM((1, lanes), F32), pltpu.VMEM((1, lanes), F32),
            pltpu.VMEM((1, lanes), F32),
            pltpu.VMEM((2 * HEAD_DIM, lanes), F32),
        ],
        compiler_params=_cparams(3),
        name="attention",
    )(*args)


def _dft_cs(n):
    k = np.arange(n)
    ang = 2.0 * np.pi * ((k[:, None] * k[None, :]) % n) / n
    return np.cos(ang), np.sin(ang)


def _fft1_kernel(x_ref, f_ref, c_ref, s_ref, o_ref, *, n1):
    y = jnp.dot(f_ref[...], x_ref[0], preferred_element_type=F32)
    yr, yi = y[:n1], y[n1:]
    c, s = c_ref[...], s_ref[...]
    o_ref[0, 0] = (yr * c + yi * s).astype(o_ref.dtype)
    o_ref[0, 1] = (yi * c - yr * s).astype(o_ref.dtype)


def _fft2_kernel(y_ref, f_ref, bc_ref, bs_ref, o_ref, *, n2, kb):
    for j in range(kb):
        y2 = jnp.concatenate([y_ref[0, 0, j], y_ref[0, 1, j]], axis=0)
        x2 = jnp.dot(f_ref[...], y2, preferred_element_type=F32)
        xr = x2[:n2].astype(BF16)
        xi = x2[n2:].astype(BF16)
        z = (jnp.dot(xr, bc_ref[...], preferred_element_type=F32) + jnp.dot(xi, bs_ref[...], preferred_element_type=F32))
        o_ref[0, j] = z.astype(o_ref.dtype)


def fourier_mix(u, *, batch, seq, n1, n2):
    cw = u.shape[1]
    lanes = n2 * cw
    tl = min(lanes, 4096)
    c1, s1 = _dft_cs(n1)
    f1 = jnp.asarray(np.concatenate([c1, -s1], axis=0), BF16)
    k1 = np.arange(n1)[:, None]
    t2 = np.arange(n2)[None, :]
    ang = 2.0 * np.pi * ((k1 * t2) % seq) / seq
    twc = jnp.asarray(np.repeat(np.cos(ang), cw, axis=1), F32)
    tws = jnp.asarray(np.repeat(np.sin(ang), cw, axis=1), F32)
    x2 = u.reshape(batch, n1, lanes)
    yp = pl.pallas_call(
        functools.partial(_fft1_kernel, n1=n1),
        grid=(batch, lanes // tl),
        in_specs=[
            pl.BlockSpec((1, n1, tl), lambda b, j: (b, 0, j)),
            pl.BlockSpec((2 * n1, n1), lambda b, j: (0, 0)),
            pl.BlockSpec((n1, tl), lambda b, j: (0, j)),
            pl.BlockSpec((n1, tl), lambda b, j: (0, j)),
        ],
        out_specs=pl.BlockSpec((1, 2, n1, tl), lambda b, j: (b, 0, 0, j)),
        out_shape=jax.ShapeDtypeStruct((batch, 2, n1, lanes), BF16),
        compiler_params=_cparams(2),
        name="fft_stage1",
    )(x2, f1, twc, tws)

    c2, s2 = _dft_cs(n2)
    f2 = jnp.asarray(np.block([[c2, s2], [-s2, c2]]), BF16)
    cg, sg = _dft_cs(FOURIER_GW)
    norm = 1.0 / math.sqrt(seq * FOURIER_GW)
    bdc = jnp.asarray(np.kron(np.eye(cw // FOURIER_GW), cg) * norm, BF16)
    bds = jnp.asarray(np.kron(np.eye(cw // FOURIER_GW), sg) * norm, BF16)
    kb = min(n1, 16)
    y5 = yp.reshape(batch, 2, n1, n2, cw)
    z = pl.pallas_call(
        functools.partial(_fft2_kernel, n2=n2, kb=kb),
        grid=(batch, n1 // kb),
        in_specs=[
            pl.BlockSpec((1, 2, kb, n2, cw), lambda b, j: (b, 0, j, 0, 0)),
            pl.BlockSpec((2 * n2, 2 * n2), lambda b, j: (0, 0)),
            pl.BlockSpec((cw, cw), lambda b, j: (0, 0)),
            pl.BlockSpec((cw, cw), lambda b, j: (0, 0)),
        ],
        out_specs=pl.BlockSpec((1, kb, n2, cw), lambda b, j: (b, j, 0, 0)),
        out_shape=jax.ShapeDtypeStruct((batch, n1, n2, cw), BF16),
        compiler_params=_cparams(2),
        name="fft_stage2",
    )(y5, f2, bdc, bds)
    return z.transpose(0, 2, 1, 3).reshape(batch * seq, cw)


def _merge_kernel(x_ref, mod_ref, gate_ref, cp_ref, cpp_ref, cpn_ref, yf_ref, at_ref,
                  wf_ref, wc_ref, wp_ref, wa_ref, wo_ref, dw_ref, cb_ref, cg_ref, pw_ref, ps_ref,
                  o_ref, ybuf, xbuf, ysh, cacc, *, t, tps, seq):
    i = pl.program_id(0)
    pos_tile = i % tps
    keep_prev = jnp.where(pos_tile != 0, 1.0, 0.0).astype(F32)
    keep_next = jnp.where(pos_tile != tps - 1, 1.0, 0.0).astype(F32)

    def glu(blk):
        return blk[:, 0:CONV_W].astype(F32)

    cp, cpp, cpn = cp_ref[...], cpp_ref[...], cpn_ref[...]
    ybuf[0:HALO, :] = glu(cpp) * keep_prev
    ybuf[HALO:HALO + t, :] = glu(cp)
    ybuf[HALO + t:HALO + t + HALO, :] = glu(cpn) * keep_next
    xbuf[0:HALO, :] = cpp[:, 2 * CONV_W:].astype(F32) * keep_prev
    xbuf[HALO:HALO + t, :] = cp[:, 2 * CONV_W:].astype(F32)
    xbuf[HALO + t:HALO + t + HALO, :] = cpn[:, 2 * CONV_W:].astype(F32) * keep_next

    n_sh = t + 2 * HALO - 8
    for b in range(1, 8):
        ysh[b - 1, 0:n_sh, :] = ybuf[pl.ds(b, n_sh), :]
    for r0 in range(0, t, CONV_ROWS):
        part = jnp.zeros((CONV_ROWS, CONV_W), F32)
        for k in range(CONV_K):
            a, b = divmod(HALO - CONV_HALF + k, 8)
            src = ybuf if b == 0 else ysh.at[b - 1]
            part = part + dw_ref[k:k + 1, :] * src[8 * a + r0:8 * a + r0 + CONV_ROWS, :]
        cacc[r0:r0 + CONV_ROWS, :] = part + cb_ref[...]
    acc = cacc[...]
    ms = jnp.mean(acc * acc, axis=-1, keepdims=True)
    conv_out = _silu(acc * lax.rsqrt(ms + EPS) * cg_ref[...]).astype(BF16)

    def xs(d):
        return xbuf[pl.ds(HALO + d, t), :]

    x0 = xs(0)
    s2 = xs(-1) + x0
    s4 = s2 + xs(-2) + xs(1)
    s8 = s4 + xs(-4) + xs(-3) + xs(2) + xs(3)
    s16 = s8 + xs(-8) + xs(-7) + xs(-6) + xs(-5) + xs(4) + xs(5) + xs(6) + xs(7)
    grp = lax.broadcasted_iota(jnp.int32, (t, POOL_W), 1) // POOL_GW
    pos = pos_tile * t + lax.broadcasted_iota(jnp.int32, (t, POOL_W), 0)
    half = jnp.where(grp == 0, 1, jnp.where(grp == 1, 2, jnp.where(grp == 2, 4, 8)))
    cnt = (jnp.minimum(pos + half, seq) - jnp.maximum(pos - half, 0)).astype(F32)
    wsum = jnp.where(grp == 0, s2, jnp.where(grp == 1, s4, jnp.where(grp == 2, s8, s16)))
    pool_in = (wsum / cnt - x0).astype(BF16)
    pool_out = (jnp.dot(pool_in, pw_ref[...], preferred_element_type=F32) * ps_ref[...]).astype(BF16)

    def gate(b):
        return gate_ref[:, b * D_MODEL:(b + 1) * D_MODEL].astype(F32)

    merged = gate(0) * jnp.dot(yf_ref[...], wf_ref[...], preferred_element_type=F32)
    merged = merged + gate(1) * jnp.dot(conv_out, wc_ref[...], preferred_element_type=F32)
    merged = merged + gate(2) * jnp.dot(pool_out, wp_ref[...], preferred_element_type=F32)
    merged = merged + gate(3) * jnp.dot(at_ref[...], wa_ref[...], preferred_element_type=F32)
    out = jnp.dot(merged.astype(BF16), wo_ref[...], preferred_element_type=F32)
    o_ref[...] = x_ref[...] + mod_ref[0, 2:3, :] * out


def merge_branches(x2d, mod, proj, yf, attn, lw, *, seq, t):
    m, d = x2d.shape
    tps = seq // t
    hb = t // HALO
    n_halo = m // HALO
    n_mod = mod.shape[0]
    mod_idx = (lambda i: (i // tps, 0, 0)) if n_mod > 1 else (lambda i: (0, 0, 0))
    const = lambda i: (0, 0)
    cp_blk = P_OFF_CP // CP_W
    return pl.pallas_call(
        functools.partial(_merge_kernel, t=t, tps=tps, seq=seq),
        grid=(m // t,),
        in_specs=[
            pl.BlockSpec((t, d), lambda i: (i, 0)),
            pl.BlockSpec((1, 6, d), mod_idx),
            pl.BlockSpec((t, 4 * d), lambda i: (i, 0)),
            pl.BlockSpec((t, CP_W), lambda i: (i, cp_blk)),
            pl.BlockSpec((HALO, CP_W), lambda i: (jnp.maximum(i * hb - 1, 0), cp_blk)),
            pl.BlockSpec((HALO, CP_W), lambda i: (jnp.minimum((i + 1) * hb, n_halo - 1), cp_blk)),
            pl.BlockSpec((t, FOURIER_W), lambda i: (i, 0)),
            pl.BlockSpec((t, Q_W), lambda i: (i, 0)),
            pl.BlockSpec((FOURIER_W, d), const),
            pl.BlockSpec((CONV_W, d), const),
            pl.BlockSpec((POOL_W, d), const),
            pl.BlockSpec((Q_W, d), const),
            pl.BlockSpec((d, d), const),
            pl.BlockSpec((CONV_K, CONV_W), const),
            pl.BlockSpec((1, CONV_W), const),
            pl.BlockSpec((1, CONV_W), const),
            pl.BlockSpec((POOL_W, POOL_W), const),
            pl.BlockSpec((1, POOL_W), const),
        ],
        out_specs=pl.BlockSpec((t, d), lambda i: (i, 0)),
        out_shape=jax.ShapeDtypeStruct((m, d), F32),
        scratch_shapes=[pltpu.VMEM((t + 2 * HALO, CONV_W), F32), pltpu.VMEM((t + 2 * HALO, POOL_W), F32),
                        pltpu.VMEM((7, t + 2 * HALO, CONV_W), F32), pltpu.VMEM((t, CONV_W), F32)],
        compiler_params=_cparams(1),
        name="merge_branches",
    )(x2d, mod, proj, proj, proj, proj, yf, attn,
      lw["wf"], lw["wc"], lw["wp"], lw["wa"], lw["wo"], lw["dw"], lw["cb"], lw["cg"], lw["pw"], lw["ps"])


def _ffn_kernel(x_ref, mod_ref, g_ref, w1_ref, w3_ref, w2_ref, o_ref, *, chunks):
    x = x_ref[...]
    h = _norm_mod(x, g_ref[...], mod_ref[0, 3:4, :], mod_ref[0, 4:5, :]).astype(BF16)
    acc = jnp.zeros(x.shape, F32)
    for c0, cw in chunks:
        a = jnp.dot(h, w1_ref[:, c0:c0 + cw], preferred_element_type=F32)
        b = jnp.dot(h, w3_ref[:, c0:c0 + cw], preferred_element_type=F32)
        acc = acc + jnp.dot((_silu(a) * b).astype(BF16), w2_ref[c0:c0 + cw, :], preferred_element_type=F32)
    o_ref[...] = x + mod_ref[0, 5:6, :] * acc


def ffn_dense(x2d, mod, g, w1, w3, w2, *, tm, tiles_per_mod):
    m, d = x2d.shape
    dff = w1.shape[1]
    chunks = tuple((c0, min(1024, dff - c0)) for c0 in range(0, dff, 1024))
    n_mod = mod.shape[0]
    mod_idx = (lambda i: (i // tiles_per_mod, 0, 0)) if n_mod > 1 else (lambda i: (0, 0, 0))
    const = lambda i: (0, 0)
    return pl.pallas_call(
        functools.partial(_ffn_kernel, chunks=chunks),
        grid=(m // tm,),
        in_specs=[
            pl.BlockSpec((tm, d), lambda i: (i, 0)),
            pl.BlockSpec((1, 6, d), mod_idx),
            pl.BlockSpec((1, d), const),
            pl.BlockSpec((d, dff), const, pipeline_mode=pl.Buffered(1)),
            pl.BlockSpec((d, dff), const, pipeline_mode=pl.Buffered(1)),
            pl.BlockSpec((dff, d), const, pipeline_mode=pl.Buffered(1)),
        ],
        out_specs=pl.BlockSpec((tm, d), lambda i: (i, 0)),
        out_shape=jax.ShapeDtypeStruct((m, d), F32),
        compiler_params=_cparams(1),
        name="ffn_dense",
    )(x2d, mod, g.reshape(1, d), w1, w3, w2)


def _top2(logits):
    t = logits.shape[0]
    lane = lax.broadcasted_iota(jnp.int32, (t, 128), 1).astype(F32)
    neg = jnp.float32(-jnp.inf)
    lg = jnp.where(lane < N_EXPERTS, logits, neg)
    v1 = jnp.max(lg, axis=-1, keepdims=True)
    i1 = jnp.min(jnp.where(lg == v1, lane, 128.0), axis=-1, keepdims=True)
    lg2 = jnp.where(lane == i1, neg, lg)
    v2 = jnp.max(lg2, axis=-1, keepdims=True)
    i2 = jnp.min(jnp.where(lg2 == v2, lane, 128.0), axis=-1, keepdims=True)
    e2 = jnp.exp(v2 - v1)
    return i1, i2, 1.0 / (1.0 + e2), e2 / (1.0 + e2)


R_E1, R_E2, R_W1, R_W2, R_RANK1, R_RANK2 = range(6)


def _route_kernel(x_ref, mod_ref, g_ref, r_ref, tri_ref, route_ref, cnt_ref, carry_ref):
    @pl.when(pl.program_id(0) == 0)
    def _():
        carry_ref[...] = jnp.zeros(carry_ref.shape, F32)

    t = x_ref.shape[0]
    h = _norm_mod(x_ref[...], g_ref[...], mod_ref[0, 3:4, :], mod_ref[0, 4:5, :])
    r = r_ref[...]
    h_hi, r_hi = h.astype(BF16), r.astype(BF16)
    h_lo, r_lo = (h - h_hi.astype(F32)).astype(BF16), (r - r_hi.astype(F32)).astype(BF16)
    logits = (jnp.dot(h_hi, r_hi, preferred_element_type=F32) + jnp.dot(h_hi, r_lo, preferred_element_type=F32)
              + jnp.dot(h_lo, r_hi, preferred_element_type=F32))
    i1, i2, w1, w2 = _top2(logits)
    lane = lax.broadcasted_iota(jnp.int32, (t, 128), 1).astype(F32)
    oh1 = jnp.where(lane == i1, 1.0, 0.0)
    oh2 = jnp.where(lane == i2, 1.0, 0.0)
    both = oh1 + oh2
    before = carry_ref[...] + jnp.dot(tri_ref[...], both.astype(BF16), preferred_element_type=F32)
    rank1 = jnp.sum(oh1 * before, axis=-1, keepdims=True)
    rank2 = jnp.sum(oh2 * before, axis=-1, keepdims=True)
    carry_ref[...] += jnp.sum(both, axis=0, keepdims=True)
    rec = jnp.zeros((t, 128), F32)
    for col, val in ((R_E1, i1), (R_E2, i2), (R_W1, w1), (R_W2, w2), (R_RANK1, rank1), (R_RANK2, rank2)):
        rec = jnp.where(lane == col, val, rec)
    route_ref[...] = rec
    cnt_ref[...] = carry_ref[...]


def moe_route(x2d, mod, g, router_pad, *, tm, tiles_per_mod):
    m, d = x2d.shape
    n_mod = mod.shape[0]
    mod_idx = (lambda i: (i // tiles_per_mod, 0, 0)) if n_mod > 1 else (lambda i: (0, 0, 0))
    tri = jnp.asarray(np.tril(np.ones((tm, tm), np.float32), -1), BF16)
    return pl.pallas_call(
        _route_kernel,
        grid=(m // tm,),
        in_specs=[
            pl.BlockSpec((tm, d), lambda i: (i, 0)),
            pl.BlockSpec((1, 6, d), mod_idx),
            pl.BlockSpec((1, d), lambda i: (0, 0)),
            pl.BlockSpec((d, 128), lambda i: (0, 0)),
            pl.BlockSpec((tm, tm), lambda i: (0, 0)),
        ],
        out_specs=[pl.BlockSpec((tm, 128), lambda i: (i, 0)), pl.BlockSpec((1, 128), lambda i: (0, 0))],
        out_shape=[jax.ShapeDtypeStruct((m, 128), F32), jax.ShapeDtypeStruct((1, 128), F32)],
        scratch_shapes=[pltpu.VMEM((1, 128), F32)],
        compiler_params=_cparams(1),
        name="moe_route",
    )(x2d, mod, g.reshape(1, d), router_pad, tri)


def _dispatch_kernel(pos_ref, pad_ref, x_ref, mod_ref, g_ref, xs_ref, h_ref, zrow_ref, sem, zsem):
    i = pl.program_id(0)
    t = x_ref.shape[0]
    slot = i % 2
    h_ref[slot] = _norm_mod(x_ref[...], g_ref[...], mod_ref[0, 3:4, :], mod_ref[0, 4:5, :])

    def row_copy(r, dst_row):
        return pltpu.make_async_copy(h_ref.at[slot, pl.ds(r, 1), :], xs_ref.at[pl.ds(dst_row, 1), :], sem.at[slot])

    def issue(r, carry):
        row_copy(r, pos_ref[0, 0, r]).start()
        row_copy(r, pos_ref[0, 0, t + r]).start()
        return carry

    lax.fori_loop(0, t, issue, 0, unroll=DMA_ISSUE_UNROLL)

    def drain(s):
        for _ in range(2):
            pltpu.make_async_copy(h_ref.at[s], xs_ref.at[pl.ds(0, t), :], sem.at[s]).wait()

    @pl.when(i > 0)
    def _():
        drain(1 - slot)

    @pl.when(i == pl.num_programs(0) - 1)
    def _():
        drain(slot)
        zrow_ref[...] = jnp.zeros(zrow_ref.shape, F32)

        def zero_copy(r):
            return pltpu.make_async_copy(zrow_ref.at[pl.ds(0, 1), :], xs_ref.at[pl.ds(r, 1), :], zsem)

        def start_one(r, carry):
            zero_copy(r).start()
            return carry

        def wait_one(r, carry):
            zero_copy(r).wait()
            return carry

        for e in range(N_EXPERTS):
            lax.fori_loop(pad_ref[0, e], pad_ref[0, N_EXPERTS + e], start_one, 0)
        for e in range(N_EXPERTS):
            lax.fori_loop(pad_ref[0, e], pad_ref[0, N_EXPERTS + e], wait_one, 0)

        h_ref[1 - slot] = jnp.zeros((t, h_ref.shape[2]), F32)

        def tile_copy(k):
            return pltpu.make_async_copy(h_ref.at[1 - slot], xs_ref.at[pl.ds(pl.multiple_of(k * t, t), t), :], zsem)

        def start_tile(k, carry):
            tile_copy(k).start()
            return carry

        def wait_tile(k, carry):
            tile_copy(k).wait()
            return carry

        first_unused, n_tiles = pad_ref[0, 2 * N_EXPERTS], xs_ref.shape[0] // t
        lax.fori_loop(first_unused, n_tiles, start_tile, 0)
        lax.fori_loop(first_unused, n_tiles, wait_tile, 0)


def moe_dispatch(x2d, mod, g, pos_tiles, pad_rows, n_rows, *, tm, tiles_per_mod):
    m, d = x2d.shape
    n_mod = mod.shape[0]
    mod_idx = (lambda i: (i // tiles_per_mod, 0, 0)) if n_mod > 1 else (lambda i: (0, 0, 0))
    return pl.pallas_call(
        _dispatch_kernel,
        grid=(m // tm,),
        in_specs=[
            pl.BlockSpec((1, 1, 2 * tm), lambda i: (i, 0, 0), memory_space=pltpu.SMEM),
            pl.BlockSpec((1, 2 * N_EXPERTS + 1), lambda i: (0, 0), memory_space=pltpu.SMEM),
            pl.BlockSpec((tm, d), lambda i: (i, 0)),
            pl.BlockSpec((1, 6, d), mod_idx),
            pl.BlockSpec((1, d), lambda i: (0, 0)),
        ],
        out_specs=pl.BlockSpec(memory_space=pl.ANY),
        out_shape=jax.ShapeDtypeStruct((n_rows, d), F32),
        scratch_shapes=[pltpu.VMEM((2, tm, d), F32), pltpu.VMEM((8, d), F32),
                        pltpu.SemaphoreType.DMA((2,)), pltpu.SemaphoreType.DMA(())],
        compiler_params=_cparams(1),
        name="moe_dispatch",
    )(pos_tiles, pad_rows, x2d, mod, g.reshape(1, d))


def _experts_kernel(te_ref, nv_ref, xs_ref, w1_ref, w3_ref, w2_ref, ys_ref, xb_ref, acc_ref):
    i = pl.program_id(0)
    j = pl.program_id(1)
    valid = i < nv_ref[0]

    @pl.when(jnp.logical_and(valid, j == 0))
    def _():
        xb_ref[...] = xs_ref[...].astype(BF16)
        acc_ref[...] = jnp.zeros(acc_ref.shape, F32)

    @pl.when(valid)
    def _():
        h = xb_ref[...]
        a = jnp.dot(h, w1_ref[0], preferred_element_type=F32)
        b = jnp.dot(h, w3_ref[0], preferred_element_type=F32)
        acc_ref[...] += jnp.dot((_silu(a) * b).astype(BF16), w2_ref[0], preferred_element_type=F32)

    @pl.when(jnp.logical_and(valid, j == pl.num_programs(1) - 1))
    def _():
        ys_ref[...] = acc_ref[...]

    @pl.when(jnp.logical_and(jnp.logical_not(valid), j == pl.num_programs(1) - 1))
    def _():
        ys_ref[...] = jnp.zeros(ys_ref.shape, F32)


def moe_experts_grouped(xs, tile_expert, n_valid, w1, w3, w2, *, tm, tf):
    n_rows, d = xs.shape
    dff = w1.shape[2]
    nf = dff // tf

    def w13_idx(i, j, te, nv):
        return (te[i], 0, jnp.where(i < nv[0], j, nf - 1))

    def w2_idx(i, j, te, nv):
        return (te[i], jnp.where(i < nv[0], j, nf - 1), 0)

    grid_spec = pltpu.PrefetchScalarGridSpec(
        num_scalar_prefetch=2,
        grid=(n_rows // tm, nf),
        in_specs=[
            pl.BlockSpec((tm, d), lambda i, j, te, nv: (jnp.minimum(i, nv[0] - 1), 0)),
            pl.BlockSpec((1, d, tf), w13_idx),
            pl.BlockSpec((1, d, tf), w13_idx),
            pl.BlockSpec((1, tf, d), w2_idx),
        ],
        out_specs=pl.BlockSpec((tm, d), lambda i, j, te, nv: (i, 0)),
        scratch_shapes=[pltpu.VMEM((tm, d), BF16), pltpu.VMEM((tm, d), F32)],
    )
    return pl.pallas_call(
        _experts_kernel,
        grid_spec=grid_spec,
        out_shape=jax.ShapeDtypeStruct((n_rows, d), F32),
        compiler_params=_cparams(2),
        name="moe_experts_grouped",
    )(tile_expert, n_valid, xs, w1, w3, w2)


def _combine_kernel(pos_ref, pos_next_ref, x_ref, mod_ref, rt_ref, ys_ref, o_ref, y_ref, sem):
    i = pl.program_id(0)
    t = x_ref.shape[0]
    slot = i % 2

    def issue_tile(p_ref, s):
        def issue(r, carry):
            for k in range(2):
                pltpu.make_async_copy(ys_ref.at[pl.ds(p_ref[0, 0, k * t + r], 1), :],
                                      y_ref.at[s, k, pl.ds(r, 1), :], sem.at[s]).start()
            return carry

        lax.fori_loop(0, t, issue, 0, unroll=DMA_ISSUE_UNROLL)

    @pl.when(i == 0)
    def _():
        issue_tile(pos_ref, 0)

    @pl.when(i + 1 < pl.num_programs(0))
    def _():
        issue_tile(pos_next_ref, 1 - slot)

    for k in range(2):
        pltpu.make_async_copy(ys_ref.at[pl.ds(0, t), :], y_ref.at[slot, k], sem.at[slot]).wait()
    rt = rt_ref[...]
    mix = rt[:, R_W1:R_W1 + 1] * y_ref[slot, 0] + rt[:, R_W2:R_W2 + 1] * y_ref[slot, 1]
    o_ref[...] = x_ref[...] + mod_ref[0, 5:6, :] * mix


def moe_combine(x2d, mod, route, pos_tiles, ys, *, tm, tiles_per_mod):
    m, d = x2d.shape
    n_mod = mod.shape[0]
    mod_idx = (lambda i: (i // tiles_per_mod, 0, 0)) if n_mod > 1 else (lambda i: (0, 0, 0))
    n_tiles = m // tm
    return pl.pallas_call(
        _combine_kernel,
        grid=(n_tiles,),
        in_specs=[
            pl.BlockSpec((1, 1, 2 * tm), lambda i: (i, 0, 0), memory_space=pltpu.SMEM),
            pl.BlockSpec((1, 1, 2 * tm), lambda i: (jnp.minimum(i + 1, n_tiles - 1), 0, 0), memory_space=pltpu.SMEM),
            pl.BlockSpec((tm, d), lambda i: (i, 0)),
            pl.BlockSpec((1, 6, d), mod_idx),
            pl.BlockSpec((tm, 128), lambda i: (i, 0)),
            pl.BlockSpec(memory_space=pl.ANY),
        ],
        out_specs=pl.BlockSpec((tm, d), lambda i: (i, 0)),
        out_shape=jax.ShapeDtypeStruct((m, d), F32),
        scratch_shapes=[pltpu.VMEM((2, 2, tm, d), F32), pltpu.SemaphoreType.DMA((2,))],
        compiler_params=_cparams(1),
        name="moe_combine",
    )(pos_tiles, pos_tiles, x2d, mod, route, ys)


def _pos_tiles(pos1, pos2, tm):
    n = pos1.shape[0] // tm
    return jnp.concatenate([pos1.reshape(n, 1, tm), pos2.reshape(n, 1, tm)], axis=2)


def moe_sparse(x2d, mod, g, router_pad, w1, w3, w2, *, rows_per_mod):
    m, d = x2d.shape
    tr, td, tc, te = MOE_ROUTE_TM, MOE_DISPATCH_TM, MOE_COMBINE_TM, MOE_EXPERT_TM
    tiles_per_seq_row = rows_per_mod
    route, cnt = moe_route(x2d, mod, g, router_pad, tm=tr, tiles_per_mod=tiles_per_seq_row // tr)
    counts = cnt[0, 0:N_EXPERTS].astype(jnp.int32)
    group = ((counts + te - 1) // te) * te
    ends = jnp.cumsum(group)
    starts = ends - group
    e1 = route[:, R_E1].astype(jnp.int32)
    e2 = route[:, R_E2].astype(jnp.int32)
    pos1 = starts[e1] + route[:, R_RANK1].astype(jnp.int32)
    pos2 = starts[e2] + route[:, R_RANK2].astype(jnp.int32)
    n_rows = 2 * m + N_EXPERTS * te
    n_tiles = n_rows // te
    tile_start = jnp.arange(n_tiles, dtype=jnp.int32)[:, None] * te
    tile_expert = jnp.minimum(jnp.sum((tile_start >= ends[None, :]).astype(jnp.int32), axis=1), N_EXPERTS - 1)
    n_valid = (ends[-1:] // te).astype(jnp.int32)
    assert td == te
    pad_rows = jnp.concatenate([starts + counts, ends, n_valid]).astype(jnp.int32).reshape(1, 2 * N_EXPERTS + 1)
    xs = moe_dispatch(x2d, mod, g, _pos_tiles(pos1, pos2, td), pad_rows, n_rows, tm=td,
                      tiles_per_mod=tiles_per_seq_row // td)
    ys = moe_experts_grouped(xs, tile_expert, n_valid, w1, w3, w2, tm=te, tf=MOE_EXPERT_TF)
    return moe_combine(x2d, mod, route, _pos_tiles(pos1, pos2, tc), ys, tm=tc, tiles_per_mod=tiles_per_seq_row // tc)


def _permute_w_in(w):
    f, c, p, q, kv, gts = w[:, 0:256], w[:, 256:768], w[:, 768:1024], w[:, 1024:1536], w[:, 1536:1792], w[:, 1792:]
    return jnp.concatenate([gts, q, c, p, f, kv], axis=1).astype(BF16)


def _layer_weights(layer, w_br_fourier, conv_dw, conv_b, conv_norm_g, w_br_conv, pool_w, pool_scale, w_br_pool,
                   w_br_attn, w_out):
    pw = jax.scipy.linalg.block_diag(*[pool_w[layer, i] for i in range(len(POOL_WINDOWS))])
    return {
        "wf": w_br_fourier[layer].astype(BF16), "wc": w_br_conv[layer].astype(BF16),
        "wp": w_br_pool[layer].astype(BF16), "wa": w_br_attn[layer].astype(BF16), "wo": w_out[layer].astype(BF16),
        "dw": conv_dw[layer], "cb": conv_b[layer].reshape(1, CONV_W), "cg": conv_norm_g[layer].reshape(1, CONV_W),
        "pw": pw.astype(BF16), "ps": pool_scale[layer].reshape(1, POOL_W),
    }


def _cast_slices(w, max_slices):
    w2 = w.reshape(-1, w.shape[-1])
    rows = w2.shape[0]
    n = 1
    while 2 * n <= max_slices and rows % (32 * n) == 0:
        n *= 2
    return w2.reshape(n, rows // n, w2.shape[1])


def kernel(x, c, ctx, c_ctx, w_mod, b_mod, norm1_g, norm2_g, w_in, w_br_fourier, conv_dw, conv_b, conv_norm_g,
           w_br_conv, pool_w, pool_scale, w_br_pool, q_norm_g, k_norm_g, w_br_attn, w_out, ffn_w1, ffn_w3, ffn_w2,
           moe_router, moe_w1, moe_w3, moe_w2):
    batch, seq, d = x.shape
    ctx_len = ctx.shape[1]
    depth = w_in.shape[0]
    rope = rope_tables(seq)

    c_rows = jnp.zeros((8, d), F32).at[0:batch].set(c).at[batch].set(c_ctx)
    mods = modulation_all(c_rows, w_mod, b_mod).reshape(depth, 8, 6, d)

    xl = x.reshape(batch * seq, d)
    xc = ctx.reshape(batch * ctx_len, d)
    for layer in range(depth):
        is_last = layer == depth - 1
        mod_l = mods[layer, 0:batch]
        mod_c = mods[layer, batch:batch + 1]
        w_in_l = _permute_w_in(w_in[layer])
        lw = _layer_weights(layer, w_br_fourier, conv_dw, conv_b, conv_norm_g, w_br_conv, pool_w, pool_scale,
                            w_br_pool, w_br_attn, w_out)

        proj_c, qc, ktc, vc = input_projection(xc, mod_c, norm1_g[layer], w_in_l, q_norm_g[layer], k_norm_g[layer],
                                               None, batch=batch, seq=ctx_len, tm=256)

        proj, q, kt, v = input_projection(xl, mod_l, norm1_g[layer], w_in_l, q_norm_g[layer], k_norm_g[layer], rope,
                                          batch=batch, seq=seq, tm=512)
        spread = (2.0 * 1.02 * HEAD_DIM * Q_SCALE) * jnp.max(jnp.abs(q_norm_g[layer])) * jnp.max(
            jnp.abs(k_norm_g[layer]))
        n_steps = batch * N_KV_HEADS * (seq // ATTN_TQ)
        mixer_w = ((ffn_w1, ffn_w3, ffn_w2) if layer % 2 == 0 else (moe_w1, moe_w3, moe_w2))
        mixer_w = tuple(w[layer // 2] for w in mixer_w)
        side = tuple(_cast_slices(w, n_steps) for w in mixer_w)
        attn, *side_bf16 = lax.cond(
            spread < STALE_MAX_EXP_LIMIT,
            lambda ops, sd: attention_stale_max(*ops, sd, batch=batch, seq_q=seq, tq=ATTN_TQ, tk=ATTN_TK),
            lambda ops, sd: (attention(*ops, batch=batch, seq_q=seq, tq=ATTN_TQ, tk=ATTN_TK),)
            + tuple(w.astype(BF16) for w in sd),
            (q, kt, v, ktc, vc), side)
        yf = fourier_mix(proj[:, P_OFF_F:P_OFF_F + FOURIER_W], batch=batch, seq=seq, n1=64, n2=seq // 64)
        xl = merge_branches(xl, mod_l, proj, yf, attn, lw, seq=seq, t=512)

        if not is_last:
            attn_c = attention(qc, None, None, ktc, vc, batch=batch, seq_q=ctx_len, tq=256, tk=ctx_len)
            yf_c = fourier_mix(proj_c[:, P_OFF_F:P_OFF_F + FOURIER_W], batch=batch, seq=ctx_len, n1=16,
                               n2=ctx_len // 16)
            xc = merge_branches(xc, mod_c, proj_c, yf_c, attn_c, lw, seq=ctx_len, t=256)

        j = layer // 2
        w1, w3, w2 = (wb.reshape(w.shape) for wb, w in zip(side_bf16, mixer_w))
        if layer % 2 == 0:
            xl = ffn_dense(xl, mod_l, norm2_g[layer], w1, w3, w2, tm=512, tiles_per_mod=seq // 512)
            if not is_last:
                xc = ffn_dense(xc, mod_c, norm2_g[layer], w1, w3, w2, tm=256, tiles_per_mod=1)
        else:
            router_pad = jnp.zeros((d, 128), F32).at[:, 0:N_EXPERTS].set(moe_router[j])
            xl = moe_sparse(xl, mod_l, norm2_g[layer], router_pad, w1, w3, w2, rows_per_mod=seq)
            if not is_last:
                xc = moe_sparse(xc, mod_c, norm2_g[layer], router_pad, w1, w3, w2, rows_per_mod=batch * ctx_len)
    return xl.reshape(batch, seq, d)
```

```python
import functools
import math

import numpy as np
import jax
import jax.numpy as jnp
from jax import lax
from jax.experimental import pallas as pl
from jax.experimental.pallas import tpu as pltpu

F32 = jnp.float32
BF16 = jnp.bfloat16

D_MODEL = 1024
GRID_W = 64
EPS = 1e-6
FOURIER_GW = 64
FOURIER_W = 256
CONV_W = 256
CONV_K = 31
CONV_HALF = CONV_K // 2
POOL_WINDOWS = (2, 4, 8, 16)
POOL_GW = 64
POOL_W = 256
HEAD_DIM = 64
N_HEADS = 8
N_KV_HEADS = 2
Q_PER_KV = 4
Q_W = 512
KV_W = 128
ROPE_THETA = 10000.0
N_EXPERTS = 8
IN_W = 5888

P_OFF_G = 0
P_OFF_Q = 4096
P_OFF_CP = 4608
P_OFF_F = 5376
P_OFF_KV = 5632
CP_W = 2 * CONV_W + POOL_W

Q_SCALE = (HEAD_DIM ** -0.5) * math.log2(math.e)

ATTN_TQ = 512
ATTN_TK = 1024
STALE_MAX_EXP_LIMIT = 64.0

MOE_ROUTE_TM = 512
MOE_DISPATCH_TM = 512
MOE_COMBINE_TM = 256
MOE_EXPERT_TM = 512
MOE_EXPERT_TF = 1792
DMA_ISSUE_UNROLL = 8

CONV_ROWS = 64
HALO = 16
VMEM_LIMIT = 56 * 1024 * 1024


def _cparams(n_axes):
    return pltpu.CompilerParams(dimension_semantics=("arbitrary",) * n_axes, vmem_limit_bytes=VMEM_LIMIT)


def _sigmoid(v):
    return 0.5 * jnp.tanh(0.5 * v) + 0.5


def _silu(v):
    return v * _sigmoid(v)


def _norm_mod(x, g, shift, scale):
    ms = jnp.mean(x * x, axis=-1, keepdims=True)
    return x * lax.rsqrt(ms + EPS) * g * (1.0 + scale) + shift


def _mod_kernel(c_ref, w_ref, b_ref, o_ref):
    s = _silu(c_ref[...])
    o_ref[0] = jnp.dot(s, w_ref[0], preferred_element_type=F32, precision=lax.Precision.HIGHEST) + b_ref[0]


def modulation_all(c_rows, w_mod, b_mod):
    n_layers, d, n = w_mod.shape
    tn = 1536
    return pl.pallas_call(
        _mod_kernel,
        grid=(n_layers, n // tn),
        in_specs=[
            pl.BlockSpec((8, d), lambda l, j: (0, 0)),
            pl.BlockSpec((1, d, tn), lambda l, j: (l, 0, j)),
            pl.BlockSpec((1, 1, tn), lambda l, j: (l, 0, j)),
        ],
        out_specs=pl.BlockSpec((1, 8, tn), lambda l, j: (l, 0, j)),
        out_shape=jax.ShapeDtypeStruct((n_layers, 8, n), F32),
        compiler_params=_cparams(2),
        name="modulation",
    )(c_rows, w_mod, b_mod.reshape(n_layers, 1, n))


def _inproj_kernel(*refs, chunks, use_rope):
    if use_rope:
        x_ref, mod_ref, g_ref, w_ref, gq_ref, gk_ref, ones_ref, cos_ref, sin_ref, o_ref, qo_ref, kt_ref, v_ref = refs
        cos, sin = cos_ref[...], sin_ref[...]
    else:
        x_ref, mod_ref, g_ref, w_ref, gq_ref, gk_ref, ones_ref, o_ref, qo_ref, kt_ref, v_ref = refs
        cos = sin = None
    h = _norm_mod(x_ref[...], g_ref[...], mod_ref[0, 0:1, :], mod_ref[0, 1:2, :]).astype(BF16)
    for c0, cw in chunks:
        r = jnp.dot(h, w_ref[:, c0:c0 + cw], preferred_element_type=F32)
        if c0 + cw <= P_OFF_Q:
            r = _sigmoid(r)
        elif c0 == P_OFF_CP:
            sg = _sigmoid(r[:, CONV_W:2 * CONV_W])
            r = jnp.concatenate([r[:, 0:CONV_W] * sg, sg], axis=1)
        elif c0 == P_OFF_Q:
            _q_epilogue(r, gq_ref[...], ones_ref[...], cos, sin, qo_ref)
        elif c0 == P_OFF_KV:
            _kv_epilogue(r, gk_ref[...], ones_ref[...], cos, sin, kt_ref, v_ref)
        o_ref[:, c0:c0 + cw] = r.astype(o_ref.dtype)


def input_projection(x2d, mod, g, w_bf16, gq, gk, rope, *, batch, seq, tm):
    m, d = x2d.shape
    n = w_bf16.shape[1]
    tps = seq // tm
    chunks = tuple((c0, min(512, n - c0)) for c0 in range(0, n, 512))
    assert P_OFF_Q % 512 == 0 and {(P_OFF_CP, 2 * CONV_W), (P_OFF_Q, Q_W), (P_OFF_KV, 2 * KV_W)} <= set(chunks)
    use_rope = rope is not None
    n_mod = mod.shape[0]
    mod_idx = (lambda i: (i // tps, 0, 0)) if n_mod > 1 else (lambda i: (0, 0, 0))
    const = lambda i: (0, 0)
    ones_bd = jnp.asarray(np.kron(np.eye(2, dtype=np.float32), np.ones((64, 64), np.float32)), BF16)
    in_specs = [
        pl.BlockSpec((tm, d), lambda i: (i, 0)),
        pl.BlockSpec((1, 6, d), mod_idx),
        pl.BlockSpec((1, d), const),
        pl.BlockSpec((d, n), const, pipeline_mode=pl.Buffered(1)),
        pl.BlockSpec((1, 128), const),
        pl.BlockSpec((1, 128), const),
        pl.BlockSpec((128, 128), const),
    ]
    args = [x2d, mod, g.reshape(1, d), w_bf16, jnp.tile(gq, 2).reshape(1, 128), jnp.tile(gk, 2).reshape(1, 128), ones_bd]
    if use_rope:
        in_specs += [pl.BlockSpec((tm, 128), lambda i: (i % tps, 0))] * 2
        args += list(rope)
    return pl.pallas_call(
        functools.partial(_inproj_kernel, chunks=chunks, use_rope=use_rope),
        grid=(m // tm,),
        in_specs=in_specs,
        out_specs=[
            pl.BlockSpec((tm, n), lambda i: (i, 0)),
            pl.BlockSpec((1, N_KV_HEADS, 256, tm), lambda i: (i // tps, 0, 0, i % tps)),
            pl.BlockSpec((1, N_KV_HEADS, tm, 256), lambda i: (i // tps, 0, i % tps, 0)),
            pl.BlockSpec((1, N_KV_HEADS, 128, tm), lambda i: (i // tps, 0, 0, i % tps)),
        ],
        out_shape=[
            jax.ShapeDtypeStruct((m, n), BF16),
            jax.ShapeDtypeStruct((batch, N_KV_HEADS, 256, seq), BF16),
            jax.ShapeDtypeStruct((batch, N_KV_HEADS, seq, 256), BF16),
            jax.ShapeDtypeStruct((batch, N_KV_HEADS, 128, seq), BF16),
        ],
        compiler_params=_cparams(1),
        name="input_projection",
    )(*args)


def _seg_sum64(v, ones_bd):
    hi = v.astype(BF16)
    lo = (v - hi.astype(F32)).astype(BF16)
    return (jnp.dot(hi, ones_bd, preferred_element_type=F32) + jnp.dot(lo, ones_bd, preferred_element_type=F32))


def _head_norm_rope(x, g, ones_bd, cos, sin, low_mask):
    y = x * lax.rsqrt(_seg_sum64(x * x, ones_bd) * (1.0 / HEAD_DIM) + EPS) * g
    if cos is None:
        return y
    partner = jnp.where(low_mask, pltpu.roll(y, 128 - 16, axis=1), pltpu.roll(y, 16, axis=1))
    return y * cos + partner * sin


def _rope_low_mask(t):
    return (lax.broadcasted_iota(jnp.int32, (t, 128), 1) % 32) < 16


def _q_epilogue(rq, gq, ones_bd, cos, sin, qo_ref):
    low_mask = _rope_low_mask(rq.shape[0])
    for c in range(Q_W // 128):
        yq = _head_norm_rope(rq[:, 128 * c:128 * (c + 1)], gq, ones_bd, cos, sin, low_mask) * Q_SCALE
        qo_ref[0, c // 2, 128 * (c % 2):128 * (c % 2 + 1), :] = yq.T.astype(BF16)


def _kv_epilogue(rkv, gk, ones_bd, cos, sin, kt_ref, v_ref):
    t = rkv.shape[0]
    yk = _head_norm_rope(rkv[:, 0:128], gk, ones_bd, cos, sin, _rope_low_mask(t))
    ykr = pltpu.roll(yk, 64, axis=1)
    first = lax.broadcasted_iota(jnp.int32, (t, 128), 1) < 64
    k0 = jnp.where(first, yk, ykr).astype(BF16)
    k1 = jnp.where(first, ykr, yk).astype(BF16)
    kt_ref[0, 0] = jnp.concatenate([k0, k0], axis=1)
    kt_ref[0, 1] = jnp.concatenate([k1, k1], axis=1)
    vt = rkv[:, 128:256].T
    ones = jnp.ones((HEAD_DIM, t), F32)
    for h in range(N_KV_HEADS):
        v_ref[0, h] = jnp.concatenate([vt[64 * h:64 * (h + 1), :], ones], axis=0).astype(BF16)


def rope_tables(seq):
    n_freq = HEAD_DIM // 4
    freqs = ROPE_THETA ** (-jnp.arange(n_freq, dtype=F32) / n_freq)
    t = jnp.arange(seq)
    row = (t // GRID_W).astype(F32)
    col = (t % GRID_W).astype(F32)
    ang_r = row[:, None] * freqs
    ang_c = col[:, None] * freqs
    cos = jnp.concatenate([jnp.cos(ang_r)] * 2 + [jnp.cos(ang_c)] * 2, axis=1)
    sin = jnp.concatenate([-jnp.sin(ang_r), jnp.sin(ang_r), -jnp.sin(ang_c), jnp.sin(ang_c)], axis=1)
    return jnp.tile(cos, (1, 2)), jnp.tile(sin, (1, 2))


def _attn_kernel(*refs, tq, tk, nk, tail):
    refs = list(refs)
    qt_ref = refs.pop(0)
    k_ref, vt_ref = (refs.pop(0), refs.pop(0)) if nk else (None, None)
    kc_ref, vtc_ref = (refs.pop(0), refs.pop(0)) if tail else (None, None)
    o_ref, qs_ref, s0, s1, p0, p1, a0, a1, mx0, mx1, m_ref, acc_ref = refs
    s_bufs, p_bufs, a_bufs, mx_bufs = (s0, s1), (p0, p1), (a0, a1), (mx0, mx1)
    n_blocks = nk + (1 if tail else 0)

    _attn_stack_queries(qt_ref, qs_ref, tq)
    m_ref[...] = jnp.full(m_ref.shape, -jnp.inf, F32)
    acc_ref[...] = jnp.zeros(acc_ref.shape, F32)

    def block(t):
        if isinstance(t, int) and t >= nk:
            return kc_ref[0, 0], vtc_ref[0, 0], tail
        off = t * tk if isinstance(t, int) else pl.multiple_of(t * tk, tk)
        return k_ref[0, 0, pl.ds(off, tk), :], vt_ref[0, 0, :, pl.ds(off, tk)], tk

    def scores(t, slot):
        k_rows, _, n = block(t)
        s = jnp.dot(k_rows, qs_ref[...], preferred_element_type=F32)
        s_bufs[slot][0:n, :] = s
        mx_bufs[slot][...] = jnp.max(s, axis=0, keepdims=True)

    def numerators(n, slot):
        s_ref, p_ref, a_ref = s_bufs[slot], p_bufs[slot], a_bufs[slot]
        for c0 in range(0, Q_PER_KV * tq, 128):
            cols = slice(c0, c0 + 128)
            m_old = m_ref[:, cols]
            m_new = jnp.maximum(m_old, mx_bufs[slot][:, cols])
            a_ref[:, cols] = jnp.exp2(m_old - m_new)
            p_ref[0:n, cols] = jnp.exp2(s_ref[0:n, cols] - m_new).astype(BF16)
            m_ref[:, cols] = m_new

    def weighted_sum(t, slot):
        _, vt, n = block(t)
        pv = jnp.dot(vt, p_bufs[slot][0:n, :], preferred_element_type=F32)
        acc_ref[...] = a_bufs[slot][...] * acc_ref[...] + pv

    def rows_of(t):
        return tk if t < nk else tail

    def step(t, slot, n_mid):
        scores(t, slot)
        numerators(n_mid, 1 - slot)
        weighted_sum(t - 2, slot)

    scores(0, 0)
    if n_blocks > 1:
        scores(1, 1)
        numerators(rows_of(0), 0)
        n_pairs = max(nk - 2, 0) // 2

        def pair(i, carry):
            t = 2 + 2 * i
            step(t, 0, tk)
            step(t + 1, 1, tk)
            return carry

        if n_pairs:
            lax.fori_loop(0, n_pairs, pair, 0)
        for t in range(2 + 2 * n_pairs, n_blocks):
            step(t, t % 2, rows_of(t - 1))
        last = n_blocks - 1
        numerators(rows_of(last), last % 2)
        weighted_sum(last - 1, (last - 1) % 2)
        weighted_sum(last, last % 2)
    else:
        numerators(rows_of(0), 0)
        weighted_sum(0, 0)

    _attn_write_output(acc_ref, o_ref, tq)


def _attn_write_output(acc_ref, o_ref, tq):
    acc = acc_ref[...]
    ot = acc[0:HEAD_DIM, :] / acc[HEAD_DIM:2 * HEAD_DIM, :]
    for half in range(2):
        pair_t = jnp.concatenate([ot[:, (2 * half) * tq:(2 * half + 1) * tq],
                                  ot[:, (2 * half + 1) * tq:(2 * half + 2) * tq]], axis=0)
        o_ref[:, 128 * half:128 * (half + 1)] = pair_t.T.astype(o_ref.dtype)


def _attn_stack_queries(qt_ref, qs_ref, tq):
    row_group = lax.broadcasted_iota(jnp.int32, (256, tq), 0) // HEAD_DIM
    qt = qt_ref[0, 0]
    for g in range(Q_PER_KV):
        qs_ref[:, g * tq:(g + 1) * tq] = jnp.where(row_group == g, qt, jnp.zeros_like(qt))


def _attn_stale_max_kernel(*refs, tq, tk, nk, n_side):
    qt_ref, k_ref, vt_ref, kc_ref, vtc_ref = refs[:5]
    side_in = refs[5:5 + n_side]
    o_ref = refs[5 + n_side]
    side_out = refs[6 + n_side:6 + 2 * n_side]
    qs_ref, p0, p1, f0, f1, m_ref, acc_ref = refs[6 + 2 * n_side:]
    for src, dst in zip(side_in, side_out):
        dst[...] = src[...].astype(dst.dtype)
    p_bufs, f_bufs = (p0, p1), (f0, f1)
    _attn_stack_queries(qt_ref, qs_ref, tq)

    s = jnp.dot(kc_ref[0, 0], qs_ref[...], preferred_element_type=F32)
    m0 = jnp.max(s, axis=0, keepdims=True)
    m_ref[...] = m0
    acc_ref[...] = jnp.dot(vtc_ref[0, 0], jnp.exp2(s - m0).astype(BF16), preferred_element_type=F32)

    def numerators(t, slot):
        off = t * tk if isinstance(t, int) else pl.multiple_of(t * tk, tk)
        s = jnp.dot(k_ref[0, 0, pl.ds(off, tk), :], qs_ref[...], preferred_element_type=F32)
        m_old = m_ref[...]
        p_bufs[slot][...] = jnp.exp2(s - m_old).astype(BF16)
        m_new = jnp.maximum(m_old, jnp.max(s, axis=0, keepdims=True))
        f_bufs[slot][...] = jnp.exp2(m_old - m_new)
        m_ref[...] = m_new

    def weighted_sum(t, slot):
        off = t * tk if isinstance(t, int) else pl.multiple_of(t * tk, tk)
        pv = jnp.dot(vt_ref[0, 0, :, pl.ds(off, tk)], p_bufs[slot][...], preferred_element_type=F32)
        acc_ref[...] = (acc_ref[...] + pv) * f_bufs[slot][...]

    def step(t, slot):
        numerators(t, slot)
        weighted_sum(t - 1, 1 - slot)

    numerators(0, 0)
    n_pairs = (nk - 1) // 2

    def pair(i, carry):
        t = 1 + 2 * i
        step(t, 1)
        step(t + 1, 0)
        return carry

    if n_pairs:
        lax.fori_loop(0, n_pairs, pair, 0)
    for t in range(1 + 2 * n_pairs, nk):
        step(t, t % 2)
    weighted_sum(nk - 1, (nk - 1) % 2)
    _attn_write_output(acc_ref, o_ref, tq)


def attention_stale_max(qt, k4, vt1, k4_tail, vt1_tail, side=(), *, batch, seq_q, tq, tk):
    nq = seq_q // tq
    lanes = Q_PER_KV * tq
    lk = k4.shape[2]
    tail = k4_tail.shape[2]
    n_steps = batch * N_KV_HEADS * nq
    assert all(n_steps % w.shape[0] == 0 for w in side)

    def side_spec(w):
        repeat = n_steps // w.shape[0]
        return pl.BlockSpec((1,) + w.shape[1:], lambda b, h, i: (((b * N_KV_HEADS + h) * nq + i) // repeat, 0, 0))

    side_specs = [side_spec(w) for w in side]
    outs = pl.pallas_call(
        functools.partial(_attn_stale_max_kernel, tq=tq, tk=tk, nk=lk // tk, n_side=len(side)),
        grid=(batch, N_KV_HEADS, nq),
        in_specs=[
            pl.BlockSpec((1, 1, 256, tq), lambda b, h, i: (b, h, 0, i)),
            pl.BlockSpec((1, 1, lk, 256), lambda b, h, i: (b, h, 0, 0)),
            pl.BlockSpec((1, 1, 128, lk), lambda b, h, i: (b, h, 0, 0)),
            pl.BlockSpec((1, 1, tail, 256), lambda b, h, i: (b, h, 0, 0)),
            pl.BlockSpec((1, 1, 128, tail), lambda b, h, i: (b, h, 0, 0)),
        ] + side_specs,
        out_specs=[pl.BlockSpec((tq, 256), lambda b, h, i: (b * nq + i, h))] + side_specs,
        out_shape=[jax.ShapeDtypeStruct((batch * seq_q, Q_W), BF16)]
        + [jax.ShapeDtypeStruct(w.shape, BF16) for w in side],
        scratch_shapes=[
            pltpu.VMEM((256, lanes), BF16),
            pltpu.VMEM((tk, lanes), BF16), pltpu.VMEM((tk, lanes), BF16),
            pltpu.VMEM((1, lanes), F32), pltpu.VMEM((1, lanes), F32),
            pltpu.VMEM((1, lanes), F32),
            pltpu.VMEM((2 * HEAD_DIM, lanes), F32),
        ],
        compiler_params=_cparams(3),
        name="attention_stale_max",
    )(qt, k4, vt1, k4_tail, vt1_tail, *side)
    return tuple(outs)


def attention(qt, k4, vt1, k4_tail, vt1_tail, *, batch, seq_q, tq, tk):
    nq = seq_q // tq
    lanes = Q_PER_KV * tq
    nk = 0 if k4 is None else k4.shape[2] // tk
    tail = 0 if k4_tail is None else k4_tail.shape[2]
    buf_rows = max(tk if nk else 0, tail)
    in_specs = [pl.BlockSpec((1, 1, 256, tq), lambda b, h, i: (b, h, 0, i))]
    args = [qt]
    if nk:
        lk = k4.shape[2]
        in_specs += [pl.BlockSpec((1, 1, lk, 256), lambda b, h, i: (b, h, 0, 0)),
                     pl.BlockSpec((1, 1, 128, lk), lambda b, h, i: (b, h, 0, 0))]
        args += [k4, vt1]
    if tail:
        in_specs += [pl.BlockSpec((1, 1, tail, 256), lambda b, h, i: (b, h, 0, 0)),
                     pl.BlockSpec((1, 1, 128, tail), lambda b, h, i: (b, h, 0, 0))]
        args += [k4_tail, vt1_tail]
    return pl.pallas_call(
        functools.partial(_attn_kernel, tq=tq, tk=tk, nk=nk, tail=tail),
        grid=(batch, N_KV_HEADS, nq),
        in_specs=in_specs,
        out_specs=pl.BlockSpec((tq, 256), lambda b, h, i: (b * nq + i, h)),
        out_shape=jax.ShapeDtypeStruct((batch * seq_q, Q_W), BF16),
        scratch_shapes=[
            pltpu.VMEM((256, lanes), BF16),
            pltpu.VMEM((buf_rows, lanes), F32), pltpu.VMEM((buf_rows, lanes), F32),
            pltpu.VMEM((buf_rows, lanes), BF16), pltpu.VMEM((buf_rows, lanes), BF16),
            pltpu.VMEM((1, lanes), F32), pltpu.VMEM((1, lanes), F32),
            pltpu.VMEM((1, lanes), F32), pltpu.VMEM((1, lanes), F32),
            pltpu.VMEM((1, lanes), F32),
            pltpu.VMEM((2 * HEAD_DIM, lanes), F32),
        ],
        compiler_params=_cparams(3),
        name="attention",
    )(*args)


def _dft_cs(n):
    k = np.arange(n)
    ang = 2.0 * np.pi * ((k[:, None] * k[None, :]) % n) / n
    return np.cos(ang), np.sin(ang)


def _fft1_kernel(x_ref, f_ref, c_ref, s_ref, o_ref, *, n1):
    y = jnp.dot(f_ref[...], x_ref[0], preferred_element_type=F32)
    yr, yi = y[:n1], y[n1:]
    c, s = c_ref[...], s_ref[...]
    o_ref[0, 0] = (yr * c + yi * s).astype(o_ref.dtype)
    o_ref[0, 1] = (yi * c - yr * s).astype(o_ref.dtype)


def _fft2_kernel(y_ref, f_ref, bc_ref, bs_ref, o_ref, *, n2, kb):
    for j in range(kb):
        y2 = jnp.concatenate([y_ref[0, 0, j], y_ref[0, 1, j]], axis=0)
        x2 = jnp.dot(f_ref[...], y2, preferred_element_type=F32)
        xr = x2[:n2].astype(BF16)
        xi = x2[n2:].astype(BF16)
        z = (jnp.dot(xr, bc_ref[...], preferred_element_type=F32) + jnp.dot(xi, bs_ref[...], preferred_element_type=F32))
        o_ref[0, j] = z.astype(o_ref.dtype)


def fourier_mix(u, *, batch, seq, n1, n2):
    cw = u.shape[1]
    lanes = n2 * cw
    tl = min(lanes, 4096)
    c1, s1 = _dft_cs(n1)
    f1 = jnp.asarray(np.concatenate([c1, -s1], axis=0), BF16)
    k1 = np.arange(n1)[:, None]
    t2 = np.arange(n2)[None, :]
    ang = 2.0 * np.pi * ((k1 * t2) % seq) / seq
    twc = jnp.asarray(np.repeat(np.cos(ang), cw, axis=1), F32)
    tws = jnp.asarray(np.repeat(np.sin(ang), cw, axis=1), F32)
    x2 = u.reshape(batch, n1, lanes)
    yp = pl.pallas_call(
        functools.partial(_fft1_kernel, n1=n1),
        grid=(batch, lanes // tl),
        in_specs=[
            pl.BlockSpec((1, n1, tl), lambda b, j: (b, 0, j)),
            pl.BlockSpec((2 * n1, n1), lambda b, j: (0, 0)),
            pl.BlockSpec((n1, tl), lambda b, j: (0, j)),
            pl.BlockSpec((n1, tl), lambda b, j: (0, j)),
        ],
        out_specs=pl.BlockSpec((1, 2, n1, tl), lambda b, j: (b, 0, 0, j)),
        out_shape=jax.ShapeDtypeStruct((batch, 2, n1, lanes), BF16),
        compiler_params=_cparams(2),
        name="fft_stage1",
    )(x2, f1, twc, tws)

    c2, s2 = _dft_cs(n2)
    f2 = jnp.asarray(np.block([[c2, s2], [-s2, c2]]), BF16)
    cg, sg = _dft_cs(FOURIER_GW)
    norm = 1.0 / math.sqrt(seq * FOURIER_GW)
    bdc = jnp.asarray(np.kron(np.eye(cw // FOURIER_GW), cg) * norm, BF16)
    bds = jnp.asarray(np.kron(np.eye(cw // FOURIER_GW), sg) * norm, BF16)
    kb = min(n1, 16)
    y5 = yp.reshape(batch, 2, n1, n2, cw)
    z = pl.pallas_call(
        functools.partial(_fft2_kernel, n2=n2, kb=kb),
        grid=(batch, n1 // kb),
        in_specs=[
            pl.BlockSpec((1, 2, kb, n2, cw), lambda b, j: (b, 0, j, 0, 0)),
            pl.BlockSpec((2 * n2, 2 * n2), lambda b, j: (0, 0)),
            pl.BlockSpec((cw, cw), lambda b, j: (0, 0)),
            pl.BlockSpec((cw, cw), lambda b, j: (0, 0)),
        ],
        out_specs=pl.BlockSpec((1, kb, n2, cw), lambda b, j: (b, j, 0, 0)),
        out_shape=jax.ShapeDtypeStruct((batch, n1, n2, cw), BF16),
        compiler_params=_cparams(2),
        name="fft_stage2",
    )(y5, f2, bdc, bds)
    return z.transpose(0, 2, 1, 3).reshape(batch * seq, cw)


def _merge_kernel(x_ref, mod_ref, gate_ref, cp_ref, cpp_ref, cpn_ref, yf_ref, at_ref,
                  wf_ref, wc_ref, wp_ref, wa_ref, wo_ref, dw_ref, cb_ref, cg_ref, pw_ref, ps_ref, band_ref, icnt_ref,
                  o_ref, ybuf, xbuf, ysh, cacc, *, t, tps):
    i = pl.program_id(0)
    pos_tile = i % tps
    keep_prev = jnp.where(pos_tile != 0, 1.0, 0.0).astype(F32)
    keep_next = jnp.where(pos_tile != tps - 1, 1.0, 0.0).astype(F32)

    def glu(blk):
        return blk[:, 0:CONV_W].astype(F32)

    cp, cpp, cpn = cp_ref[...], cpp_ref[...], cpn_ref[...]
    ybuf[0:HALO, :] = glu(cpp) * keep_prev
    ybuf[HALO:HALO + t, :] = glu(cp)
    ybuf[HALO + t:HALO + t + HALO, :] = glu(cpn) * keep_next
    xbuf[0:HALO, :] = cpp[:, 2 * CONV_W:] * keep_prev.astype(BF16)
    xbuf[HALO:HALO + t, :] = cp[:, 2 * CONV_W:]
    xbuf[HALO + t:HALO + t + HALO, :] = cpn[:, 2 * CONV_W:] * keep_next.astype(BF16)

    n_sh = t + 2 * HALO - 8
    for b in range(1, 8):
        ysh[b - 1, 0:n_sh, :] = ybuf[pl.ds(b, n_sh), :]
    for r0 in range(0, t, CONV_ROWS):
        part = jnp.zeros((CONV_ROWS, CONV_W), F32)
        for k in range(CONV_K):
            a, b = divmod(HALO - CONV_HALF + k, 8)
            src = ybuf if b == 0 else ysh.at[b - 1]
            part = part + dw_ref[k:k + 1, :] * src[8 * a + r0:8 * a + r0 + CONV_ROWS, :]
        cacc[r0:r0 + CONV_ROWS, :] = part + cb_ref[...]
    acc = cacc[...]
    ms = jnp.mean(acc * acc, axis=-1, keepdims=True)
    conv_out = _silu(acc * lax.rsqrt(ms + EPS) * cg_ref[...]).astype(BF16)

    pb = band_ref.shape[1]
    grp = lax.broadcasted_iota(jnp.int32, (pb + 2 * HALO, POOL_W), 1) // POOL_GW
    parts = []
    for r0 in range(0, t, pb):
        xw = xbuf[r0:r0 + pb + 2 * HALO, :]
        wsum = jnp.zeros((pb, POOL_W), F32)
        for gi in range(len(POOL_WINDOWS)):
            wsum = wsum + jnp.dot(band_ref[gi], jnp.where(grp == gi, xw, jnp.zeros_like(xw)),
                                  preferred_element_type=F32)
        x0 = xbuf[HALO + r0:HALO + r0 + pb, :].astype(F32)
        parts.append((wsum * icnt_ref[0, r0:r0 + pb, :] - x0).astype(BF16))
    pool_in = parts[0] if len(parts) == 1 else jnp.concatenate(parts, axis=0)
    pool_out = (jnp.dot(pool_in, pw_ref[...], preferred_element_type=F32) * ps_ref[...]).astype(BF16)

    def gate(b):
        return gate_ref[:, b * D_MODEL:(b + 1) * D_MODEL].astype(F32)

    merged = gate(0) * jnp.dot(yf_ref[...], wf_ref[...], preferred_element_type=F32)
    merged = merged + gate(1) * jnp.dot(conv_out, wc_ref[...], preferred_element_type=F32)
    merged = merged + gate(2) * jnp.dot(pool_out, wp_ref[...], preferred_element_type=F32)
    merged = merged + gate(3) * jnp.dot(at_ref[...], wa_ref[...], preferred_element_type=F32)
    out = jnp.dot(merged.astype(BF16), wo_ref[...], preferred_element_type=F32)
    o_ref[...] = x_ref[...] + mod_ref[0, 2:3, :] * out


def _pool_tables(seq, t):
    pb = min(t, 256)
    r = np.arange(pb)[:, None]
    j = np.arange(pb + 2 * HALO)[None, :]
    band = np.stack([((j >= r + HALO - w // 2) & (j < r + HALO + w // 2)) for w in POOL_WINDOWS]).astype(np.float32)
    half = np.repeat(np.array(POOL_WINDOWS) // 2, POOL_GW)[None, :]
    rows = np.arange(t)[:, None]
    icnt = []
    for first, last in ((0, 0), (1, 0), (0, 1), (1, 1)):
        if first and last:
            pos, length = rows, t
        elif first:
            pos, length = rows, 2 * t + 2 * HALO
        elif last:
            pos, length = rows + seq - t, seq
        else:
            pos, length = rows + t + 2 * HALO, 4 * t
        cnt = np.minimum(pos + half, length) - np.maximum(pos - half, 0)
        icnt.append(1.0 / cnt)
    return jnp.asarray(band, BF16), jnp.asarray(np.stack(icnt), F32)


def merge_branches(x2d, mod, proj, yf, attn, lw, *, seq, t):
    m, d = x2d.shape
    tps = seq // t
    hb = t // HALO
    n_halo = m // HALO
    n_mod = mod.shape[0]
    mod_idx = (lambda i: (i // tps, 0, 0)) if n_mod > 1 else (lambda i: (0, 0, 0))
    const = lambda i: (0, 0)
    cp_blk = P_OFF_CP // CP_W
    band, icnt = _pool_tables(seq, t)
    pb = band.shape[1]

    def icnt_idx(i):
        pos_tile = i % tps
        return ((pos_tile == 0).astype(jnp.int32) + 2 * (pos_tile == tps - 1).astype(jnp.int32), 0, 0)

    return pl.pallas_call(
        functools.partial(_merge_kernel, t=t, tps=tps),
        grid=(m // t,),
        in_specs=[
            pl.BlockSpec((t, d), lambda i: (i, 0)),
            pl.BlockSpec((1, 6, d), mod_idx),
            pl.BlockSpec((t, 4 * d), lambda i: (i, 0)),
            pl.BlockSpec((t, CP_W), lambda i: (i, cp_blk)),
            pl.BlockSpec((HALO, CP_W), lambda i: (jnp.maximum(i * hb - 1, 0), cp_blk)),
            pl.BlockSpec((HALO, CP_W), lambda i: (jnp.minimum((i + 1) * hb, n_halo - 1), cp_blk)),
            pl.BlockSpec((t, FOURIER_W), lambda i: (i, 0)),
            pl.BlockSpec((t, Q_W), lambda i: (i, 0)),
            pl.BlockSpec((FOURIER_W, d), const),
            pl.BlockSpec((CONV_W, d), const),
            pl.BlockSpec((POOL_W, d), const),
            pl.BlockSpec((Q_W, d), const),
            pl.BlockSpec((d, d), const),
            pl.BlockSpec((CONV_K, CONV_W), const),
            pl.BlockSpec((1, CONV_W), const),
            pl.BlockSpec((1, CONV_W), const),
            pl.BlockSpec((POOL_W, POOL_W), const),
            pl.BlockSpec((1, POOL_W), const),
            pl.BlockSpec((len(POOL_WINDOWS), pb, pb + 2 * HALO), lambda i: (0, 0, 0)),
            pl.BlockSpec((1, t, POOL_W), icnt_idx),
        ],
        out_specs=pl.BlockSpec((t, d), lambda i: (i, 0)),
        out_shape=jax.ShapeDtypeStruct((m, d), F32),
        scratch_shapes=[pltpu.VMEM((t + 2 * HALO, CONV_W), F32), pltpu.VMEM((t + 2 * HALO, POOL_W), BF16),
                        pltpu.VMEM((7, t + 2 * HALO, CONV_W), F32), pltpu.VMEM((t, CONV_W), F32)],
        compiler_params=_cparams(1),
        name="merge_branches",
    )(x2d, mod, proj, proj, proj, proj, yf, attn,
      lw["wf"], lw["wc"], lw["wp"], lw["wa"], lw["wo"], lw["dw"], lw["cb"], lw["cg"], lw["pw"], lw["ps"], band, icnt)


def _ffn_kernel(x_ref, mod_ref, g_ref, w1_ref, w3_ref, w2_ref, o_ref, *, chunks):
    x = x_ref[...]
    h = _norm_mod(x, g_ref[...], mod_ref[0, 3:4, :], mod_ref[0, 4:5, :]).astype(BF16)
    acc = jnp.zeros(x.shape, F32)
    for c0, cw in chunks:
        a = jnp.dot(h, w1_ref[:, c0:c0 + cw], preferred_element_type=F32)
        b = jnp.dot(h, w3_ref[:, c0:c0 + cw], preferred_element_type=F32)
        acc = acc + jnp.dot((_silu(a) * b).astype(BF16), w2_ref[c0:c0 + cw, :], preferred_element_type=F32)
    o_ref[...] = x + mod_ref[0, 5:6, :] * acc


def ffn_dense(x2d, mod, g, w1, w3, w2, *, tm, tiles_per_mod):
    m, d = x2d.shape
    dff = w1.shape[1]
    chunks = tuple((c0, min(1024, dff - c0)) for c0 in range(0, dff, 1024))
    n_mod = mod.shape[0]
    mod_idx = (lambda i: (i // tiles_per_mod, 0, 0)) if n_mod > 1 else (lambda i: (0, 0, 0))
    const = lambda i: (0, 0)
    return pl.pallas_call(
        functools.partial(_ffn_kernel, chunks=chunks),
        grid=(m // tm,),
        in_specs=[
            pl.BlockSpec((tm, d), lambda i: (i, 0)),
            pl.BlockSpec((1, 6, d), mod_idx),
            pl.BlockSpec((1, d), const),
            pl.BlockSpec((d, dff), const, pipeline_mode=pl.Buffered(1)),
            pl.BlockSpec((d, dff), const, pipeline_mode=pl.Buffered(1)),
            pl.BlockSpec((dff, d), const, pipeline_mode=pl.Buffered(1)),
        ],
        out_specs=pl.BlockSpec((tm, d), lambda i: (i, 0)),
        out_shape=jax.ShapeDtypeStruct((m, d), F32),
        compiler_params=_cparams(1),
        name="ffn_dense",
    )(x2d, mod, g.reshape(1, d), w1, w3, w2)


def _top2(logits):
    t = logits.shape[0]
    lane = lax.broadcasted_iota(jnp.int32, (t, 128), 1).astype(F32)
    neg = jnp.float32(-jnp.inf)
    lg = jnp.where(lane < N_EXPERTS, logits, neg)
    v1 = jnp.max(lg, axis=-1, keepdims=True)
    i1 = jnp.min(jnp.where(lg == v1, lane, 128.0), axis=-1, keepdims=True)
    lg2 = jnp.where(lane == i1, neg, lg)
    v2 = jnp.max(lg2, axis=-1, keepdims=True)
    i2 = jnp.min(jnp.where(lg2 == v2, lane, 128.0), axis=-1, keepdims=True)
    e2 = jnp.exp(v2 - v1)
    return i1, i2, 1.0 / (1.0 + e2), e2 / (1.0 + e2)


R_E1, R_E2, R_W1, R_W2, R_RANK1, R_RANK2 = range(6)


def _route_kernel(x_ref, mod_ref, g_ref, r_ref, tri_ref, route_ref, route_t_ref, cnt_ref, carry_ref):
    @pl.when(pl.program_id(0) == 0)
    def _():
        carry_ref[...] = jnp.zeros(carry_ref.shape, F32)

    t = x_ref.shape[0]
    h = _norm_mod(x_ref[...], g_ref[...], mod_ref[0, 3:4, :], mod_ref[0, 4:5, :])
    r = r_ref[...]
    h_hi, r_hi = h.astype(BF16), r.astype(BF16)
    h_lo, r_lo = (h - h_hi.astype(F32)).astype(BF16), (r - r_hi.astype(F32)).astype(BF16)
    logits = (jnp.dot(h_hi, r_hi, preferred_element_type=F32) + jnp.dot(h_hi, r_lo, preferred_element_type=F32)
              + jnp.dot(h_lo, r_hi, preferred_element_type=F32))
    i1, i2, w1, w2 = _top2(logits)
    lane = lax.broadcasted_iota(jnp.int32, (t, 128), 1).astype(F32)
    oh1 = jnp.where(lane == i1, 1.0, 0.0)
    oh2 = jnp.where(lane == i2, 1.0, 0.0)
    both = oh1 + oh2
    before = carry_ref[...] + jnp.dot(tri_ref[...], both.astype(BF16), preferred_element_type=F32)
    rank1 = jnp.sum(oh1 * before, axis=-1, keepdims=True)
    rank2 = jnp.sum(oh2 * before, axis=-1, keepdims=True)
    carry_ref[...] += jnp.sum(both, axis=0, keepdims=True)
    rec = jnp.zeros((t, 128), F32)
    for col, val in ((R_E1, i1), (R_E2, i2), (R_W1, w1), (R_W2, w2), (R_RANK1, rank1), (R_RANK2, rank2)):
        rec = jnp.where(lane == col, val, rec)
    route_ref[...] = rec
    route_t_ref[0] = rec.T[0:8, :]
    cnt_ref[...] = carry_ref[...]


def moe_route(x2d, mod, g, router_pad, *, tm, tiles_per_mod):
    m, d = x2d.shape
    n_mod = mod.shape[0]
    mod_idx = (lambda i: (i // tiles_per_mod, 0, 0)) if n_mod > 1 else (lambda i: (0, 0, 0))
    tri = jnp.asarray(np.tril(np.ones((tm, tm), np.float32), -1), BF16)
    return pl.pallas_call(
        _route_kernel,
        grid=(m // tm,),
        in_specs=[
            pl.BlockSpec((tm, d), lambda i: (i, 0)),
            pl.BlockSpec((1, 6, d), mod_idx),
            pl.BlockSpec((1, d), lambda i: (0, 0)),
            pl.BlockSpec((d, 128), lambda i: (0, 0)),
            pl.BlockSpec((tm, tm), lambda i: (0, 0)),
        ],
        out_specs=[pl.BlockSpec((tm, 128), lambda i: (i, 0)), pl.BlockSpec((1, 8, tm), lambda i: (i, 0, 0)),
                   pl.BlockSpec((1, 128), lambda i: (0, 0))],
        out_shape=[jax.ShapeDtypeStruct((m, 128), F32), jax.ShapeDtypeStruct((m // tm, 8, tm), F32),
                   jax.ShapeDtypeStruct((1, 128), F32)],
        scratch_shapes=[pltpu.VMEM((1, 128), F32)],
        compiler_params=_cparams(1),
        name="moe_route",
    )(x2d, mod, g.reshape(1, d), router_pad, tri)


def _dispatch_kernel(pos_ref, pad_ref, x_ref, mod_ref, g_ref, xs_ref, h_ref, zrow_ref, sem, zsem):
    i = pl.program_id(0)
    t = x_ref.shape[0]
    slot = i % 2
    h_ref[slot] = _norm_mod(x_ref[...], g_ref[...], mod_ref[0, 3:4, :], mod_ref[0, 4:5, :])

    def row_copy(r, dst_row):
        return pltpu.make_async_copy(h_ref.at[slot, pl.ds(r, 1), :], xs_ref.at[pl.ds(dst_row, 1), :], sem.at[slot])

    def issue(r, carry):
        row_copy(r, pos_ref[0, 0, r]).start()
        row_copy(r, pos_ref[0, 0, t + r]).start()
        return carry

    lax.fori_loop(0, t, issue, 0, unroll=DMA_ISSUE_UNROLL)

    def drain(s):
        for _ in range(2):
            pltpu.make_async_copy(h_ref.at[s], xs_ref.at[pl.ds(0, t), :], sem.at[s]).wait()

    @pl.when(i > 0)
    def _():
        drain(1 - slot)

    @pl.when(i == pl.num_programs(0) - 1)
    def _():
        drain(slot)
        zrow_ref[...] = jnp.zeros(zrow_ref.shape, F32)

        def zero_copy(r):
            return pltpu.make_async_copy(zrow_ref.at[pl.ds(0, 1), :], xs_ref.at[pl.ds(r, 1), :], zsem)

        def start_one(r, carry):
            zero_copy(r).start()
            return carry

        def wait_one(r, carry):
            zero_copy(r).wait()
            return carry

        for e in range(N_EXPERTS):
            lax.fori_loop(pad_ref[0, e], pad_ref[0, N_EXPERTS + e], start_one, 0)
        for e in range(N_EXPERTS):
            lax.fori_loop(pad_ref[0, e], pad_ref[0, N_EXPERTS + e], wait_one, 0)

        h_ref[1 - slot] = jnp.zeros((t, h_ref.shape[2]), F32)

        def tile_copy(k):
            return pltpu.make_async_copy(h_ref.at[1 - slot], xs_ref.at[pl.ds(pl.multiple_of(k * t, t), t), :], zsem)

        def start_tile(k, carry):
            tile_copy(k).start()
            return carry

        def wait_tile(k, carry):
            tile_copy(k).wait()
            return carry

        first_unused, n_tiles = pad_ref[0, 2 * N_EXPERTS], xs_ref.shape[0] // t
        lax.fori_loop(first_unused, n_tiles, start_tile, 0)
        lax.fori_loop(first_unused, n_tiles, wait_tile, 0)


def moe_dispatch(x2d, mod, g, pos_tiles, pad_rows, n_rows, *, tm, tiles_per_mod):
    m, d = x2d.shape
    n_mod = mod.shape[0]
    mod_idx = (lambda i: (i // tiles_per_mod, 0, 0)) if n_mod > 1 else (lambda i: (0, 0, 0))
    return pl.pallas_call(
        _dispatch_kernel,
        grid=(m // tm,),
        in_specs=[
            pl.BlockSpec((1, 1, 2 * tm), lambda i: (i, 0, 0), memory_space=pltpu.SMEM),
            pl.BlockSpec((1, 2 * N_EXPERTS + 1), lambda i: (0, 0), memory_space=pltpu.SMEM),
            pl.BlockSpec((tm, d), lambda i: (i, 0)),
            pl.BlockSpec((1, 6, d), mod_idx),
            pl.BlockSpec((1, d), lambda i: (0, 0)),
        ],
        out_specs=pl.BlockSpec(memory_space=pl.ANY),
        out_shape=jax.ShapeDtypeStruct((n_rows, d), F32),
        scratch_shapes=[pltpu.VMEM((2, tm, d), F32), pltpu.VMEM((8, d), F32),
                        pltpu.SemaphoreType.DMA((2,)), pltpu.SemaphoreType.DMA(())],
        compiler_params=_cparams(1),
        name="moe_dispatch",
    )(pos_tiles, pad_rows, x2d, mod, g.reshape(1, d))


def _experts_kernel(te_ref, nv_ref, xs_ref, w1_ref, w3_ref, w2_ref, ys_ref, xb_ref, acc_ref):
    i = pl.program_id(0)
    j = pl.program_id(1)
    valid = i < nv_ref[0]

    @pl.when(jnp.logical_and(valid, j == 0))
    def _():
        xb_ref[...] = xs_ref[...].astype(BF16)
        acc_ref[...] = jnp.zeros(acc_ref.shape, F32)

    @pl.when(valid)
    def _():
        h = xb_ref[...]
        a = jnp.dot(h, w1_ref[0], preferred_element_type=F32)
        b = jnp.dot(h, w3_ref[0], preferred_element_type=F32)
        acc_ref[...] += jnp.dot((_silu(a) * b).astype(BF16), w2_ref[0], preferred_element_type=F32)

    @pl.when(jnp.logical_and(valid, j == pl.num_programs(1) - 1))
    def _():
        ys_ref[...] = acc_ref[...]

    @pl.when(jnp.logical_and(jnp.logical_not(valid), j == pl.num_programs(1) - 1))
    def _():
        ys_ref[...] = jnp.zeros(ys_ref.shape, F32)


def moe_experts_grouped(xs, tile_expert, n_valid, w1, w3, w2, *, tm, tf):
    n_rows, d = xs.shape
    dff = w1.shape[2]
    nf = dff // tf

    def w13_idx(i, j, te, nv):
        return (te[i], 0, jnp.where(i < nv[0], j, nf - 1))

    def w2_idx(i, j, te, nv):
        return (te[i], jnp.where(i < nv[0], j, nf - 1), 0)

    grid_spec = pltpu.PrefetchScalarGridSpec(
        num_scalar_prefetch=2,
        grid=(n_rows // tm, nf),
        in_specs=[
            pl.BlockSpec((tm, d), lambda i, j, te, nv: (jnp.minimum(i, nv[0] - 1), 0)),
            pl.BlockSpec((1, d, tf), w13_idx),
            pl.BlockSpec((1, d, tf), w13_idx),
            pl.BlockSpec((1, tf, d), w2_idx),
        ],
        out_specs=pl.BlockSpec((tm, d), lambda i, j, te, nv: (i, 0)),
        scratch_shapes=[pltpu.VMEM((tm, d), BF16), pltpu.VMEM((tm, d), F32)],
    )
    return pl.pallas_call(
        _experts_kernel,
        grid_spec=grid_spec,
        out_shape=jax.ShapeDtypeStruct((n_rows, d), F32),
        compiler_params=_cparams(2),
        name="moe_experts_grouped",
    )(tile_expert, n_valid, xs, w1, w3, w2)


def _combine_kernel(pos_ref, pos_next_ref, x_ref, mod_ref, rt_ref, ys_ref, o_ref, y_ref, sem):
    i = pl.program_id(0)
    t = x_ref.shape[0]
    slot = i % 2

    def issue_tile(p_ref, s):
        def issue(r, carry):
            for k in range(2):
                pltpu.make_async_copy(ys_ref.at[pl.ds(p_ref[0, 0, k * t + r], 1), :],
                                      y_ref.at[s, k, pl.ds(r, 1), :], sem.at[s]).start()
            return carry

        lax.fori_loop(0, t, issue, 0, unroll=DMA_ISSUE_UNROLL)

    @pl.when(i == 0)
    def _():
        issue_tile(pos_ref, 0)

    @pl.when(i + 1 < pl.num_programs(0))
    def _():
        issue_tile(pos_next_ref, 1 - slot)

    for k in range(2):
        pltpu.make_async_copy(ys_ref.at[pl.ds(0, t), :], y_ref.at[slot, k], sem.at[slot]).wait()
    rt = rt_ref[...]
    mix = rt[:, R_W1:R_W1 + 1] * y_ref[slot, 0] + rt[:, R_W2:R_W2 + 1] * y_ref[slot, 1]
    o_ref[...] = x_ref[...] + mod_ref[0, 5:6, :] * mix


def moe_combine(x2d, mod, route, pos_tiles, ys, *, tm, tiles_per_mod):
    m, d = x2d.shape
    n_mod = mod.shape[0]
    mod_idx = (lambda i: (i // tiles_per_mod, 0, 0)) if n_mod > 1 else (lambda i: (0, 0, 0))
    n_tiles = m // tm
    return pl.pallas_call(
        _combine_kernel,
        grid=(n_tiles,),
        in_specs=[
            pl.BlockSpec((1, 1, 2 * tm), lambda i: (i, 0, 0), memory_space=pltpu.SMEM),
            pl.BlockSpec((1, 1, 2 * tm), lambda i: (jnp.minimum(i + 1, n_tiles - 1), 0, 0), memory_space=pltpu.SMEM),
            pl.BlockSpec((tm, d), lambda i: (i, 0)),
            pl.BlockSpec((1, 6, d), mod_idx),
            pl.BlockSpec((tm, 128), lambda i: (i, 0)),
            pl.BlockSpec(memory_space=pl.ANY),
        ],
        out_specs=pl.BlockSpec((tm, d), lambda i: (i, 0)),
        out_shape=jax.ShapeDtypeStruct((m, d), F32),
        scratch_shapes=[pltpu.VMEM((2, 2, tm, d), F32), pltpu.SemaphoreType.DMA((2,))],
        compiler_params=_cparams(1),
        name="moe_combine",
    )(pos_tiles, pos_tiles, x2d, mod, route, ys)


def _pos_tiles(pos1, pos2, tm):
    n = pos1.shape[0] // tm
    return jnp.concatenate([pos1.reshape(n, 1, tm), pos2.reshape(n, 1, tm)], axis=2)


def moe_sparse(x2d, mod, g, router_pad, w1, w3, w2, *, rows_per_mod):
    m, d = x2d.shape
    tr, td, tc, te = MOE_ROUTE_TM, MOE_DISPATCH_TM, MOE_COMBINE_TM, MOE_EXPERT_TM
    tiles_per_seq_row = rows_per_mod
    route, route_t, cnt = moe_route(x2d, mod, g, router_pad, tm=tr, tiles_per_mod=tiles_per_seq_row // tr)
    counts = cnt[0, 0:N_EXPERTS].astype(jnp.int32)
    group = ((counts + te - 1) // te) * te
    ends = jnp.cumsum(group)
    starts = ends - group
    field = lambda f: route_t[:, f, :].reshape(m).astype(jnp.int32)
    pos1 = starts[field(R_E1)] + field(R_RANK1)
    pos2 = starts[field(R_E2)] + field(R_RANK2)
    n_rows = 2 * m + N_EXPERTS * te
    n_tiles = n_rows // te
    tile_start = jnp.arange(n_tiles, dtype=jnp.int32)[:, None] * te
    tile_expert = jnp.minimum(jnp.sum((tile_start >= ends[None, :]).astype(jnp.int32), axis=1), N_EXPERTS - 1)
    n_valid = (ends[-1:] // te).astype(jnp.int32)
    assert td == te
    pad_rows = jnp.concatenate([starts + counts, ends, n_valid]).astype(jnp.int32).reshape(1, 2 * N_EXPERTS + 1)
    xs = moe_dispatch(x2d, mod, g, _pos_tiles(pos1, pos2, td), pad_rows, n_rows, tm=td,
                      tiles_per_mod=tiles_per_seq_row // td)
    ys = moe_experts_grouped(xs, tile_expert, n_valid, w1, w3, w2, tm=te, tf=MOE_EXPERT_TF)
    return moe_combine(x2d, mod, route, _pos_tiles(pos1, pos2, tc), ys, tm=tc, tiles_per_mod=tiles_per_seq_row // tc)


def _permute_w_in(w):
    f, c, p, q, kv, gts = w[:, 0:256], w[:, 256:768], w[:, 768:1024], w[:, 1024:1536], w[:, 1536:1792], w[:, 1792:]
    return jnp.concatenate([gts, q, c, p, f, kv], axis=1).astype(BF16)


def _layer_weights(layer, w_br_fourier, conv_dw, conv_b, conv_norm_g, w_br_conv, pool_w, pool_scale, w_br_pool,
                   w_br_attn, w_out):
    pw = jax.scipy.linalg.block_diag(*[pool_w[layer, i] for i in range(len(POOL_WINDOWS))])
    return {
        "wf": w_br_fourier[layer].astype(BF16), "wc": w_br_conv[layer].astype(BF16),
        "wp": w_br_pool[layer].astype(BF16), "wa": w_br_attn[layer].astype(BF16), "wo": w_out[layer].astype(BF16),
        "dw": conv_dw[layer], "cb": conv_b[layer].reshape(1, CONV_W), "cg": conv_norm_g[layer].reshape(1, CONV_W),
        "pw": pw.astype(BF16), "ps": pool_scale[layer].reshape(1, POOL_W),
    }


def _cast_slices(w, max_slices):
    w2 = w.reshape(-1, w.shape[-1])
    rows = w2.shape[0]
    n = 1
    while 2 * n <= max_slices and rows % (32 * n) == 0:
        n *= 2
    return w2.reshape(n, rows // n, w2.shape[1])


def kernel(x, c, ctx, c_ctx, w_mod, b_mod, norm1_g, norm2_g, w_in, w_br_fourier, conv_dw, conv_b, conv_norm_g,
           w_br_conv, pool_w, pool_scale, w_br_pool, q_norm_g, k_norm_g, w_br_attn, w_out, ffn_w1, ffn_w3, ffn_w2,
           moe_router, moe_w1, moe_w3, moe_w2):
    batch, seq, d = x.shape
    ctx_len = ctx.shape[1]
    depth = w_in.shape[0]
    rope = rope_tables(seq)

    c_rows = jnp.zeros((8, d), F32).at[0:batch].set(c).at[batch].set(c_ctx)
    mods = modulation_all(c_rows, w_mod, b_mod).reshape(depth, 8, 6, d)

    xl = x.reshape(batch * seq, d)
    xc = ctx.reshape(batch * ctx_len, d)
    for layer in range(depth):
        is_last = layer == depth - 1
        mod_l = mods[layer, 0:batch]
        mod_c = mods[layer, batch:batch + 1]
        w_in_l = _permute_w_in(w_in[layer])
        lw = _layer_weights(layer, w_br_fourier, conv_dw, conv_b, conv_norm_g, w_br_conv, pool_w, pool_scale,
                            w_br_pool, w_br_attn, w_out)

        proj_c, qc, ktc, vc = input_projection(xc, mod_c, norm1_g[layer], w_in_l, q_norm_g[layer], k_norm_g[layer],
                                               None, batch=batch, seq=ctx_len, tm=256)

        proj, q, kt, v = input_projection(xl, mod_l, norm1_g[layer], w_in_l, q_norm_g[layer], k_norm_g[layer], rope,
                                          batch=batch, seq=seq, tm=512)
        spread = (2.0 * 1.02 * HEAD_DIM * Q_SCALE) * jnp.max(jnp.abs(q_norm_g[layer])) * jnp.max(
            jnp.abs(k_norm_g[layer]))
        n_steps = batch * N_KV_HEADS * (seq // ATTN_TQ)
        mixer_w = ((ffn_w1, ffn_w3, ffn_w2) if layer % 2 == 0 else (moe_w1, moe_w3, moe_w2))
        mixer_w = tuple(w[layer // 2] for w in mixer_w)
        side = tuple(_cast_slices(w, n_steps) for w in mixer_w)
        attn, *side_bf16 = lax.cond(
            spread < STALE_MAX_EXP_LIMIT,
            lambda ops, sd: attention_stale_max(*ops, sd, batch=batch, seq_q=seq, tq=ATTN_TQ, tk=ATTN_TK),
            lambda ops, sd: (attention(*ops, batch=batch, seq_q=seq, tq=ATTN_TQ, tk=ATTN_TK),)
            + tuple(w.astype(BF16) for w in sd),
            (q, kt, v, ktc, vc), side)
        yf = fourier_mix(proj[:, P_OFF_F:P_OFF_F + FOURIER_W], batch=batch, seq=seq, n1=64, n2=seq // 64)
        xl = merge_branches(xl, mod_l, proj, yf, attn, lw, seq=seq, t=512)

        if not is_last:
            attn_c = attention(qc, None, None, ktc, vc, batch=batch, seq_q=ctx_len, tq=256, tk=ctx_len)
            yf_c = fourier_mix(proj_c[:, P_OFF_F:P_OFF_F + FOURIER_W], batch=batch, seq=ctx_len, n1=16,
                               n2=ctx_len // 16)
            xc = merge_branches(xc, mod_c, proj_c, yf_c, attn_c, lw, seq=ctx_len, t=256)

        j = layer // 2
        w1, w3, w2 = (wb.reshape(w.shape) for wb, w in zip(side_bf16, mixer_w))
        if layer % 2 == 0:
            xl = ffn_dense(xl, mod_l, norm2_g[layer], w1, w3, w2, tm=512, tiles_per_mod=seq // 512)
            if not is_last:
                xc = ffn_dense(xc, mod_c, norm2_g[layer], w1, w3, w2, tm=256, tiles_per_mod=1)
        else:
            router_pad = jnp.zeros((d, 128), F32).at[:, 0:N_EXPERTS].set(moe_router[j])
            xl = moe_sparse(xl, mod_l, norm2_g[layer], router_pad, w1, w3, w2, rows_per_mod=seq)
            if not is_last:
                xc = moe_sparse(xc, mod_c, norm2_g[layer], router_pad, w1, w3, w2, rows_per_mod=batch * ctx_len)
    return xl.reshape(batch, seq, d)
```

```python
import functools
import math

import numpy as np
import jax
import jax.numpy as jnp
from jax import lax
from jax.experimental import pallas as pl
from jax.experimental.pallas import tpu as pltpu

F32 = jnp.float32
BF16 = jnp.bfloat16

D_MODEL = 1024
GRID_W = 64
EPS = 1e-6
FOURIER_GW = 64
FOURIER_W = 256
CONV_W = 256
CONV_K = 31
CONV_HALF = CONV_K // 2
POOL_WINDOWS = (2, 4, 8, 16)
POOL_GW = 64
POOL_W = 256
HEAD_DIM = 64
N_HEADS = 8
N_KV_HEADS = 2
Q_PER_KV = 4
Q_W = 512
KV_W = 128
ROPE_THETA = 10000.0
N_EXPERTS = 8
IN_W = 5888

P_OFF_G = 0
P_OFF_Q = 4096
P_OFF_CP = 4608
P_OFF_F = 5376
P_OFF_KV = 5632
CP_W = 2 * CONV_W + POOL_W

Q_SCALE = (HEAD_DIM ** -0.5) * math.log2(math.e)

ATTN_TQ = 512
ATTN_TK = 1024
STALE_MAX_EXP_LIMIT = 64.0

MOE_ROUTE_TM = 512
MOE_DISPATCH_TM = 512
MOE_COMBINE_TM = 512
MOE_EXPERT_TM = 512
MOE_EXPERT_TF = 1792
DMA_ISSUE_UNROLL = 8

CONV_ROWS = 64
HALO = 16
VMEM_LIMIT = 56 * 1024 * 1024


def _cparams(n_axes):
    return pltpu.CompilerParams(dimension_semantics=("arbitrary",) * n_axes, vmem_limit_bytes=VMEM_LIMIT)


def _sigmoid(v):
    return 0.5 * jnp.tanh(0.5 * v) + 0.5


def _silu(v):
    return v * _sigmoid(v)


def _norm_mod(x, g, shift, scale):
    ms = jnp.mean(x * x, axis=-1, keepdims=True)
    return x * lax.rsqrt(ms + EPS) * g * (1.0 + scale) + shift


def _mod_kernel(c_ref, w_ref, b_ref, o_ref):
    s = _silu(c_ref[...])
    o_ref[0] = jnp.dot(s, w_ref[0], preferred_element_type=F32, precision=lax.Precision.HIGHEST) + b_ref[0]


def modulation_all(c_rows, w_mod, b_mod):
    n_layers, d, n = w_mod.shape
    tn = 1536
    return pl.pallas_call(
        _mod_kernel,
        grid=(n_layers, n // tn),
        in_specs=[
            pl.BlockSpec((8, d), lambda l, j: (0, 0)),
            pl.BlockSpec((1, d, tn), lambda l, j: (l, 0, j)),
            pl.BlockSpec((1, 1, tn), lambda l, j: (l, 0, j)),
        ],
        out_specs=pl.BlockSpec((1, 8, tn), lambda l, j: (l, 0, j)),
        out_shape=jax.ShapeDtypeStruct((n_layers, 8, n), F32),
        compiler_params=_cparams(2),
        name="modulation",
    )(c_rows, w_mod, b_mod.reshape(n_layers, 1, n))


def _inproj_kernel(*refs, chunks, use_rope):
    if use_rope:
        x_ref, mod_ref, g_ref, w_ref, gq_ref, gk_ref, ones_ref, cos_ref, sin_ref, o_ref, qo_ref, kt_ref, v_ref = refs
        cos, sin = cos_ref[...], sin_ref[...]
    else:
        x_ref, mod_ref, g_ref, w_ref, gq_ref, gk_ref, ones_ref, o_ref, qo_ref, kt_ref, v_ref = refs
        cos = sin = None
    h = _norm_mod(x_ref[...], g_ref[...], mod_ref[0, 0:1, :], mod_ref[0, 1:2, :]).astype(BF16)
    heavy_first = sorted(chunks, key=lambda c: (c[0] not in (P_OFF_Q, P_OFF_KV), c[0] != P_OFF_CP, c[0]))
    for c0, cw in heavy_first:
        r = jnp.dot(h, w_ref[:, c0:c0 + cw], preferred_element_type=F32)
        if c0 + cw <= P_OFF_Q:
            r = _sigmoid(r)
        elif c0 == P_OFF_CP:
            sg = _sigmoid(r[:, CONV_W:2 * CONV_W])
            r = jnp.concatenate([r[:, 0:CONV_W] * sg, sg], axis=1)
        elif c0 == P_OFF_Q:
            _q_epilogue(r, gq_ref[...], ones_ref[...], cos, sin, qo_ref)
        elif c0 == P_OFF_KV:
            _kv_epilogue(r, gk_ref[...], ones_ref[...], cos, sin, kt_ref, v_ref)
        o_ref[:, c0:c0 + cw] = r.astype(o_ref.dtype)


def input_projection(x2d, mod, g, w_bf16, gq, gk, rope, *, batch, seq, tm):
    m, d = x2d.shape
    n = w_bf16.shape[1]
    tps = seq // tm
    chunks = tuple((c0, min(512, n - c0)) for c0 in range(0, n, 512))
    assert P_OFF_Q % 512 == 0 and {(P_OFF_CP, 2 * CONV_W), (P_OFF_Q, Q_W), (P_OFF_KV, 2 * KV_W)} <= set(chunks)
    use_rope = rope is not None
    n_mod = mod.shape[0]
    mod_idx = (lambda i: (i // tps, 0, 0)) if n_mod > 1 else (lambda i: (0, 0, 0))
    const = lambda i: (0, 0)
    ones_bd = jnp.asarray(np.kron(np.eye(2, dtype=np.float32), np.ones((64, 64), np.float32)), BF16)
    in_specs = [
        pl.BlockSpec((tm, d), lambda i: (i, 0)),
        pl.BlockSpec((1, 6, d), mod_idx),
        pl.BlockSpec((1, d), const),
        pl.BlockSpec((d, n), const, pipeline_mode=pl.Buffered(1)),
        pl.BlockSpec((1, 128), const),
        pl.BlockSpec((1, 128), const),
        pl.BlockSpec((128, 128), const),
    ]
    args = [x2d, mod, g.reshape(1, d), w_bf16, jnp.tile(gq, 2).reshape(1, 128), jnp.tile(gk, 2).reshape(1, 128), ones_bd]
    if use_rope:
        in_specs += [pl.BlockSpec((tm, 128), lambda i: (i % tps, 0))] * 2
        args += list(rope)
    return pl.pallas_call(
        functools.partial(_inproj_kernel, chunks=chunks, use_rope=use_rope),
        grid=(m // tm,),
        in_specs=in_specs,
        out_specs=[
            pl.BlockSpec((tm, n), lambda i: (i, 0)),
            pl.BlockSpec((1, N_KV_HEADS, 256, tm), lambda i: (i // tps, 0, 0, i % tps)),
            pl.BlockSpec((1, N_KV_HEADS, tm, 256), lambda i: (i // tps, 0, i % tps, 0)),
            pl.BlockSpec((1, N_KV_HEADS, 128, tm), lambda i: (i // tps, 0, 0, i % tps)),
        ],
        out_shape=[
            jax.ShapeDtypeStruct((m, n), BF16),
            jax.ShapeDtypeStruct((batch, N_KV_HEADS, 256, seq), BF16),
            jax.ShapeDtypeStruct((batch, N_KV_HEADS, seq, 256), BF16),
            jax.ShapeDtypeStruct((batch, N_KV_HEADS, 128, seq), BF16),
        ],
        compiler_params=_cparams(1),
        name="input_projection",
    )(*args)


def _seg_sum64(v, ones_bd):
    hi = v.astype(BF16)
    lo = (v - hi.astype(F32)).astype(BF16)
    return (jnp.dot(hi, ones_bd, preferred_element_type=F32) + jnp.dot(lo, ones_bd, preferred_element_type=F32))


def _head_norm_rope(x, g, ones_bd, cos, sin, low_mask):
    y = x * lax.rsqrt(_seg_sum64(x * x, ones_bd) * (1.0 / HEAD_DIM) + EPS) * g
    if cos is None:
        return y
    partner = jnp.where(low_mask, pltpu.roll(y, 128 - 16, axis=1), pltpu.roll(y, 16, axis=1))
    return y * cos + partner * sin


def _rope_low_mask(t):
    return (lax.broadcasted_iota(jnp.int32, (t, 128), 1) % 32) < 16


def _q_epilogue(rq, gq, ones_bd, cos, sin, qo_ref):
    low_mask = _rope_low_mask(rq.shape[0])
    for c in range(Q_W // 128):
        yq = _head_norm_rope(rq[:, 128 * c:128 * (c + 1)], gq, ones_bd, cos, sin, low_mask) * Q_SCALE
        qo_ref[0, c // 2, 128 * (c % 2):128 * (c % 2 + 1), :] = yq.T.astype(BF16)


def _kv_epilogue(rkv, gk, ones_bd, cos, sin, kt_ref, v_ref):
    t = rkv.shape[0]
    yk = _head_norm_rope(rkv[:, 0:128], gk, ones_bd, cos, sin, _rope_low_mask(t))
    ykr = pltpu.roll(yk, 64, axis=1)
    first = lax.broadcasted_iota(jnp.int32, (t, 128), 1) < 64
    k0 = jnp.where(first, yk, ykr).astype(BF16)
    k1 = jnp.where(first, ykr, yk).astype(BF16)
    kt_ref[0, 0] = jnp.concatenate([k0, k0], axis=1)
    kt_ref[0, 1] = jnp.concatenate([k1, k1], axis=1)
    vt = rkv[:, 128:256].T
    ones = jnp.ones((HEAD_DIM, t), F32)
    for h in range(N_KV_HEADS):
        v_ref[0, h] = jnp.concatenate([vt[64 * h:64 * (h + 1), :], ones], axis=0).astype(BF16)


def rope_tables(seq):
    n_freq = HEAD_DIM // 4
    freqs = ROPE_THETA ** (-jnp.arange(n_freq, dtype=F32) / n_freq)
    t = jnp.arange(seq)
    row = (t // GRID_W).astype(F32)
    col = (t % GRID_W).astype(F32)
    ang_r = row[:, None] * freqs
    ang_c = col[:, None] * freqs
    cos = jnp.concatenate([jnp.cos(ang_r)] * 2 + [jnp.cos(ang_c)] * 2, axis=1)
    sin = jnp.concatenate([-jnp.sin(ang_r), jnp.sin(ang_r), -jnp.sin(ang_c), jnp.sin(ang_c)], axis=1)
    return jnp.tile(cos, (1, 2)), jnp.tile(sin, (1, 2))


def _attn_kernel(*refs, tq, tk, nk, tail):
    refs = list(refs)
    qt_ref = refs.pop(0)
    k_ref, vt_ref = (refs.pop(0), refs.pop(0)) if nk else (None, None)
    kc_ref, vtc_ref = (refs.pop(0), refs.pop(0)) if tail else (None, None)
    o_ref, qs_ref, s0, s1, p0, p1, a0, a1, mx0, mx1, m_ref, acc_ref = refs
    s_bufs, p_bufs, a_bufs, mx_bufs = (s0, s1), (p0, p1), (a0, a1), (mx0, mx1)
    n_blocks = nk + (1 if tail else 0)

    _attn_stack_queries(qt_ref, qs_ref, tq)
    m_ref[...] = jnp.full(m_ref.shape, -jnp.inf, F32)
    acc_ref[...] = jnp.zeros(acc_ref.shape, F32)

    def block(t):
        if isinstance(t, int) and t >= nk:
            return kc_ref[0, 0], vtc_ref[0, 0], tail
        off = t * tk if isinstance(t, int) else pl.multiple_of(t * tk, tk)
        return k_ref[0, 0, pl.ds(off, tk), :], vt_ref[0, 0, :, pl.ds(off, tk)], tk

    def scores(t, slot):
        k_rows, _, n = block(t)
        s = jnp.dot(k_rows, qs_ref[...], preferred_element_type=F32)
        s_bufs[slot][0:n, :] = s
        mx_bufs[slot][...] = jnp.max(s, axis=0, keepdims=True)

    def numerators(n, slot):
        s_ref, p_ref, a_ref = s_bufs[slot], p_bufs[slot], a_bufs[slot]
        for c0 in range(0, Q_PER_KV * tq, 128):
            cols = slice(c0, c0 + 128)
            m_old = m_ref[:, cols]
            m_new = jnp.maximum(m_old, mx_bufs[slot][:, cols])
            a_ref[:, cols] = jnp.exp2(m_old - m_new)
            p_ref[0:n, cols] = jnp.exp2(s_ref[0:n, cols] - m_new).astype(BF16)
            m_ref[:, cols] = m_new

    def weighted_sum(t, slot):
        _, vt, n = block(t)
        pv = jnp.dot(vt, p_bufs[slot][0:n, :], preferred_element_type=F32)
        acc_ref[...] = a_bufs[slot][...] * acc_ref[...] + pv

    def rows_of(t):
        return tk if t < nk else tail

    def step(t, slot, n_mid):
        scores(t, slot)
        numerators(n_mid, 1 - slot)
        weighted_sum(t - 2, slot)

    scores(0, 0)
    if n_blocks > 1:
        scores(1, 1)
        numerators(rows_of(0), 0)
        n_pairs = max(nk - 2, 0) // 2

        def pair(i, carry):
            t = 2 + 2 * i
            step(t, 0, tk)
            step(t + 1, 1, tk)
            return carry

        if n_pairs:
            lax.fori_loop(0, n_pairs, pair, 0)
        for t in range(2 + 2 * n_pairs, n_blocks):
            step(t, t % 2, rows_of(t - 1))
        last = n_blocks - 1
        numerators(rows_of(last), last % 2)
        weighted_sum(last - 1, (last - 1) % 2)
        weighted_sum(last, last % 2)
    else:
        numerators(rows_of(0), 0)
        weighted_sum(0, 0)

    _attn_write_output(acc_ref, o_ref, tq)


def _attn_write_output(acc_ref, o_ref, tq):
    acc = acc_ref[...]
    ot = acc[0:HEAD_DIM, :] / acc[HEAD_DIM:2 * HEAD_DIM, :]
    for half in range(2):
        pair_t = jnp.concatenate([ot[:, (2 * half) * tq:(2 * half + 1) * tq],
                                  ot[:, (2 * half + 1) * tq:(2 * half + 2) * tq]], axis=0)
        o_ref[:, 128 * half:128 * (half + 1)] = pair_t.T.astype(o_ref.dtype)


def _attn_stack_queries(qt_ref, qs_ref, tq):
    row_group = lax.broadcasted_iota(jnp.int32, (256, tq), 0) // HEAD_DIM
    qt = qt_ref[0, 0]
    for g in range(Q_PER_KV):
        qs_ref[:, g * tq:(g + 1) * tq] = jnp.where(row_group == g, qt, jnp.zeros_like(qt))


def _attn_stale_max_kernel(*refs, tq, tk, nk, n_side):
    qt_ref, k_ref, vt_ref, kc_ref, vtc_ref = refs[:5]
    side_in = refs[5:5 + n_side]
    o_ref = refs[5 + n_side]
    side_out = refs[6 + n_side:6 + 2 * n_side]
    qs_ref, p0, p1, f0, f1, m_ref, acc_ref = refs[6 + 2 * n_side:]
    for src, dst in zip(side_in, side_out):
        dst[...] = src[...].astype(dst.dtype)
    p_bufs, f_bufs = (p0, p1), (f0, f1)
    _attn_stack_queries(qt_ref, qs_ref, tq)

    s = jnp.dot(kc_ref[0, 0], qs_ref[...], preferred_element_type=F32)
    m0 = jnp.max(s, axis=0, keepdims=True)
    m_ref[...] = m0
    acc_ref[...] = jnp.dot(vtc_ref[0, 0], jnp.exp2(s - m0).astype(BF16), preferred_element_type=F32)

    def numerators(t, slot):
        off = t * tk if isinstance(t, int) else pl.multiple_of(t * tk, tk)
        s = jnp.dot(k_ref[0, 0, pl.ds(off, tk), :], qs_ref[...], preferred_element_type=F32)
        m_old = m_ref[...]
        p_bufs[slot][...] = jnp.exp2(s - m_old).astype(BF16)
        m_new = jnp.maximum(m_old, jnp.max(s, axis=0, keepdims=True))
        f_bufs[slot][...] = jnp.exp2(m_old - m_new)
        m_ref[...] = m_new

    def weighted_sum(t, slot):
        off = t * tk if isinstance(t, int) else pl.multiple_of(t * tk, tk)
        pv = jnp.dot(vt_ref[0, 0, :, pl.ds(off, tk)], p_bufs[slot][...], preferred_element_type=F32)
        acc_ref[...] = (acc_ref[...] + pv) * f_bufs[slot][...]

    def step(t, slot):
        numerators(t, slot)
        weighted_sum(t - 1, 1 - slot)

    numerators(0, 0)
    n_pairs = (nk - 1) // 2

    def pair(i, carry):
        t = 1 + 2 * i
        step(t, 1)
        step(t + 1, 0)
        return carry

    if n_pairs:
        lax.fori_loop(0, n_pairs, pair, 0)
    for t in range(1 + 2 * n_pairs, nk):
        step(t, t % 2)
    weighted_sum(nk - 1, (nk - 1) % 2)
    _attn_write_output(acc_ref, o_ref, tq)


def attention_stale_max(qt, k4, vt1, k4_tail, vt1_tail, side=(), *, batch, seq_q, tq, tk):
    nq = seq_q // tq
    lanes = Q_PER_KV * tq
    lk = k4.shape[2]
    tail = k4_tail.shape[2]
    n_steps = batch * N_KV_HEADS * nq
    assert all(n_steps % w.shape[0] == 0 for w in side)

    def side_spec(w):
        repeat = n_steps // w.shape[0]
        return pl.BlockSpec((1,) + w.shape[1:], lambda b, h, i: (((b * N_KV_HEADS + h) * nq + i) // repeat, 0, 0))

    side_specs = [side_spec(w) for w in side]
    outs = pl.pallas_call(
        functools.partial(_attn_stale_max_kernel, tq=tq, tk=tk, nk=lk // tk, n_side=len(side)),
        grid=(batch, N_KV_HEADS, nq),
        in_specs=[
            pl.BlockSpec((1, 1, 256, tq), lambda b, h, i: (b, h, 0, i)),
            pl.BlockSpec((1, 1, lk, 256), lambda b, h, i: (b, h, 0, 0)),
            pl.BlockSpec((1, 1, 128, lk), lambda b, h, i: (b, h, 0, 0)),
            pl.BlockSpec((1, 1, tail, 256), lambda b, h, i: (b, h, 0, 0)),
            pl.BlockSpec((1, 1, 128, tail), lambda b, h, i: (b, h, 0, 0)),
        ] + side_specs,
        out_specs=[pl.BlockSpec((tq, 256), lambda b, h, i: (b * nq + i, h))] + side_specs,
        out_shape=[jax.ShapeDtypeStruct((batch * seq_q, Q_W), BF16)]
        + [jax.ShapeDtypeStruct(w.shape, BF16) for w in side],
        scratch_shapes=[
            pltpu.VMEM((256, lanes), BF16),
            pltpu.VMEM((tk, lanes), BF16), pltpu.VMEM((tk, lanes), BF16),
            pltpu.VMEM((1, lanes), F32), pltpu.VMEM((1, lanes), F32),
            pltpu.VMEM((1, lanes), F32),
            pltpu.VMEM((2 * HEAD_DIM, lanes), F32),
        ],
        compiler_params=_cparams(3),
        name="attention_stale_max",
    )(qt, k4, vt1, k4_tail, vt1_tail, *side)
    return tuple(outs)


def attention(qt, k4, vt1, k4_tail, vt1_tail, *, batch, seq_q, tq, tk):
    nq = seq_q // tq
    lanes = Q_PER_KV * tq
    nk = 0 if k4 is None else k4.shape[2] // tk
    tail = 0 if k4_tail is None else k4_tail.shape[2]
    buf_rows = max(tk if nk else 0, tail)
    in_specs = [pl.BlockSpec((1, 1, 256, tq), lambda b, h, i: (b, h, 0, i))]
    args = [qt]
    if nk:
        lk = k4.shape[2]
        in_specs += [pl.BlockSpec((1, 1, lk, 256), lambda b, h, i: (b, h, 0, 0)),
                     pl.BlockSpec((1, 1, 128, lk), lambda b, h, i: (b, h, 0, 0))]
        args += [k4, vt1]
    if tail:
        in_specs += [pl.BlockSpec((1, 1, tail, 256), lambda b, h, i: (b, h, 0, 0)),
                     pl.BlockSpec((1, 1, 128, tail), lambda b, h, i: (b, h, 0, 0))]
        args += [k4_tail, vt1_tail]
    return pl.pallas_call(
        functools.partial(_attn_kernel, tq=tq, tk=tk, nk=nk, tail=tail),
        grid=(batch, N_KV_HEADS, nq),
        in_specs=in_specs,
        out_specs=pl.BlockSpec((tq, 256), lambda b, h, i: (b * nq + i, h)),
        out_shape=jax.ShapeDtypeStruct((batch * seq_q, Q_W), BF16),
        scratch_shapes=[
            pltpu.VMEM((256, lanes), BF16),
            pltpu.VMEM((buf_rows, lanes), F32), pltpu.VMEM((buf_rows, lanes), F32),
            pltpu.VMEM((buf_rows, lanes), BF16), pltpu.VMEM((buf_rows, lanes), BF16),
            pltpu.VMEM((1, lanes), F32), pltpu.VMEM((1, lanes), F32),
            pltpu.VMEM((1, lanes), F32), pltpu.VMEM((1, lanes), F32),
            pltpu.VMEM((1, lanes), F32),
            pltpu.VMEM((2 * HEAD_DIM, lanes), F32),
        ],
        compiler_params=_cparams(3),
        name="attention",
    )(*args)


def _dft_cs(n):
    k = np.arange(n)
    ang = 2.0 * np.pi * ((k[:, None] * k[None, :]) % n) / n
    return np.cos(ang), np.sin(ang)


def _fft1_kernel(x_ref, f_ref, c_ref, s_ref, o_ref, *, n1):
    y = jnp.dot(f_ref[...], x_ref[0], preferred_element_type=F32)
    yr, yi = y[:n1], y[n1:]
    c, s = c_ref[...], s_ref[...]
    o_ref[0, 0] = (yr * c + yi * s).astype(o_ref.dtype)
    o_ref[0, 1] = (yi * c - yr * s).astype(o_ref.dtype)


def _fft2_kernel(y_ref, f_ref, bc_ref, bs_ref, o_ref, *, n2, kb):
    for j in range(kb):
        y2 = jnp.concatenate([y_ref[0, 0, j], y_ref[0, 1, j]], axis=0)
        x2 = jnp.dot(f_ref[...], y2, preferred_element_type=F32)
        xr = x2[:n2].astype(BF16)
        xi = x2[n2:].astype(BF16)
        z = (jnp.dot(xr, bc_ref[...], preferred_element_type=F32) + jnp.dot(xi, bs_ref[...], preferred_element_type=F32))
        o_ref[0, j] = z.astype(o_ref.dtype)


def fourier_mix(u, *, batch, seq, n1, n2):
    cw = u.shape[1]
    lanes = n2 * cw
    tl = min(lanes, 4096)
    c1, s1 = _dft_cs(n1)
    f1 = jnp.asarray(np.concatenate([c1, -s1], axis=0), BF16)
    k1 = np.arange(n1)[:, None]
    t2 = np.arange(n2)[None, :]
    ang = 2.0 * np.pi * ((k1 * t2) % seq) / seq
    twc = jnp.asarray(np.repeat(np.cos(ang), cw, axis=1), F32)
    tws = jnp.asarray(np.repeat(np.sin(ang), cw, axis=1), F32)
    x2 = u.reshape(batch, n1, lanes)
    yp = pl.pallas_call(
        functools.partial(_fft1_kernel, n1=n1),
        grid=(batch, lanes // tl),
        in_specs=[
            pl.BlockSpec((1, n1, tl), lambda b, j: (b, 0, j)),
            pl.BlockSpec((2 * n1, n1), lambda b, j: (0, 0)),
            pl.BlockSpec((n1, tl), lambda b, j: (0, j)),
            pl.BlockSpec((n1, tl), lambda b, j: (0, j)),
        ],
        out_specs=pl.BlockSpec((1, 2, n1, tl), lambda b, j: (b, 0, 0, j)),
        out_shape=jax.ShapeDtypeStruct((batch, 2, n1, lanes), BF16),
        compiler_params=_cparams(2),
        name="fft_stage1",
    )(x2, f1, twc, tws)

    c2, s2 = _dft_cs(n2)
    f2 = jnp.asarray(np.block([[c2, s2], [-s2, c2]]), BF16)
    cg, sg = _dft_cs(FOURIER_GW)
    norm = 1.0 / math.sqrt(seq * FOURIER_GW)
    bdc = jnp.asarray(np.kron(np.eye(cw // FOURIER_GW), cg) * norm, BF16)
    bds = jnp.asarray(np.kron(np.eye(cw // FOURIER_GW), sg) * norm, BF16)
    kb = min(n1, 16)
    y5 = yp.reshape(batch, 2, n1, n2, cw)
    z = pl.pallas_call(
        functools.partial(_fft2_kernel, n2=n2, kb=kb),
        grid=(batch, n1 // kb),
        in_specs=[
            pl.BlockSpec((1, 2, kb, n2, cw), lambda b, j: (b, 0, j, 0, 0)),
            pl.BlockSpec((2 * n2, 2 * n2), lambda b, j: (0, 0)),
            pl.BlockSpec((cw, cw), lambda b, j: (0, 0)),
            pl.BlockSpec((cw, cw), lambda b, j: (0, 0)),
        ],
        out_specs=pl.BlockSpec((1, kb, n2, cw), lambda b, j: (b, j, 0, 0)),
        out_shape=jax.ShapeDtypeStruct((batch, n1, n2, cw), BF16),
        compiler_params=_cparams(2),
        name="fft_stage2",
    )(y5, f2, bdc, bds)
    return z.transpose(0, 2, 1, 3).reshape(batch * seq, cw)


def _merge_kernel(x_ref, mod_ref, gate_ref, cp_ref, cpp_ref, cpn_ref, yf_ref, at_ref,
                  wf_ref, wc_ref, wp_ref, wa_ref, wo_ref, dw_ref, cb_ref, cg_ref, pw_ref, ps_ref, band_ref, icnt_ref,
                  o_ref, ybuf, xbuf, ysh, cacc, *, t, tps):
    i = pl.program_id(0)
    pos_tile = i % tps
    keep_prev = jnp.where(pos_tile != 0, 1.0, 0.0).astype(F32)
    keep_next = jnp.where(pos_tile != tps - 1, 1.0, 0.0).astype(F32)

    def glu(blk):
        return blk[:, 0:CONV_W].astype(F32)

    cp, cpp, cpn = cp_ref[...], cpp_ref[...], cpn_ref[...]
    ybuf[0:HALO, :] = glu(cpp) * keep_prev
    ybuf[HALO:HALO + t, :] = glu(cp)
    ybuf[HALO + t:HALO + t + HALO, :] = glu(cpn) * keep_next
    xbuf[0:HALO, :] = cpp[:, 2 * CONV_W:] * keep_prev.astype(BF16)
    xbuf[HALO:HALO + t, :] = cp[:, 2 * CONV_W:]
    xbuf[HALO + t:HALO + t + HALO, :] = cpn[:, 2 * CONV_W:] * keep_next.astype(BF16)

    n_sh = t + 2 * HALO - 8
    for b in range(1, 8):
        ysh[b - 1, 0:n_sh, :] = ybuf[pl.ds(b, n_sh), :]
    for r0 in range(0, t, CONV_ROWS):
        part = jnp.zeros((CONV_ROWS, CONV_W), F32)
        for k in range(CONV_K):
            a, b = divmod(HALO - CONV_HALF + k, 8)
            src = ybuf if b == 0 else ysh.at[b - 1]
            part = part + dw_ref[k:k + 1, :] * src[8 * a + r0:8 * a + r0 + CONV_ROWS, :]
        cacc[r0:r0 + CONV_ROWS, :] = part + cb_ref[...]
    acc = cacc[...]
    ms = jnp.mean(acc * acc, axis=-1, keepdims=True)
    conv_out = _silu(acc * lax.rsqrt(ms + EPS) * cg_ref[...]).astype(BF16)

    pb = band_ref.shape[1]
    grp = lax.broadcasted_iota(jnp.int32, (pb + 2 * HALO, POOL_W), 1) // POOL_GW
    parts = []
    for r0 in range(0, t, pb):
        xw = xbuf[r0:r0 + pb + 2 * HALO, :]
        wsum = jnp.zeros((pb, POOL_W), F32)
        for gi in range(len(POOL_WINDOWS)):
            wsum = wsum + jnp.dot(band_ref[gi], jnp.where(grp == gi, xw, jnp.zeros_like(xw)),
                                  preferred_element_type=F32)
        x0 = xbuf[HALO + r0:HALO + r0 + pb, :].astype(F32)
        parts.append((wsum * icnt_ref[0, r0:r0 + pb, :] - x0).astype(BF16))
    pool_in = parts[0] if len(parts) == 1 else jnp.concatenate(parts, axis=0)
    pool_out = (jnp.dot(pool_in, pw_ref[...], preferred_element_type=F32) * ps_ref[...]).astype(BF16)

    def gate(b):
        return gate_ref[:, b * D_MODEL:(b + 1) * D_MODEL].astype(F32)

    merged = gate(0) * jnp.dot(yf_ref[...], wf_ref[...], preferred_element_type=F32)
    merged = merged + gate(1) * jnp.dot(conv_out, wc_ref[...], preferred_element_type=F32)
    merged = merged + gate(2) * jnp.dot(pool_out, wp_ref[...], preferred_element_type=F32)
    merged = merged + gate(3) * jnp.dot(at_ref[...], wa_ref[...], preferred_element_type=F32)
    out = jnp.dot(merged.astype(BF16), wo_ref[...], preferred_element_type=F32)
    o_ref[...] = x_ref[...] + mod_ref[0, 2:3, :] * out


def _pool_tables(seq, t):
    pb = min(t, 256)
    r = np.arange(pb)[:, None]
    j = np.arange(pb + 2 * HALO)[None, :]
    band = np.stack([((j >= r + HALO - w // 2) & (j < r + HALO + w // 2)) for w in POOL_WINDOWS]).astype(np.float32)
    half = np.repeat(np.array(POOL_WINDOWS) // 2, POOL_GW)[None, :]
    rows = np.arange(t)[:, None]
    icnt = []
    for first, last in ((0, 0), (1, 0), (0, 1), (1, 1)):
        if first and last:
            pos, length = rows, t
        elif first:
            pos, length = rows, 2 * t + 2 * HALO
        elif last:
            pos, length = rows + seq - t, seq
        else:
            pos, length = rows + t + 2 * HALO, 4 * t
        cnt = np.minimum(pos + half, length) - np.maximum(pos - half, 0)
        icnt.append(1.0 / cnt)
    return jnp.asarray(band, BF16), jnp.asarray(np.stack(icnt), F32)


def merge_branches(x2d, mod, proj, yf, attn, lw, *, seq, t):
    m, d = x2d.shape
    tps = seq // t
    hb = t // HALO
    n_halo = m // HALO
    n_mod = mod.shape[0]
    mod_idx = (lambda i: (i // tps, 0, 0)) if n_mod > 1 else (lambda i: (0, 0, 0))
    const = lambda i: (0, 0)
    cp_blk = P_OFF_CP // CP_W
    band, icnt = _pool_tables(seq, t)
    pb = band.shape[1]

    def icnt_idx(i):
        pos_tile = i % tps
        return ((pos_tile == 0).astype(jnp.int32) + 2 * (pos_tile == tps - 1).astype(jnp.int32), 0, 0)

    return pl.pallas_call(
        functools.partial(_merge_kernel, t=t, tps=tps),
        grid=(m // t,),
        in_specs=[
            pl.BlockSpec((t, d), lambda i: (i, 0)),
            pl.BlockSpec((1, 6, d), mod_idx),
            pl.BlockSpec((t, 4 * d), lambda i: (i, 0)),
            pl.BlockSpec((t, CP_W), lambda i: (i, cp_blk)),
            pl.BlockSpec((HALO, CP_W), lambda i: (jnp.maximum(i * hb - 1, 0), cp_blk)),
            pl.BlockSpec((HALO, CP_W), lambda i: (jnp.minimum((i + 1) * hb, n_halo - 1), cp_blk)),
            pl.BlockSpec((t, FOURIER_W), lambda i: (i, 0)),
            pl.BlockSpec((t, Q_W), lambda i: (i, 0)),
            pl.BlockSpec((FOURIER_W, d), const),
            pl.BlockSpec((CONV_W, d), const),
            pl.BlockSpec((POOL_W, d), const),
            pl.BlockSpec((Q_W, d), const),
            pl.BlockSpec((d, d), const),
            pl.BlockSpec((CONV_K, CONV_W), const),
            pl.BlockSpec((1, CONV_W), const),
            pl.BlockSpec((1, CONV_W), const),
            pl.BlockSpec((POOL_W, POOL_W), const),
            pl.BlockSpec((1, POOL_W), const),
            pl.BlockSpec((len(POOL_WINDOWS), pb, pb + 2 * HALO), lambda i: (0, 0, 0)),
            pl.BlockSpec((1, t, POOL_W), icnt_idx),
        ],
        out_specs=pl.BlockSpec((t, d), lambda i: (i, 0)),
        out_shape=jax.ShapeDtypeStruct((m, d), F32),
        scratch_shapes=[pltpu.VMEM((t + 2 * HALO, CONV_W), F32), pltpu.VMEM((t + 2 * HALO, POOL_W), BF16),
                        pltpu.VMEM((7, t + 2 * HALO, CONV_W), F32), pltpu.VMEM((t, CONV_W), F32)],
        compiler_params=_cparams(1),
        name="merge_branches",
    )(x2d, mod, proj, proj, proj, proj, yf, attn,
      lw["wf"], lw["wc"], lw["wp"], lw["wa"], lw["wo"], lw["dw"], lw["cb"], lw["cg"], lw["pw"], lw["ps"], band, icnt)


def _ffn_kernel(x_ref, mod_ref, g_ref, w1_ref, w3_ref, w2_ref, o_ref, *, chunks):
    x = x_ref[...]
    h = _norm_mod(x, g_ref[...], mod_ref[0, 3:4, :], mod_ref[0, 4:5, :]).astype(BF16)
    acc = jnp.zeros(x.shape, F32)
    for c0, cw in chunks:
        a = jnp.dot(h, w1_ref[:, c0:c0 + cw], preferred_element_type=F32)
        b = jnp.dot(h, w3_ref[:, c0:c0 + cw], preferred_element_type=F32)
        acc = acc + jnp.dot((_silu(a) * b).astype(BF16), w2_ref[c0:c0 + cw, :], preferred_element_type=F32)
    o_ref[...] = x + mod_ref[0, 5:6, :] * acc


def ffn_dense(x2d, mod, g, w1, w3, w2, *, tm, tiles_per_mod):
    m, d = x2d.shape
    dff = w1.shape[1]
    chunks = tuple((c0, min(1024, dff - c0)) for c0 in range(0, dff, 1024))
    n_mod = mod.shape[0]
    mod_idx = (lambda i: (i // tiles_per_mod, 0, 0)) if n_mod > 1 else (lambda i: (0, 0, 0))
    const = lambda i: (0, 0)
    return pl.pallas_call(
        functools.partial(_ffn_kernel, chunks=chunks),
        grid=(m // tm,),
        in_specs=[
            pl.BlockSpec((tm, d), lambda i: (i, 0)),
            pl.BlockSpec((1, 6, d), mod_idx),
            pl.BlockSpec((1, d), const),
            pl.BlockSpec((d, dff), const, pipeline_mode=pl.Buffered(1)),
            pl.BlockSpec((d, dff), const, pipeline_mode=pl.Buffered(1)),
            pl.BlockSpec((dff, d), const, pipeline_mode=pl.Buffered(1)),
        ],
        out_specs=pl.BlockSpec((tm, d), lambda i: (i, 0)),
        out_shape=jax.ShapeDtypeStruct((m, d), F32),
        compiler_params=_cparams(1),
        name="ffn_dense",
    )(x2d, mod, g.reshape(1, d), w1, w3, w2)


def _top2(logits):
    t = logits.shape[0]
    lane = lax.broadcasted_iota(jnp.int32, (t, 128), 1).astype(F32)
    neg = jnp.float32(-jnp.inf)
    lg = jnp.where(lane < N_EXPERTS, logits, neg)
    v1 = jnp.max(lg, axis=-1, keepdims=True)
    i1 = jnp.min(jnp.where(lg == v1, lane, 128.0), axis=-1, keepdims=True)
    lg2 = jnp.where(lane == i1, neg, lg)
    v2 = jnp.max(lg2, axis=-1, keepdims=True)
    i2 = jnp.min(jnp.where(lg2 == v2, lane, 128.0), axis=-1, keepdims=True)
    e2 = jnp.exp(v2 - v1)
    return i1, i2, 1.0 / (1.0 + e2), e2 / (1.0 + e2)


R_E1, R_E2, R_W1, R_W2, R_RANK1, R_RANK2 = range(6)


def _route_kernel(x_ref, mod_ref, g_ref, r_ref, tri_ref, route_ref, route_t_ref, cnt_ref, carry_ref):
    @pl.when(pl.program_id(0) == 0)
    def _():
        carry_ref[...] = jnp.zeros(carry_ref.shape, F32)

    t = x_ref.shape[0]
    h = _norm_mod(x_ref[...], g_ref[...], mod_ref[0, 3:4, :], mod_ref[0, 4:5, :])
    r = r_ref[...]
    h_hi, r_hi = h.astype(BF16), r.astype(BF16)
    h_lo, r_lo = (h - h_hi.astype(F32)).astype(BF16), (r - r_hi.astype(F32)).astype(BF16)
    logits = (jnp.dot(h_hi, r_hi, preferred_element_type=F32) + jnp.dot(h_hi, r_lo, preferred_element_type=F32)
              + jnp.dot(h_lo, r_hi, preferred_element_type=F32))
    i1, i2, w1, w2 = _top2(logits)
    lane = lax.broadcasted_iota(jnp.int32, (t, 128), 1).astype(F32)
    oh1 = jnp.where(lane == i1, 1.0, 0.0)
    oh2 = jnp.where(lane == i2, 1.0, 0.0)
    both = oh1 + oh2
    before = carry_ref[...] + jnp.dot(tri_ref[...], both.astype(BF16), preferred_element_type=F32)
    rank1 = jnp.sum(oh1 * before, axis=-1, keepdims=True)
    rank2 = jnp.sum(oh2 * before, axis=-1, keepdims=True)
    carry_ref[...] += jnp.sum(both, axis=0, keepdims=True)
    rec = jnp.zeros((t, 128), F32)
    for col, val in ((R_E1, i1), (R_E2, i2), (R_W1, w1), (R_W2, w2), (R_RANK1, rank1), (R_RANK2, rank2)):
        rec = jnp.where(lane == col, val, rec)
    route_ref[...] = rec
    route_t_ref[0] = rec.T[0:8, :]
    cnt_ref[...] = carry_ref[...]


def moe_route(x2d, mod, g, router_pad, *, tm, tiles_per_mod):
    m, d = x2d.shape
    n_mod = mod.shape[0]
    mod_idx = (lambda i: (i // tiles_per_mod, 0, 0)) if n_mod > 1 else (lambda i: (0, 0, 0))
    tri = jnp.asarray(np.tril(np.ones((tm, tm), np.float32), -1), BF16)
    return pl.pallas_call(
        _route_kernel,
        grid=(m // tm,),
        in_specs=[
            pl.BlockSpec((tm, d), lambda i: (i, 0)),
            pl.BlockSpec((1, 6, d), mod_idx),
            pl.BlockSpec((1, d), lambda i: (0, 0)),
            pl.BlockSpec((d, 128), lambda i: (0, 0)),
            pl.BlockSpec((tm, tm), lambda i: (0, 0)),
        ],
        out_specs=[pl.BlockSpec((tm, 128), lambda i: (i, 0)), pl.BlockSpec((1, 8, tm), lambda i: (i, 0, 0)),
                   pl.BlockSpec((1, 128), lambda i: (0, 0))],
        out_shape=[jax.ShapeDtypeStruct((m, 128), F32), jax.ShapeDtypeStruct((m // tm, 8, tm), F32),
                   jax.ShapeDtypeStruct((1, 128), F32)],
        scratch_shapes=[pltpu.VMEM((1, 128), F32)],
        compiler_params=_cparams(1),
        name="moe_route",
    )(x2d, mod, g.reshape(1, d), router_pad, tri)


def _dispatch_kernel(pos_ref, pad_ref, x_ref, mod_ref, g_ref, xs_ref, h_ref, zero_ref, sem, zsem):
    i = pl.program_id(0)
    t = x_ref.shape[0]
    slot = i % 2
    h_ref[slot] = _norm_mod(x_ref[...], g_ref[...], mod_ref[0, 3:4, :], mod_ref[0, 4:5, :])

    def row_copy(r, dst_row):
        return pltpu.make_async_copy(h_ref.at[slot, pl.ds(r, 1), :], xs_ref.at[pl.ds(dst_row, 1), :], sem.at[slot])

    def issue(r, carry):
        row_copy(r, pos_ref[0, 0, r]).start()
        row_copy(r, pos_ref[0, 0, t + r]).start()
        return carry

    lax.fori_loop(0, t, issue, 0, unroll=DMA_ISSUE_UNROLL)

    def drain(s):
        for _ in range(2):
            pltpu.make_async_copy(h_ref.at[s], xs_ref.at[pl.ds(0, t), :], sem.at[s]).wait()

    @pl.when(i > 0)
    def _():
        drain(1 - slot)

    def zero_row_copy(r):
        return pltpu.make_async_copy(zero_ref.at[pl.ds(0, 1), :], xs_ref.at[pl.ds(r, 1), :], zsem)

    def zero_tile_copy(k):
        return pltpu.make_async_copy(zero_ref, xs_ref.at[pl.ds(pl.multiple_of(k * t, t), t), :], zsem)

    def for_each_zero_copy(act):
        def row(r, carry):
            act(zero_row_copy(r))
            return carry

        def tile(k, carry):
            act(zero_tile_copy(k))
            return carry

        for e in range(N_EXPERTS):
            lax.fori_loop(pad_ref[0, e], pad_ref[0, N_EXPERTS + e], row, 0)
        lax.fori_loop(pad_ref[0, 2 * N_EXPERTS], xs_ref.shape[0] // t, tile, 0)

    @pl.when(i == 0)
    def _():
        zero_ref[...] = jnp.zeros(zero_ref.shape, F32)
        for_each_zero_copy(lambda cp: cp.start())

    @pl.when(i == pl.num_programs(0) - 1)
    def _():
        drain(slot)
        for_each_zero_copy(lambda cp: cp.wait())


def moe_dispatch(x2d, mod, g, pos_tiles, pad_rows, n_rows, *, tm, tiles_per_mod):
    m, d = x2d.shape
    n_mod = mod.shape[0]
    mod_idx = (lambda i: (i // tiles_per_mod, 0, 0)) if n_mod > 1 else (lambda i: (0, 0, 0))
    return pl.pallas_call(
        _dispatch_kernel,
        grid=(m // tm,),
        in_specs=[
            pl.BlockSpec((1, 1, 2 * tm), lambda i: (i, 0, 0), memory_space=pltpu.SMEM),
            pl.BlockSpec((1, 2 * N_EXPERTS + 1), lambda i: (0, 0), memory_space=pltpu.SMEM),
            pl.BlockSpec((tm, d), lambda i: (i, 0)),
            pl.BlockSpec((1, 6, d), mod_idx),
            pl.BlockSpec((1, d), lambda i: (0, 0)),
        ],
        out_specs=pl.BlockSpec(memory_space=pl.ANY),
        out_shape=jax.ShapeDtypeStruct((n_rows, d), F32),
        scratch_shapes=[pltpu.VMEM((2, tm, d), F32), pltpu.VMEM((tm, d), F32),
                        pltpu.SemaphoreType.DMA((2,)), pltpu.SemaphoreType.DMA(())],
        compiler_params=_cparams(1),
        name="moe_dispatch",
    )(pos_tiles, pad_rows, x2d, mod, g.reshape(1, d))


def _experts_kernel(te_ref, nv_ref, xs_ref, w1_ref, w3_ref, w2_ref, ys_ref, xb_ref, acc_ref):
    i = pl.program_id(0)
    j = pl.program_id(1)
    valid = i < nv_ref[0]

    @pl.when(jnp.logical_and(valid, j == 0))
    def _():
        xb_ref[...] = xs_ref[...].astype(BF16)
        acc_ref[...] = jnp.zeros(acc_ref.shape, F32)

    @pl.when(valid)
    def _():
        h = xb_ref[...]
        tf = w1_ref.shape[2]
        y = None
        for c0 in range(0, tf, 1024):
            cw = min(1024, tf - c0)
            a = jnp.dot(h, w1_ref[0, :, c0:c0 + cw], preferred_element_type=F32)
            b = jnp.dot(h, w3_ref[0, :, c0:c0 + cw], preferred_element_type=F32)
            yc = jnp.dot((_silu(a) * b).astype(BF16), w2_ref[0, c0:c0 + cw, :], preferred_element_type=F32)
            y = yc if y is None else y + yc
        acc_ref[...] += y

    @pl.when(jnp.logical_and(valid, j == pl.num_programs(1) - 1))
    def _():
        ys_ref[...] = acc_ref[...]

    @pl.when(jnp.logical_and(jnp.logical_not(valid), j == pl.num_programs(1) - 1))
    def _():
        ys_ref[...] = jnp.zeros(ys_ref.shape, F32)


def moe_experts_grouped(xs, tile_expert, n_valid, w1, w3, w2, *, tm, tf):
    n_rows, d = xs.shape
    dff = w1.shape[2]
    nf = dff // tf

    def w13_idx(i, j, te, nv):
        return (te[i], 0, jnp.where(i < nv[0], j, nf - 1))

    def w2_idx(i, j, te, nv):
        return (te[i], jnp.where(i < nv[0], j, nf - 1), 0)

    grid_spec = pltpu.PrefetchScalarGridSpec(
        num_scalar_prefetch=2,
        grid=(n_rows // tm, nf),
        in_specs=[
            pl.BlockSpec((tm, d), lambda i, j, te, nv: (jnp.minimum(i, nv[0] - 1), 0)),
            pl.BlockSpec((1, d, tf), w13_idx),
            pl.BlockSpec((1, d, tf), w13_idx),
            pl.BlockSpec((1, tf, d), w2_idx),
        ],
        out_specs=pl.BlockSpec((tm, d), lambda i, j, te, nv: (i, 0)),
        scratch_shapes=[pltpu.VMEM((tm, d), BF16), pltpu.VMEM((tm, d), F32)],
    )
    return pl.pallas_call(
        _experts_kernel,
        grid_spec=grid_spec,
        out_shape=jax.ShapeDtypeStruct((n_rows, d), F32),
        compiler_params=_cparams(2),
        name="moe_experts_grouped",
    )(tile_expert, n_valid, xs, w1, w3, w2)


def _combine_kernel(pos_ref, pos_next_ref, x_ref, mod_ref, rt_ref, ys_ref, o_ref, y_ref, sem):
    i = pl.program_id(0)
    t = x_ref.shape[0]
    slot = i % 2

    def issue_tile(p_ref, s):
        def issue(r, carry):
            for k in range(2):
                pltpu.make_async_copy(ys_ref.at[pl.ds(p_ref[0, 0, k * t + r], 1), :],
                                      y_ref.at[s, k, pl.ds(r, 1), :], sem.at[s]).start()
            return carry

        lax.fori_loop(0, t, issue, 0, unroll=DMA_ISSUE_UNROLL)

    @pl.when(i == 0)
    def _():
        issue_tile(pos_ref, 0)

    @pl.when(i + 1 < pl.num_programs(0))
    def _():
        issue_tile(pos_next_ref, 1 - slot)

    for k in range(2):
        pltpu.make_async_copy(ys_ref.at[pl.ds(0, t), :], y_ref.at[slot, k], sem.at[slot]).wait()
    rt = rt_ref[...]
    mix = rt[:, R_W1:R_W1 + 1] * y_ref[slot, 0] + rt[:, R_W2:R_W2 + 1] * y_ref[slot, 1]
    o_ref[...] = x_ref[...] + mod_ref[0, 5:6, :] * mix


def moe_combine(x2d, mod, route, pos_tiles, ys, *, tm, tiles_per_mod):
    m, d = x2d.shape
    n_mod = mod.shape[0]
    mod_idx = (lambda i: (i // tiles_per_mod, 0, 0)) if n_mod > 1 else (lambda i: (0, 0, 0))
    n_tiles = m // tm
    return pl.pallas_call(
        _combine_kernel,
        grid=(n_tiles,),
        in_specs=[
            pl.BlockSpec((1, 1, 2 * tm), lambda i: (i, 0, 0), memory_space=pltpu.SMEM),
            pl.BlockSpec((1, 1, 2 * tm), lambda i: (jnp.minimum(i + 1, n_tiles - 1), 0, 0), memory_space=pltpu.SMEM),
            pl.BlockSpec((tm, d), lambda i: (i, 0)),
            pl.BlockSpec((1, 6, d), mod_idx),
            pl.BlockSpec((tm, 128), lambda i: (i, 0)),
            pl.BlockSpec(memory_space=pl.ANY),
        ],
        out_specs=pl.BlockSpec((tm, d), lambda i: (i, 0)),
        out_shape=jax.ShapeDtypeStruct((m, d), F32),
        scratch_shapes=[pltpu.VMEM((2, 2, tm, d), F32), pltpu.SemaphoreType.DMA((2,))],
        compiler_params=_cparams(1),
        name="moe_combine",
    )(pos_tiles, pos_tiles, x2d, mod, route, ys)


def _pos_tiles(pos1, pos2, tm):
    n = pos1.shape[0] // tm
    return jnp.concatenate([pos1.reshape(n, 1, tm), pos2.reshape(n, 1, tm)], axis=2)


def moe_sparse(x2d, mod, g, router_pad, w1, w3, w2, *, rows_per_mod):
    m, d = x2d.shape
    tr, td, tc, te = MOE_ROUTE_TM, MOE_DISPATCH_TM, MOE_COMBINE_TM, MOE_EXPERT_TM
    tiles_per_seq_row = rows_per_mod
    route, route_t, cnt = moe_route(x2d, mod, g, router_pad, tm=tr, tiles_per_mod=tiles_per_seq_row // tr)
    counts = cnt[0, 0:N_EXPERTS].astype(jnp.int32)
    group = ((counts + te - 1) // te) * te
    ends = jnp.cumsum(group)
    starts = ends - group
    field = lambda f: route_t[:, f, :].reshape(m).astype(jnp.int32)
    pos1 = starts[field(R_E1)] + field(R_RANK1)
    pos2 = starts[field(R_E2)] + field(R_RANK2)
    n_rows = 2 * m + N_EXPERTS * te
    n_tiles = n_rows // te
    tile_start = jnp.arange(n_tiles, dtype=jnp.int32)[:, None] * te
    tile_expert = jnp.minimum(jnp.sum((tile_start >= ends[None, :]).astype(jnp.int32), axis=1), N_EXPERTS - 1)
    n_valid = (ends[-1:] // te).astype(jnp.int32)
    assert td == te
    pad_rows = jnp.concatenate([starts + counts, ends, n_valid]).astype(jnp.int32).reshape(1, 2 * N_EXPERTS + 1)
    xs = moe_dispatch(x2d, mod, g, _pos_tiles(pos1, pos2, td), pad_rows, n_rows, tm=td,
                      tiles_per_mod=tiles_per_seq_row // td)
    ys = moe_experts_grouped(xs, tile_expert, n_valid, w1, w3, w2, tm=te, tf=MOE_EXPERT_TF)
    return moe_combine(x2d, mod, route, _pos_tiles(pos1, pos2, tc), ys, tm=tc, tiles_per_mod=tiles_per_seq_row // tc)


def _permute_w_in(w):
    f, c, p, q, kv, gts = w[:, 0:256], w[:, 256:768], w[:, 768:1024], w[:, 1024:1536], w[:, 1536:1792], w[:, 1792:]
    return jnp.concatenate([gts, q, c, p, f, kv], axis=1).astype(BF16)


def _layer_weights(layer, w_br_fourier, conv_dw, conv_b, conv_norm_g, w_br_conv, pool_w, pool_scale, w_br_pool,
                   w_br_attn, w_out):
    pw = jax.scipy.linalg.block_diag(*[pool_w[layer, i] for i in range(len(POOL_WINDOWS))])
    return {
        "wf": w_br_fourier[layer].astype(BF16), "wc": w_br_conv[layer].astype(BF16),
        "wp": w_br_pool[layer].astype(BF16), "wa": w_br_attn[layer].astype(BF16), "wo": w_out[layer].astype(BF16),
        "dw": conv_dw[layer], "cb": conv_b[layer].reshape(1, CONV_W), "cg": conv_norm_g[layer].reshape(1, CONV_W),
        "pw": pw.astype(BF16), "ps": pool_scale[layer].reshape(1, POOL_W),
    }


def _cast_slices(w, max_slices):
    w2 = w.reshape(-1, w.shape[-1])
    rows = w2.shape[0]
    n = 1
    while 2 * n <= max_slices and rows % (32 * n) == 0:
        n *= 2
    return w2.reshape(n, rows // n, w2.shape[1])


def kernel(x, c, ctx, c_ctx, w_mod, b_mod, norm1_g, norm2_g, w_in, w_br_fourier, conv_dw, conv_b, conv_norm_g,
           w_br_conv, pool_w, pool_scale, w_br_pool, q_norm_g, k_norm_g, w_br_attn, w_out, ffn_w1, ffn_w3, ffn_w2,
           moe_router, moe_w1, moe_w3, moe_w2):
    batch, seq, d = x.shape
    ctx_len = ctx.shape[1]
    depth = w_in.shape[0]
    rope = rope_tables(seq)

    c_rows = jnp.zeros((8, d), F32).at[0:batch].set(c).at[batch].set(c_ctx)
    mods = modulation_all(c_rows, w_mod, b_mod).reshape(depth, 8, 6, d)

    xl = x.reshape(batch * seq, d)
    xc = ctx.reshape(batch * ctx_len, d)
    for layer in range(depth):
        is_last = layer == depth - 1
        mod_l = mods[layer, 0:batch]
        mod_c = mods[layer, batch:batch + 1]
        w_in_l = _permute_w_in(w_in[layer])
        lw = _layer_weights(layer, w_br_fourier, conv_dw, conv_b, conv_norm_g, w_br_conv, pool_w, pool_scale,
                            w_br_pool, w_br_attn, w_out)

        proj_c, qc, ktc, vc = input_projection(xc, mod_c, norm1_g[layer], w_in_l, q_norm_g[layer], k_norm_g[layer],
                                               None, batch=batch, seq=ctx_len, tm=256)

        proj, q, kt, v = input_projection(xl, mod_l, norm1_g[layer], w_in_l, q_norm_g[layer], k_norm_g[layer], rope,
                                          batch=batch, seq=seq, tm=512)
        spread = (2.0 * 1.02 * HEAD_DIM * Q_SCALE) * jnp.max(jnp.abs(q_norm_g[layer])) * jnp.max(
            jnp.abs(k_norm_g[layer]))
        n_steps = batch * N_KV_HEADS * (seq // ATTN_TQ)
        mixer_w = ((ffn_w1, ffn_w3, ffn_w2) if layer % 2 == 0 else (moe_w1, moe_w3, moe_w2))
        mixer_w = tuple(w[layer // 2] for w in mixer_w)
        side = tuple(_cast_slices(w, n_steps) for w in mixer_w)
        attn, *side_bf16 = lax.cond(
            spread < STALE_MAX_EXP_LIMIT,
            lambda ops, sd: attention_stale_max(*ops, sd, batch=batch, seq_q=seq, tq=ATTN_TQ, tk=ATTN_TK),
            lambda ops, sd: (attention(*ops, batch=batch, seq_q=seq, tq=ATTN_TQ, tk=ATTN_TK),)
            + tuple(w.astype(BF16) for w in sd),
            (q, kt, v, ktc, vc), side)
        yf = fourier_mix(proj[:, P_OFF_F:P_OFF_F + FOURIER_W], batch=batch, seq=seq, n1=64, n2=seq // 64)
        xl = merge_branches(xl, mod_l, proj, yf, attn, lw, seq=seq, t=512)

        if not is_last:
            attn_c = attention(qc, None, None, ktc, vc, batch=batch, seq_q=ctx_len, tq=256, tk=ctx_len)
            yf_c = fourier_mix(proj_c[:, P_OFF_F:P_OFF_F + FOURIER_W], batch=batch, seq=ctx_len, n1=16,
                               n2=ctx_len // 16)
            xc = merge_branches(xc, mod_c, proj_c, yf_c, attn_c, lw, seq=ctx_len, t=256)

        j = layer // 2
        w1, w3, w2 = (wb.reshape(w.shape) for wb, w in zip(side_bf16, mixer_w))
        if layer % 2 == 0:
            xl = ffn_dense(xl, mod_l, norm2_g[layer], w1, w3, w2, tm=512, tiles_per_mod=seq // 512)
            if not is_last:
                xc = ffn_dense(xc, mod_c, norm2_g[layer], w1, w3, w2, tm=256, tiles_per_mod=1)
        else:
            router_pad = jnp.zeros((d, 128), F32).at[:, 0:N_EXPERTS].set(moe_router[j])
            xl = moe_sparse(xl, mod_l, norm2_g[layer], router_pad, w1, w3, w2, rows_per_mod=seq)
            if not is_last:
                xc = moe_sparse(xc, mod_c, norm2_g[layer], router_pad, w1, w3, w2, rows_per_mod=batch * ctx_len)
    return xl.reshape(batch, seq, d)
```

```python
import functools
import math

import numpy as np
import jax
import jax.numpy as jnp
from jax import lax
from jax.experimental import pallas as pl
from jax.experimental.pallas import tpu as pltpu

F32 = jnp.float32
BF16 = jnp.bfloat16

D_MODEL = 1024
GRID_W = 64
EPS = 1e-6
FOURIER_GW = 64
FOURIER_W = 256
CONV_W = 256
CONV_K = 31
CONV_HALF = CONV_K // 2
POOL_WINDOWS = (2, 4, 8, 16)
POOL_GW = 64
POOL_W = 256
HEAD_DIM = 64
N_KV_HEADS = 2
Q_PER_KV = 4
Q_W = 512
KV_W = 128
ROPE_THETA = 10000.0
N_EXPERTS = 8

P_OFF_Q = 4096
P_OFF_CP = 4608
P_OFF_F = 5376
P_OFF_KV = 5632
CP_W = 2 * CONV_W + POOL_W

Q_SCALE = (HEAD_DIM ** -0.5) * math.log2(math.e)

ATTN_TQ = 512
ATTN_TK = 1024
STALE_MAX_EXP_LIMIT = 64.0

MOE_ROUTE_TM = 512
MOE_DISPATCH_TM = 512
MOE_COMBINE_TM = 512
MOE_EXPERT_TM = 512
MOE_EXPERT_TF = 1792
DMA_ISSUE_UNROLL = 8

CONV_ROWS = 64
HALO = 16
VMEM_LIMIT = 56 * 1024 * 1024


def _cparams(n_axes):
    return pltpu.CompilerParams(dimension_semantics=("arbitrary",) * n_axes, vmem_limit_bytes=VMEM_LIMIT)


def _sigmoid(v):
    return 0.5 * jnp.tanh(0.5 * v) + 0.5


def _silu(v):
    return v * _sigmoid(v)


def _norm_mod(x, g, shift, scale):
    ms = jnp.mean(x * x, axis=-1, keepdims=True)
    return x * lax.rsqrt(ms + EPS) * g * (1.0 + scale) + shift


def _mod_kernel(c_ref, w_ref, b_ref, o_ref):
    s = _silu(c_ref[...])
    o_ref[0] = jnp.dot(s, w_ref[0], preferred_element_type=F32, precision=lax.Precision.HIGHEST) + b_ref[0]


def modulation_all(c_rows, w_mod, b_mod):
    n_layers, d, n = w_mod.shape
    tn = 1536
    return pl.pallas_call(
        _mod_kernel,
        grid=(n_layers, n // tn),
        in_specs=[
            pl.BlockSpec((8, d), lambda l, j: (0, 0)),
            pl.BlockSpec((1, d, tn), lambda l, j: (l, 0, j)),
            pl.BlockSpec((1, 1, tn), lambda l, j: (l, 0, j)),
        ],
        out_specs=pl.BlockSpec((1, 8, tn), lambda l, j: (l, 0, j)),
        out_shape=jax.ShapeDtypeStruct((n_layers, 8, n), F32),
        compiler_params=_cparams(2),
        name="modulation",
    )(c_rows, w_mod, b_mod.reshape(n_layers, 1, n))


def _inproj_kernel(*refs, chunks, use_rope):
    if use_rope:
        x_ref, mod_ref, g_ref, w_ref, gq_ref, gk_ref, ones_ref, cos_ref, sin_ref, o_ref, qo_ref, kt_ref, v_ref = refs
        cos, sin = cos_ref[...], sin_ref[...]
    else:
        x_ref, mod_ref, g_ref, w_ref, gq_ref, gk_ref, ones_ref, o_ref, qo_ref, kt_ref, v_ref = refs
        cos = sin = None
    h = _norm_mod(x_ref[...], g_ref[...], mod_ref[0, 0:1, :], mod_ref[0, 1:2, :]).astype(BF16)
    heavy_first = sorted(chunks, key=lambda c: (c[0] not in (P_OFF_Q, P_OFF_KV), c[0] != P_OFF_CP, c[0]))
    for c0, cw in heavy_first:
        r = jnp.dot(h, w_ref[:, c0:c0 + cw], preferred_element_type=F32)
        if c0 + cw <= P_OFF_Q:
            r = _sigmoid(r)
        elif c0 == P_OFF_CP:
            sg = _sigmoid(r[:, CONV_W:2 * CONV_W])
            r = jnp.concatenate([r[:, 0:CONV_W] * sg, sg], axis=1)
        elif c0 == P_OFF_Q:
            _q_epilogue(r, gq_ref[...], ones_ref[...], cos, sin, qo_ref)
        elif c0 == P_OFF_KV:
            _kv_epilogue(r, gk_ref[...], ones_ref[...], cos, sin, kt_ref, v_ref)
        o_ref[:, c0:c0 + cw] = r.astype(o_ref.dtype)


def input_projection(x2d, mod, g, w_bf16, gq, gk, rope, *, batch, seq, tm):
    m, d = x2d.shape
    n = w_bf16.shape[1]
    tps = seq // tm
    chunks = tuple((c0, min(512, n - c0)) for c0 in range(0, n, 512))
    assert P_OFF_Q % 512 == 0 and {(P_OFF_CP, 2 * CONV_W), (P_OFF_Q, Q_W), (P_OFF_KV, 2 * KV_W)} <= set(chunks)
    use_rope = rope is not None
    n_mod = mod.shape[0]
    mod_idx = (lambda i: (i // tps, 0, 0)) if n_mod > 1 else (lambda i: (0, 0, 0))
    const = lambda i: (0, 0)
    ones_bd = jnp.asarray(np.kron(np.eye(2, dtype=np.float32), np.ones((64, 64), np.float32)), BF16)
    in_specs = [
        pl.BlockSpec((tm, d), lambda i: (i, 0)),
        pl.BlockSpec((1, 6, d), mod_idx),
        pl.BlockSpec((1, d), const),
        pl.BlockSpec((d, n), const, pipeline_mode=pl.Buffered(1)),
        pl.BlockSpec((1, 128), const),
        pl.BlockSpec((1, 128), const),
        pl.BlockSpec((128, 128), const),
    ]
    args = [x2d, mod, g.reshape(1, d), w_bf16, jnp.tile(gq, 2).reshape(1, 128), jnp.tile(gk, 2).reshape(1, 128), ones_bd]
    if use_rope:
        in_specs += [pl.BlockSpec((tm, 128), lambda i: (i % tps, 0))] * 2
        args += list(rope)
    return pl.pallas_call(
        functools.partial(_inproj_kernel, chunks=chunks, use_rope=use_rope),
        grid=(m // tm,),
        in_specs=in_specs,
        out_specs=[
            pl.BlockSpec((tm, n), lambda i: (i, 0)),
            pl.BlockSpec((1, N_KV_HEADS, 256, tm), lambda i: (i // tps, 0, 0, i % tps)),
            pl.BlockSpec((1, N_KV_HEADS, tm, 256), lambda i: (i // tps, 0, i % tps, 0)),
            pl.BlockSpec((1, N_KV_HEADS, 128, tm), lambda i: (i // tps, 0, 0, i % tps)),
        ],
        out_shape=[
            jax.ShapeDtypeStruct((m, n), BF16),
            jax.ShapeDtypeStruct((batch, N_KV_HEADS, 256, seq), BF16),
            jax.ShapeDtypeStruct((batch, N_KV_HEADS, seq, 256), BF16),
            jax.ShapeDtypeStruct((batch, N_KV_HEADS, 128, seq), BF16),
        ],
        compiler_params=_cparams(1),
        name="input_projection",
    )(*args)


def _seg_sum64(v, ones_bd):
    hi = v.astype(BF16)
    lo = (v - hi.astype(F32)).astype(BF16)
    return (jnp.dot(hi, ones_bd, preferred_element_type=F32) + jnp.dot(lo, ones_bd, preferred_element_type=F32))


def _head_norm_rope(x, g, ones_bd, cos, sin, low_mask):
    y = x * lax.rsqrt(_seg_sum64(x * x, ones_bd) * (1.0 / HEAD_DIM) + EPS) * g
    if cos is None:
        return y
    partner = jnp.where(low_mask, pltpu.roll(y, 128 - 16, axis=1), pltpu.roll(y, 16, axis=1))
    return y * cos + partner * sin


def _rope_low_mask(t):
    return (lax.broadcasted_iota(jnp.int32, (t, 128), 1) % 32) < 16


def _q_epilogue(rq, gq, ones_bd, cos, sin, qo_ref):
    low_mask = _rope_low_mask(rq.shape[0])
    for c in range(Q_W // 128):
        yq = _head_norm_rope(rq[:, 128 * c:128 * (c + 1)], gq, ones_bd, cos, sin, low_mask) * Q_SCALE
        qo_ref[0, c // 2, 128 * (c % 2):128 * (c % 2 + 1), :] = yq.T.astype(BF16)


def _kv_epilogue(rkv, gk, ones_bd, cos, sin, kt_ref, v_ref):
    t = rkv.shape[0]
    yk = _head_norm_rope(rkv[:, 0:128], gk, ones_bd, cos, sin, _rope_low_mask(t))
    ykr = pltpu.roll(yk, 64, axis=1)
    first = lax.broadcasted_iota(jnp.int32, (t, 128), 1) < 64
    k0 = jnp.where(first, yk, ykr).astype(BF16)
    k1 = jnp.where(first, ykr, yk).astype(BF16)
    kt_ref[0, 0] = jnp.concatenate([k0, k0], axis=1)
    kt_ref[0, 1] = jnp.concatenate([k1, k1], axis=1)
    vt = rkv[:, 128:256].T
    ones = jnp.ones((HEAD_DIM, t), F32)
    for h in range(N_KV_HEADS):
        v_ref[0, h] = jnp.concatenate([vt[64 * h:64 * (h + 1), :], ones], axis=0).astype(BF16)


def rope_tables(seq):
    n_freq = HEAD_DIM // 4
    freqs = ROPE_THETA ** (-jnp.arange(n_freq, dtype=F32) / n_freq)
    t = jnp.arange(seq)
    row = (t // GRID_W).astype(F32)
    col = (t % GRID_W).astype(F32)
    ang_r = row[:, None] * freqs
    ang_c = col[:, None] * freqs
    cos = jnp.concatenate([jnp.cos(ang_r)] * 2 + [jnp.cos(ang_c)] * 2, axis=1)
    sin = jnp.concatenate([-jnp.sin(ang_r), jnp.sin(ang_r), -jnp.sin(ang_c), jnp.sin(ang_c)], axis=1)
    return jnp.tile(cos, (1, 2)), jnp.tile(sin, (1, 2))


def _attn_kernel(*refs, tq, tk, nk, tail):
    refs = list(refs)
    qt_ref = refs.pop(0)
    k_ref, vt_ref = (refs.pop(0), refs.pop(0)) if nk else (None, None)
    kc_ref, vtc_ref = (refs.pop(0), refs.pop(0)) if tail else (None, None)
    o_ref, qs_ref, s0, s1, p0, p1, a0, a1, mx0, mx1, m_ref, acc_ref = refs
    s_bufs, p_bufs, a_bufs, mx_bufs = (s0, s1), (p0, p1), (a0, a1), (mx0, mx1)
    n_blocks = nk + (1 if tail else 0)

    _attn_stack_queries(qt_ref, qs_ref, tq)
    m_ref[...] = jnp.full(m_ref.shape, -jnp.inf, F32)
    acc_ref[...] = jnp.zeros(acc_ref.shape, F32)

    def block(t):
        if isinstance(t, int) and t >= nk:
            return kc_ref[0, 0], vtc_ref[0, 0], tail
        off = t * tk if isinstance(t, int) else pl.multiple_of(t * tk, tk)
        return k_ref[0, 0, pl.ds(off, tk), :], vt_ref[0, 0, :, pl.ds(off, tk)], tk

    def scores(t, slot):
        k_rows, _, n = block(t)
        s = jnp.dot(k_rows, qs_ref[...], preferred_element_type=F32)
        s_bufs[slot][0:n, :] = s
        mx_bufs[slot][...] = jnp.max(s, axis=0, keepdims=True)

    def numerators(n, slot):
        s_ref, p_ref, a_ref = s_bufs[slot], p_bufs[slot], a_bufs[slot]
        for c0 in range(0, Q_PER_KV * tq, 128):
            cols = slice(c0, c0 + 128)
            m_old = m_ref[:, cols]
            m_new = jnp.maximum(m_old, mx_bufs[slot][:, cols])
            a_ref[:, cols] = jnp.exp2(m_old - m_new)
            p_ref[0:n, cols] = jnp.exp2(s_ref[0:n, cols] - m_new).astype(BF16)
            m_ref[:, cols] = m_new

    def weighted_sum(t, slot):
        _, vt, n = block(t)
        pv = jnp.dot(vt, p_bufs[slot][0:n, :], preferred_element_type=F32)
        acc_ref[...] = a_bufs[slot][...] * acc_ref[...] + pv

    def rows_of(t):
        return tk if t < nk else tail

    def step(t, slot, n_mid):
        scores(t, slot)
        numerators(n_mid, 1 - slot)
        weighted_sum(t - 2, slot)

    scores(0, 0)
    if n_blocks > 1:
        scores(1, 1)
        numerators(rows_of(0), 0)
        n_pairs = max(nk - 2, 0) // 2

        def pair(i, carry):
            t = 2 + 2 * i
            step(t, 0, tk)
            step(t + 1, 1, tk)
            return carry

        if n_pairs:
            lax.fori_loop(0, n_pairs, pair, 0)
        for t in range(2 + 2 * n_pairs, n_blocks):
            step(t, t % 2, rows_of(t - 1))
        last = n_blocks - 1
        numerators(rows_of(last), last % 2)
        weighted_sum(last - 1, (last - 1) % 2)
        weighted_sum(last, last % 2)
    else:
        numerators(rows_of(0), 0)
        weighted_sum(0, 0)

    _attn_write_output(acc_ref, o_ref, tq)


def _attn_write_output(acc_ref, o_ref, tq):
    acc = acc_ref[...]
    ot = acc[0:HEAD_DIM, :] / acc[HEAD_DIM:2 * HEAD_DIM, :]
    for half in range(2):
        pair_t = jnp.concatenate([ot[:, (2 * half) * tq:(2 * half + 1) * tq],
                                  ot[:, (2 * half + 1) * tq:(2 * half + 2) * tq]], axis=0)
        o_ref[:, 128 * half:128 * (half + 1)] = pair_t.T.astype(o_ref.dtype)


def _attn_stack_queries(qt_ref, qs_ref, tq):
    row_group = lax.broadcasted_iota(jnp.int32, (256, tq), 0) // HEAD_DIM
    qt = qt_ref[0, 0]
    for g in range(Q_PER_KV):
        qs_ref[:, g * tq:(g + 1) * tq] = jnp.where(row_group == g, qt, jnp.zeros_like(qt))


def _attn_stale_max_kernel(*refs, tq, tk, nk, n_side):
    qt_ref, k_ref, vt_ref, kc_ref, vtc_ref = refs[:5]
    side_in = refs[5:5 + n_side]
    o_ref = refs[5 + n_side]
    side_out = refs[6 + n_side:6 + 2 * n_side]
    qs_ref, p0, p1, f0, f1, m_ref, acc_ref = refs[6 + 2 * n_side:]
    for src, dst in zip(side_in, side_out):
        dst[...] = src[...].astype(dst.dtype)
    p_bufs, f_bufs = (p0, p1), (f0, f1)
    _attn_stack_queries(qt_ref, qs_ref, tq)

    s = jnp.dot(kc_ref[0, 0], qs_ref[...], preferred_element_type=F32)
    m0 = jnp.max(s, axis=0, keepdims=True)
    m_ref[...] = m0
    acc_ref[...] = jnp.dot(vtc_ref[0, 0], jnp.exp2(s - m0).astype(BF16), preferred_element_type=F32)

    def numerators(t, slot):
        off = t * tk if isinstance(t, int) else pl.multiple_of(t * tk, tk)
        s = jnp.dot(k_ref[0, 0, pl.ds(off, tk), :], qs_ref[...], preferred_element_type=F32)
        m_old = m_ref[...]
        p_bufs[slot][...] = jnp.exp2(s - m_old).astype(BF16)
        m_new = jnp.maximum(m_old, jnp.max(s, axis=0, keepdims=True))
        f_bufs[slot][...] = jnp.exp2(m_old - m_new)
        m_ref[...] = m_new

    def weighted_sum(t, slot):
        off = t * tk if isinstance(t, int) else pl.multiple_of(t * tk, tk)
        pv = jnp.dot(vt_ref[0, 0, :, pl.ds(off, tk)], p_bufs[slot][...], preferred_element_type=F32)
        acc_ref[...] = (acc_ref[...] + pv) * f_bufs[slot][...]

    def step(t, slot):
        numerators(t, slot)
        weighted_sum(t - 1, 1 - slot)

    numerators(0, 0)
    n_pairs = (nk - 1) // 2

    def pair(i, carry):
        t = 1 + 2 * i
        step(t, 1)
        step(t + 1, 0)
        return carry

    if n_pairs:
        lax.fori_loop(0, n_pairs, pair, 0)
    for t in range(1 + 2 * n_pairs, nk):
        step(t, t % 2)
    weighted_sum(nk - 1, (nk - 1) % 2)
    _attn_write_output(acc_ref, o_ref, tq)


def attention_stale_max(qt, k4, vt1, k4_tail, vt1_tail, side=(), *, batch, seq_q, tq, tk):
    nq = seq_q // tq
    lanes = Q_PER_KV * tq
    lk = k4.shape[2]
    tail = k4_tail.shape[2]
    n_steps = batch * N_KV_HEADS * nq
    assert all(n_steps % w.shape[0] == 0 for w in side)

    def side_spec(w):
        repeat = n_steps // w.shape[0]
        return pl.BlockSpec((1,) + w.shape[1:], lambda b, h, i: (((b * N_KV_HEADS + h) * nq + i) // repeat, 0, 0))

    side_specs = [side_spec(w) for w in side]
    outs = pl.pallas_call(
        functools.partial(_attn_stale_max_kernel, tq=tq, tk=tk, nk=lk // tk, n_side=len(side)),
        grid=(batch, N_KV_HEADS, nq),
        in_specs=[
            pl.BlockSpec((1, 1, 256, tq), lambda b, h, i: (b, h, 0, i)),
            pl.BlockSpec((1, 1, lk, 256), lambda b, h, i: (b, h, 0, 0)),
            pl.BlockSpec((1, 1, 128, lk), lambda b, h, i: (b, h, 0, 0)),
            pl.BlockSpec((1, 1, tail, 256), lambda b, h, i: (b, h, 0, 0)),
            pl.BlockSpec((1, 1, 128, tail), lambda b, h, i: (b, h, 0, 0)),
        ] + side_specs,
        out_specs=[pl.BlockSpec((tq, 256), lambda b, h, i: (b * nq + i, h))] + side_specs,
        out_shape=[jax.ShapeDtypeStruct((batch * seq_q, Q_W), BF16)]
        + [jax.ShapeDtypeStruct(w.shape, BF16) for w in side],
        scratch_shapes=[
            pltpu.VMEM((256, lanes), BF16),
            pltpu.VMEM((tk, lanes), BF16), pltpu.VMEM((tk, lanes), BF16),
            pltpu.VMEM((1, lanes), F32), pltpu.VMEM((1, lanes), F32),
            pltpu.VMEM((1, lanes), F32),
            pltpu.VMEM((2 * HEAD_DIM, lanes), F32),
        ],
        compiler_params=_cparams(3),
        name="attention_stale_max",
    )(qt, k4, vt1, k4_tail, vt1_tail, *side)
    return tuple(outs)


def attention(qt, k4, vt1, k4_tail, vt1_tail, *, batch, seq_q, tq, tk):
    nq = seq_q // tq
    lanes = Q_PER_KV * tq
    nk = 0 if k4 is None else k4.shape[2] // tk
    tail = 0 if k4_tail is None else k4_tail.shape[2]
    buf_rows = max(tk if nk else 0, tail)
    in_specs = [pl.BlockSpec((1, 1, 256, tq), lambda b, h, i: (b, h, 0, i))]
    args = [qt]
    if nk:
        lk = k4.shape[2]
        in_specs += [pl.BlockSpec((1, 1, lk, 256), lambda b, h, i: (b, h, 0, 0)),
                     pl.BlockSpec((1, 1, 128, lk), lambda b, h, i: (b, h, 0, 0))]
        args += [k4, vt1]
    if tail:
        in_specs += [pl.BlockSpec((1, 1, tail, 256), lambda b, h, i: (b, h, 0, 0)),
                     pl.BlockSpec((1, 1, 128, tail), lambda b, h, i: (b, h, 0, 0))]
        args += [k4_tail, vt1_tail]
    return pl.pallas_call(
        functools.partial(_attn_kernel, tq=tq, tk=tk, nk=nk, tail=tail),
        grid=(batch, N_KV_HEADS, nq),
        in_specs=in_specs,
        out_specs=pl.BlockSpec((tq, 256), lambda b, h, i: (b * nq + i, h)),
        out_shape=jax.ShapeDtypeStruct((batch * seq_q, Q_W), BF16),
        scratch_shapes=[
            pltpu.VMEM((256, lanes), BF16),
            pltpu.VMEM((buf_rows, lanes), F32), pltpu.VMEM((buf_rows, lanes), F32),
            pltpu.VMEM((buf_rows, lanes), BF16), pltpu.VMEM((buf_rows, lanes), BF16),
            pltpu.VMEM((1, lanes), F32), pltpu.VMEM((1, lanes), F32),
            pltpu.VMEM((1, lanes), F32), pltpu.VMEM((1, lanes), F32),
            pltpu.VMEM((1, lanes), F32),
            pltpu.VMEM((2 * HEAD_DIM, lanes), F32),
        ],
        compiler_params=_cparams(3),
        name="attention",
    )(*args)


def _dft_cs(n):
    k = np.arange(n)
    ang = 2.0 * np.pi * ((k[:, None] * k[None, :]) % n) / n
    return np.cos(ang), np.sin(ang)


def _fft1_kernel(x_ref, f_ref, c_ref, s_ref, o_ref, *, n1):
    y = jnp.dot(f_ref[...], x_ref[0], preferred_element_type=F32)
    yr, yi = y[:n1], y[n1:]
    c, s = c_ref[...], s_ref[...]
    o_ref[0, 0] = (yr * c + yi * s).astype(o_ref.dtype)
    o_ref[0, 1] = (yi * c - yr * s).astype(o_ref.dtype)


def _fft2_kernel(y_ref, f_ref, bc_ref, bs_ref, o_ref, *, n2, kb):
    for j in range(kb):
        y2 = jnp.concatenate([y_ref[0, 0, j], y_ref[0, 1, j]], axis=0)
        x2 = jnp.dot(f_ref[...], y2, preferred_element_type=F32)
        xr = x2[:n2].astype(BF16)
        xi = x2[n2:].astype(BF16)
        z = (jnp.dot(xr, bc_ref[...], preferred_element_type=F32) + jnp.dot(xi, bs_ref[...], preferred_element_type=F32))
        o_ref[0, j] = z.astype(o_ref.dtype)


def fourier_mix(u, *, batch, seq, n1, n2):
    cw = u.shape[1]
    lanes = n2 * cw
    tl = min(lanes, 4096)
    c1, s1 = _dft_cs(n1)
    f1 = jnp.asarray(np.concatenate([c1, -s1], axis=0), BF16)
    k1 = np.arange(n1)[:, None]
    t2 = np.arange(n2)[None, :]
    ang = 2.0 * np.pi * ((k1 * t2) % seq) / seq
    twc = jnp.asarray(np.repeat(np.cos(ang), cw, axis=1), F32)
    tws = jnp.asarray(np.repeat(np.sin(ang), cw, axis=1), F32)
    x2 = u.reshape(batch, n1, lanes)
    yp = pl.pallas_call(
        functools.partial(_fft1_kernel, n1=n1),
        grid=(batch, lanes // tl),
        in_specs=[
            pl.BlockSpec((1, n1, tl), lambda b, j: (b, 0, j)),
            pl.BlockSpec((2 * n1, n1), lambda b, j: (0, 0)),
            pl.BlockSpec((n1, tl), lambda b, j: (0, j)),
            pl.BlockSpec((n1, tl), lambda b, j: (0, j)),
        ],
        out_specs=pl.BlockSpec((1, 2, n1, tl), lambda b, j: (b, 0, 0, j)),
        out_shape=jax.ShapeDtypeStruct((batch, 2, n1, lanes), BF16),
        compiler_params=_cparams(2),
        name="fft_stage1",
    )(x2, f1, twc, tws)

    c2, s2 = _dft_cs(n2)
    f2 = jnp.asarray(np.block([[c2, s2], [-s2, c2]]), BF16)
    cg, sg = _dft_cs(FOURIER_GW)
    norm = 1.0 / math.sqrt(seq * FOURIER_GW)
    bdc = jnp.asarray(np.kron(np.eye(cw // FOURIER_GW), cg) * norm, BF16)
    bds = jnp.asarray(np.kron(np.eye(cw // FOURIER_GW), sg) * norm, BF16)
    kb = min(n1, 16)
    y5 = yp.reshape(batch, 2, n1, n2, cw)
    z = pl.pallas_call(
        functools.partial(_fft2_kernel, n2=n2, kb=kb),
        grid=(batch, n1 // kb),
        in_specs=[
            pl.BlockSpec((1, 2, kb, n2, cw), lambda b, j: (b, 0, j, 0, 0)),
            pl.BlockSpec((2 * n2, 2 * n2), lambda b, j: (0, 0)),
            pl.BlockSpec((cw, cw), lambda b, j: (0, 0)),
            pl.BlockSpec((cw, cw), lambda b, j: (0, 0)),
        ],
        out_specs=pl.BlockSpec((1, kb, n2, cw), lambda b, j: (b, j, 0, 0)),
        out_shape=jax.ShapeDtypeStruct((batch, n1, n2, cw), BF16),
        compiler_params=_cparams(2),
        name="fft_stage2",
    )(y5, f2, bdc, bds)
    return z.transpose(0, 2, 1, 3).reshape(batch * seq, cw)


def _merge_kernel(x_ref, mod_ref, gate_ref, cp_ref, cpp_ref, cpn_ref, yf_ref, at_ref,
                  wf_ref, wc_ref, wp_ref, wa_ref, wo_ref, dw_ref, cb_ref, cg_ref, pw_ref, ps_ref, band_ref, icnt_ref,
                  o_ref, ybuf, xbuf, ysh, cacc, *, t, tps):
    i = pl.program_id(0)
    pos_tile = i % tps
    keep_prev = jnp.where(pos_tile != 0, 1.0, 0.0).astype(F32)
    keep_next = jnp.where(pos_tile != tps - 1, 1.0, 0.0).astype(F32)

    def glu(blk):
        return blk[:, 0:CONV_W].astype(F32)

    cp, cpp, cpn = cp_ref[...], cpp_ref[...], cpn_ref[...]
    ybuf[0:HALO, :] = glu(cpp) * keep_prev
    ybuf[HALO:HALO + t, :] = glu(cp)
    ybuf[HALO + t:HALO + t + HALO, :] = glu(cpn) * keep_next
    xbuf[0:HALO, :] = cpp[:, 2 * CONV_W:] * keep_prev.astype(BF16)
    xbuf[HALO:HALO + t, :] = cp[:, 2 * CONV_W:]
    xbuf[HALO + t:HALO + t + HALO, :] = cpn[:, 2 * CONV_W:] * keep_next.astype(BF16)

    n_sh = t + 2 * HALO - 8
    for b in range(1, 8):
        ysh[b - 1, 0:n_sh, :] = ybuf[pl.ds(b, n_sh), :]
    for r0 in range(0, t, CONV_ROWS):
        part = jnp.zeros((CONV_ROWS, CONV_W), F32)
        for k in range(CONV_K):
            a, b = divmod(HALO - CONV_HALF + k, 8)
            src = ybuf if b == 0 else ysh.at[b - 1]
            part = part + dw_ref[k:k + 1, :] * src[8 * a + r0:8 * a + r0 + CONV_ROWS, :]
        cacc[r0:r0 + CONV_ROWS, :] = part + cb_ref[...]
    acc = cacc[...]
    ms = jnp.mean(acc * acc, axis=-1, keepdims=True)
    conv_out = _silu(acc * lax.rsqrt(ms + EPS) * cg_ref[...]).astype(BF16)

    pb = band_ref.shape[1]
    grp = lax.broadcasted_iota(jnp.int32, (pb + 2 * HALO, POOL_W), 1) // POOL_GW
    parts = []
    for r0 in range(0, t, pb):
        xw = xbuf[r0:r0 + pb + 2 * HALO, :]
        wsum = jnp.zeros((pb, POOL_W), F32)
        for gi in range(len(POOL_WINDOWS)):
            wsum = wsum + jnp.dot(band_ref[gi], jnp.where(grp == gi, xw, jnp.zeros_like(xw)),
                                  preferred_element_type=F32)
        x0 = xbuf[HALO + r0:HALO + r0 + pb, :].astype(F32)
        parts.append((wsum * icnt_ref[0, r0:r0 + pb, :] - x0).astype(BF16))
    pool_in = parts[0] if len(parts) == 1 else jnp.concatenate(parts, axis=0)
    pool_out = (jnp.dot(pool_in, pw_ref[...], preferred_element_type=F32) * ps_ref[...]).astype(BF16)

    def gate(b):
        return gate_ref[:, b * D_MODEL:(b + 1) * D_MODEL].astype(F32)

    merged = gate(0) * jnp.dot(yf_ref[...], wf_ref[...], preferred_element_type=F32)
    merged = merged + gate(1) * jnp.dot(conv_out, wc_ref[...], preferred_element_type=F32)
    merged = merged + gate(2) * jnp.dot(pool_out, wp_ref[...], preferred_element_type=F32)
    merged = merged + gate(3) * jnp.dot(at_ref[...], wa_ref[...], preferred_element_type=F32)
    out = jnp.dot(merged.astype(BF16), wo_ref[...], preferred_element_type=F32)
    o_ref[...] = x_ref[...] + mod_ref[0, 2:3, :] * out


def _pool_tables(seq, t):
    pb = min(t, 256)
    r = np.arange(pb)[:, None]
    j = np.arange(pb + 2 * HALO)[None, :]
    band = np.stack([((j >= r + HALO - w // 2) & (j < r + HALO + w // 2)) for w in POOL_WINDOWS]).astype(np.float32)
    half = np.repeat(np.array(POOL_WINDOWS) // 2, POOL_GW)[None, :]
    rows = np.arange(t)[:, None]
    icnt = []
    for first, last in ((0, 0), (1, 0), (0, 1), (1, 1)):
        if first and last:
            pos, length = rows, t
        elif first:
            pos, length = rows, 2 * t + 2 * HALO
        elif last:
            pos, length = rows + seq - t, seq
        else:
            pos, length = rows + t + 2 * HALO, 4 * t
        cnt = np.minimum(pos + half, length) - np.maximum(pos - half, 0)
        icnt.append(1.0 / cnt)
    return jnp.asarray(band, BF16), jnp.asarray(np.stack(icnt), F32)


def merge_branches(x2d, mod, proj, yf, attn, lw, *, seq, t):
    m, d = x2d.shape
    tps = seq // t
    hb = t // HALO
    n_halo = m // HALO
    n_mod = mod.shape[0]
    mod_idx = (lambda i: (i // tps, 0, 0)) if n_mod > 1 else (lambda i: (0, 0, 0))
    const = lambda i: (0, 0)
    cp_blk = P_OFF_CP // CP_W
    band, icnt = _pool_tables(seq, t)
    pb = band.shape[1]

    def icnt_idx(i):
        pos_tile = i % tps
        return ((pos_tile == 0).astype(jnp.int32) + 2 * (pos_tile == tps - 1).astype(jnp.int32), 0, 0)

    return pl.pallas_call(
        functools.partial(_merge_kernel, t=t, tps=tps),
        grid=(m // t,),
        in_specs=[
            pl.BlockSpec((t, d), lambda i: (i, 0)),
            pl.BlockSpec((1, 6, d), mod_idx),
            pl.BlockSpec((t, 4 * d), lambda i: (i, 0)),
            pl.BlockSpec((t, CP_W), lambda i: (i, cp_blk)),
            pl.BlockSpec((HALO, CP_W), lambda i: (jnp.maximum(i * hb - 1, 0), cp_blk)),
            pl.BlockSpec((HALO, CP_W), lambda i: (jnp.minimum((i + 1) * hb, n_halo - 1), cp_blk)),
            pl.BlockSpec((t, FOURIER_W), lambda i: (i, 0)),
            pl.BlockSpec((t, Q_W), lambda i: (i, 0)),
            pl.BlockSpec((FOURIER_W, d), const),
            pl.BlockSpec((CONV_W, d), const),
            pl.BlockSpec((POOL_W, d), const),
            pl.BlockSpec((Q_W, d), const),
            pl.BlockSpec((d, d), const),
            pl.BlockSpec((CONV_K, CONV_W), const),
            pl.BlockSpec((1, CONV_W), const),
            pl.BlockSpec((1, CONV_W), const),
            pl.BlockSpec((POOL_W, POOL_W), const),
            pl.BlockSpec((1, POOL_W), const),
            pl.BlockSpec((len(POOL_WINDOWS), pb, pb + 2 * HALO), lambda i: (0, 0, 0)),
            pl.BlockSpec((1, t, POOL_W), icnt_idx),
        ],
        out_specs=pl.BlockSpec((t, d), lambda i: (i, 0)),
        out_shape=jax.ShapeDtypeStruct((m, d), F32),
        scratch_shapes=[pltpu.VMEM((t + 2 * HALO, CONV_W), F32), pltpu.VMEM((t + 2 * HALO, POOL_W), BF16),
                        pltpu.VMEM((7, t + 2 * HALO, CONV_W), F32), pltpu.VMEM((t, CONV_W), F32)],
        compiler_params=_cparams(1),
        name="merge_branches",
    )(x2d, mod, proj, proj, proj, proj, yf, attn,
      lw["wf"], lw["wc"], lw["wp"], lw["wa"], lw["wo"], lw["dw"], lw["cb"], lw["cg"], lw["pw"], lw["ps"], band, icnt)


def _ffn_kernel(x_ref, mod_ref, g_ref, w1_ref, w3_ref, w2_ref, o_ref, *, chunks):
    x = x_ref[...]
    h = _norm_mod(x, g_ref[...], mod_ref[0, 3:4, :], mod_ref[0, 4:5, :]).astype(BF16)
    acc = jnp.zeros(x.shape, F32)
    for c0, cw in chunks:
        a = jnp.dot(h, w1_ref[:, c0:c0 + cw], preferred_element_type=F32)
        b = jnp.dot(h, w3_ref[:, c0:c0 + cw], preferred_element_type=F32)
        acc = acc + jnp.dot((_silu(a) * b).astype(BF16), w2_ref[c0:c0 + cw, :], preferred_element_type=F32)
    o_ref[...] = x + mod_ref[0, 5:6, :] * acc


def ffn_dense(x2d, mod, g, w1, w3, w2, *, tm, tiles_per_mod):
    m, d = x2d.shape
    dff = w1.shape[1]
    chunks = tuple((c0, min(1024, dff - c0)) for c0 in range(0, dff, 1024))
    n_mod = mod.shape[0]
    mod_idx = (lambda i: (i // tiles_per_mod, 0, 0)) if n_mod > 1 else (lambda i: (0, 0, 0))
    const = lambda i: (0, 0)
    return pl.pallas_call(
        functools.partial(_ffn_kernel, chunks=chunks),
        grid=(m // tm,),
        in_specs=[
            pl.BlockSpec((tm, d), lambda i: (i, 0)),
            pl.BlockSpec((1, 6, d), mod_idx),
            pl.BlockSpec((1, d), const),
            pl.BlockSpec((d, dff), const, pipeline_mode=pl.Buffered(1)),
            pl.BlockSpec((d, dff), const, pipeline_mode=pl.Buffered(1)),
            pl.BlockSpec((dff, d), const, pipeline_mode=pl.Buffered(1)),
        ],
        out_specs=pl.BlockSpec((tm, d), lambda i: (i, 0)),
        out_shape=jax.ShapeDtypeStruct((m, d), F32),
        compiler_params=_cparams(1),
        name="ffn_dense",
    )(x2d, mod, g.reshape(1, d), w1, w3, w2)


def _top2(logits):
    t = logits.shape[0]
    lane = lax.broadcasted_iota(jnp.int32, (t, 128), 1).astype(F32)
    neg = jnp.float32(-jnp.inf)
    lg = jnp.where(lane < N_EXPERTS, logits, neg)
    v1 = jnp.max(lg, axis=-1, keepdims=True)
    i1 = jnp.min(jnp.where(lg == v1, lane, 128.0), axis=-1, keepdims=True)
    lg2 = jnp.where(lane == i1, neg, lg)
    v2 = jnp.max(lg2, axis=-1, keepdims=True)
    i2 = jnp.min(jnp.where(lg2 == v2, lane, 128.0), axis=-1, keepdims=True)
    e2 = jnp.exp(v2 - v1)
    return i1, i2, 1.0 / (1.0 + e2), e2 / (1.0 + e2)


R_E1, R_E2, R_W1, R_W2, R_RANK1, R_RANK2 = range(6)


def _route_kernel(x_ref, mod_ref, g_ref, r_ref, tri_ref, route_ref, route_t_ref, cnt_ref, carry_ref):
    @pl.when(pl.program_id(0) == 0)
    def _():
        carry_ref[...] = jnp.zeros(carry_ref.shape, F32)

    t = x_ref.shape[0]
    h = _norm_mod(x_ref[...], g_ref[...], mod_ref[0, 3:4, :], mod_ref[0, 4:5, :])
    r = r_ref[...]
    h_hi, r_hi = h.astype(BF16), r.astype(BF16)
    h_lo, r_lo = (h - h_hi.astype(F32)).astype(BF16), (r - r_hi.astype(F32)).astype(BF16)
    logits = (jnp.dot(h_hi, r_hi, preferred_element_type=F32) + jnp.dot(h_hi, r_lo, preferred_element_type=F32)
              + jnp.dot(h_lo, r_hi, preferred_element_type=F32))
    i1, i2, w1, w2 = _top2(logits)
    lane = lax.broadcasted_iota(jnp.int32, (t, 128), 1).astype(F32)
    oh1 = jnp.where(lane == i1, 1.0, 0.0)
    oh2 = jnp.where(lane == i2, 1.0, 0.0)
    both = oh1 + oh2
    before = carry_ref[...] + jnp.dot(tri_ref[...], both.astype(BF16), preferred_element_type=F32)
    rank1 = jnp.sum(oh1 * before, axis=-1, keepdims=True)
    rank2 = jnp.sum(oh2 * before, axis=-1, keepdims=True)
    carry_ref[...] += jnp.sum(both, axis=0, keepdims=True)
    rec = jnp.zeros((t, 128), F32)
    for col, val in ((R_E1, i1), (R_E2, i2), (R_W1, w1), (R_W2, w2), (R_RANK1, rank1), (R_RANK2, rank2)):
        rec = jnp.where(lane == col, val, rec)
    route_ref[...] = rec
    route_t_ref[0] = rec.T[0:8, :]
    cnt_ref[...] = carry_ref[...]


def moe_route(x2d, mod, g, router_pad, *, tm, tiles_per_mod):
    m, d = x2d.shape
    n_mod = mod.shape[0]
    mod_idx = (lambda i: (i // tiles_per_mod, 0, 0)) if n_mod > 1 else (lambda i: (0, 0, 0))
    tri = jnp.asarray(np.tril(np.ones((tm, tm), np.float32), -1), BF16)
    return pl.pallas_call(
        _route_kernel,
        grid=(m // tm,),
        in_specs=[
            pl.BlockSpec((tm, d), lambda i: (i, 0)),
            pl.BlockSpec((1, 6, d), mod_idx),
            pl.BlockSpec((1, d), lambda i: (0, 0)),
            pl.BlockSpec((d, 128), lambda i: (0, 0)),
            pl.BlockSpec((tm, tm), lambda i: (0, 0)),
        ],
        out_specs=[pl.BlockSpec((tm, 128), lambda i: (i, 0)), pl.BlockSpec((1, 8, tm), lambda i: (i, 0, 0)),
                   pl.BlockSpec((1, 128), lambda i: (0, 0))],
        out_shape=[jax.ShapeDtypeStruct((m, 128), F32), jax.ShapeDtypeStruct((m // tm, 8, tm), F32),
                   jax.ShapeDtypeStruct((1, 128), F32)],
        scratch_shapes=[pltpu.VMEM((1, 128), F32)],
        compiler_params=_cparams(1),
        name="moe_route",
    )(x2d, mod, g.reshape(1, d), router_pad, tri)


def _dispatch_kernel(pos_ref, pad_ref, x_ref, mod_ref, g_ref, xs_ref, h_ref, zero_ref, sem, zsem):
    i = pl.program_id(0)
    t = x_ref.shape[0]
    slot = i % 2
    h_ref[slot] = _norm_mod(x_ref[...], g_ref[...], mod_ref[0, 3:4, :], mod_ref[0, 4:5, :])

    def row_copy(r, dst_row):
        return pltpu.make_async_copy(h_ref.at[slot, pl.ds(r, 1), :], xs_ref.at[pl.ds(dst_row, 1), :], sem.at[slot])

    def issue(r, carry):
        row_copy(r, pos_ref[0, 0, r]).start()
        row_copy(r, pos_ref[0, 0, t + r]).start()
        return carry

    lax.fori_loop(0, t, issue, 0, unroll=DMA_ISSUE_UNROLL)

    def drain(s):
        for _ in range(2):
            pltpu.make_async_copy(h_ref.at[s], xs_ref.at[pl.ds(0, t), :], sem.at[s]).wait()

    @pl.when(i > 0)
    def _():
        drain(1 - slot)

    def zero_row_copy(r):
        return pltpu.make_async_copy(zero_ref.at[pl.ds(0, 1), :], xs_ref.at[pl.ds(r, 1), :], zsem)

    def zero_tile_copy(k):
        return pltpu.make_async_copy(zero_ref, xs_ref.at[pl.ds(pl.multiple_of(k * t, t), t), :], zsem)

    def for_each_zero_copy(act):
        def row(r, carry):
            act(zero_row_copy(r))
            return carry

        def tile(k, carry):
            act(zero_tile_copy(k))
            return carry

        for e in range(N_EXPERTS):
            lax.fori_loop(pad_ref[0, e], pad_ref[0, N_EXPERTS + e], row, 0)
        lax.fori_loop(pad_ref[0, 2 * N_EXPERTS], xs_ref.shape[0] // t, tile, 0)

    @pl.when(i == 0)
    def _():
        zero_ref[...] = jnp.zeros(zero_ref.shape, F32)
        for_each_zero_copy(lambda cp: cp.start())

    @pl.when(i == pl.num_programs(0) - 1)
    def _():
        drain(slot)
        for_each_zero_copy(lambda cp: cp.wait())


def moe_dispatch(x2d, mod, g, pos_tiles, pad_rows, n_rows, *, tm, tiles_per_mod):
    m, d = x2d.shape
    n_mod = mod.shape[0]
    mod_idx = (lambda i: (i // tiles_per_mod, 0, 0)) if n_mod > 1 else (lambda i: (0, 0, 0))
    return pl.pallas_call(
        _dispatch_kernel,
        grid=(m // tm,),
        in_specs=[
            pl.BlockSpec((1, 1, 2 * tm), lambda i: (i, 0, 0), memory_space=pltpu.SMEM),
            pl.BlockSpec((1, 2 * N_EXPERTS + 1), lambda i: (0, 0), memory_space=pltpu.SMEM),
            pl.BlockSpec((tm, d), lambda i: (i, 0)),
            pl.BlockSpec((1, 6, d), mod_idx),
            pl.BlockSpec((1, d), lambda i: (0, 0)),
        ],
        out_specs=pl.BlockSpec(memory_space=pl.ANY),
        out_shape=jax.ShapeDtypeStruct((n_rows, d), F32),
        scratch_shapes=[pltpu.VMEM((2, tm, d), F32), pltpu.VMEM((tm, d), F32),
                        pltpu.SemaphoreType.DMA((2,)), pltpu.SemaphoreType.DMA(())],
        compiler_params=_cparams(1),
        name="moe_dispatch",
    )(pos_tiles, pad_rows, x2d, mod, g.reshape(1, d))


def _experts_kernel(te_ref, nv_ref, xs_ref, w1_ref, w3_ref, w2_ref, ys_ref, xb_ref, acc_ref):
    i = pl.program_id(0)
    j = pl.program_id(1)
    valid = i < nv_ref[0]

    @pl.when(jnp.logical_and(valid, j == 0))
    def _():
        xb_ref[...] = xs_ref[...].astype(BF16)
        acc_ref[...] = jnp.zeros(acc_ref.shape, F32)

    @pl.when(valid)
    def _():
        h = xb_ref[...]
        tf = w1_ref.shape[2]
        y = None
        for c0 in range(0, tf, 1024):
            cw = min(1024, tf - c0)
            a = jnp.dot(h, w1_ref[0, :, c0:c0 + cw], preferred_element_type=F32)
            b = jnp.dot(h, w3_ref[0, :, c0:c0 + cw], preferred_element_type=F32)
            yc = jnp.dot((_silu(a) * b).astype(BF16), w2_ref[0, c0:c0 + cw, :], preferred_element_type=F32)
            y = yc if y is None else y + yc
        acc_ref[...] += y

    @pl.when(jnp.logical_and(valid, j == pl.num_programs(1) - 1))
    def _():
        ys_ref[...] = acc_ref[...]

    @pl.when(jnp.logical_and(jnp.logical_not(valid), j == pl.num_programs(1) - 1))
    def _():
        ys_ref[...] = jnp.zeros(ys_ref.shape, F32)


def moe_experts_grouped(xs, tile_expert, n_valid, w1, w3, w2, *, tm, tf):
    n_rows, d = xs.shape
    dff = w1.shape[2]
    nf = dff // tf

    def w13_idx(i, j, te, nv):
        return (te[i], 0, jnp.where(i < nv[0], j, nf - 1))

    def w2_idx(i, j, te, nv):
        return (te[i], jnp.where(i < nv[0], j, nf - 1), 0)

    grid_spec = pltpu.PrefetchScalarGridSpec(
        num_scalar_prefetch=2,
        grid=(n_rows // tm, nf),
        in_specs=[
            pl.BlockSpec((tm, d), lambda i, j, te, nv: (jnp.minimum(i, nv[0] - 1), 0)),
            pl.BlockSpec((1, d, tf), w13_idx),
            pl.BlockSpec((1, d, tf), w13_idx),
            pl.BlockSpec((1, tf, d), w2_idx),
        ],
        out_specs=pl.BlockSpec((tm, d), lambda i, j, te, nv: (i, 0)),
        scratch_shapes=[pltpu.VMEM((tm, d), BF16), pltpu.VMEM((tm, d), F32)],
    )
    return pl.pallas_call(
        _experts_kernel,
        grid_spec=grid_spec,
        out_shape=jax.ShapeDtypeStruct((n_rows, d), F32),
        compiler_params=_cparams(2),
        name="moe_experts_grouped",
    )(tile_expert, n_valid, xs, w1, w3, w2)


def _combine_kernel(pos_ref, pos_next_ref, x_ref, mod_ref, rt_ref, ys_ref, o_ref, y_ref, sem):
    i = pl.program_id(0)
    t = x_ref.shape[0]
    slot = i % 2

    def issue_tile(p_ref, s):
        def issue(r, carry):
            for k in range(2):
                pltpu.make_async_copy(ys_ref.at[pl.ds(p_ref[0, 0, k * t + r], 1), :],
                                      y_ref.at[s, k, pl.ds(r, 1), :], sem.at[s]).start()
            return carry

        lax.fori_loop(0, t, issue, 0, unroll=DMA_ISSUE_UNROLL)

    @pl.when(i == 0)
    def _():
        issue_tile(pos_ref, 0)

    @pl.when(i + 1 < pl.num_programs(0))
    def _():
        issue_tile(pos_next_ref, 1 - slot)

    for k in range(2):
        pltpu.make_async_copy(ys_ref.at[pl.ds(0, t), :], y_ref.at[slot, k], sem.at[slot]).wait()
    rt = rt_ref[...]
    mix = rt[:, R_W1:R_W1 + 1] * y_ref[slot, 0] + rt[:, R_W2:R_W2 + 1] * y_ref[slot, 1]
    o_ref[...] = x_ref[...] + mod_ref[0, 5:6, :] * mix


def moe_combine(x2d, mod, route, pos_tiles, ys, *, tm, tiles_per_mod):
    m, d = x2d.shape
    n_mod = mod.shape[0]
    mod_idx = (lambda i: (i // tiles_per_mod, 0, 0)) if n_mod > 1 else (lambda i: (0, 0, 0))
    n_tiles = m // tm
    return pl.pallas_call(
        _combine_kernel,
        grid=(n_tiles,),
        in_specs=[
            pl.BlockSpec((1, 1, 2 * tm), lambda i: (i, 0, 0), memory_space=pltpu.SMEM),
            pl.BlockSpec((1, 1, 2 * tm), lambda i: (jnp.minimum(i + 1, n_tiles - 1), 0, 0), memory_space=pltpu.SMEM),
            pl.BlockSpec((tm, d), lambda i: (i, 0)),
            pl.BlockSpec((1, 6, d), mod_idx),
            pl.BlockSpec((tm, 128), lambda i: (i, 0)),
            pl.BlockSpec(memory_space=pl.ANY),
        ],
        out_specs=pl.BlockSpec((tm, d), lambda i: (i, 0)),
        out_shape=jax.ShapeDtypeStruct((m, d), F32),
        scratch_shapes=[pltpu.VMEM((2, 2, tm, d), F32), pltpu.SemaphoreType.DMA((2,))],
        compiler_params=_cparams(1),
        name="moe_combine",
    )(pos_tiles, pos_tiles, x2d, mod, route, ys)


def _pos_tiles(pos1, pos2, tm):
    n = pos1.shape[0] // tm
    return jnp.concatenate([pos1.reshape(n, 1, tm), pos2.reshape(n, 1, tm)], axis=2)


def moe_sparse(x2d, mod, g, router_pad, w1, w3, w2, *, rows_per_mod):
    m, d = x2d.shape
    tr, td, tc, te = MOE_ROUTE_TM, MOE_DISPATCH_TM, MOE_COMBINE_TM, MOE_EXPERT_TM
    route, route_t, cnt = moe_route(x2d, mod, g, router_pad, tm=tr, tiles_per_mod=rows_per_mod // tr)
    counts = cnt[0, 0:N_EXPERTS].astype(jnp.int32)
    group = ((counts + te - 1) // te) * te
    ends = jnp.cumsum(group)
    starts = ends - group
    field = lambda f: route_t[:, f, :].reshape(m).astype(jnp.int32)
    pos1 = starts[field(R_E1)] + field(R_RANK1)
    pos2 = starts[field(R_E2)] + field(R_RANK2)
    n_rows = 2 * m + N_EXPERTS * te
    n_tiles = n_rows // te
    tile_start = jnp.arange(n_tiles, dtype=jnp.int32)[:, None] * te
    tile_expert = jnp.minimum(jnp.sum((tile_start >= ends[None, :]).astype(jnp.int32), axis=1), N_EXPERTS - 1)
    n_valid = (ends[-1:] // te).astype(jnp.int32)
    assert td == te
    pad_rows = jnp.concatenate([starts + counts, ends, n_valid]).astype(jnp.int32).reshape(1, 2 * N_EXPERTS + 1)
    xs = moe_dispatch(x2d, mod, g, _pos_tiles(pos1, pos2, td), pad_rows, n_rows, tm=td,
                      tiles_per_mod=rows_per_mod // td)
    ys = moe_experts_grouped(xs, tile_expert, n_valid, w1, w3, w2, tm=te, tf=MOE_EXPERT_TF)
    return moe_combine(x2d, mod, route, _pos_tiles(pos1, pos2, tc), ys, tm=tc, tiles_per_mod=rows_per_mod // tc)


def _permute_w_in(w):
    f, c, p, q, kv, gts = w[:, 0:256], w[:, 256:768], w[:, 768:1024], w[:, 1024:1536], w[:, 1536:1792], w[:, 1792:]
    return jnp.concatenate([gts, q, c, p, f, kv], axis=1).astype(BF16)


def _layer_weights(layer, w_br_fourier, conv_dw, conv_b, conv_norm_g, w_br_conv, pool_w, pool_scale, w_br_pool,
                   w_br_attn, w_out):
    pw = jax.scipy.linalg.block_diag(*[pool_w[layer, i] for i in range(len(POOL_WINDOWS))])
    return {
        "wf": w_br_fourier[layer].astype(BF16), "wc": w_br_conv[layer].astype(BF16),
        "wp": w_br_pool[layer].astype(BF16), "wa": w_br_attn[layer].astype(BF16), "wo": w_out[layer].astype(BF16),
        "dw": conv_dw[layer], "cb": conv_b[layer].reshape(1, CONV_W), "cg": conv_norm_g[layer].reshape(1, CONV_W),
        "pw": pw.astype(BF16), "ps": pool_scale[layer].reshape(1, POOL_W),
    }


def _cast_slices(w, max_slices):
    w2 = w.reshape(-1, w.shape[-1])
    rows = w2.shape[0]
    n = 1
    while 2 * n <= max_slices and rows % (32 * n) == 0:
        n *= 2
    return w2.reshape(n, rows // n, w2.shape[1])


def kernel(x, c, ctx, c_ctx, w_mod, b_mod, norm1_g, norm2_g, w_in, w_br_fourier, conv_dw, conv_b, conv_norm_g,
           w_br_conv, pool_w, pool_scale, w_br_pool, q_norm_g, k_norm_g, w_br_attn, w_out, ffn_w1, ffn_w3, ffn_w2,
           moe_router, moe_w1, moe_w3, moe_w2):
    batch, seq, d = x.shape
    ctx_len = ctx.shape[1]
    depth = w_in.shape[0]
    rope = rope_tables(seq)

    c_rows = jnp.zeros((8, d), F32).at[0:batch].set(c).at[batch].set(c_ctx)
    mods = modulation_all(c_rows, w_mod, b_mod).reshape(depth, 8, 6, d)

    xl = x.reshape(batch * seq, d)
    xc = ctx.reshape(batch * ctx_len, d)
    for layer in range(depth):
        is_last = layer == depth - 1
        mod_l = mods[layer, 0:batch]
        mod_c = mods[layer, batch:batch + 1]
        w_in_l = _permute_w_in(w_in[layer])
        lw = _layer_weights(layer, w_br_fourier, conv_dw, conv_b, conv_norm_g, w_br_conv, pool_w, pool_scale,
                            w_br_pool, w_br_attn, w_out)

        proj_c, qc, ktc, vc = input_projection(xc, mod_c, norm1_g[layer], w_in_l, q_norm_g[layer], k_norm_g[layer],
                                               None, batch=batch, seq=ctx_len, tm=256)

        proj, q, kt, v = input_projection(xl, mod_l, norm1_g[layer], w_in_l, q_norm_g[layer], k_norm_g[layer], rope,
                                          batch=batch, seq=seq, tm=512)
        spread = (2.0 * 1.02 * HEAD_DIM * Q_SCALE) * jnp.max(jnp.abs(q_norm_g[layer])) * jnp.max(
            jnp.abs(k_norm_g[layer]))
        n_steps = batch * N_KV_HEADS * (seq // ATTN_TQ)
        mixer_w = ((ffn_w1, ffn_w3, ffn_w2) if layer % 2 == 0 else (moe_w1, moe_w3, moe_w2))
        mixer_w = tuple(w[layer // 2] for w in mixer_w)
        side = tuple(_cast_slices(w, n_steps) for w in mixer_w)
        attn, *side_bf16 = lax.cond(
            spread < STALE_MAX_EXP_LIMIT,
            lambda ops, sd: attention_stale_max(*ops, sd, batch=batch, seq_q=seq, tq=ATTN_TQ, tk=ATTN_TK),
            lambda ops, sd: (attention(*ops, batch=batch, seq_q=seq, tq=ATTN_TQ, tk=ATTN_TK),)
            + tuple(w.astype(BF16) for w in sd),
            (q, kt, v, ktc, vc), side)
        yf = fourier_mix(proj[:, P_OFF_F:P_OFF_F + FOURIER_W], batch=batch, seq=seq, n1=64, n2=seq // 64)
        xl = merge_branches(xl, mod_l, proj, yf, attn, lw, seq=seq, t=512)

        if not is_last:
            attn_c = attention(qc, None, None, ktc, vc, batch=batch, seq_q=ctx_len, tq=256, tk=ctx_len)
            yf_c = fourier_mix(proj_c[:, P_OFF_F:P_OFF_F + FOURIER_W], batch=batch, seq=ctx_len, n1=16,
                               n2=ctx_len // 16)
            xc = merge_branches(xc, mod_c, proj_c, yf_c, attn_c, lw, seq=ctx_len, t=256)

        j = layer // 2
        w1, w3, w2 = (wb.reshape(w.shape) for wb, w in zip(side_bf16, mixer_w))
        if layer % 2 == 0:
            xl = ffn_dense(xl, mod_l, norm2_g[layer], w1, w3, w2, tm=512, tiles_per_mod=seq // 512)
            if not is_last:
                xc = ffn_dense(xc, mod_c, norm2_g[layer], w1, w3, w2, tm=256, tiles_per_mod=1)
        else:
            router_pad = jnp.zeros((d, 128), F32).at[:, 0:N_EXPERTS].set(moe_router[j])
            xl = moe_sparse(xl, mod_l, norm2_g[layer], router_pad, w1, w3, w2, rows_per_mod=seq)
            if not is_last:
                xc = moe_sparse(xc, mod_c, norm2_g[layer], router_pad, w1, w3, w2, rows_per_mod=batch * ctx_len)
    return xl.reshape(batch, seq, d)
```

```python
import functools
import math

import numpy as np
import jax
import jax.numpy as jnp
from jax import lax
from jax.experimental import pallas as pl
from jax.experimental.pallas import tpu as pltpu

F32 = jnp.float32
BF16 = jnp.bfloat16

D_MODEL = 1024
GRID_W = 64
EPS = 1e-6
FOURIER_GW = 64
FOURIER_W = 256
CONV_W = 256
CONV_K = 31
CONV_HALF = CONV_K // 2
POOL_WINDOWS = (2, 4, 8, 16)
POOL_GW = 64
POOL_W = 256
HEAD_DIM = 64
N_KV_HEADS = 2
Q_PER_KV = 4
Q_W = 512
KV_W = 128
ROPE_THETA = 10000.0
N_EXPERTS = 8

P_OFF_Q = 4096
P_OFF_CP = 4608
P_OFF_F = 5376
P_OFF_KV = 5632
CP_W = 2 * CONV_W + POOL_W

Q_SCALE = (HEAD_DIM ** -0.5) * math.log2(math.e)

ATTN_TQ = 512
ATTN_TK = 1024
STALE_MAX_EXP_LIMIT = 64.0

MOE_ROUTE_TM = 512
MOE_DISPATCH_TM = 512
MOE_COMBINE_TM = 512
MOE_EXPERT_TM = 512
MOE_EXPERT_TF = 1792
DMA_ISSUE_UNROLL = 8

CONV_ROWS = 64
HALO = 16
VMEM_LIMIT = 56 * 1024 * 1024


def _cparams(n_axes):
    return pltpu.CompilerParams(dimension_semantics=("arbitrary",) * n_axes, vmem_limit_bytes=VMEM_LIMIT)


def _sigmoid(v):
    return 0.5 * jnp.tanh(0.5 * v) + 0.5


def _silu(v):
    return v * _sigmoid(v)


def _norm_mod(x, g, shift, scale):
    ms = jnp.mean(x * x, axis=-1, keepdims=True)
    return x * lax.rsqrt(ms + EPS) * g * (1.0 + scale) + shift


def _mod_kernel(c_ref, w_ref, b_ref, o_ref):
    s = _silu(c_ref[...])
    o_ref[0] = jnp.dot(s, w_ref[0], preferred_element_type=F32, precision=lax.Precision.HIGHEST) + b_ref[0]


def modulation_all(c_rows, w_mod, b_mod):
    n_layers, d, n = w_mod.shape
    tn = 1536
    return pl.pallas_call(
        _mod_kernel,
        grid=(n_layers, n // tn),
        in_specs=[
            pl.BlockSpec((8, d), lambda l, j: (0, 0)),
            pl.BlockSpec((1, d, tn), lambda l, j: (l, 0, j)),
            pl.BlockSpec((1, 1, tn), lambda l, j: (l, 0, j)),
        ],
        out_specs=pl.BlockSpec((1, 8, tn), lambda l, j: (l, 0, j)),
        out_shape=jax.ShapeDtypeStruct((n_layers, 8, n), F32),
        compiler_params=_cparams(2),
        name="modulation",
    )(c_rows, w_mod, b_mod.reshape(n_layers, 1, n))


def _inproj_kernel(*refs, chunks, use_rope):
    if use_rope:
        x_ref, mod_ref, g_ref, w_ref, gq_ref, gk_ref, ones_ref, cos_ref, sin_ref, o_ref, qo_ref, kt_ref, v_ref = refs
        cos, sin = cos_ref[...], sin_ref[...]
    else:
        x_ref, mod_ref, g_ref, w_ref, gq_ref, gk_ref, ones_ref, o_ref, qo_ref, kt_ref, v_ref = refs
        cos = sin = None
    h = _norm_mod(x_ref[...], g_ref[...], mod_ref[0, 0:1, :], mod_ref[0, 1:2, :]).astype(BF16)
    heavy_first = sorted(chunks, key=lambda c: (c[0] not in (P_OFF_Q, P_OFF_KV), c[0] != P_OFF_CP, c[0]))
    for c0, cw in heavy_first:
        r = jnp.dot(h, w_ref[:, c0:c0 + cw], preferred_element_type=F32)
        if c0 + cw <= P_OFF_Q:
            r = _sigmoid(r)
        elif c0 == P_OFF_CP:
            sg = _sigmoid(r[:, CONV_W:2 * CONV_W])
            r = jnp.concatenate([r[:, 0:CONV_W] * sg, sg], axis=1)
        elif c0 == P_OFF_Q:
            _q_epilogue(r, gq_ref[...], ones_ref[...], cos, sin, qo_ref)
        elif c0 == P_OFF_KV:
            _kv_epilogue(r, gk_ref[...], ones_ref[...], cos, sin, kt_ref, v_ref)
        o_ref[:, c0:c0 + cw] = r.astype(o_ref.dtype)


def input_projection(x2d, mod, g, w_bf16, gq, gk, rope, *, batch, seq, tm):
    m, d = x2d.shape
    n = w_bf16.shape[1]
    tps = seq // tm
    chunks = tuple((c0, min(512, n - c0)) for c0 in range(0, n, 512))
    assert P_OFF_Q % 512 == 0 and {(P_OFF_CP, 2 * CONV_W), (P_OFF_Q, Q_W), (P_OFF_KV, 2 * KV_W)} <= set(chunks)
    use_rope = rope is not None
    n_mod = mod.shape[0]
    mod_idx = (lambda i: (i // tps, 0, 0)) if n_mod > 1 else (lambda i: (0, 0, 0))
    const = lambda i: (0, 0)
    ones_bd = jnp.asarray(np.kron(np.eye(2, dtype=np.float32), np.ones((64, 64), np.float32)), BF16)
    in_specs = [
        pl.BlockSpec((tm, d), lambda i: (i, 0)),
        pl.BlockSpec((1, 6, d), mod_idx),
        pl.BlockSpec((1, d), const),
        pl.BlockSpec((d, n), const, pipeline_mode=pl.Buffered(1)),
        pl.BlockSpec((1, 128), const),
        pl.BlockSpec((1, 128), const),
        pl.BlockSpec((128, 128), const),
    ]
    args = [x2d, mod, g.reshape(1, d), w_bf16, jnp.tile(gq, 2).reshape(1, 128), jnp.tile(gk, 2).reshape(1, 128), ones_bd]
    if use_rope:
        in_specs += [pl.BlockSpec((tm, 128), lambda i: (i % tps, 0))] * 2
        args += list(rope)
    return pl.pallas_call(
        functools.partial(_inproj_kernel, chunks=chunks, use_rope=use_rope),
        grid=(m // tm,),
        in_specs=in_specs,
        out_specs=[
            pl.BlockSpec((tm, n), lambda i: (i, 0)),
            pl.BlockSpec((1, N_KV_HEADS, 256, tm), lambda i: (i // tps, 0, 0, i % tps)),
            pl.BlockSpec((1, N_KV_HEADS, tm, 256), lambda i: (i // tps, 0, i % tps, 0)),
            pl.BlockSpec((1, N_KV_HEADS, 128, tm), lambda i: (i // tps, 0, 0, i % tps)),
        ],
        out_shape=[
            jax.ShapeDtypeStruct((m, n), BF16),
            jax.ShapeDtypeStruct((batch, N_KV_HEADS, 256, seq), BF16),
            jax.ShapeDtypeStruct((batch, N_KV_HEADS, seq, 256), BF16),
            jax.ShapeDtypeStruct((batch, N_KV_HEADS, 128, seq), BF16),
        ],
        compiler_params=_cparams(1),
        name="input_projection",
    )(*args)


def _seg_sum64(v, ones_bd):
    hi = v.astype(BF16)
    lo = (v - hi.astype(F32)).astype(BF16)
    return (jnp.dot(hi, ones_bd, preferred_element_type=F32) + jnp.dot(lo, ones_bd, preferred_element_type=F32))


def _head_norm_rope(x, g, ones_bd, cos, sin, low_mask):
    y = x * lax.rsqrt(_seg_sum64(x * x, ones_bd) * (1.0 / HEAD_DIM) + EPS) * g
    if cos is None:
        return y
    partner = jnp.where(low_mask, pltpu.roll(y, 128 - 16, axis=1), pltpu.roll(y, 16, axis=1))
    return y * cos + partner * sin


def _rope_low_mask(t):
    return (lax.broadcasted_iota(jnp.int32, (t, 128), 1) % 32) < 16


def _q_epilogue(rq, gq, ones_bd, cos, sin, qo_ref):
    low_mask = _rope_low_mask(rq.shape[0])
    for c in range(Q_W // 128):
        yq = _head_norm_rope(rq[:, 128 * c:128 * (c + 1)], gq, ones_bd, cos, sin, low_mask) * Q_SCALE
        qo_ref[0, c // 2, 128 * (c % 2):128 * (c % 2 + 1), :] = yq.T.astype(BF16)


def _kv_epilogue(rkv, gk, ones_bd, cos, sin, kt_ref, v_ref):
    t = rkv.shape[0]
    yk = _head_norm_rope(rkv[:, 0:128], gk, ones_bd, cos, sin, _rope_low_mask(t))
    ykr = pltpu.roll(yk, 64, axis=1)
    first = lax.broadcasted_iota(jnp.int32, (t, 128), 1) < 64
    k0 = jnp.where(first, yk, ykr).astype(BF16)
    k1 = jnp.where(first, ykr, yk).astype(BF16)
    kt_ref[0, 0] = jnp.concatenate([k0, k0], axis=1)
    kt_ref[0, 1] = jnp.concatenate([k1, k1], axis=1)
    vt = rkv[:, 128:256].T
    ones = jnp.ones((HEAD_DIM, t), F32)
    for h in range(N_KV_HEADS):
        v_ref[0, h] = jnp.concatenate([vt[64 * h:64 * (h + 1), :], ones], axis=0).astype(BF16)


def rope_tables(seq):
    n_freq = HEAD_DIM // 4
    freqs = ROPE_THETA ** (-jnp.arange(n_freq, dtype=F32) / n_freq)
    t = jnp.arange(seq)
    row = (t // GRID_W).astype(F32)
    col = (t % GRID_W).astype(F32)
    ang_r = row[:, None] * freqs
    ang_c = col[:, None] * freqs
    cos = jnp.concatenate([jnp.cos(ang_r)] * 2 + [jnp.cos(ang_c)] * 2, axis=1)
    sin = jnp.concatenate([-jnp.sin(ang_r), jnp.sin(ang_r), -jnp.sin(ang_c), jnp.sin(ang_c)], axis=1)
    return jnp.tile(cos, (1, 2)), jnp.tile(sin, (1, 2))


def _attn_kernel(*refs, tq, tk, nk, tail):
    refs = list(refs)
    qt_ref = refs.pop(0)
    k_ref, vt_ref = (refs.pop(0), refs.pop(0)) if nk else (None, None)
    kc_ref, vtc_ref = (refs.pop(0), refs.pop(0)) if tail else (None, None)
    o_ref, qs_ref, s0, s1, p0, p1, a0, a1, mx0, mx1, m_ref, acc_ref = refs
    s_bufs, p_bufs, a_bufs, mx_bufs = (s0, s1), (p0, p1), (a0, a1), (mx0, mx1)
    n_blocks = nk + (1 if tail else 0)

    _attn_stack_queries(qt_ref, qs_ref, tq)
    m_ref[...] = jnp.full(m_ref.shape, -jnp.inf, F32)
    acc_ref[...] = jnp.zeros(acc_ref.shape, F32)

    def block(t):
        if isinstance(t, int) and t >= nk:
            return kc_ref[0, 0], vtc_ref[0, 0], tail
        off = t * tk if isinstance(t, int) else pl.multiple_of(t * tk, tk)
        return k_ref[0, 0, pl.ds(off, tk), :], vt_ref[0, 0, :, pl.ds(off, tk)], tk

    def scores(t, slot):
        k_rows, _, n = block(t)
        s = jnp.dot(k_rows, qs_ref[...], preferred_element_type=F32)
        s_bufs[slot][0:n, :] = s
        mx_bufs[slot][...] = jnp.max(s, axis=0, keepdims=True)

    def numerators(n, slot):
        s_ref, p_ref, a_ref = s_bufs[slot], p_bufs[slot], a_bufs[slot]
        for c0 in range(0, Q_PER_KV * tq, 128):
            cols = slice(c0, c0 + 128)
            m_old = m_ref[:, cols]
            m_new = jnp.maximum(m_old, mx_bufs[slot][:, cols])
            a_ref[:, cols] = jnp.exp2(m_old - m_new)
            p_ref[0:n, cols] = jnp.exp2(s_ref[0:n, cols] - m_new).astype(BF16)
            m_ref[:, cols] = m_new

    def weighted_sum(t, slot):
        _, vt, n = block(t)
        pv = jnp.dot(vt, p_bufs[slot][0:n, :], preferred_element_type=F32)
        acc_ref[...] = a_bufs[slot][...] * acc_ref[...] + pv

    def rows_of(t):
        return tk if t < nk else tail

    def step(t, slot, n_mid):
        scores(t, slot)
        numerators(n_mid, 1 - slot)
        weighted_sum(t - 2, slot)

    scores(0, 0)
    if n_blocks > 1:
        scores(1, 1)
        numerators(rows_of(0), 0)
        n_pairs = max(nk - 2, 0) // 2

        def pair(i, carry):
            t = 2 + 2 * i
            step(t, 0, tk)
            step(t + 1, 1, tk)
            return carry

        if n_pairs:
            lax.fori_loop(0, n_pairs, pair, 0)
        for t in range(2 + 2 * n_pairs, n_blocks):
            step(t, t % 2, rows_of(t - 1))
        last = n_blocks - 1
        numerators(rows_of(last), last % 2)
        weighted_sum(last - 1, (last - 1) % 2)
        weighted_sum(last, last % 2)
    else:
        numerators(rows_of(0), 0)
        weighted_sum(0, 0)

    _attn_write_output(acc_ref, o_ref, tq)


def _attn_write_output(acc_ref, o_ref, tq):
    acc = acc_ref[...]
    ot = acc[0:HEAD_DIM, :] / acc[HEAD_DIM:2 * HEAD_DIM, :]
    for half in range(2):
        pair_t = jnp.concatenate([ot[:, (2 * half) * tq:(2 * half + 1) * tq],
                                  ot[:, (2 * half + 1) * tq:(2 * half + 2) * tq]], axis=0)
        o_ref[:, 128 * half:128 * (half + 1)] = pair_t.T.astype(o_ref.dtype)


def _attn_stack_queries(qt_ref, qs_ref, tq):
    row_group = lax.broadcasted_iota(jnp.int32, (256, tq), 0) // HEAD_DIM
    qt = qt_ref[0, 0]
    for g in range(Q_PER_KV):
        qs_ref[:, g * tq:(g + 1) * tq] = jnp.where(row_group == g, qt, jnp.zeros_like(qt))


def _attn_stale_max_kernel(*refs, tq, tk, nk, n_side):
    qt_ref, k_ref, vt_ref, kc_ref, vtc_ref = refs[:5]
    side_in = refs[5:5 + n_side]
    o_ref = refs[5 + n_side]
    side_out = refs[6 + n_side:6 + 2 * n_side]
    qs_ref, p0, p1, f0, f1, m_ref, acc_ref = refs[6 + 2 * n_side:]
    for src, dst in zip(side_in, side_out):
        dst[...] = src[...].astype(dst.dtype)
    p_bufs, f_bufs = (p0, p1), (f0, f1)
    _attn_stack_queries(qt_ref, qs_ref, tq)

    s = jnp.dot(kc_ref[0, 0], qs_ref[...], preferred_element_type=F32)
    m0 = jnp.max(s, axis=0, keepdims=True)
    m_ref[...] = m0
    acc_ref[...] = jnp.dot(vtc_ref[0, 0], jnp.exp2(s - m0).astype(BF16), preferred_element_type=F32)

    def numerators(t, slot):
        off = t * tk if isinstance(t, int) else pl.multiple_of(t * tk, tk)
        s = jnp.dot(k_ref[0, 0, pl.ds(off, tk), :], qs_ref[...], preferred_element_type=F32)
        m_old = m_ref[...]
        p_bufs[slot][...] = jnp.exp2(s - m_old).astype(BF16)
        m_new = jnp.maximum(m_old, jnp.max(s, axis=0, keepdims=True))
        f_bufs[slot][...] = jnp.exp2(m_old - m_new)
        m_ref[...] = m_new

    def weighted_sum(t, slot):
        off = t * tk if isinstance(t, int) else pl.multiple_of(t * tk, tk)
        pv = jnp.dot(vt_ref[0, 0, :, pl.ds(off, tk)], p_bufs[slot][...], preferred_element_type=F32)
        acc_ref[...] = (acc_ref[...] + pv) * f_bufs[slot][...]

    def step(t, slot):
        numerators(t, slot)
        weighted_sum(t - 1, 1 - slot)

    numerators(0, 0)
    n_pairs = (nk - 1) // 2

    def pair(i, carry):
        t = 1 + 2 * i
        step(t, 1)
        step(t + 1, 0)
        return carry

    if n_pairs:
        lax.fori_loop(0, n_pairs, pair, 0)
    for t in range(1 + 2 * n_pairs, nk):
        step(t, t % 2)
    weighted_sum(nk - 1, (nk - 1) % 2)
    _attn_write_output(acc_ref, o_ref, tq)


def attention_stale_max(qt, k4, vt1, k4_tail, vt1_tail, side=(), *, batch, seq_q, tq, tk):
    nq = seq_q // tq
    lanes = Q_PER_KV * tq
    lk = k4.shape[2]
    tail = k4_tail.shape[2]
    n_steps = batch * N_KV_HEADS * nq
    assert all(n_steps % w.shape[0] == 0 for w in side)

    def side_spec(w):
        repeat = n_steps // w.shape[0]
        return pl.BlockSpec((1,) + w.shape[1:], lambda b, h, i: (((b * N_KV_HEADS + h) * nq + i) // repeat, 0, 0))

    side_specs = [side_spec(w) for w in side]
    outs = pl.pallas_call(
        functools.partial(_attn_stale_max_kernel, tq=tq, tk=tk, nk=lk // tk, n_side=len(side)),
        grid=(batch, N_KV_HEADS, nq),
        in_specs=[
            pl.BlockSpec((1, 1, 256, tq), lambda b, h, i: (b, h, 0, i)),
            pl.BlockSpec((1, 1, lk, 256), lambda b, h, i: (b, h, 0, 0)),
            pl.BlockSpec((1, 1, 128, lk), lambda b, h, i: (b, h, 0, 0)),
            pl.BlockSpec((1, 1, tail, 256), lambda b, h, i: (b, h, 0, 0)),
            pl.BlockSpec((1, 1, 128, tail), lambda b, h, i: (b, h, 0, 0)),
        ] + side_specs,
        out_specs=[pl.BlockSpec((tq, 256), lambda b, h, i: (b * nq + i, h))] + side_specs,
        out_shape=[jax.ShapeDtypeStruct((batch * seq_q, Q_W), BF16)]
        + [jax.ShapeDtypeStruct(w.shape, BF16) for w in side],
        scratch_shapes=[
            pltpu.VMEM((256, lanes), BF16),
            pltpu.VMEM((tk, lanes), BF16), pltpu.VMEM((tk, lanes), BF16),
            pltpu.VMEM((1, lanes), F32), pltpu.VMEM((1, lanes), F32),
            pltpu.VMEM((1, lanes), F32),
            pltpu.VMEM((2 * HEAD_DIM, lanes), F32),
        ],
        compiler_params=_cparams(3),
        name="attention_stale_max",
    )(qt, k4, vt1, k4_tail, vt1_tail, *side)
    return tuple(outs)


def attention(qt, k4, vt1, k4_tail, vt1_tail, *, batch, seq_q, tq, tk):
    nq = seq_q // tq
    lanes = Q_PER_KV * tq
    nk = 0 if k4 is None else k4.shape[2] // tk
    tail = 0 if k4_tail is None else k4_tail.shape[2]
    buf_rows = max(tk if nk else 0, tail)
    in_specs = [pl.BlockSpec((1, 1, 256, tq), lambda b, h, i: (b, h, 0, i))]
    args = [qt]
    if nk:
        lk = k4.shape[2]
        in_specs += [pl.BlockSpec((1, 1, lk, 256), lambda b, h, i: (b, h, 0, 0)),
                     pl.BlockSpec((1, 1, 128, lk), lambda b, h, i: (b, h, 0, 0))]
        args += [k4, vt1]
    if tail:
        in_specs += [pl.BlockSpec((1, 1, tail, 256), lambda b, h, i: (b, h, 0, 0)),
                     pl.BlockSpec((1, 1, 128, tail), lambda b, h, i: (b, h, 0, 0))]
        args += [k4_tail, vt1_tail]
    return pl.pallas_call(
        functools.partial(_attn_kernel, tq=tq, tk=tk, nk=nk, tail=tail),
        grid=(batch, N_KV_HEADS, nq),
        in_specs=in_specs,
        out_specs=pl.BlockSpec((tq, 256), lambda b, h, i: (b * nq + i, h)),
        out_shape=jax.ShapeDtypeStruct((batch * seq_q, Q_W), BF16),
        scratch_shapes=[
            pltpu.VMEM((256, lanes), BF16),
            pltpu.VMEM((buf_rows, lanes), F32), pltpu.VMEM((buf_rows, lanes), F32),
            pltpu.VMEM((buf_rows, lanes), BF16), pltpu.VMEM((buf_rows, lanes), BF16),
            pltpu.VMEM((1, lanes), F32), pltpu.VMEM((1, lanes), F32),
            pltpu.VMEM((1, lanes), F32), pltpu.VMEM((1, lanes), F32),
            pltpu.VMEM((1, lanes), F32),
            pltpu.VMEM((2 * HEAD_DIM, lanes), F32),
        ],
        compiler_params=_cparams(3),
        name="attention",
    )(*args)


def _dft_cs(n):
    k = np.arange(n)
    ang = 2.0 * np.pi * ((k[:, None] * k[None, :]) % n) / n
    return np.cos(ang), np.sin(ang)


def _fft1_kernel(x_ref, f_ref, c_ref, s_ref, o_ref, *, n1):
    y = jnp.dot(f_ref[...], x_ref[0], preferred_element_type=F32)
    yr, yi = y[:n1], y[n1:]
    c, s = c_ref[...], s_ref[...]
    o_ref[0, 0] = (yr * c + yi * s).astype(o_ref.dtype)
    o_ref[0, 1] = (yi * c - yr * s).astype(o_ref.dtype)


def _fft2_kernel(y_ref, f_ref, bc_ref, bs_ref, o_ref, *, n2, kb):
    for j in range(kb):
        y2 = jnp.concatenate([y_ref[0, 0, j], y_ref[0, 1, j]], axis=0)
        x2 = jnp.dot(f_ref[...], y2, preferred_element_type=F32)
        xr = x2[:n2].astype(BF16)
        xi = x2[n2:].astype(BF16)
        z = (jnp.dot(xr, bc_ref[...], preferred_element_type=F32) + jnp.dot(xi, bs_ref[...], preferred_element_type=F32))
        o_ref[0, j] = z.astype(o_ref.dtype)


def fourier_mix(u, *, batch, seq, n1, n2):
    cw = u.shape[1]
    lanes = n2 * cw
    tl = min(lanes, 4096)
    c1, s1 = _dft_cs(n1)
    f1 = jnp.asarray(np.concatenate([c1, -s1], axis=0), BF16)
    k1 = np.arange(n1)[:, None]
    t2 = np.arange(n2)[None, :]
    ang = 2.0 * np.pi * ((k1 * t2) % seq) / seq
    twc = jnp.asarray(np.repeat(np.cos(ang), cw, axis=1), F32)
    tws = jnp.asarray(np.repeat(np.sin(ang), cw, axis=1), F32)
    x2 = u.reshape(batch, n1, lanes)
    yp = pl.pallas_call(
        functools.partial(_fft1_kernel, n1=n1),
        grid=(batch, lanes // tl),
        in_specs=[
            pl.BlockSpec((1, n1, tl), lambda b, j: (b, 0, j)),
            pl.BlockSpec((2 * n1, n1), lambda b, j: (0, 0)),
            pl.BlockSpec((n1, tl), lambda b, j: (0, j)),
            pl.BlockSpec((n1, tl), lambda b, j: (0, j)),
        ],
        out_specs=pl.BlockSpec((1, 2, n1, tl), lambda b, j: (b, 0, 0, j)),
        out_shape=jax.ShapeDtypeStruct((batch, 2, n1, lanes), BF16),
        compiler_params=_cparams(2),
        name="fft_stage1",
    )(x2, f1, twc, tws)

    c2, s2 = _dft_cs(n2)
    f2 = jnp.asarray(np.block([[c2, s2], [-s2, c2]]), BF16)
    cg, sg = _dft_cs(FOURIER_GW)
    norm = 1.0 / math.sqrt(seq * FOURIER_GW)
    bdc = jnp.asarray(np.kron(np.eye(cw // FOURIER_GW), cg) * norm, BF16)
    bds = jnp.asarray(np.kron(np.eye(cw // FOURIER_GW), sg) * norm, BF16)
    kb = min(n1, 16)
    y5 = yp.reshape(batch, 2, n1, n2, cw)
    z = pl.pallas_call(
        functools.partial(_fft2_kernel, n2=n2, kb=kb),
        grid=(batch, n1 // kb),
        in_specs=[
            pl.BlockSpec((1, 2, kb, n2, cw), lambda b, j: (b, 0, j, 0, 0)),
            pl.BlockSpec((2 * n2, 2 * n2), lambda b, j: (0, 0)),
            pl.BlockSpec((cw, cw), lambda b, j: (0, 0)),
            pl.BlockSpec((cw, cw), lambda b, j: (0, 0)),
        ],
        out_specs=pl.BlockSpec((1, kb, n2, cw), lambda b, j: (b, j, 0, 0)),
        out_shape=jax.ShapeDtypeStruct((batch, n1, n2, cw), BF16),
        compiler_params=_cparams(2),
        name="fft_stage2",
    )(y5, f2, bdc, bds)
    return z.transpose(0, 2, 1, 3).reshape(batch * seq, cw)


def _merge_kernel(*refs, t, tps, with_route):
    (x_ref, mod_ref, gate_ref, cp_ref, cpp_ref, cpn_ref, yf_ref, at_ref, wf_ref, wc_ref, wp_ref, wa_ref, wo_ref,
     dw_ref, cb_ref, cg_ref, pw_ref, ps_ref, band_ref, icnt_ref) = refs[:20]
    if with_route:
        g2_ref, r_ref, tri_ref, o_ref, route_ref, route_t_ref, cnt_ref, ybuf, xbuf, ysh, cacc, carry_ref = refs[20:]
    else:
        o_ref, ybuf, xbuf, ysh, cacc = refs[20:]
    i = pl.program_id(0)
    pos_tile = i % tps
    keep_prev = jnp.where(pos_tile != 0, 1.0, 0.0).astype(F32)
    keep_next = jnp.where(pos_tile != tps - 1, 1.0, 0.0).astype(F32)

    def glu(blk):
        return blk[:, 0:CONV_W].astype(F32)

    cp, cpp, cpn = cp_ref[...], cpp_ref[...], cpn_ref[...]
    ybuf[0:HALO, :] = glu(cpp) * keep_prev
    ybuf[HALO:HALO + t, :] = glu(cp)
    ybuf[HALO + t:HALO + t + HALO, :] = glu(cpn) * keep_next
    xbuf[0:HALO, :] = cpp[:, 2 * CONV_W:] * keep_prev.astype(BF16)
    xbuf[HALO:HALO + t, :] = cp[:, 2 * CONV_W:]
    xbuf[HALO + t:HALO + t + HALO, :] = cpn[:, 2 * CONV_W:] * keep_next.astype(BF16)

    n_sh = t + 2 * HALO - 8
    for b in range(1, 8):
        ysh[b - 1, 0:n_sh, :] = ybuf[pl.ds(b, n_sh), :]
    for r0 in range(0, t, CONV_ROWS):
        part = jnp.zeros((CONV_ROWS, CONV_W), F32)
        for k in range(CONV_K):
            a, b = divmod(HALO - CONV_HALF + k, 8)
            src = ybuf if b == 0 else ysh.at[b - 1]
            part = part + dw_ref[k:k + 1, :] * src[8 * a + r0:8 * a + r0 + CONV_ROWS, :]
        cacc[r0:r0 + CONV_ROWS, :] = part + cb_ref[...]
    acc = cacc[...]
    ms = jnp.mean(acc * acc, axis=-1, keepdims=True)
    conv_out = _silu(acc * lax.rsqrt(ms + EPS) * cg_ref[...]).astype(BF16)

    pb = band_ref.shape[1]
    grp = lax.broadcasted_iota(jnp.int32, (pb + 2 * HALO, POOL_W), 1) // POOL_GW
    parts = []
    for r0 in range(0, t, pb):
        xw = xbuf[r0:r0 + pb + 2 * HALO, :]
        wsum = jnp.zeros((pb, POOL_W), F32)
        for gi in range(len(POOL_WINDOWS)):
            wsum = wsum + jnp.dot(band_ref[gi], jnp.where(grp == gi, xw, jnp.zeros_like(xw)),
                                  preferred_element_type=F32)
        x0 = xbuf[HALO + r0:HALO + r0 + pb, :].astype(F32)
        parts.append((wsum * icnt_ref[0, r0:r0 + pb, :] - x0).astype(BF16))
    pool_in = parts[0] if len(parts) == 1 else jnp.concatenate(parts, axis=0)
    pool_out = (jnp.dot(pool_in, pw_ref[...], preferred_element_type=F32) * ps_ref[...]).astype(BF16)

    def gate(b):
        return gate_ref[:, b * D_MODEL:(b + 1) * D_MODEL].astype(F32)

    merged = gate(0) * jnp.dot(yf_ref[...], wf_ref[...], preferred_element_type=F32)
    merged = merged + gate(1) * jnp.dot(conv_out, wc_ref[...], preferred_element_type=F32)
    merged = merged + gate(2) * jnp.dot(pool_out, wp_ref[...], preferred_element_type=F32)
    merged = merged + gate(3) * jnp.dot(at_ref[...], wa_ref[...], preferred_element_type=F32)
    out = jnp.dot(merged.astype(BF16), wo_ref[...], preferred_element_type=F32)
    x_new = x_ref[...] + mod_ref[0, 2:3, :] * out
    o_ref[...] = x_new
    if with_route:
        _route_tile(x_new, mod_ref, g2_ref, r_ref, tri_ref, route_ref, route_t_ref, cnt_ref, carry_ref)


def _pool_tables(seq, t):
    pb = min(t, 256)
    r = np.arange(pb)[:, None]
    j = np.arange(pb + 2 * HALO)[None, :]
    band = np.stack([((j >= r + HALO - w // 2) & (j < r + HALO + w // 2)) for w in POOL_WINDOWS]).astype(np.float32)
    half = np.repeat(np.array(POOL_WINDOWS) // 2, POOL_GW)[None, :]
    rows = np.arange(t)[:, None]
    icnt = []
    for first, last in ((0, 0), (1, 0), (0, 1), (1, 1)):
        if first and last:
            pos, length = rows, t
        elif first:
            pos, length = rows, 2 * t + 2 * HALO
        elif last:
            pos, length = rows + seq - t, seq
        else:
            pos, length = rows + t + 2 * HALO, 4 * t
        cnt = np.minimum(pos + half, length) - np.maximum(pos - half, 0)
        icnt.append(1.0 / cnt)
    return jnp.asarray(band, BF16), jnp.asarray(np.stack(icnt), F32)


def merge_branches(x2d, mod, proj, yf, attn, lw, route_with=None, *, seq, t):
    m, d = x2d.shape
    tps = seq // t
    hb = t // HALO
    n_halo = m // HALO
    n_mod = mod.shape[0]
    mod_idx = (lambda i: (i // tps, 0, 0)) if n_mod > 1 else (lambda i: (0, 0, 0))
    const = lambda i: (0, 0)
    cp_blk = P_OFF_CP // CP_W
    band, icnt = _pool_tables(seq, t)
    pb = band.shape[1]

    def icnt_idx(i):
        pos_tile = i % tps
        return ((pos_tile == 0).astype(jnp.int32) + 2 * (pos_tile == tps - 1).astype(jnp.int32), 0, 0)

    with_route = route_with is not None
    extra_in, extra_args, extra_out_specs, extra_out_shapes, extra_scratch = [], [], [], [], []
    if with_route:
        extra_in = [pl.BlockSpec((1, d), const), pl.BlockSpec((d, 128), const), pl.BlockSpec((t, t), const)]
        extra_args = [route_with[0].reshape(1, d), route_with[1],
                      jnp.asarray(np.tril(np.ones((t, t), np.float32), -1), BF16)]
        extra_out_specs = [pl.BlockSpec((t, 128), lambda i: (i, 0)), pl.BlockSpec((1, 8, t), lambda i: (i, 0, 0)),
                           pl.BlockSpec((1, 128), const)]
        extra_out_shapes = [jax.ShapeDtypeStruct((m, 128), F32), jax.ShapeDtypeStruct((m // t, 8, t), F32),
                            jax.ShapeDtypeStruct((1, 128), F32)]
        extra_scratch = [pltpu.VMEM((1, 128), F32)]
    outs = pl.pallas_call(
        functools.partial(_merge_kernel, t=t, tps=tps, with_route=with_route),
        grid=(m // t,),
        in_specs=[
            pl.BlockSpec((t, d), lambda i: (i, 0)),
            pl.BlockSpec((1, 6, d), mod_idx),
            pl.BlockSpec((t, 4 * d), lambda i: (i, 0)),
            pl.BlockSpec((t, CP_W), lambda i: (i, cp_blk)),
            pl.BlockSpec((HALO, CP_W), lambda i: (jnp.maximum(i * hb - 1, 0), cp_blk)),
            pl.BlockSpec((HALO, CP_W), lambda i: (jnp.minimum((i + 1) * hb, n_halo - 1), cp_blk)),
            pl.BlockSpec((t, FOURIER_W), lambda i: (i, 0)),
            pl.BlockSpec((t, Q_W), lambda i: (i, 0)),
            pl.BlockSpec((FOURIER_W, d), const),
            pl.BlockSpec((CONV_W, d), const),
            pl.BlockSpec((POOL_W, d), const),
            pl.BlockSpec((Q_W, d), const),
            pl.BlockSpec((d, d), const),
            pl.BlockSpec((CONV_K, CONV_W), const),
            pl.BlockSpec((1, CONV_W), const),
            pl.BlockSpec((1, CONV_W), const),
            pl.BlockSpec((POOL_W, POOL_W), const),
            pl.BlockSpec((1, POOL_W), const),
            pl.BlockSpec((len(POOL_WINDOWS), pb, pb + 2 * HALO), lambda i: (0, 0, 0)),
            pl.BlockSpec((1, t, POOL_W), icnt_idx),
        ] + extra_in,
        out_specs=[pl.BlockSpec((t, d), lambda i: (i, 0))] + extra_out_specs,
        out_shape=[jax.ShapeDtypeStruct((m, d), F32)] + extra_out_shapes,
        scratch_shapes=[pltpu.VMEM((t + 2 * HALO, CONV_W), F32), pltpu.VMEM((t + 2 * HALO, POOL_W), BF16),
                        pltpu.VMEM((7, t + 2 * HALO, CONV_W), F32), pltpu.VMEM((t, CONV_W), F32)] + extra_scratch,
        compiler_params=_cparams(1),
        name="merge_branches",
    )(x2d, mod, proj, proj, proj, proj, yf, attn,
      lw["wf"], lw["wc"], lw["wp"], lw["wa"], lw["wo"], lw["dw"], lw["cb"], lw["cg"], lw["pw"], lw["ps"], band, icnt,
      *extra_args)
    return tuple(outs) if with_route else outs[0]


def _ffn_kernel(x_ref, mod_ref, g_ref, w1_ref, w3_ref, w2_ref, o_ref, *, chunks):
    x = x_ref[...]
    h = _norm_mod(x, g_ref[...], mod_ref[0, 3:4, :], mod_ref[0, 4:5, :]).astype(BF16)
    acc = jnp.zeros(x.shape, F32)
    for c0, cw in chunks:
        a = jnp.dot(h, w1_ref[:, c0:c0 + cw], preferred_element_type=F32)
        b = jnp.dot(h, w3_ref[:, c0:c0 + cw], preferred_element_type=F32)
        acc = acc + jnp.dot((_silu(a) * b).astype(BF16), w2_ref[c0:c0 + cw, :], preferred_element_type=F32)
    o_ref[...] = x + mod_ref[0, 5:6, :] * acc


def ffn_dense(x2d, mod, g, w1, w3, w2, *, tm, tiles_per_mod):
    m, d = x2d.shape
    dff = w1.shape[1]
    chunks = tuple((c0, min(1024, dff - c0)) for c0 in range(0, dff, 1024))
    n_mod = mod.shape[0]
    mod_idx = (lambda i: (i // tiles_per_mod, 0, 0)) if n_mod > 1 else (lambda i: (0, 0, 0))
    const = lambda i: (0, 0)
    return pl.pallas_call(
        functools.partial(_ffn_kernel, chunks=chunks),
        grid=(m // tm,),
        in_specs=[
            pl.BlockSpec((tm, d), lambda i: (i, 0)),
            pl.BlockSpec((1, 6, d), mod_idx),
            pl.BlockSpec((1, d), const),
            pl.BlockSpec((d, dff), const, pipeline_mode=pl.Buffered(1)),
            pl.BlockSpec((d, dff), const, pipeline_mode=pl.Buffered(1)),
            pl.BlockSpec((dff, d), const, pipeline_mode=pl.Buffered(1)),
        ],
        out_specs=pl.BlockSpec((tm, d), lambda i: (i, 0)),
        out_shape=jax.ShapeDtypeStruct((m, d), F32),
        compiler_params=_cparams(1),
        name="ffn_dense",
    )(x2d, mod, g.reshape(1, d), w1, w3, w2)


def _top2(logits):
    t = logits.shape[0]
    lane = lax.broadcasted_iota(jnp.int32, (t, 128), 1).astype(F32)
    neg = jnp.float32(-jnp.inf)
    lg = jnp.where(lane < N_EXPERTS, logits, neg)
    v1 = jnp.max(lg, axis=-1, keepdims=True)
    i1 = jnp.min(jnp.where(lg == v1, lane, 128.0), axis=-1, keepdims=True)
    lg2 = jnp.where(lane == i1, neg, lg)
    v2 = jnp.max(lg2, axis=-1, keepdims=True)
    i2 = jnp.min(jnp.where(lg2 == v2, lane, 128.0), axis=-1, keepdims=True)
    e2 = jnp.exp(v2 - v1)
    return i1, i2, 1.0 / (1.0 + e2), e2 / (1.0 + e2)


R_E1, R_E2, R_W1, R_W2, R_RANK1, R_RANK2 = range(6)


def _route_kernel(x_ref, mod_ref, g_ref, r_ref, tri_ref, route_ref, route_t_ref, cnt_ref, carry_ref):
    _route_tile(x_ref[...], mod_ref, g_ref, r_ref, tri_ref, route_ref, route_t_ref, cnt_ref, carry_ref)


def _route_tile(x, mod_ref, g_ref, r_ref, tri_ref, route_ref, route_t_ref, cnt_ref, carry_ref):
    @pl.when(pl.program_id(0) == 0)
    def _():
        carry_ref[...] = jnp.zeros(carry_ref.shape, F32)

    t = x.shape[0]
    h = _norm_mod(x, g_ref[...], mod_ref[0, 3:4, :], mod_ref[0, 4:5, :])
    r = r_ref[...]
    h_hi, r_hi = h.astype(BF16), r.astype(BF16)
    h_lo, r_lo = (h - h_hi.astype(F32)).astype(BF16), (r - r_hi.astype(F32)).astype(BF16)
    logits = (jnp.dot(h_hi, r_hi, preferred_element_type=F32) + jnp.dot(h_hi, r_lo, preferred_element_type=F32)
              + jnp.dot(h_lo, r_hi, preferred_element_type=F32))
    i1, i2, w1, w2 = _top2(logits)
    lane = lax.broadcasted_iota(jnp.int32, (t, 128), 1).astype(F32)
    oh1 = jnp.where(lane == i1, 1.0, 0.0)
    oh2 = jnp.where(lane == i2, 1.0, 0.0)
    both = oh1 + oh2
    before = carry_ref[...] + jnp.dot(tri_ref[...], both.astype(BF16), preferred_element_type=F32)
    rank1 = jnp.sum(oh1 * before, axis=-1, keepdims=True)
    rank2 = jnp.sum(oh2 * before, axis=-1, keepdims=True)
    carry_ref[...] += jnp.sum(both, axis=0, keepdims=True)
    rec = jnp.zeros((t, 128), F32)
    for col, val in ((R_E1, i1), (R_E2, i2), (R_W1, w1), (R_W2, w2), (R_RANK1, rank1), (R_RANK2, rank2)):
        rec = jnp.where(lane == col, val, rec)
    route_ref[...] = rec
    route_t_ref[0] = rec.T[0:8, :]
    cnt_ref[...] = carry_ref[...]


def moe_route(x2d, mod, g, router_pad, *, tm, tiles_per_mod):
    m, d = x2d.shape
    n_mod = mod.shape[0]
    mod_idx = (lambda i: (i // tiles_per_mod, 0, 0)) if n_mod > 1 else (lambda i: (0, 0, 0))
    tri = jnp.asarray(np.tril(np.ones((tm, tm), np.float32), -1), BF16)
    return pl.pallas_call(
        _route_kernel,
        grid=(m // tm,),
        in_specs=[
            pl.BlockSpec((tm, d), lambda i: (i, 0)),
            pl.BlockSpec((1, 6, d), mod_idx),
            pl.BlockSpec((1, d), lambda i: (0, 0)),
            pl.BlockSpec((d, 128), lambda i: (0, 0)),
            pl.BlockSpec((tm, tm), lambda i: (0, 0)),
        ],
        out_specs=[pl.BlockSpec((tm, 128), lambda i: (i, 0)), pl.BlockSpec((1, 8, tm), lambda i: (i, 0, 0)),
                   pl.BlockSpec((1, 128), lambda i: (0, 0))],
        out_shape=[jax.ShapeDtypeStruct((m, 128), F32), jax.ShapeDtypeStruct((m // tm, 8, tm), F32),
                   jax.ShapeDtypeStruct((1, 128), F32)],
        scratch_shapes=[pltpu.VMEM((1, 128), F32)],
        compiler_params=_cparams(1),
        name="moe_route",
    )(x2d, mod, g.reshape(1, d), router_pad, tri)


def _dispatch_kernel(pos_ref, pad_ref, x_ref, mod_ref, g_ref, xs_ref, h_ref, zero_ref, sem, zsem):
    i = pl.program_id(0)
    t = x_ref.shape[0]
    slot = i % 2
    h_ref[slot] = _norm_mod(x_ref[...], g_ref[...], mod_ref[0, 3:4, :], mod_ref[0, 4:5, :])

    def row_copy(r, dst_row):
        return pltpu.make_async_copy(h_ref.at[slot, pl.ds(r, 1), :], xs_ref.at[pl.ds(dst_row, 1), :], sem.at[slot])

    def issue(r, carry):
        row_copy(r, pos_ref[0, 0, r]).start()
        row_copy(r, pos_ref[0, 0, t + r]).start()
        return carry

    lax.fori_loop(0, t, issue, 0, unroll=DMA_ISSUE_UNROLL)

    def drain(s):
        for _ in range(2):
            pltpu.make_async_copy(h_ref.at[s], xs_ref.at[pl.ds(0, t), :], sem.at[s]).wait()

    @pl.when(i > 0)
    def _():
        drain(1 - slot)

    def zero_row_copy(r):
        return pltpu.make_async_copy(zero_ref.at[pl.ds(0, 1), :], xs_ref.at[pl.ds(r, 1), :], zsem)

    def zero_tile_copy(k):
        return pltpu.make_async_copy(zero_ref, xs_ref.at[pl.ds(pl.multiple_of(k * t, t), t), :], zsem)

    def for_each_zero_copy(act):
        def row(r, carry):
            act(zero_row_copy(r))
            return carry

        def tile(k, carry):
            act(zero_tile_copy(k))
            return carry

        for e in range(N_EXPERTS):
            lax.fori_loop(pad_ref[0, e], pad_ref[0, N_EXPERTS + e], row, 0)
        lax.fori_loop(pad_ref[0, 2 * N_EXPERTS], xs_ref.shape[0] // t, tile, 0)

    @pl.when(i == 0)
    def _():
        zero_ref[...] = jnp.zeros(zero_ref.shape, F32)
        for_each_zero_copy(lambda cp: cp.start())

    @pl.when(i == pl.num_programs(0) - 1)
    def _():
        drain(slot)
        for_each_zero_copy(lambda cp: cp.wait())


def moe_dispatch(x2d, mod, g, pos_tiles, pad_rows, n_rows, *, tm, tiles_per_mod):
    m, d = x2d.shape
    n_mod = mod.shape[0]
    mod_idx = (lambda i: (i // tiles_per_mod, 0, 0)) if n_mod > 1 else (lambda i: (0, 0, 0))
    return pl.pallas_call(
        _dispatch_kernel,
        grid=(m // tm,),
        in_specs=[
            pl.BlockSpec((1, 1, 2 * tm), lambda i: (i, 0, 0), memory_space=pltpu.SMEM),
            pl.BlockSpec((1, 2 * N_EXPERTS + 1), lambda i: (0, 0), memory_space=pltpu.SMEM),
            pl.BlockSpec((tm, d), lambda i: (i, 0)),
            pl.BlockSpec((1, 6, d), mod_idx),
            pl.BlockSpec((1, d), lambda i: (0, 0)),
        ],
        out_specs=pl.BlockSpec(memory_space=pl.ANY),
        out_shape=jax.ShapeDtypeStruct((n_rows, d), F32),
        scratch_shapes=[pltpu.VMEM((2, tm, d), F32), pltpu.VMEM((tm, d), F32),
                        pltpu.SemaphoreType.DMA((2,)), pltpu.SemaphoreType.DMA(())],
        compiler_params=_cparams(1),
        name="moe_dispatch",
    )(pos_tiles, pad_rows, x2d, mod, g.reshape(1, d))


def _experts_kernel(te_ref, nv_ref, xs_ref, w1_ref, w3_ref, w2_ref, ys_ref, xb_ref, acc_ref):
    i = pl.program_id(0)
    j = pl.program_id(1)
    valid = i < nv_ref[0]

    @pl.when(jnp.logical_and(valid, j == 0))
    def _():
        xb_ref[...] = xs_ref[...].astype(BF16)
        acc_ref[...] = jnp.zeros(acc_ref.shape, F32)

    @pl.when(valid)
    def _():
        h = xb_ref[...]
        tf = w1_ref.shape[2]
        y = None
        for c0 in range(0, tf, 1024):
            cw = min(1024, tf - c0)
            a = jnp.dot(h, w1_ref[0, :, c0:c0 + cw], preferred_element_type=F32)
            b = jnp.dot(h, w3_ref[0, :, c0:c0 + cw], preferred_element_type=F32)
            yc = jnp.dot((_silu(a) * b).astype(BF16), w2_ref[0, c0:c0 + cw, :], preferred_element_type=F32)
            y = yc if y is None else y + yc
        acc_ref[...] += y

    @pl.when(jnp.logical_and(valid, j == pl.num_programs(1) - 1))
    def _():
        ys_ref[...] = acc_ref[...]

    @pl.when(jnp.logical_and(jnp.logical_not(valid), j == pl.num_programs(1) - 1))
    def _():
        ys_ref[...] = jnp.zeros(ys_ref.shape, F32)


def moe_experts_grouped(xs, tile_expert, n_valid, w1, w3, w2, *, tm, tf):
    n_rows, d = xs.shape
    dff = w1.shape[2]
    nf = dff // tf

    def w13_idx(i, j, te, nv):
        return (te[i], 0, jnp.where(i < nv[0], j, nf - 1))

    def w2_idx(i, j, te, nv):
        return (te[i], jnp.where(i < nv[0], j, nf - 1), 0)

    grid_spec = pltpu.PrefetchScalarGridSpec(
        num_scalar_prefetch=2,
        grid=(n_rows // tm, nf),
        in_specs=[
            pl.BlockSpec((tm, d), lambda i, j, te, nv: (jnp.minimum(i, nv[0] - 1), 0)),
            pl.BlockSpec((1, d, tf), w13_idx),
            pl.BlockSpec((1, d, tf), w13_idx),
            pl.BlockSpec((1, tf, d), w2_idx),
        ],
        out_specs=pl.BlockSpec((tm, d), lambda i, j, te, nv: (i, 0)),
        scratch_shapes=[pltpu.VMEM((tm, d), BF16), pltpu.VMEM((tm, d), F32)],
    )
    return pl.pallas_call(
        _experts_kernel,
        grid_spec=grid_spec,
        out_shape=jax.ShapeDtypeStruct((n_rows, d), F32),
        compiler_params=_cparams(2),
        name="moe_experts_grouped",
    )(tile_expert, n_valid, xs, w1, w3, w2)


def _combine_kernel(pos_ref, pos_next_ref, x_ref, mod_ref, rt_ref, ys_ref, o_ref, y_ref, sem):
    i = pl.program_id(0)
    t = x_ref.shape[0]
    slot = i % 2

    def issue_tile(p_ref, s):
        def issue(r, carry):
            for k in range(2):
                pltpu.make_async_copy(ys_ref.at[pl.ds(p_ref[0, 0, k * t + r], 1), :],
                                      y_ref.at[s, k, pl.ds(r, 1), :], sem.at[s]).start()
            return carry

        lax.fori_loop(0, t, issue, 0, unroll=DMA_ISSUE_UNROLL)

    @pl.when(i == 0)
    def _():
        issue_tile(pos_ref, 0)

    @pl.when(i + 1 < pl.num_programs(0))
    def _():
        issue_tile(pos_next_ref, 1 - slot)

    for k in range(2):
        pltpu.make_async_copy(ys_ref.at[pl.ds(0, t), :], y_ref.at[slot, k], sem.at[slot]).wait()
    rt = rt_ref[...]
    mix = rt[:, R_W1:R_W1 + 1] * y_ref[slot, 0] + rt[:, R_W2:R_W2 + 1] * y_ref[slot, 1]
    o_ref[...] = x_ref[...] + mod_ref[0, 5:6, :] * mix


def moe_combine(x2d, mod, route, pos_tiles, ys, *, tm, tiles_per_mod):
    m, d = x2d.shape
    n_mod = mod.shape[0]
    mod_idx = (lambda i: (i // tiles_per_mod, 0, 0)) if n_mod > 1 else (lambda i: (0, 0, 0))
    n_tiles = m // tm
    return pl.pallas_call(
        _combine_kernel,
        grid=(n_tiles,),
        in_specs=[
            pl.BlockSpec((1, 1, 2 * tm), lambda i: (i, 0, 0), memory_space=pltpu.SMEM),
            pl.BlockSpec((1, 1, 2 * tm), lambda i: (jnp.minimum(i + 1, n_tiles - 1), 0, 0), memory_space=pltpu.SMEM),
            pl.BlockSpec((tm, d), lambda i: (i, 0)),
            pl.BlockSpec((1, 6, d), mod_idx),
            pl.BlockSpec((tm, 128), lambda i: (i, 0)),
            pl.BlockSpec(memory_space=pl.ANY),
        ],
        out_specs=pl.BlockSpec((tm, d), lambda i: (i, 0)),
        out_shape=jax.ShapeDtypeStruct((m, d), F32),
        scratch_shapes=[pltpu.VMEM((2, 2, tm, d), F32), pltpu.SemaphoreType.DMA((2,))],
        compiler_params=_cparams(1),
        name="moe_combine",
    )(pos_tiles, pos_tiles, x2d, mod, route, ys)


def _pos_tiles(pos1, pos2, tm):
    n = pos1.shape[0] // tm
    return jnp.concatenate([pos1.reshape(n, 1, tm), pos2.reshape(n, 1, tm)], axis=2)


def moe_sparse(x2d, mod, g, router_pad, w1, w3, w2, routing=None, *, rows_per_mod):
    m, d = x2d.shape
    tr, td, tc, te = MOE_ROUTE_TM, MOE_DISPATCH_TM, MOE_COMBINE_TM, MOE_EXPERT_TM
    if routing is None:
        routing = moe_route(x2d, mod, g, router_pad, tm=tr, tiles_per_mod=rows_per_mod // tr)
    route, route_t, cnt = routing
    counts = cnt[0, 0:N_EXPERTS].astype(jnp.int32)
    group = ((counts + te - 1) // te) * te
    ends = jnp.cumsum(group)
    starts = ends - group
    field = lambda f: route_t[:, f, :].reshape(m).astype(jnp.int32)
    pos1 = starts[field(R_E1)] + field(R_RANK1)
    pos2 = starts[field(R_E2)] + field(R_RANK2)
    n_rows = 2 * m + N_EXPERTS * te
    n_tiles = n_rows // te
    tile_start = jnp.arange(n_tiles, dtype=jnp.int32)[:, None] * te
    tile_expert = jnp.minimum(jnp.sum((tile_start >= ends[None, :]).astype(jnp.int32), axis=1), N_EXPERTS - 1)
    n_valid = (ends[-1:] // te).astype(jnp.int32)
    assert td == te
    pad_rows = jnp.concatenate([starts + counts, ends, n_valid]).astype(jnp.int32).reshape(1, 2 * N_EXPERTS + 1)
    xs = moe_dispatch(x2d, mod, g, _pos_tiles(pos1, pos2, td), pad_rows, n_rows, tm=td,
                      tiles_per_mod=rows_per_mod // td)
    ys = moe_experts_grouped(xs, tile_expert, n_valid, w1, w3, w2, tm=te, tf=MOE_EXPERT_TF)
    return moe_combine(x2d, mod, route, _pos_tiles(pos1, pos2, tc), ys, tm=tc, tiles_per_mod=rows_per_mod // tc)


def _permute_w_in(w):
    f, c, p, q, kv, gts = w[:, 0:256], w[:, 256:768], w[:, 768:1024], w[:, 1024:1536], w[:, 1536:1792], w[:, 1792:]
    return jnp.concatenate([gts, q, c, p, f, kv], axis=1).astype(BF16)


def _layer_weights(layer, w_br_fourier, conv_dw, conv_b, conv_norm_g, w_br_conv, pool_w, pool_scale, w_br_pool,
                   w_br_attn, w_out):
    pw = jax.scipy.linalg.block_diag(*[pool_w[layer, i] for i in range(len(POOL_WINDOWS))])
    return {
        "wf": w_br_fourier[layer].astype(BF16), "wc": w_br_conv[layer].astype(BF16),
        "wp": w_br_pool[layer].astype(BF16), "wa": w_br_attn[layer].astype(BF16), "wo": w_out[layer].astype(BF16),
        "dw": conv_dw[layer], "cb": conv_b[layer].reshape(1, CONV_W), "cg": conv_norm_g[layer].reshape(1, CONV_W),
        "pw": pw.astype(BF16), "ps": pool_scale[layer].reshape(1, POOL_W),
    }


def _cast_slices(w, max_slices):
    w2 = w.reshape(-1, w.shape[-1])
    rows = w2.shape[0]
    n = 1
    while 2 * n <= max_slices and rows % (32 * n) == 0:
        n *= 2
    return w2.reshape(n, rows // n, w2.shape[1])


def kernel(x, c, ctx, c_ctx, w_mod, b_mod, norm1_g, norm2_g, w_in, w_br_fourier, conv_dw, conv_b, conv_norm_g,
           w_br_conv, pool_w, pool_scale, w_br_pool, q_norm_g, k_norm_g, w_br_attn, w_out, ffn_w1, ffn_w3, ffn_w2,
           moe_router, moe_w1, moe_w3, moe_w2):
    batch, seq, d = x.shape
    ctx_len = ctx.shape[1]
    depth = w_in.shape[0]
    rope = rope_tables(seq)

    c_rows = jnp.zeros((8, d), F32).at[0:batch].set(c).at[batch].set(c_ctx)
    mods = modulation_all(c_rows, w_mod, b_mod).reshape(depth, 8, 6, d)

    xl = x.reshape(batch * seq, d)
    xc = ctx.reshape(batch * ctx_len, d)
    for layer in range(depth):
        is_last = layer == depth - 1
        mod_l = mods[layer, 0:batch]
        mod_c = mods[layer, batch:batch + 1]
        w_in_l = _permute_w_in(w_in[layer])
        lw = _layer_weights(layer, w_br_fourier, conv_dw, conv_b, conv_norm_g, w_br_conv, pool_w, pool_scale,
                            w_br_pool, w_br_attn, w_out)

        proj_c, qc, ktc, vc = input_projection(xc, mod_c, norm1_g[layer], w_in_l, q_norm_g[layer], k_norm_g[layer],
                                               None, batch=batch, seq=ctx_len, tm=256)

        proj, q, kt, v = input_projection(xl, mod_l, norm1_g[layer], w_in_l, q_norm_g[layer], k_norm_g[layer], rope,
                                          batch=batch, seq=seq, tm=512)
        spread = (2.0 * 1.02 * HEAD_DIM * Q_SCALE) * jnp.max(jnp.abs(q_norm_g[layer])) * jnp.max(
            jnp.abs(k_norm_g[layer]))
        n_steps = batch * N_KV_HEADS * (seq // ATTN_TQ)
        mixer_w = ((ffn_w1, ffn_w3, ffn_w2) if layer % 2 == 0 else (moe_w1, moe_w3, moe_w2))
        mixer_w = tuple(w[layer // 2] for w in mixer_w)
        side = tuple(_cast_slices(w, n_steps) for w in mixer_w)
        attn, *side_bf16 = lax.cond(
            spread < STALE_MAX_EXP_LIMIT,
            lambda ops, sd: attention_stale_max(*ops, sd, batch=batch, seq_q=seq, tq=ATTN_TQ, tk=ATTN_TK),
            lambda ops, sd: (attention(*ops, batch=batch, seq_q=seq, tq=ATTN_TQ, tk=ATTN_TK),)
            + tuple(w.astype(BF16) for w in sd),
            (q, kt, v, ktc, vc), side)
        yf = fourier_mix(proj[:, P_OFF_F:P_OFF_F + FOURIER_W], batch=batch, seq=seq, n1=64, n2=seq // 64)
        routing = router_pad = None
        if layer % 2 == 1:
            router_pad = jnp.zeros((d, 128), F32).at[:, 0:N_EXPERTS].set(moe_router[layer // 2])
            xl, *routing = merge_branches(xl, mod_l, proj, yf, attn, lw, (norm2_g[layer], router_pad), seq=seq, t=512)
        else:
            xl = merge_branches(xl, mod_l, proj, yf, attn, lw, seq=seq, t=512)

        if not is_last:
            attn_c = attention(qc, None, None, ktc, vc, batch=batch, seq_q=ctx_len, tq=256, tk=ctx_len)
            yf_c = fourier_mix(proj_c[:, P_OFF_F:P_OFF_F + FOURIER_W], batch=batch, seq=ctx_len, n1=16,
                               n2=ctx_len // 16)
            xc = merge_branches(xc, mod_c, proj_c, yf_c, attn_c, lw, seq=ctx_len, t=256)

        j = layer // 2
        w1, w3, w2 = (wb.reshape(w.shape) for wb, w in zip(side_bf16, mixer_w))
        if layer % 2 == 0:
            xl = ffn_dense(xl, mod_l, norm2_g[layer], w1, w3, w2, tm=512, tiles_per_mod=seq // 512)
            if not is_last:
                xc = ffn_dense(xc, mod_c, norm2_g[layer], w1, w3, w2, tm=256, tiles_per_mod=1)
        else:
            xl = moe_sparse(xl, mod_l, norm2_g[layer], router_pad, w1, w3, w2, tuple(routing), rows_per_mod=seq)
            if not is_last:
                xc = moe_sparse(xc, mod_c, norm2_g[layer], router_pad, w1, w3, w2, rows_per_mod=batch * ctx_len)
    return xl.reshape(batch, seq, d)
```

```python
import functools
import math

import numpy as np
import jax
import jax.numpy as jnp
from jax import lax
from jax.experimental import pallas as pl
from jax.experimental.pallas import tpu as pltpu

F32 = jnp.float32
BF16 = jnp.bfloat16

D_MODEL = 1024
GRID_W = 64
EPS = 1e-6
FOURIER_GW = 64
FOURIER_W = 256
CONV_W = 256
CONV_K = 31
CONV_HALF = CONV_K // 2
POOL_WINDOWS = (2, 4, 8, 16)
POOL_GW = 64
POOL_W = 256
HEAD_DIM = 64
N_KV_HEADS = 2
Q_PER_KV = 4
Q_W = 512
KV_W = 128
ROPE_THETA = 10000.0
N_EXPERTS = 8

P_OFF_Q = 4096
P_OFF_CP = 4608
P_OFF_F = 5376
P_OFF_KV = 5632
CP_W = 2 * CONV_W + POOL_W

Q_SCALE = (HEAD_DIM ** -0.5) * math.log2(math.e)

ATTN_TQ = 512
ATTN_TK = 1024
STALE_MAX_EXP_LIMIT = 64.0

MOE_ROUTE_TM = 512
MOE_DISPATCH_TM = 512
MOE_COMBINE_TM = 512
MOE_EXPERT_TM = 512
MOE_EXPERT_TF = 1792
DMA_ISSUE_UNROLL = 8

CONV_ROWS = 64
HALO = 16
VMEM_LIMIT = 56 * 1024 * 1024


def _cparams(n_axes):
    return pltpu.CompilerParams(dimension_semantics=("arbitrary",) * n_axes, vmem_limit_bytes=VMEM_LIMIT)


def _sigmoid(v):
    return 0.5 * jnp.tanh(0.5 * v) + 0.5


def _silu(v):
    return v * _sigmoid(v)


def _norm_mod(x, g, shift, scale):
    ms = jnp.mean(x * x, axis=-1, keepdims=True)
    return x * lax.rsqrt(ms + EPS) * g * (1.0 + scale) + shift


def _mod_kernel(c_ref, w_ref, b_ref, o_ref):
    s = _silu(c_ref[...])
    o_ref[0] = jnp.dot(s, w_ref[0], preferred_element_type=F32, precision=lax.Precision.HIGHEST) + b_ref[0]


def modulation_all(c_rows, w_mod, b_mod):
    n_layers, d, n = w_mod.shape
    tn = 1536
    return pl.pallas_call(
        _mod_kernel,
        grid=(n_layers, n // tn),
        in_specs=[
            pl.BlockSpec((8, d), lambda l, j: (0, 0)),
            pl.BlockSpec((1, d, tn), lambda l, j: (l, 0, j)),
            pl.BlockSpec((1, 1, tn), lambda l, j: (l, 0, j)),
        ],
        out_specs=pl.BlockSpec((1, 8, tn), lambda l, j: (l, 0, j)),
        out_shape=jax.ShapeDtypeStruct((n_layers, 8, n), F32),
        compiler_params=_cparams(2),
        name="modulation",
    )(c_rows, w_mod, b_mod.reshape(n_layers, 1, n))


def _inproj_kernel(*refs, chunks, use_rope):
    if use_rope:
        x_ref, mod_ref, g_ref, w_ref, gq_ref, gk_ref, ones_ref, cos_ref, sin_ref, o_ref, qo_ref, kt_ref, v_ref = refs
        cos, sin = cos_ref[...], sin_ref[...]
    else:
        x_ref, mod_ref, g_ref, w_ref, gq_ref, gk_ref, ones_ref, o_ref, qo_ref, kt_ref, v_ref = refs
        cos = sin = None
    h = _norm_mod(x_ref[...], g_ref[...], mod_ref[0, 0:1, :], mod_ref[0, 1:2, :]).astype(BF16)
    heavy_first = sorted(chunks, key=lambda c: (c[0] not in (P_OFF_Q, P_OFF_KV), c[0] != P_OFF_CP, c[0]))
    for c0, cw in heavy_first:
        r = jnp.dot(h, w_ref[:, c0:c0 + cw], preferred_element_type=F32)
        if c0 + cw <= P_OFF_Q:
            r = _sigmoid(r)
        elif c0 == P_OFF_CP:
            sg = _sigmoid(r[:, CONV_W:2 * CONV_W])
            r = jnp.concatenate([r[:, 0:CONV_W] * sg, sg], axis=1)
        elif c0 == P_OFF_Q:
            _q_epilogue(r, gq_ref[...], ones_ref[...], cos, sin, qo_ref)
        elif c0 == P_OFF_KV:
            _kv_epilogue(r, gk_ref[...], ones_ref[...], cos, sin, kt_ref, v_ref)
        o_ref[:, c0:c0 + cw] = r.astype(o_ref.dtype)


def input_projection(x2d, mod, g, w_bf16, gq, gk, rope, *, batch, seq, tm):
    m, d = x2d.shape
    n = w_bf16.shape[1]
    tps = seq // tm
    chunks = tuple((c0, min(512, n - c0)) for c0 in range(0, n, 512))
    assert P_OFF_Q % 512 == 0 and {(P_OFF_CP, 2 * CONV_W), (P_OFF_Q, Q_W), (P_OFF_KV, 2 * KV_W)} <= set(chunks)
    use_rope = rope is not None
    n_mod = mod.shape[0]
    mod_idx = (lambda i: (i // tps, 0, 0)) if n_mod > 1 else (lambda i: (0, 0, 0))
    const = lambda i: (0, 0)
    ones_bd = jnp.asarray(np.kron(np.eye(2, dtype=np.float32), np.ones((64, 64), np.float32)), BF16)
    in_specs = [
        pl.BlockSpec((tm, d), lambda i: (i, 0)),
        pl.BlockSpec((1, 6, d), mod_idx),
        pl.BlockSpec((1, d), const),
        pl.BlockSpec((d, n), const, pipeline_mode=pl.Buffered(1)),
        pl.BlockSpec((1, 128), const),
        pl.BlockSpec((1, 128), const),
        pl.BlockSpec((128, 128), const),
    ]
    args = [x2d, mod, g.reshape(1, d), w_bf16, jnp.tile(gq, 2).reshape(1, 128), jnp.tile(gk, 2).reshape(1, 128), ones_bd]
    if use_rope:
        in_specs += [pl.BlockSpec((tm, 128), lambda i: (i % tps, 0))] * 2
        args += list(rope)
    return pl.pallas_call(
        functools.partial(_inproj_kernel, chunks=chunks, use_rope=use_rope),
        grid=(m // tm,),
        in_specs=in_specs,
        out_specs=[
            pl.BlockSpec((tm, n), lambda i: (i, 0)),
            pl.BlockSpec((1, N_KV_HEADS, 256, tm), lambda i: (i // tps, 0, 0, i % tps)),
            pl.BlockSpec((1, N_KV_HEADS, tm, 256), lambda i: (i // tps, 0, i % tps, 0)),
            pl.BlockSpec((1, N_KV_HEADS, 128, tm), lambda i: (i // tps, 0, 0, i % tps)),
        ],
        out_shape=[
            jax.ShapeDtypeStruct((m, n), BF16),
            jax.ShapeDtypeStruct((batch, N_KV_HEADS, 256, seq), BF16),
            jax.ShapeDtypeStruct((batch, N_KV_HEADS, seq, 256), BF16),
            jax.ShapeDtypeStruct((batch, N_KV_HEADS, 128, seq), BF16),
        ],
        compiler_params=_cparams(1),
        name="input_projection",
    )(*args)


def _seg_sum64(v, ones_bd):
    hi = v.astype(BF16)
    lo = (v - hi.astype(F32)).astype(BF16)
    return (jnp.dot(hi, ones_bd, preferred_element_type=F32) + jnp.dot(lo, ones_bd, preferred_element_type=F32))


def _head_norm_rope(x, g, ones_bd, cos, sin, low_mask):
    y = x * lax.rsqrt(_seg_sum64(x * x, ones_bd) * (1.0 / HEAD_DIM) + EPS) * g
    if cos is None:
        return y
    partner = jnp.where(low_mask, pltpu.roll(y, 128 - 16, axis=1), pltpu.roll(y, 16, axis=1))
    return y * cos + partner * sin


def _rope_low_mask(t):
    return (lax.broadcasted_iota(jnp.int32, (t, 128), 1) % 32) < 16


def _q_epilogue(rq, gq, ones_bd, cos, sin, qo_ref):
    low_mask = _rope_low_mask(rq.shape[0])
    for c in range(Q_W // 128):
        yq = _head_norm_rope(rq[:, 128 * c:128 * (c + 1)], gq, ones_bd, cos, sin, low_mask) * Q_SCALE
        qo_ref[0, c // 2, 128 * (c % 2):128 * (c % 2 + 1), :] = yq.T.astype(BF16)


def _kv_epilogue(rkv, gk, ones_bd, cos, sin, kt_ref, v_ref):
    t = rkv.shape[0]
    yk = _head_norm_rope(rkv[:, 0:128], gk, ones_bd, cos, sin, _rope_low_mask(t))
    ykr = pltpu.roll(yk, 64, axis=1)
    first = lax.broadcasted_iota(jnp.int32, (t, 128), 1) < 64
    k0 = jnp.where(first, yk, ykr).astype(BF16)
    k1 = jnp.where(first, ykr, yk).astype(BF16)
    kt_ref[0, 0] = jnp.concatenate([k0, k0], axis=1)
    kt_ref[0, 1] = jnp.concatenate([k1, k1], axis=1)
    vt = rkv[:, 128:256].T
    ones = jnp.ones((HEAD_DIM, t), F32)
    for h in range(N_KV_HEADS):
        v_ref[0, h] = jnp.concatenate([vt[64 * h:64 * (h + 1), :], ones], axis=0).astype(BF16)


def rope_tables(seq):
    n_freq = HEAD_DIM // 4
    freqs = ROPE_THETA ** (-jnp.arange(n_freq, dtype=F32) / n_freq)
    t = jnp.arange(seq)
    row = (t // GRID_W).astype(F32)
    col = (t % GRID_W).astype(F32)
    ang_r = row[:, None] * freqs
    ang_c = col[:, None] * freqs
    cos = jnp.concatenate([jnp.cos(ang_r)] * 2 + [jnp.cos(ang_c)] * 2, axis=1)
    sin = jnp.concatenate([-jnp.sin(ang_r), jnp.sin(ang_r), -jnp.sin(ang_c), jnp.sin(ang_c)], axis=1)
    return jnp.tile(cos, (1, 2)), jnp.tile(sin, (1, 2))


def _attn_kernel(*refs, tq, tk, nk, tail):
    refs = list(refs)
    qt_ref = refs.pop(0)
    k_ref, vt_ref = (refs.pop(0), refs.pop(0)) if nk else (None, None)
    kc_ref, vtc_ref = (refs.pop(0), refs.pop(0)) if tail else (None, None)
    o_ref, qs_ref, s0, s1, p0, p1, a0, a1, mx0, mx1, m_ref, acc_ref = refs
    s_bufs, p_bufs, a_bufs, mx_bufs = (s0, s1), (p0, p1), (a0, a1), (mx0, mx1)
    n_blocks = nk + (1 if tail else 0)

    _attn_stack_queries(qt_ref, qs_ref, tq)
    m_ref[...] = jnp.full(m_ref.shape, -jnp.inf, F32)
    acc_ref[...] = jnp.zeros(acc_ref.shape, F32)

    def block(t):
        if isinstance(t, int) and t >= nk:
            return kc_ref[0, 0], vtc_ref[0, 0], tail
        off = t * tk if isinstance(t, int) else pl.multiple_of(t * tk, tk)
        return k_ref[0, 0, pl.ds(off, tk), :], vt_ref[0, 0, :, pl.ds(off, tk)], tk

    def scores(t, slot):
        k_rows, _, n = block(t)
        s = jnp.dot(k_rows, qs_ref[...], preferred_element_type=F32)
        s_bufs[slot][0:n, :] = s
        mx_bufs[slot][...] = jnp.max(s, axis=0, keepdims=True)

    def numerators(n, slot):
        s_ref, p_ref, a_ref = s_bufs[slot], p_bufs[slot], a_bufs[slot]
        for c0 in range(0, Q_PER_KV * tq, 128):
            cols = slice(c0, c0 + 128)
            m_old = m_ref[:, cols]
            m_new = jnp.maximum(m_old, mx_bufs[slot][:, cols])
            a_ref[:, cols] = jnp.exp2(m_old - m_new)
            p_ref[0:n, cols] = jnp.exp2(s_ref[0:n, cols] - m_new).astype(BF16)
            m_ref[:, cols] = m_new

    def weighted_sum(t, slot):
        _, vt, n = block(t)
        pv = jnp.dot(vt, p_bufs[slot][0:n, :], preferred_element_type=F32)
        acc_ref[...] = a_bufs[slot][...] * acc_ref[...] + pv

    def rows_of(t):
        return tk if t < nk else tail

    def step(t, slot, n_mid):
        scores(t, slot)
        numerators(n_mid, 1 - slot)
        weighted_sum(t - 2, slot)

    scores(0, 0)
    if n_blocks > 1:
        scores(1, 1)
        numerators(rows_of(0), 0)
        n_pairs = max(nk - 2, 0) // 2

        def pair(i, carry):
            t = 2 + 2 * i
            step(t, 0, tk)
            step(t + 1, 1, tk)
            return carry

        if n_pairs:
            lax.fori_loop(0, n_pairs, pair, 0)
        for t in range(2 + 2 * n_pairs, n_blocks):
            step(t, t % 2, rows_of(t - 1))
        last = n_blocks - 1
        numerators(rows_of(last), last % 2)
        weighted_sum(last - 1, (last - 1) % 2)
        weighted_sum(last, last % 2)
    else:
        numerators(rows_of(0), 0)
        weighted_sum(0, 0)

    _attn_write_output(acc_ref, o_ref, tq)


def _attn_write_output(acc_ref, o_ref, tq):
    acc = acc_ref[...]
    ot = acc[0:HEAD_DIM, :] / acc[HEAD_DIM:2 * HEAD_DIM, :]
    for half in range(2):
        pair_t = jnp.concatenate([ot[:, (2 * half) * tq:(2 * half + 1) * tq],
                                  ot[:, (2 * half + 1) * tq:(2 * half + 2) * tq]], axis=0)
        o_ref[:, 128 * half:128 * (half + 1)] = pair_t.T.astype(o_ref.dtype)


def _attn_stack_queries(qt_ref, qs_ref, tq):
    row_group = lax.broadcasted_iota(jnp.int32, (256, tq), 0) // HEAD_DIM
    qt = qt_ref[0, 0]
    for g in range(Q_PER_KV):
        qs_ref[:, g * tq:(g + 1) * tq] = jnp.where(row_group == g, qt, jnp.zeros_like(qt))


def _attn_stale_max_kernel(*refs, tq, tk, nk, n_side):
    qt_ref, k_ref, vt_ref, kc_ref, vtc_ref = refs[:5]
    side_in = refs[5:5 + n_side]
    o_ref = refs[5 + n_side]
    side_out = refs[6 + n_side:6 + 2 * n_side]
    qs_ref, p0, p1, f0, f1, m_ref, acc_ref = refs[6 + 2 * n_side:]
    for src, dst in zip(side_in, side_out):
        dst[...] = src[...].astype(dst.dtype)
    p_bufs, f_bufs = (p0, p1), (f0, f1)
    _attn_stack_queries(qt_ref, qs_ref, tq)

    s = jnp.dot(kc_ref[0, 0], qs_ref[...], preferred_element_type=F32)
    m0 = jnp.max(s, axis=0, keepdims=True)
    m_ref[...] = m0
    acc_ref[...] = jnp.dot(vtc_ref[0, 0], jnp.exp2(s - m0).astype(BF16), preferred_element_type=F32)

    def numerators(t, slot):
        off = t * tk if isinstance(t, int) else pl.multiple_of(t * tk, tk)
        s = jnp.dot(k_ref[0, 0, pl.ds(off, tk), :], qs_ref[...], preferred_element_type=F32)
        m_old = m_ref[...]
        p_bufs[slot][...] = jnp.exp2(s - m_old).astype(BF16)
        m_new = jnp.maximum(m_old, jnp.max(s, axis=0, keepdims=True))
        f_bufs[slot][...] = jnp.exp2(m_old - m_new)
        m_ref[...] = m_new

    def weighted_sum(t, slot):
        off = t * tk if isinstance(t, int) else pl.multiple_of(t * tk, tk)
        pv = jnp.dot(vt_ref[0, 0, :, pl.ds(off, tk)], p_bufs[slot][...], preferred_element_type=F32)
        acc_ref[...] = (acc_ref[...] + pv) * f_bufs[slot][...]

    def step(t, slot):
        numerators(t, slot)
        weighted_sum(t, slot)

    n_pairs = nk // 2

    def pair(i, carry):
        t = 2 * i
        step(t, 0)
        step(t + 1, 1)
        return carry

    if n_pairs:
        lax.fori_loop(0, n_pairs, pair, 0)
    for t in range(2 * n_pairs, nk):
        step(t, t % 2)
    _attn_write_output(acc_ref, o_ref, tq)


def attention_stale_max(qt, k4, vt1, k4_tail, vt1_tail, side=(), *, batch, seq_q, tq, tk):
    nq = seq_q // tq
    lanes = Q_PER_KV * tq
    lk = k4.shape[2]
    tail = k4_tail.shape[2]
    n_steps = batch * N_KV_HEADS * nq
    assert all(n_steps % w.shape[0] == 0 for w in side)

    def side_spec(w):
        repeat = n_steps // w.shape[0]
        return pl.BlockSpec((1,) + w.shape[1:], lambda b, h, i: (((b * N_KV_HEADS + h) * nq + i) // repeat, 0, 0))

    side_specs = [side_spec(w) for w in side]
    outs = pl.pallas_call(
        functools.partial(_attn_stale_max_kernel, tq=tq, tk=tk, nk=lk // tk, n_side=len(side)),
        grid=(batch, N_KV_HEADS, nq),
        in_specs=[
            pl.BlockSpec((1, 1, 256, tq), lambda b, h, i: (b, h, 0, i)),
            pl.BlockSpec((1, 1, lk, 256), lambda b, h, i: (b, h, 0, 0)),
            pl.BlockSpec((1, 1, 128, lk), lambda b, h, i: (b, h, 0, 0)),
            pl.BlockSpec((1, 1, tail, 256), lambda b, h, i: (b, h, 0, 0)),
            pl.BlockSpec((1, 1, 128, tail), lambda b, h, i: (b, h, 0, 0)),
        ] + side_specs,
        out_specs=[pl.BlockSpec((tq, 256), lambda b, h, i: (b * nq + i, h))] + side_specs,
        out_shape=[jax.ShapeDtypeStruct((batch * seq_q, Q_W), BF16)]
        + [jax.ShapeDtypeStruct(w.shape, BF16) for w in side],
        scratch_shapes=[
            pltpu.VMEM((256, lanes), BF16),
            pltpu.VMEM((tk, lanes), BF16), pltpu.VMEM((tk, lanes), BF16),
            pltpu.VMEM((1, lanes), F32), pltpu.VMEM((1, lanes), F32),
            pltpu.VMEM((1, lanes), F32),
            pltpu.VMEM((2 * HEAD_DIM, lanes), F32),
        ],
        compiler_params=_cparams(3),
        name="attention_stale_max",
    )(qt, k4, vt1, k4_tail, vt1_tail, *side)
    return tuple(outs)


def attention(qt, k4, vt1, k4_tail, vt1_tail, *, batch, seq_q, tq, tk):
    nq = seq_q // tq
    lanes = Q_PER_KV * tq
    nk = 0 if k4 is None else k4.shape[2] // tk
    tail = 0 if k4_tail is None else k4_tail.shape[2]
    buf_rows = max(tk if nk else 0, tail)
    in_specs = [pl.BlockSpec((1, 1, 256, tq), lambda b, h, i: (b, h, 0, i))]
    args = [qt]
    if nk:
        lk = k4.shape[2]
        in_specs += [pl.BlockSpec((1, 1, lk, 256), lambda b, h, i: (b, h, 0, 0)),
                     pl.BlockSpec((1, 1, 128, lk), lambda b, h, i: (b, h, 0, 0))]
        args += [k4, vt1]
    if tail:
        in_specs += [pl.BlockSpec((1, 1, tail, 256), lambda b, h, i: (b, h, 0, 0)),
                     pl.BlockSpec((1, 1, 128, tail), lambda b, h, i: (b, h, 0, 0))]
        args += [k4_tail, vt1_tail]
    return pl.pallas_call(
        functools.partial(_attn_kernel, tq=tq, tk=tk, nk=nk, tail=tail),
        grid=(batch, N_KV_HEADS, nq),
        in_specs=in_specs,
        out_specs=pl.BlockSpec((tq, 256), lambda b, h, i: (b * nq + i, h)),
        out_shape=jax.ShapeDtypeStruct((batch * seq_q, Q_W), BF16),
        scratch_shapes=[
            pltpu.VMEM((256, lanes), BF16),
            pltpu.VMEM((buf_rows, lanes), F32), pltpu.VMEM((buf_rows, lanes), F32),
            pltpu.VMEM((buf_rows, lanes), BF16), pltpu.VMEM((buf_rows, lanes), BF16),
            pltpu.VMEM((1, lanes), F32), pltpu.VMEM((1, lanes), F32),
            pltpu.VMEM((1, lanes), F32), pltpu.VMEM((1, lanes), F32),
            pltpu.VMEM((1, lanes), F32),
            pltpu.VMEM((2 * HEAD_DIM, lanes), F32),
        ],
        compiler_params=_cparams(3),
        name="attention",
    )(*args)


def _dft_cs(n):
    k = np.arange(n)
    ang = 2.0 * np.pi * ((k[:, None] * k[None, :]) % n) / n
    return np.cos(ang), np.sin(ang)


def _fft1_kernel(x_ref, f_ref, c_ref, s_ref, o_ref, *, n1):
    y = jnp.dot(f_ref[...], x_ref[0], preferred_element_type=F32)
    yr, yi = y[:n1], y[n1:]
    c, s = c_ref[...], s_ref[...]
    o_ref[0, 0] = (yr * c + yi * s).astype(o_ref.dtype)
    o_ref[0, 1] = (yi * c - yr * s).astype(o_ref.dtype)


def _fft2_kernel(y_ref, f_ref, bc_ref, bs_ref, o_ref, *, n2, kb):
    for j in range(kb):
        y2 = jnp.concatenate([y_ref[0, 0, j], y_ref[0, 1, j]], axis=0)
        x2 = jnp.dot(f_ref[...], y2, preferred_element_type=F32)
        xr = x2[:n2].astype(BF16)
        xi = x2[n2:].astype(BF16)
        z = (jnp.dot(xr, bc_ref[...], preferred_element_type=F32) + jnp.dot(xi, bs_ref[...], preferred_element_type=F32))
        o_ref[0, j] = z.astype(o_ref.dtype)


def fourier_mix(u, *, batch, seq, n1, n2):
    cw = u.shape[1]
    lanes = n2 * cw
    tl = min(lanes, 4096)
    c1, s1 = _dft_cs(n1)
    f1 = jnp.asarray(np.concatenate([c1, -s1], axis=0), BF16)
    k1 = np.arange(n1)[:, None]
    t2 = np.arange(n2)[None, :]
    ang = 2.0 * np.pi * ((k1 * t2) % seq) / seq
    twc = jnp.asarray(np.repeat(np.cos(ang), cw, axis=1), F32)
    tws = jnp.asarray(np.repeat(np.sin(ang), cw, axis=1), F32)
    x2 = u.reshape(batch, n1, lanes)
    yp = pl.pallas_call(
        functools.partial(_fft1_kernel, n1=n1),
        grid=(batch, lanes // tl),
        in_specs=[
            pl.BlockSpec((1, n1, tl), lambda b, j: (b, 0, j)),
            pl.BlockSpec((2 * n1, n1), lambda b, j: (0, 0)),
            pl.BlockSpec((n1, tl), lambda b, j: (0, j)),
            pl.BlockSpec((n1, tl), lambda b, j: (0, j)),
        ],
        out_specs=pl.BlockSpec((1, 2, n1, tl), lambda b, j: (b, 0, 0, j)),
        out_shape=jax.ShapeDtypeStruct((batch, 2, n1, lanes), BF16),
        compiler_params=_cparams(2),
        name="fft_stage1",
    )(x2, f1, twc, tws)

    c2, s2 = _dft_cs(n2)
    f2 = jnp.asarray(np.block([[c2, s2], [-s2, c2]]), BF16)
    cg, sg = _dft_cs(FOURIER_GW)
    norm = 1.0 / math.sqrt(seq * FOURIER_GW)
    bdc = jnp.asarray(np.kron(np.eye(cw // FOURIER_GW), cg) * norm, BF16)
    bds = jnp.asarray(np.kron(np.eye(cw // FOURIER_GW), sg) * norm, BF16)
    kb = min(n1, 16)
    y5 = yp.reshape(batch, 2, n1, n2, cw)
    z = pl.pallas_call(
        functools.partial(_fft2_kernel, n2=n2, kb=kb),
        grid=(batch, n1 // kb),
        in_specs=[
            pl.BlockSpec((1, 2, kb, n2, cw), lambda b, j: (b, 0, j, 0, 0)),
            pl.BlockSpec((2 * n2, 2 * n2), lambda b, j: (0, 0)),
            pl.BlockSpec((cw, cw), lambda b, j: (0, 0)),
            pl.BlockSpec((cw, cw), lambda b, j: (0, 0)),
        ],
        out_specs=pl.BlockSpec((1, kb, n2, cw), lambda b, j: (b, j, 0, 0)),
        out_shape=jax.ShapeDtypeStruct((batch, n1, n2, cw), BF16),
        compiler_params=_cparams(2),
        name="fft_stage2",
    )(y5, f2, bdc, bds)
    return z.transpose(0, 2, 1, 3).reshape(batch * seq, cw)


def _merge_kernel(x_ref, mod_ref, gate_ref, cp_ref, cpp_ref, cpn_ref, yf_ref, at_ref,
                  wf_ref, wc_ref, wp_ref, wa_ref, wo_ref, dw_ref, cb_ref, cg_ref, pw_ref, ps_ref, band_ref, icnt_ref,
                  o_ref, ybuf, xbuf, ysh, cacc, *, t, tps):
    i = pl.program_id(0)
    pos_tile = i % tps
    keep_prev = jnp.where(pos_tile != 0, 1.0, 0.0).astype(F32)
    keep_next = jnp.where(pos_tile != tps - 1, 1.0, 0.0).astype(F32)

    def glu(blk):
        return blk[:, 0:CONV_W].astype(F32)

    cp, cpp, cpn = cp_ref[...], cpp_ref[...], cpn_ref[...]
    ybuf[0:HALO, :] = glu(cpp) * keep_prev
    ybuf[HALO:HALO + t, :] = glu(cp)
    ybuf[HALO + t:HALO + t + HALO, :] = glu(cpn) * keep_next
    xbuf[0:HALO, :] = cpp[:, 2 * CONV_W:] * keep_prev.astype(BF16)
    xbuf[HALO:HALO + t, :] = cp[:, 2 * CONV_W:]
    xbuf[HALO + t:HALO + t + HALO, :] = cpn[:, 2 * CONV_W:] * keep_next.astype(BF16)

    n_sh = t + 2 * HALO - 8
    for b in range(1, 8):
        ysh[b - 1, 0:n_sh, :] = ybuf[pl.ds(b, n_sh), :]
    for r0 in range(0, t, CONV_ROWS):
        part = jnp.zeros((CONV_ROWS, CONV_W), F32)
        for k in range(CONV_K):
            a, b = divmod(HALO - CONV_HALF + k, 8)
            src = ybuf if b == 0 else ysh.at[b - 1]
            part = part + dw_ref[k:k + 1, :] * src[8 * a + r0:8 * a + r0 + CONV_ROWS, :]
        cacc[r0:r0 + CONV_ROWS, :] = part + cb_ref[...]
    acc = cacc[...]
    ms = jnp.mean(acc * acc, axis=-1, keepdims=True)
    conv_out = _silu(acc * lax.rsqrt(ms + EPS) * cg_ref[...]).astype(BF16)

    pb = band_ref.shape[1]
    grp = lax.broadcasted_iota(jnp.int32, (pb + 2 * HALO, POOL_W), 1) // POOL_GW
    parts = []
    for r0 in range(0, t, pb):
        xw = xbuf[r0:r0 + pb + 2 * HALO, :]
        wsum = jnp.zeros((pb, POOL_W), F32)
        for gi in range(len(POOL_WINDOWS)):
            wsum = wsum + jnp.dot(band_ref[gi], jnp.where(grp == gi, xw, jnp.zeros_like(xw)),
                                  preferred_element_type=F32)
        x0 = xbuf[HALO + r0:HALO + r0 + pb, :].astype(F32)
        parts.append((wsum * icnt_ref[0, r0:r0 + pb, :] - x0).astype(BF16))
    pool_in = parts[0] if len(parts) == 1 else jnp.concatenate(parts, axis=0)
    pool_out = (jnp.dot(pool_in, pw_ref[...], preferred_element_type=F32) * ps_ref[...]).astype(BF16)

    def gate(b):
        return gate_ref[:, b * D_MODEL:(b + 1) * D_MODEL].astype(F32)

    merged = gate(0) * jnp.dot(yf_ref[...], wf_ref[...], preferred_element_type=F32)
    merged = merged + gate(1) * jnp.dot(conv_out, wc_ref[...], preferred_element_type=F32)
    merged = merged + gate(2) * jnp.dot(pool_out, wp_ref[...], preferred_element_type=F32)
    merged = merged + gate(3) * jnp.dot(at_ref[...], wa_ref[...], preferred_element_type=F32)
    out = jnp.dot(merged.astype(BF16), wo_ref[...], preferred_element_type=F32)
    o_ref[...] = x_ref[...] + mod_ref[0, 2:3, :] * out


def _pool_tables(seq, t):
    pb = min(t, 256)
    r = np.arange(pb)[:, None]
    j = np.arange(pb + 2 * HALO)[None, :]
    band = np.stack([((j >= r + HALO - w // 2) & (j < r + HALO + w // 2)) for w in POOL_WINDOWS]).astype(np.float32)
    half = np.repeat(np.array(POOL_WINDOWS) // 2, POOL_GW)[None, :]
    rows = np.arange(t)[:, None]
    icnt = []
    for first, last in ((0, 0), (1, 0), (0, 1), (1, 1)):
        if first and last:
            pos, length = rows, t
        elif first:
            pos, length = rows, 2 * t + 2 * HALO
        elif last:
            pos, length = rows + seq - t, seq
        else:
            pos, length = rows + t + 2 * HALO, 4 * t
        cnt = np.minimum(pos + half, length) - np.maximum(pos - half, 0)
        icnt.append(1.0 / cnt)
    return jnp.asarray(band, BF16), jnp.asarray(np.stack(icnt), F32)


def merge_branches(x2d, mod, proj, yf, attn, lw, *, seq, t):
    m, d = x2d.shape
    tps = seq // t
    hb = t // HALO
    n_halo = m // HALO
    n_mod = mod.shape[0]
    mod_idx = (lambda i: (i // tps, 0, 0)) if n_mod > 1 else (lambda i: (0, 0, 0))
    const = lambda i: (0, 0)
    cp_blk = P_OFF_CP // CP_W
    band, icnt = _pool_tables(seq, t)
    pb = band.shape[1]

    def icnt_idx(i):
        pos_tile = i % tps
        return ((pos_tile == 0).astype(jnp.int32) + 2 * (pos_tile == tps - 1).astype(jnp.int32), 0, 0)

    return pl.pallas_call(
        functools.partial(_merge_kernel, t=t, tps=tps),
        grid=(m // t,),
        in_specs=[
            pl.BlockSpec((t, d), lambda i: (i, 0)),
            pl.BlockSpec((1, 6, d), mod_idx),
            pl.BlockSpec((t, 4 * d), lambda i: (i, 0)),
            pl.BlockSpec((t, CP_W), lambda i: (i, cp_blk)),
            pl.BlockSpec((HALO, CP_W), lambda i: (jnp.maximum(i * hb - 1, 0), cp_blk)),
            pl.BlockSpec((HALO, CP_W), lambda i: (jnp.minimum((i + 1) * hb, n_halo - 1), cp_blk)),
            pl.BlockSpec((t, FOURIER_W), lambda i: (i, 0)),
            pl.BlockSpec((t, Q_W), lambda i: (i, 0)),
            pl.BlockSpec((FOURIER_W, d), const),
            pl.BlockSpec((CONV_W, d), const),
            pl.BlockSpec((POOL_W, d), const),
            pl.BlockSpec((Q_W, d), const),
            pl.BlockSpec((d, d), const),
            pl.BlockSpec((CONV_K, CONV_W), const),
            pl.BlockSpec((1, CONV_W), const),
            pl.BlockSpec((1, CONV_W), const),
            pl.BlockSpec((POOL_W, POOL_W), const),
            pl.BlockSpec((1, POOL_W), const),
            pl.BlockSpec((len(POOL_WINDOWS), pb, pb + 2 * HALO), lambda i: (0, 0, 0)),
            pl.BlockSpec((1, t, POOL_W), icnt_idx),
        ],
        out_specs=pl.BlockSpec((t, d), lambda i: (i, 0)),
        out_shape=jax.ShapeDtypeStruct((m, d), F32),
        scratch_shapes=[pltpu.VMEM((t + 2 * HALO, CONV_W), F32), pltpu.VMEM((t + 2 * HALO, POOL_W), BF16),
                        pltpu.VMEM((7, t + 2 * HALO, CONV_W), F32), pltpu.VMEM((t, CONV_W), F32)],
        compiler_params=_cparams(1),
        name="merge_branches",
    )(x2d, mod, proj, proj, proj, proj, yf, attn,
      lw["wf"], lw["wc"], lw["wp"], lw["wa"], lw["wo"], lw["dw"], lw["cb"], lw["cg"], lw["pw"], lw["ps"], band, icnt)


def _ffn_kernel(x_ref, mod_ref, g_ref, w1_ref, w3_ref, w2_ref, o_ref, *, chunks):
    x = x_ref[...]
    h = _norm_mod(x, g_ref[...], mod_ref[0, 3:4, :], mod_ref[0, 4:5, :]).astype(BF16)
    acc = jnp.zeros(x.shape, F32)
    for c0, cw in chunks:
        a = jnp.dot(h, w1_ref[:, c0:c0 + cw], preferred_element_type=F32)
        b = jnp.dot(h, w3_ref[:, c0:c0 + cw], preferred_element_type=F32)
        acc = acc + jnp.dot((_silu(a) * b).astype(BF16), w2_ref[c0:c0 + cw, :], preferred_element_type=F32)
    o_ref[...] = x + mod_ref[0, 5:6, :] * acc


def ffn_dense(x2d, mod, g, w1, w3, w2, *, tm, tiles_per_mod):
    m, d = x2d.shape
    dff = w1.shape[1]
    chunks = tuple((c0, min(1024, dff - c0)) for c0 in range(0, dff, 1024))
    n_mod = mod.shape[0]
    mod_idx = (lambda i: (i // tiles_per_mod, 0, 0)) if n_mod > 1 else (lambda i: (0, 0, 0))
    const = lambda i: (0, 0)
    return pl.pallas_call(
        functools.partial(_ffn_kernel, chunks=chunks),
        grid=(m // tm,),
        in_specs=[
            pl.BlockSpec((tm, d), lambda i: (i, 0)),
            pl.BlockSpec((1, 6, d), mod_idx),
            pl.BlockSpec((1, d), const),
            pl.BlockSpec((d, dff), const, pipeline_mode=pl.Buffered(1)),
            pl.BlockSpec((d, dff), const, pipeline_mode=pl.Buffered(1)),
            pl.BlockSpec((dff, d), const, pipeline_mode=pl.Buffered(1)),
        ],
        out_specs=pl.BlockSpec((tm, d), lambda i: (i, 0)),
        out_shape=jax.ShapeDtypeStruct((m, d), F32),
        compiler_params=_cparams(1),
        name="ffn_dense",
    )(x2d, mod, g.reshape(1, d), w1, w3, w2)


def _top2(logits):
    t = logits.shape[0]
    lane = lax.broadcasted_iota(jnp.int32, (t, 128), 1).astype(F32)
    neg = jnp.float32(-jnp.inf)
    lg = jnp.where(lane < N_EXPERTS, logits, neg)
    v1 = jnp.max(lg, axis=-1, keepdims=True)
    i1 = jnp.min(jnp.where(lg == v1, lane, 128.0), axis=-1, keepdims=True)
    lg2 = jnp.where(lane == i1, neg, lg)
    v2 = jnp.max(lg2, axis=-1, keepdims=True)
    i2 = jnp.min(jnp.where(lg2 == v2, lane, 128.0), axis=-1, keepdims=True)
    e2 = jnp.exp(v2 - v1)
    return i1, i2, 1.0 / (1.0 + e2), e2 / (1.0 + e2)


R_E1, R_E2, R_W1, R_W2, R_RANK1, R_RANK2 = range(6)


def _route_kernel(x_ref, mod_ref, g_ref, r_ref, tri_ref, route_ref, route_t_ref, cnt_ref, carry_ref):
    @pl.when(pl.program_id(0) == 0)
    def _():
        carry_ref[...] = jnp.zeros(carry_ref.shape, F32)

    t = x_ref.shape[0]
    h = _norm_mod(x_ref[...], g_ref[...], mod_ref[0, 3:4, :], mod_ref[0, 4:5, :])
    r = r_ref[...]
    h_hi, r_hi = h.astype(BF16), r.astype(BF16)
    h_lo, r_lo = (h - h_hi.astype(F32)).astype(BF16), (r - r_hi.astype(F32)).astype(BF16)
    logits = (jnp.dot(h_hi, r_hi, preferred_element_type=F32) + jnp.dot(h_hi, r_lo, preferred_element_type=F32)
              + jnp.dot(h_lo, r_hi, preferred_element_type=F32))
    i1, i2, w1, w2 = _top2(logits)
    lane = lax.broadcasted_iota(jnp.int32, (t, 128), 1).astype(F32)
    oh1 = jnp.where(lane == i1, 1.0, 0.0)
    oh2 = jnp.where(lane == i2, 1.0, 0.0)
    both = oh1 + oh2
    before = carry_ref[...] + jnp.dot(tri_ref[...], both.astype(BF16), preferred_element_type=F32)
    rank1 = jnp.sum(oh1 * before, axis=-1, keepdims=True)
    rank2 = jnp.sum(oh2 * before, axis=-1, keepdims=True)
    carry_ref[...] += jnp.sum(both, axis=0, keepdims=True)
    rec = jnp.zeros((t, 128), F32)
    for col, val in ((R_E1, i1), (R_E2, i2), (R_W1, w1), (R_W2, w2), (R_RANK1, rank1), (R_RANK2, rank2)):
        rec = jnp.where(lane == col, val, rec)
    route_ref[...] = rec
    route_t_ref[0] = rec.T[0:8, :]
    cnt_ref[...] = carry_ref[...]


def moe_route(x2d, mod, g, router_pad, *, tm, tiles_per_mod):
    m, d = x2d.shape
    n_mod = mod.shape[0]
    mod_idx = (lambda i: (i // tiles_per_mod, 0, 0)) if n_mod > 1 else (lambda i: (0, 0, 0))
    tri = jnp.asarray(np.tril(np.ones((tm, tm), np.float32), -1), BF16)
    return pl.pallas_call(
        _route_kernel,
        grid=(m // tm,),
        in_specs=[
            pl.BlockSpec((tm, d), lambda i: (i, 0)),
            pl.BlockSpec((1, 6, d), mod_idx),
            pl.BlockSpec((1, d), lambda i: (0, 0)),
            pl.BlockSpec((d, 128), lambda i: (0, 0)),
            pl.BlockSpec((tm, tm), lambda i: (0, 0)),
        ],
        out_specs=[pl.BlockSpec((tm, 128), lambda i: (i, 0)), pl.BlockSpec((1, 8, tm), lambda i: (i, 0, 0)),
                   pl.BlockSpec((1, 128), lambda i: (0, 0))],
        out_shape=[jax.ShapeDtypeStruct((m, 128), F32), jax.ShapeDtypeStruct((m // tm, 8, tm), F32),
                   jax.ShapeDtypeStruct((1, 128), F32)],
        scratch_shapes=[pltpu.VMEM((1, 128), F32)],
        compiler_params=_cparams(1),
        name="moe_route",
    )(x2d, mod, g.reshape(1, d), router_pad, tri)


def _dispatch_kernel(pos_ref, pad_ref, x_ref, mod_ref, g_ref, xs_ref, h_ref, zero_ref, sem, zsem):
    i = pl.program_id(0)
    t = x_ref.shape[0]
    slot = i % 2
    h_ref[slot] = _norm_mod(x_ref[...], g_ref[...], mod_ref[0, 3:4, :], mod_ref[0, 4:5, :])

    def row_copy(r, dst_row):
        return pltpu.make_async_copy(h_ref.at[slot, pl.ds(r, 1), :], xs_ref.at[pl.ds(dst_row, 1), :], sem.at[slot])

    def issue(r, carry):
        row_copy(r, pos_ref[0, 0, r]).start()
        row_copy(r, pos_ref[0, 0, t + r]).start()
        return carry

    lax.fori_loop(0, t, issue, 0, unroll=DMA_ISSUE_UNROLL)

    def drain(s):
        for _ in range(2):
            pltpu.make_async_copy(h_ref.at[s], xs_ref.at[pl.ds(0, t), :], sem.at[s]).wait()

    @pl.when(i > 0)
    def _():
        drain(1 - slot)

    def zero_row_copy(r):
        return pltpu.make_async_copy(zero_ref.at[pl.ds(0, 1), :], xs_ref.at[pl.ds(r, 1), :], zsem)

    def zero_tile_copy(k):
        return pltpu.make_async_copy(zero_ref, xs_ref.at[pl.ds(pl.multiple_of(k * t, t), t), :], zsem)

    def for_each_zero_copy(act):
        def row(r, carry):
            act(zero_row_copy(r))
            return carry

        def tile(k, carry):
            act(zero_tile_copy(k))
            return carry

        for e in range(N_EXPERTS):
            lax.fori_loop(pad_ref[0, e], pad_ref[0, N_EXPERTS + e], row, 0)
        lax.fori_loop(pad_ref[0, 2 * N_EXPERTS], xs_ref.shape[0] // t, tile, 0)

    @pl.when(i == 0)
    def _():
        zero_ref[...] = jnp.zeros(zero_ref.shape, F32)
        for_each_zero_copy(lambda cp: cp.start())

    @pl.when(i == pl.num_programs(0) - 1)
    def _():
        drain(slot)
        for_each_zero_copy(lambda cp: cp.wait())


def moe_dispatch(x2d, mod, g, pos_tiles, pad_rows, n_rows, *, tm, tiles_per_mod):
    m, d = x2d.shape
    n_mod = mod.shape[0]
    mod_idx = (lambda i: (i // tiles_per_mod, 0, 0)) if n_mod > 1 else (lambda i: (0, 0, 0))
    return pl.pallas_call(
        _dispatch_kernel,
        grid=(m // tm,),
        in_specs=[
            pl.BlockSpec((1, 1, 2 * tm), lambda i: (i, 0, 0), memory_space=pltpu.SMEM),
            pl.BlockSpec((1, 2 * N_EXPERTS + 1), lambda i: (0, 0), memory_space=pltpu.SMEM),
            pl.BlockSpec((tm, d), lambda i: (i, 0)),
            pl.BlockSpec((1, 6, d), mod_idx),
            pl.BlockSpec((1, d), lambda i: (0, 0)),
        ],
        out_specs=pl.BlockSpec(memory_space=pl.ANY),
        out_shape=jax.ShapeDtypeStruct((n_rows, d), F32),
        scratch_shapes=[pltpu.VMEM((2, tm, d), F32), pltpu.VMEM((tm, d), F32),
                        pltpu.SemaphoreType.DMA((2,)), pltpu.SemaphoreType.DMA(())],
        compiler_params=_cparams(1),
        name="moe_dispatch",
    )(pos_tiles, pad_rows, x2d, mod, g.reshape(1, d))


def _experts_kernel(te_ref, nv_ref, xs_ref, w1_ref, w3_ref, w2_ref, ys_ref, xb_ref, acc_ref):
    i = pl.program_id(0)
    j = pl.program_id(1)
    valid = i < nv_ref[0]

    @pl.when(jnp.logical_and(valid, j == 0))
    def _():
        xb_ref[...] = xs_ref[...].astype(BF16)
        acc_ref[...] = jnp.zeros(acc_ref.shape, F32)

    @pl.when(valid)
    def _():
        h = xb_ref[...]
        tf = w1_ref.shape[2]
        y = None
        for c0 in range(0, tf, 1024):
            cw = min(1024, tf - c0)
            a = jnp.dot(h, w1_ref[0, :, c0:c0 + cw], preferred_element_type=F32)
            b = jnp.dot(h, w3_ref[0, :, c0:c0 + cw], preferred_element_type=F32)
            yc = jnp.dot((_silu(a) * b).astype(BF16), w2_ref[0, c0:c0 + cw, :], preferred_element_type=F32)
            y = yc if y is None else y + yc
        acc_ref[...] += y

    @pl.when(jnp.logical_and(valid, j == pl.num_programs(1) - 1))
    def _():
        ys_ref[...] = acc_ref[...]

    @pl.when(jnp.logical_and(jnp.logical_not(valid), j == pl.num_programs(1) - 1))
    def _():
        ys_ref[...] = jnp.zeros(ys_ref.shape, F32)


def moe_experts_grouped(xs, tile_expert, n_valid, w1, w3, w2, *, tm, tf):
    n_rows, d = xs.shape
    dff = w1.shape[2]
    nf = dff // tf

    def w13_idx(i, j, te, nv):
        return (te[i], 0, jnp.where(i < nv[0], j, nf - 1))

    def w2_idx(i, j, te, nv):
        return (te[i], jnp.where(i < nv[0], j, nf - 1), 0)

    grid_spec = pltpu.PrefetchScalarGridSpec(
        num_scalar_prefetch=2,
        grid=(n_rows // tm, nf),
        in_specs=[
            pl.BlockSpec((tm, d), lambda i, j, te, nv: (jnp.minimum(i, nv[0] - 1), 0)),
            pl.BlockSpec((1, d, tf), w13_idx),
            pl.BlockSpec((1, d, tf), w13_idx),
            pl.BlockSpec((1, tf, d), w2_idx),
        ],
        out_specs=pl.BlockSpec((tm, d), lambda i, j, te, nv: (i, 0)),
        scratch_shapes=[pltpu.VMEM((tm, d), BF16), pltpu.VMEM((tm, d), F32)],
    )
    return pl.pallas_call(
        _experts_kernel,
        grid_spec=grid_spec,
        out_shape=jax.ShapeDtypeStruct((n_rows, d), F32),
        compiler_params=_cparams(2),
        name="moe_experts_grouped",
    )(tile_expert, n_valid, xs, w1, w3, w2)


def _combine_kernel(pos_ref, pos_next_ref, x_ref, mod_ref, rt_ref, ys_ref, o_ref, y_ref, sem):
    i = pl.program_id(0)
    t = x_ref.shape[0]
    slot = i % 2

    def issue_tile(p_ref, s):
        def issue(r, carry):
            for k in range(2):
                pltpu.make_async_copy(ys_ref.at[pl.ds(p_ref[0, 0, k * t + r], 1), :],
                                      y_ref.at[s, k, pl.ds(r, 1), :], sem.at[s]).start()
            return carry

        lax.fori_loop(0, t, issue, 0, unroll=DMA_ISSUE_UNROLL)

    @pl.when(i == 0)
    def _():
        issue_tile(pos_ref, 0)

    @pl.when(i + 1 < pl.num_programs(0))
    def _():
        issue_tile(pos_next_ref, 1 - slot)

    for k in range(2):
        pltpu.make_async_copy(ys_ref.at[pl.ds(0, t), :], y_ref.at[slot, k], sem.at[slot]).wait()
    rt = rt_ref[...]
    mix = rt[:, R_W1:R_W1 + 1] * y_ref[slot, 0] + rt[:, R_W2:R_W2 + 1] * y_ref[slot, 1]
    o_ref[...] = x_ref[...] + mod_ref[0, 5:6, :] * mix


def moe_combine(x2d, mod, route, pos_tiles, ys, *, tm, tiles_per_mod):
    m, d = x2d.shape
    n_mod = mod.shape[0]
    mod_idx = (lambda i: (i // tiles_per_mod, 0, 0)) if n_mod > 1 else (lambda i: (0, 0, 0))
    n_tiles = m // tm
    return pl.pallas_call(
        _combine_kernel,
        grid=(n_tiles,),
        in_specs=[
            pl.BlockSpec((1, 1, 2 * tm), lambda i: (i, 0, 0), memory_space=pltpu.SMEM),
            pl.BlockSpec((1, 1, 2 * tm), lambda i: (jnp.minimum(i + 1, n_tiles - 1), 0, 0), memory_space=pltpu.SMEM),
            pl.BlockSpec((tm, d), lambda i: (i, 0)),
            pl.BlockSpec((1, 6, d), mod_idx),
            pl.BlockSpec((tm, 128), lambda i: (i, 0)),
            pl.BlockSpec(memory_space=pl.ANY),
        ],
        out_specs=pl.BlockSpec((tm, d), lambda i: (i, 0)),
        out_shape=jax.ShapeDtypeStruct((m, d), F32),
        scratch_shapes=[pltpu.VMEM((2, 2, tm, d), F32), pltpu.SemaphoreType.DMA((2,))],
        compiler_params=_cparams(1),
        name="moe_combine",
    )(pos_tiles, pos_tiles, x2d, mod, route, ys)


def _pos_tiles(pos1, pos2, tm):
    n = pos1.shape[0] // tm
    return jnp.concatenate([pos1.reshape(n, 1, tm), pos2.reshape(n, 1, tm)], axis=2)


def moe_sparse(x2d, mod, g, router_pad, w1, w3, w2, *, rows_per_mod):
    m, d = x2d.shape
    tr, td, tc, te = MOE_ROUTE_TM, MOE_DISPATCH_TM, MOE_COMBINE_TM, MOE_EXPERT_TM
    route, route_t, cnt = moe_route(x2d, mod, g, router_pad, tm=tr, tiles_per_mod=rows_per_mod // tr)
    counts = cnt[0, 0:N_EXPERTS].astype(jnp.int32)
    group = ((counts + te - 1) // te) * te
    ends = jnp.cumsum(group)
    starts = ends - group
    field = lambda f: route_t[:, f, :].reshape(m).astype(jnp.int32)
    pos1 = starts[field(R_E1)] + field(R_RANK1)
    pos2 = starts[field(R_E2)] + field(R_RANK2)
    n_rows = 2 * m + N_EXPERTS * te
    n_tiles = n_rows // te
    tile_start = jnp.arange(n_tiles, dtype=jnp.int32)[:, None] * te
    tile_expert = jnp.minimum(jnp.sum((tile_start >= ends[None, :]).astype(jnp.int32), axis=1), N_EXPERTS - 1)
    n_valid = (ends[-1:] // te).astype(jnp.int32)
    assert td == te
    pad_rows = jnp.concatenate([starts + counts, ends, n_valid]).astype(jnp.int32).reshape(1, 2 * N_EXPERTS + 1)
    xs = moe_dispatch(x2d, mod, g, _pos_tiles(pos1, pos2, td), pad_rows, n_rows, tm=td,
                      tiles_per_mod=rows_per_mod // td)
    ys = moe_experts_grouped(xs, tile_expert, n_valid, w1, w3, w2, tm=te, tf=MOE_EXPERT_TF)
    return moe_combine(x2d, mod, route, _pos_tiles(pos1, pos2, tc), ys, tm=tc, tiles_per_mod=rows_per_mod // tc)


def _permute_w_in(w):
    f, c, p, q, kv, gts = w[:, 0:256], w[:, 256:768], w[:, 768:1024], w[:, 1024:1536], w[:, 1536:1792], w[:, 1792:]
    return jnp.concatenate([gts, q, c, p, f, kv], axis=1).astype(BF16)


def _layer_weights(layer, w_br_fourier, conv_dw, conv_b, conv_norm_g, w_br_conv, pool_w, pool_scale, w_br_pool,
                   w_br_attn, w_out):
    pw = jax.scipy.linalg.block_diag(*[pool_w[layer, i] for i in range(len(POOL_WINDOWS))])
    return {
        "wf": w_br_fourier[layer].astype(BF16), "wc": w_br_conv[layer].astype(BF16),
        "wp": w_br_pool[layer].astype(BF16), "wa": w_br_attn[layer].astype(BF16), "wo": w_out[layer].astype(BF16),
        "dw": conv_dw[layer], "cb": conv_b[layer].reshape(1, CONV_W), "cg": conv_norm_g[layer].reshape(1, CONV_W),
        "pw": pw.astype(BF16), "ps": pool_scale[layer].reshape(1, POOL_W),
    }


def _cast_slices(w, max_slices):
    w2 = w.reshape(-1, w.shape[-1])
    rows = w2.shape[0]
    n = 1
    while 2 * n <= max_slices and rows % (32 * n) == 0:
        n *= 2
    return w2.reshape(n, rows // n, w2.shape[1])


def kernel(x, c, ctx, c_ctx, w_mod, b_mod, norm1_g, norm2_g, w_in, w_br_fourier, conv_dw, conv_b, conv_norm_g,
           w_br_conv, pool_w, pool_scale, w_br_pool, q_norm_g, k_norm_g, w_br_attn, w_out, ffn_w1, ffn_w3, ffn_w2,
           moe_router, moe_w1, moe_w3, moe_w2):
    batch, seq, d = x.shape
    ctx_len = ctx.shape[1]
    depth = w_in.shape[0]
    rope = rope_tables(seq)

    c_rows = jnp.zeros((8, d), F32).at[0:batch].set(c).at[batch].set(c_ctx)
    mods = modulation_all(c_rows, w_mod, b_mod).reshape(depth, 8, 6, d)

    xl = x.reshape(batch * seq, d)
    xc = ctx.reshape(batch * ctx_len, d)
    for layer in range(depth):
        is_last = layer == depth - 1
        mod_l = mods[layer, 0:batch]
        mod_c = mods[layer, batch:batch + 1]
        w_in_l = _permute_w_in(w_in[layer])
        lw = _layer_weights(layer, w_br_fourier, conv_dw, conv_b, conv_norm_g, w_br_conv, pool_w, pool_scale,
                            w_br_pool, w_br_attn, w_out)

        proj_c, qc, ktc, vc = input_projection(xc, mod_c, norm1_g[layer], w_in_l, q_norm_g[layer], k_norm_g[layer],
                                               None, batch=batch, seq=ctx_len, tm=256)

        proj, q, kt, v = input_projection(xl, mod_l, norm1_g[layer], w_in_l, q_norm_g[layer], k_norm_g[layer], rope,
                                          batch=batch, seq=seq, tm=512)
        spread = (2.0 * 1.02 * HEAD_DIM * Q_SCALE) * jnp.max(jnp.abs(q_norm_g[layer])) * jnp.max(
            jnp.abs(k_norm_g[layer]))
        n_steps = batch * N_KV_HEADS * (seq // ATTN_TQ)
        mixer_w = ((ffn_w1, ffn_w3, ffn_w2) if layer % 2 == 0 else (moe_w1, moe_w3, moe_w2))
        mixer_w = tuple(w[layer // 2] for w in mixer_w)
        side = tuple(_cast_slices(w, n_steps) for w in mixer_w)
        attn, *side_bf16 = lax.cond(
            spread < STALE_MAX_EXP_LIMIT,
            lambda ops, sd: attention_stale_max(*ops, sd, batch=batch, seq_q=seq, tq=ATTN_TQ, tk=ATTN_TK),
            lambda ops, sd: (attention(*ops, batch=batch, seq_q=seq, tq=ATTN_TQ, tk=ATTN_TK),)
            + tuple(w.astype(BF16) for w in sd),
            (q, kt, v, ktc, vc), side)
        yf = fourier_mix(proj[:, P_OFF_F:P_OFF_F + FOURIER_W], batch=batch, seq=seq, n1=64, n2=seq // 64)
        xl = merge_branches(xl, mod_l, proj, yf, attn, lw, seq=seq, t=512)

        if not is_last:
            attn_c = attention(qc, None, None, ktc, vc, batch=batch, seq_q=ctx_len, tq=256, tk=ctx_len)
            yf_c = fourier_mix(proj_c[:, P_OFF_F:P_OFF_F + FOURIER_W], batch=batch, seq=ctx_len, n1=16,
                               n2=ctx_len // 16)
            xc = merge_branches(xc, mod_c, proj_c, yf_c, attn_c, lw, seq=ctx_len, t=256)

        j = layer // 2
        w1, w3, w2 = (wb.reshape(w.shape) for wb, w in zip(side_bf16, mixer_w))
        if layer % 2 == 0:
            xl = ffn_dense(xl, mod_l, norm2_g[layer], w1, w3, w2, tm=512, tiles_per_mod=seq // 512)
            if not is_last:
                xc = ffn_dense(xc, mod_c, norm2_g[layer], w1, w3, w2, tm=256, tiles_per_mod=1)
        else:
            router_pad = jnp.zeros((d, 128), F32).at[:, 0:N_EXPERTS].set(moe_router[j])
            xl = moe_sparse(xl, mod_l, norm2_g[layer], router_pad, w1, w3, w2, rows_per_mod=seq)
            if not is_last:
                xc = moe_sparse(xc, mod_c, norm2_g[layer], router_pad, w1, w3, w2, rows_per_mod=batch * ctx_len)
    return xl.reshape(batch, seq, d)
```

```python
import functools
import math

import numpy as np
import jax
import jax.numpy as jnp
from jax import lax
from jax.experimental import pallas as pl
from jax.experimental.pallas import tpu as pltpu

F32 = jnp.float32
BF16 = jnp.bfloat16

D_MODEL = 1024
GRID_W = 64
EPS = 1e-6
FOURIER_GW = 64
FOURIER_W = 256
CONV_W = 256
CONV_K = 31
CONV_HALF = CONV_K // 2
POOL_WINDOWS = (2, 4, 8, 16)
POOL_GW = 64
POOL_W = 256
HEAD_DIM = 64
N_KV_HEADS = 2
Q_PER_KV = 4
Q_W = 512
KV_W = 128
ROPE_THETA = 10000.0
N_EXPERTS = 8

P_OFF_Q = 4096
P_OFF_CP = 4608
P_OFF_F = 5376
P_OFF_KV = 5632
CP_W = 2 * CONV_W + POOL_W

Q_SCALE = (HEAD_DIM ** -0.5) * math.log2(math.e)

ATTN_TQ = 512
ATTN_TK = 1024
STALE_MAX_EXP_LIMIT = 64.0

MOE_ROUTE_TM = 512
MOE_DISPATCH_TM = 512
MOE_COMBINE_TM = 512
MOE_EXPERT_TM = 512
MOE_EXPERT_TF = 1792
DMA_ISSUE_UNROLL = 8

CONV_ROWS = 64
HALO = 16
VMEM_LIMIT = 56 * 1024 * 1024


def _cparams(n_axes):
    return pltpu.CompilerParams(dimension_semantics=("arbitrary",) * n_axes, vmem_limit_bytes=VMEM_LIMIT)


def _sigmoid(v):
    return 0.5 * jnp.tanh(0.5 * v) + 0.5


def _silu(v):
    return v * _sigmoid(v)


def _norm_mod(x, g, shift, scale):
    ms = jnp.mean(x * x, axis=-1, keepdims=True)
    return x * lax.rsqrt(ms + EPS) * g * (1.0 + scale) + shift


def _mod_kernel(c_ref, w_ref, b_ref, o_ref):
    s = _silu(c_ref[...])
    o_ref[0] = jnp.dot(s, w_ref[0], preferred_element_type=F32, precision=lax.Precision.HIGHEST) + b_ref[0]


def modulation_all(c_rows, w_mod, b_mod):
    n_layers, d, n = w_mod.shape
    tn = 1536
    return pl.pallas_call(
        _mod_kernel,
        grid=(n_layers, n // tn),
        in_specs=[
            pl.BlockSpec((8, d), lambda l, j: (0, 0)),
            pl.BlockSpec((1, d, tn), lambda l, j: (l, 0, j)),
            pl.BlockSpec((1, 1, tn), lambda l, j: (l, 0, j)),
        ],
        out_specs=pl.BlockSpec((1, 8, tn), lambda l, j: (l, 0, j)),
        out_shape=jax.ShapeDtypeStruct((n_layers, 8, n), F32),
        compiler_params=_cparams(2),
        name="modulation",
    )(c_rows, w_mod, b_mod.reshape(n_layers, 1, n))


def _inproj_kernel(*refs, chunks, use_rope):
    if use_rope:
        x_ref, mod_ref, g_ref, w_ref, gq_ref, gk_ref, ones_ref, cos_ref, sin_ref, o_ref, qo_ref, kt_ref, v_ref = refs
        cos, sin = cos_ref[...], sin_ref[...]
    else:
        x_ref, mod_ref, g_ref, w_ref, gq_ref, gk_ref, ones_ref, o_ref, qo_ref, kt_ref, v_ref = refs
        cos = sin = None
    h = _norm_mod(x_ref[...], g_ref[...], mod_ref[0, 0:1, :], mod_ref[0, 1:2, :]).astype(BF16)
    heavy_first = sorted(chunks, key=lambda c: (c[0] not in (P_OFF_Q, P_OFF_KV), c[0] != P_OFF_CP, c[0]))
    for c0, cw in heavy_first:
        r = jnp.dot(h, w_ref[:, c0:c0 + cw], preferred_element_type=F32)
        if c0 + cw <= P_OFF_Q:
            r = _sigmoid(r)
        elif c0 == P_OFF_CP:
            sg = _sigmoid(r[:, CONV_W:2 * CONV_W])
            r = jnp.concatenate([r[:, 0:CONV_W] * sg, sg], axis=1)
        elif c0 == P_OFF_Q:
            _q_epilogue(r, gq_ref[...], ones_ref[...], cos, sin, qo_ref)
        elif c0 == P_OFF_KV:
            _kv_epilogue(r, gk_ref[...], ones_ref[...], cos, sin, kt_ref, v_ref)
        o_ref[:, c0:c0 + cw] = r.astype(o_ref.dtype)


def input_projection(x2d, mod, g, w_bf16, gq, gk, rope, *, batch, seq, tm):
    m, d = x2d.shape
    n = w_bf16.shape[1]
    tps = seq // tm
    chunks = tuple((c0, min(512, n - c0)) for c0 in range(0, n, 512))
    assert P_OFF_Q % 512 == 0 and {(P_OFF_CP, 2 * CONV_W), (P_OFF_Q, Q_W), (P_OFF_KV, 2 * KV_W)} <= set(chunks)
    use_rope = rope is not None
    n_mod = mod.shape[0]
    mod_idx = (lambda i: (i // tps, 0, 0)) if n_mod > 1 else (lambda i: (0, 0, 0))
    const = lambda i: (0, 0)
    ones_bd = jnp.asarray(np.kron(np.eye(2, dtype=np.float32), np.ones((64, 64), np.float32)), BF16)
    in_specs = [
        pl.BlockSpec((tm, d), lambda i: (i, 0)),
        pl.BlockSpec((1, 6, d), mod_idx),
        pl.BlockSpec((1, d), const),
        pl.BlockSpec((d, n), const, pipeline_mode=pl.Buffered(1)),
        pl.BlockSpec((1, 128), const),
        pl.BlockSpec((1, 128), const),
        pl.BlockSpec((128, 128), const),
    ]
    args = [x2d, mod, g.reshape(1, d), w_bf16, jnp.tile(gq, 2).reshape(1, 128), jnp.tile(gk, 2).reshape(1, 128), ones_bd]
    if use_rope:
        in_specs += [pl.BlockSpec((tm, 128), lambda i: (i % tps, 0))] * 2
        args += list(rope)
    return pl.pallas_call(
        functools.partial(_inproj_kernel, chunks=chunks, use_rope=use_rope),
        grid=(m // tm,),
        in_specs=in_specs,
        out_specs=[
            pl.BlockSpec((tm, n), lambda i: (i, 0)),
            pl.BlockSpec((1, N_KV_HEADS, 256, tm), lambda i: (i // tps, 0, 0, i % tps)),
            pl.BlockSpec((1, N_KV_HEADS, tm, 256), lambda i: (i // tps, 0, i % tps, 0)),
            pl.BlockSpec((1, N_KV_HEADS, 128, tm), lambda i: (i // tps, 0, 0, i % tps)),
        ],
        out_shape=[
            jax.ShapeDtypeStruct((m, n), BF16),
            jax.ShapeDtypeStruct((batch, N_KV_HEADS, 256, seq), BF16),
            jax.ShapeDtypeStruct((batch, N_KV_HEADS, seq, 256), BF16),
            jax.ShapeDtypeStruct((batch, N_KV_HEADS, 128, seq), BF16),
        ],
        compiler_params=_cparams(1),
        name="input_projection",
    )(*args)


def _seg_sum64(v, ones_bd):
    hi = v.astype(BF16)
    lo = (v - hi.astype(F32)).astype(BF16)
    return (jnp.dot(hi, ones_bd, preferred_element_type=F32) + jnp.dot(lo, ones_bd, preferred_element_type=F32))


def _head_norm_rope(x, g, ones_bd, cos, sin, low_mask):
    y = x * lax.rsqrt(_seg_sum64(x * x, ones_bd) * (1.0 / HEAD_DIM) + EPS) * g
    if cos is None:
        return y
    partner = jnp.where(low_mask, pltpu.roll(y, 128 - 16, axis=1), pltpu.roll(y, 16, axis=1))
    return y * cos + partner * sin


def _rope_low_mask(t):
    return (lax.broadcasted_iota(jnp.int32, (t, 128), 1) % 32) < 16


def _q_epilogue(rq, gq, ones_bd, cos, sin, qo_ref):
    low_mask = _rope_low_mask(rq.shape[0])
    for c in range(Q_W // 128):
        yq = _head_norm_rope(rq[:, 128 * c:128 * (c + 1)], gq, ones_bd, cos, sin, low_mask) * Q_SCALE
        qo_ref[0, c // 2, 128 * (c % 2):128 * (c % 2 + 1), :] = yq.T.astype(BF16)


def _kv_epilogue(rkv, gk, ones_bd, cos, sin, kt_ref, v_ref):
    t = rkv.shape[0]
    yk = _head_norm_rope(rkv[:, 0:128], gk, ones_bd, cos, sin, _rope_low_mask(t))
    ykr = pltpu.roll(yk, 64, axis=1)
    first = lax.broadcasted_iota(jnp.int32, (t, 128), 1) < 64
    k0 = jnp.where(first, yk, ykr).astype(BF16)
    k1 = jnp.where(first, ykr, yk).astype(BF16)
    kt_ref[0, 0] = jnp.concatenate([k0, k0], axis=1)
    kt_ref[0, 1] = jnp.concatenate([k1, k1], axis=1)
    vt = rkv[:, 128:256].T
    ones = jnp.ones((HEAD_DIM, t), F32)
    for h in range(N_KV_HEADS):
        v_ref[0, h] = jnp.concatenate([vt[64 * h:64 * (h + 1), :], ones], axis=0).astype(BF16)


def rope_tables(seq):
    n_freq = HEAD_DIM // 4
    freqs = ROPE_THETA ** (-jnp.arange(n_freq, dtype=F32) / n_freq)
    t = jnp.arange(seq)
    row = (t // GRID_W).astype(F32)
    col = (t % GRID_W).astype(F32)
    ang_r = row[:, None] * freqs
    ang_c = col[:, None] * freqs
    cos = jnp.concatenate([jnp.cos(ang_r)] * 2 + [jnp.cos(ang_c)] * 2, axis=1)
    sin = jnp.concatenate([-jnp.sin(ang_r), jnp.sin(ang_r), -jnp.sin(ang_c), jnp.sin(ang_c)], axis=1)
    return jnp.tile(cos, (1, 2)), jnp.tile(sin, (1, 2))


def _attn_kernel(*refs, tq, tk, nk, tail):
    refs = list(refs)
    qt_ref = refs.pop(0)
    k_ref, vt_ref = (refs.pop(0), refs.pop(0)) if nk else (None, None)
    kc_ref, vtc_ref = (refs.pop(0), refs.pop(0)) if tail else (None, None)
    o_ref, qs_ref, s0, s1, p0, p1, a0, a1, mx0, mx1, m_ref, acc_ref = refs
    s_bufs, p_bufs, a_bufs, mx_bufs = (s0, s1), (p0, p1), (a0, a1), (mx0, mx1)
    n_blocks = nk + (1 if tail else 0)

    _attn_stack_queries(qt_ref, qs_ref, tq)
    m_ref[...] = jnp.full(m_ref.shape, -jnp.inf, F32)
    acc_ref[...] = jnp.zeros(acc_ref.shape, F32)

    def block(t):
        if isinstance(t, int) and t >= nk:
            return kc_ref[0, 0], vtc_ref[0, 0], tail
        off = t * tk if isinstance(t, int) else pl.multiple_of(t * tk, tk)
        return k_ref[0, 0, pl.ds(off, tk), :], vt_ref[0, 0, :, pl.ds(off, tk)], tk

    def scores(t, slot):
        k_rows, _, n = block(t)
        s = jnp.dot(k_rows, qs_ref[...], preferred_element_type=F32)
        s_bufs[slot][0:n, :] = s
        mx_bufs[slot][...] = jnp.max(s, axis=0, keepdims=True)

    def numerators(n, slot):
        s_ref, p_ref, a_ref = s_bufs[slot], p_bufs[slot], a_bufs[slot]
        for c0 in range(0, Q_PER_KV * tq, 128):
            cols = slice(c0, c0 + 128)
            m_old = m_ref[:, cols]
            m_new = jnp.maximum(m_old, mx_bufs[slot][:, cols])
            a_ref[:, cols] = jnp.exp2(m_old - m_new)
            p_ref[0:n, cols] = jnp.exp2(s_ref[0:n, cols] - m_new).astype(BF16)
            m_ref[:, cols] = m_new

    def weighted_sum(t, slot):
        _, vt, n = block(t)
        pv = jnp.dot(vt, p_bufs[slot][0:n, :], preferred_element_type=F32)
        acc_ref[...] = a_bufs[slot][...] * acc_ref[...] + pv

    def rows_of(t):
        return tk if t < nk else tail

    def step(t, slot, n_mid):
        scores(t, slot)
        numerators(n_mid, 1 - slot)
        weighted_sum(t - 2, slot)

    scores(0, 0)
    if n_blocks > 1:
        scores(1, 1)
        numerators(rows_of(0), 0)
        n_pairs = max(nk - 2, 0) // 2

        def pair(i, carry):
            t = 2 + 2 * i
            step(t, 0, tk)
            step(t + 1, 1, tk)
            return carry

        if n_pairs:
            lax.fori_loop(0, n_pairs, pair, 0)
        for t in range(2 + 2 * n_pairs, n_blocks):
            step(t, t % 2, rows_of(t - 1))
        last = n_blocks - 1
        numerators(rows_of(last), last % 2)
        weighted_sum(last - 1, (last - 1) % 2)
        weighted_sum(last, last % 2)
    else:
        numerators(rows_of(0), 0)
        weighted_sum(0, 0)

    _attn_write_output(acc_ref, o_ref, tq)


def _attn_write_output(acc_ref, o_ref, tq):
    acc = acc_ref[...]
    ot = acc[0:HEAD_DIM, :] / acc[HEAD_DIM:2 * HEAD_DIM, :]
    for half in range(2):
        pair_t = jnp.concatenate([ot[:, (2 * half) * tq:(2 * half + 1) * tq],
                                  ot[:, (2 * half + 1) * tq:(2 * half + 2) * tq]], axis=0)
        o_ref[:, 128 * half:128 * (half + 1)] = pair_t.T.astype(o_ref.dtype)


def _attn_stack_queries(qt_ref, qs_ref, tq):
    row_group = lax.broadcasted_iota(jnp.int32, (256, tq), 0) // HEAD_DIM
    qt = qt_ref[0, 0]
    for g in range(Q_PER_KV):
        qs_ref[:, g * tq:(g + 1) * tq] = jnp.where(row_group == g, qt, jnp.zeros_like(qt))


def _attn_stale_max_kernel(*refs, tq, tk, nk, n_side):
    qt_ref, k_ref, vt_ref, kc_ref, vtc_ref = refs[:5]
    side_in = refs[5:5 + n_side]
    o_ref = refs[5 + n_side]
    side_out = refs[6 + n_side:6 + 2 * n_side]
    qs_ref, p0, p1, f0, f1, m_ref, acc_ref = refs[6 + 2 * n_side:]
    for src, dst in zip(side_in, side_out):
        dst[...] = src[...].astype(dst.dtype)
    p_bufs, f_bufs = (p0, p1), (f0, f1)
    _attn_stack_queries(qt_ref, qs_ref, tq)

    s = jnp.dot(kc_ref[0, 0], qs_ref[...], preferred_element_type=F32)
    m0 = jnp.max(s, axis=0, keepdims=True)
    m_ref[...] = m0
    acc_ref[...] = jnp.dot(vtc_ref[0, 0], jnp.exp2(s - m0).astype(BF16), preferred_element_type=F32)

    def numerators(t, slot):
        off = t * tk if isinstance(t, int) else pl.multiple_of(t * tk, tk)
        s = jnp.dot(k_ref[0, 0, pl.ds(off, tk), :], qs_ref[...], preferred_element_type=F32)
        m_old = m_ref[...]
        p_bufs[slot][...] = jnp.exp2(s - m_old).astype(BF16)
        m_new = jnp.maximum(m_old, jnp.max(s, axis=0, keepdims=True))
        f_bufs[slot][...] = jnp.exp2(m_old - m_new)
        m_ref[...] = m_new

    def weighted_sum(t, slot):
        off = t * tk if isinstance(t, int) else pl.multiple_of(t * tk, tk)
        pv = jnp.dot(vt_ref[0, 0, :, pl.ds(off, tk)], p_bufs[slot][...], preferred_element_type=F32)
        acc_ref[...] = (acc_ref[...] + pv) * f_bufs[slot][...]

    def step(t, slot):
        numerators(t, slot)
        weighted_sum(t - 1, 1 - slot)

    numerators(0, 0)
    n_pairs = (nk - 1) // 2

    def pair(i, carry):
        t = 1 + 2 * i
        step(t, 1)
        step(t + 1, 0)
        return carry

    if n_pairs:
        lax.fori_loop(0, n_pairs, pair, 0)
    for t in range(1 + 2 * n_pairs, nk):
        step(t, t % 2)
    weighted_sum(nk - 1, (nk - 1) % 2)
    _attn_write_output(acc_ref, o_ref, tq)


def attention_stale_max(qt, k4, vt1, k4_tail, vt1_tail, side=(), *, batch, seq_q, tq, tk):
    nq = seq_q // tq
    lanes = Q_PER_KV * tq
    lk = k4.shape[2]
    tail = k4_tail.shape[2]
    n_steps = batch * N_KV_HEADS * nq
    assert all(n_steps % w.shape[0] == 0 for w in side)

    def side_spec(w):
        repeat = n_steps // w.shape[0]
        return pl.BlockSpec((1,) + w.shape[1:], lambda b, h, i: (((b * N_KV_HEADS + h) * nq + i) // repeat, 0, 0))

    side_specs = [side_spec(w) for w in side]
    outs = pl.pallas_call(
        functools.partial(_attn_stale_max_kernel, tq=tq, tk=tk, nk=lk // tk, n_side=len(side)),
        grid=(batch, N_KV_HEADS, nq),
        in_specs=[
            pl.BlockSpec((1, 1, 256, tq), lambda b, h, i: (b, h, 0, i)),
            pl.BlockSpec((1, 1, lk, 256), lambda b, h, i: (b, h, 0, 0)),
            pl.BlockSpec((1, 1, 128, lk), lambda b, h, i: (b, h, 0, 0)),
            pl.BlockSpec((1, 1, tail, 256), lambda b, h, i: (b, h, 0, 0)),
            pl.BlockSpec((1, 1, 128, tail), lambda b, h, i: (b, h, 0, 0)),
        ] + side_specs,
        out_specs=[pl.BlockSpec((tq, 256), lambda b, h, i: (b * nq + i, h))] + side_specs,
        out_shape=[jax.ShapeDtypeStruct((batch * seq_q, Q_W), BF16)]
        + [jax.ShapeDtypeStruct(w.shape, BF16) for w in side],
        scratch_shapes=[
            pltpu.VMEM((256, lanes), BF16),
            pltpu.VMEM((tk, lanes), BF16), pltpu.VMEM((tk, lanes), BF16),
            pltpu.VMEM((1, lanes), F32), pltpu.VMEM((1, lanes), F32),
            pltpu.VMEM((1, lanes), F32),
            pltpu.VMEM((2 * HEAD_DIM, lanes), F32),
        ],
        compiler_params=_cparams(3),
        name="attention_stale_max",
    )(qt, k4, vt1, k4_tail, vt1_tail, *side)
    return tuple(outs)


def attention(qt, k4, vt1, k4_tail, vt1_tail, *, batch, seq_q, tq, tk):
    nq = seq_q // tq
    lanes = Q_PER_KV * tq
    nk = 0 if k4 is None else k4.shape[2] // tk
    tail = 0 if k4_tail is None else k4_tail.shape[2]
    buf_rows = max(tk if nk else 0, tail)
    in_specs = [pl.BlockSpec((1, 1, 256, tq), lambda b, h, i: (b, h, 0, i))]
    args = [qt]
    if nk:
        lk = k4.shape[2]
        in_specs += [pl.BlockSpec((1, 1, lk, 256), lambda b, h, i: (b, h, 0, 0)),
                     pl.BlockSpec((1, 1, 128, lk), lambda b, h, i: (b, h, 0, 0))]
        args += [k4, vt1]
    if tail:
        in_specs += [pl.BlockSpec((1, 1, tail, 256), lambda b, h, i: (b, h, 0, 0)),
                     pl.BlockSpec((1, 1, 128, tail), lambda b, h, i: (b, h, 0, 0))]
        args += [k4_tail, vt1_tail]
    return pl.pallas_call(
        functools.partial(_attn_kernel, tq=tq, tk=tk, nk=nk, tail=tail),
        grid=(batch, N_KV_HEADS, nq),
        in_specs=in_specs,
        out_specs=pl.BlockSpec((tq, 256), lambda b, h, i: (b * nq + i, h)),
        out_shape=jax.ShapeDtypeStruct((batch * seq_q, Q_W), BF16),
        scratch_shapes=[
            pltpu.VMEM((256, lanes), BF16),
            pltpu.VMEM((buf_rows, lanes), F32), pltpu.VMEM((buf_rows, lanes), F32),
            pltpu.VMEM((buf_rows, lanes), BF16), pltpu.VMEM((buf_rows, lanes), BF16),
            pltpu.VMEM((1, lanes), F32), pltpu.VMEM((1, lanes), F32),
            pltpu.VMEM((1, lanes), F32), pltpu.VMEM((1, lanes), F32),
            pltpu.VMEM((1, lanes), F32),
            pltpu.VMEM((2 * HEAD_DIM, lanes), F32),
        ],
        compiler_params=_cparams(3),
        name="attention",
    )(*args)


def _dft_cs(n):
    k = np.arange(n)
    ang = 2.0 * np.pi * ((k[:, None] * k[None, :]) % n) / n
    return np.cos(ang), np.sin(ang)


def _fft1_kernel(x_ref, f_ref, c_ref, s_ref, o_ref, *, n1):
    y = jnp.dot(f_ref[...], x_ref[0], preferred_element_type=F32)
    yr, yi = y[:n1], y[n1:]
    c, s = c_ref[...], s_ref[...]
    o_ref[0, 0] = (yr * c + yi * s).astype(o_ref.dtype)
    o_ref[0, 1] = (yi * c - yr * s).astype(o_ref.dtype)


def _fft2_kernel(y_ref, f_ref, bc_ref, bs_ref, o_ref, *, n2, kb):
    for j in range(kb):
        y2 = jnp.concatenate([y_ref[0, 0, j], y_ref[0, 1, j]], axis=0)
        x2 = jnp.dot(f_ref[...], y2, preferred_element_type=F32)
        xr = x2[:n2].astype(BF16)
        xi = x2[n2:].astype(BF16)
        z = (jnp.dot(xr, bc_ref[...], preferred_element_type=F32) + jnp.dot(xi, bs_ref[...], preferred_element_type=F32))
        o_ref[0, j] = z.astype(o_ref.dtype)


def fourier_mix(u, *, batch, seq, n1, n2):
    cw = u.shape[1]
    lanes = n2 * cw
    tl = min(lanes, 4096)
    c1, s1 = _dft_cs(n1)
    f1 = jnp.asarray(np.concatenate([c1, -s1], axis=0), BF16)
    k1 = np.arange(n1)[:, None]
    t2 = np.arange(n2)[None, :]
    ang = 2.0 * np.pi * ((k1 * t2) % seq) / seq
    twc = jnp.asarray(np.repeat(np.cos(ang), cw, axis=1), F32)
    tws = jnp.asarray(np.repeat(np.sin(ang), cw, axis=1), F32)
    x2 = u.reshape(batch, n1, lanes)
    yp = pl.pallas_call(
        functools.partial(_fft1_kernel, n1=n1),
        grid=(batch, lanes // tl),
        in_specs=[
            pl.BlockSpec((1, n1, tl), lambda b, j: (b, 0, j)),
            pl.BlockSpec((2 * n1, n1), lambda b, j: (0, 0)),
            pl.BlockSpec((n1, tl), lambda b, j: (0, j)),
            pl.BlockSpec((n1, tl), lambda b, j: (0, j)),
        ],
        out_specs=pl.BlockSpec((1, 2, n1, tl), lambda b, j: (b, 0, 0, j)),
        out_shape=jax.ShapeDtypeStruct((batch, 2, n1, lanes), BF16),
        compiler_params=_cparams(2),
        name="fft_stage1",
    )(x2, f1, twc, tws)

    c2, s2 = _dft_cs(n2)
    f2 = jnp.asarray(np.block([[c2, s2], [-s2, c2]]), BF16)
    cg, sg = _dft_cs(FOURIER_GW)
    norm = 1.0 / math.sqrt(seq * FOURIER_GW)
    bdc = jnp.asarray(np.kron(np.eye(cw // FOURIER_GW), cg) * norm, BF16)
    bds = jnp.asarray(np.kron(np.eye(cw // FOURIER_GW), sg) * norm, BF16)
    kb = min(n1, 16)
    y5 = yp.reshape(batch, 2, n1, n2, cw)
    z = pl.pallas_call(
        functools.partial(_fft2_kernel, n2=n2, kb=kb),
        grid=(batch, n1 // kb),
        in_specs=[
            pl.BlockSpec((1, 2, kb, n2, cw), lambda b, j: (b, 0, j, 0, 0)),
            pl.BlockSpec((2 * n2, 2 * n2), lambda b, j: (0, 0)),
            pl.BlockSpec((cw, cw), lambda b, j: (0, 0)),
            pl.BlockSpec((cw, cw), lambda b, j: (0, 0)),
        ],
        out_specs=pl.BlockSpec((1, kb, n2, cw), lambda b, j: (b, j, 0, 0)),
        out_shape=jax.ShapeDtypeStruct((batch, n1, n2, cw), BF16),
        compiler_params=_cparams(2),
        name="fft_stage2",
    )(y5, f2, bdc, bds)
    return z.transpose(0, 2, 1, 3).reshape(batch * seq, cw)


def _merge_kernel(x_ref, mod_ref, gate_ref, cp_ref, cpp_ref, cpn_ref, yf_ref, at_ref,
                  wf_ref, wc_ref, wp_ref, wa_ref, wo_ref, dw_ref, cb_ref, cg_ref, pw_ref, ps_ref, band_ref, icnt_ref,
                  o_ref, ybuf, xbuf, ysh, cacc, *, t, tps):
    i = pl.program_id(0)
    pos_tile = i % tps
    keep_prev = jnp.where(pos_tile != 0, 1.0, 0.0).astype(F32)
    keep_next = jnp.where(pos_tile != tps - 1, 1.0, 0.0).astype(F32)

    def glu(blk):
        return blk[:, 0:CONV_W].astype(F32)

    cp, cpp, cpn = cp_ref[...], cpp_ref[...], cpn_ref[...]
    ybuf[0:HALO, :] = glu(cpp) * keep_prev
    ybuf[HALO:HALO + t, :] = glu(cp)
    ybuf[HALO + t:HALO + t + HALO, :] = glu(cpn) * keep_next
    xbuf[0:HALO, :] = cpp[:, 2 * CONV_W:] * keep_prev.astype(BF16)
    xbuf[HALO:HALO + t, :] = cp[:, 2 * CONV_W:]
    xbuf[HALO + t:HALO + t + HALO, :] = cpn[:, 2 * CONV_W:] * keep_next.astype(BF16)

    def gate(b):
        return gate_ref[:, b * D_MODEL:(b + 1) * D_MODEL].astype(F32)

    merged = gate(0) * jnp.dot(yf_ref[...], wf_ref[...], preferred_element_type=F32)
    merged = merged + gate(3) * jnp.dot(at_ref[...], wa_ref[...], preferred_element_type=F32)

    n_sh = t + 2 * HALO - 8
    for b in range(1, 8):
        ysh[b - 1, 0:n_sh, :] = ybuf[pl.ds(b, n_sh), :]
    for r0 in range(0, t, CONV_ROWS):
        part = jnp.zeros((CONV_ROWS, CONV_W), F32)
        for k in range(CONV_K):
            a, b = divmod(HALO - CONV_HALF + k, 8)
            src = ybuf if b == 0 else ysh.at[b - 1]
            part = part + dw_ref[k:k + 1, :] * src[8 * a + r0:8 * a + r0 + CONV_ROWS, :]
        cacc[r0:r0 + CONV_ROWS, :] = part + cb_ref[...]
    acc = cacc[...]
    ms = jnp.mean(acc * acc, axis=-1, keepdims=True)
    conv_out = _silu(acc * lax.rsqrt(ms + EPS) * cg_ref[...]).astype(BF16)

    pb = band_ref.shape[1]
    grp = lax.broadcasted_iota(jnp.int32, (pb + 2 * HALO, POOL_W), 1) // POOL_GW
    parts = []
    for r0 in range(0, t, pb):
        xw = xbuf[r0:r0 + pb + 2 * HALO, :]
        wsum = jnp.zeros((pb, POOL_W), F32)
        for gi in range(len(POOL_WINDOWS)):
            wsum = wsum + jnp.dot(band_ref[gi], jnp.where(grp == gi, xw, jnp.zeros_like(xw)),
                                  preferred_element_type=F32)
        x0 = xbuf[HALO + r0:HALO + r0 + pb, :].astype(F32)
        parts.append((wsum * icnt_ref[0, r0:r0 + pb, :] - x0).astype(BF16))
    pool_in = parts[0] if len(parts) == 1 else jnp.concatenate(parts, axis=0)
    pool_out = (jnp.dot(pool_in, pw_ref[...], preferred_element_type=F32) * ps_ref[...]).astype(BF16)

    merged = merged + gate(1) * jnp.dot(conv_out, wc_ref[...], preferred_element_type=F32)
    merged = merged + gate(2) * jnp.dot(pool_out, wp_ref[...], preferred_element_type=F32)
    out = jnp.dot(merged.astype(BF16), wo_ref[...], preferred_element_type=F32)
    o_ref[...] = x_ref[...] + mod_ref[0, 2:3, :] * out


def _pool_tables(seq, t):
    pb = min(t, 256)
    r = np.arange(pb)[:, None]
    j = np.arange(pb + 2 * HALO)[None, :]
    band = np.stack([((j >= r + HALO - w // 2) & (j < r + HALO + w // 2)) for w in POOL_WINDOWS]).astype(np.float32)
    half = np.repeat(np.array(POOL_WINDOWS) // 2, POOL_GW)[None, :]
    rows = np.arange(t)[:, None]
    icnt = []
    for first, last in ((0, 0), (1, 0), (0, 1), (1, 1)):
        if first and last:
            pos, length = rows, t
        elif first:
            pos, length = rows, 2 * t + 2 * HALO
        elif last:
            pos, length = rows + seq - t, seq
        else:
            pos, length = rows + t + 2 * HALO, 4 * t
        cnt = np.minimum(pos + half, length) - np.maximum(pos - half, 0)
        icnt.append(1.0 / cnt)
    return jnp.asarray(band, BF16), jnp.asarray(np.stack(icnt), F32)


def merge_branches(x2d, mod, proj, yf, attn, lw, *, seq, t):
    m, d = x2d.shape
    tps = seq // t
    hb = t // HALO
    n_halo = m // HALO
    n_mod = mod.shape[0]
    mod_idx = (lambda i: (i // tps, 0, 0)) if n_mod > 1 else (lambda i: (0, 0, 0))
    const = lambda i: (0, 0)
    cp_blk = P_OFF_CP // CP_W
    band, icnt = _pool_tables(seq, t)
    pb = band.shape[1]

    def icnt_idx(i):
        pos_tile = i % tps
        return ((pos_tile == 0).astype(jnp.int32) + 2 * (pos_tile == tps - 1).astype(jnp.int32), 0, 0)

    return pl.pallas_call(
        functools.partial(_merge_kernel, t=t, tps=tps),
        grid=(m // t,),
        in_specs=[
            pl.BlockSpec((t, d), lambda i: (i, 0)),
            pl.BlockSpec((1, 6, d), mod_idx),
            pl.BlockSpec((t, 4 * d), lambda i: (i, 0)),
            pl.BlockSpec((t, CP_W), lambda i: (i, cp_blk)),
            pl.BlockSpec((HALO, CP_W), lambda i: (jnp.maximum(i * hb - 1, 0), cp_blk)),
            pl.BlockSpec((HALO, CP_W), lambda i: (jnp.minimum((i + 1) * hb, n_halo - 1), cp_blk)),
            pl.BlockSpec((t, FOURIER_W), lambda i: (i, 0)),
            pl.BlockSpec((t, Q_W), lambda i: (i, 0)),
            pl.BlockSpec((FOURIER_W, d), const),
            pl.BlockSpec((CONV_W, d), const),
            pl.BlockSpec((POOL_W, d), const),
            pl.BlockSpec((Q_W, d), const),
            pl.BlockSpec((d, d), const),
            pl.BlockSpec((CONV_K, CONV_W), const),
            pl.BlockSpec((1, CONV_W), const),
            pl.BlockSpec((1, CONV_W), const),
            pl.BlockSpec((POOL_W, POOL_W), const),
            pl.BlockSpec((1, POOL_W), const),
            pl.BlockSpec((len(POOL_WINDOWS), pb, pb + 2 * HALO), lambda i: (0, 0, 0)),
            pl.BlockSpec((1, t, POOL_W), icnt_idx),
        ],
        out_specs=pl.BlockSpec((t, d), lambda i: (i, 0)),
        out_shape=jax.ShapeDtypeStruct((m, d), F32),
        scratch_shapes=[pltpu.VMEM((t + 2 * HALO, CONV_W), F32), pltpu.VMEM((t + 2 * HALO, POOL_W), BF16),
                        pltpu.VMEM((7, t + 2 * HALO, CONV_W), F32), pltpu.VMEM((t, CONV_W), F32)],
        compiler_params=_cparams(1),
        name="merge_branches",
    )(x2d, mod, proj, proj, proj, proj, yf, attn,
      lw["wf"], lw["wc"], lw["wp"], lw["wa"], lw["wo"], lw["dw"], lw["cb"], lw["cg"], lw["pw"], lw["ps"], band, icnt)


def _ffn_kernel(x_ref, mod_ref, g_ref, w1_ref, w3_ref, w2_ref, o_ref, *, chunks):
    x = x_ref[...]
    h = _norm_mod(x, g_ref[...], mod_ref[0, 3:4, :], mod_ref[0, 4:5, :]).astype(BF16)
    acc = jnp.zeros(x.shape, F32)
    for c0, cw in chunks:
        a = jnp.dot(h, w1_ref[:, c0:c0 + cw], preferred_element_type=F32)
        b = jnp.dot(h, w3_ref[:, c0:c0 + cw], preferred_element_type=F32)
        acc = acc + jnp.dot((_silu(a) * b).astype(BF16), w2_ref[c0:c0 + cw, :], preferred_element_type=F32)
    o_ref[...] = x + mod_ref[0, 5:6, :] * acc


def ffn_dense(x2d, mod, g, w1, w3, w2, *, tm, tiles_per_mod):
    m, d = x2d.shape
    dff = w1.shape[1]
    chunks = tuple((c0, min(1024, dff - c0)) for c0 in range(0, dff, 1024))
    n_mod = mod.shape[0]
    mod_idx = (lambda i: (i // tiles_per_mod, 0, 0)) if n_mod > 1 else (lambda i: (0, 0, 0))
    const = lambda i: (0, 0)
    return pl.pallas_call(
        functools.partial(_ffn_kernel, chunks=chunks),
        grid=(m // tm,),
        in_specs=[
            pl.BlockSpec((tm, d), lambda i: (i, 0)),
            pl.BlockSpec((1, 6, d), mod_idx),
            pl.BlockSpec((1, d), const),
            pl.BlockSpec((d, dff), const, pipeline_mode=pl.Buffered(1)),
            pl.BlockSpec((d, dff), const, pipeline_mode=pl.Buffered(1)),
            pl.BlockSpec((dff, d), const, pipeline_mode=pl.Buffered(1)),
        ],
        out_specs=pl.BlockSpec((tm, d), lambda i: (i, 0)),
        out_shape=jax.ShapeDtypeStruct((m, d), F32),
        compiler_params=_cparams(1),
        name="ffn_dense",
    )(x2d, mod, g.reshape(1, d), w1, w3, w2)


def _top2(logits):
    t = logits.shape[0]
    lane = lax.broadcasted_iota(jnp.int32, (t, 128), 1).astype(F32)
    neg = jnp.float32(-jnp.inf)
    lg = jnp.where(lane < N_EXPERTS, logits, neg)
    v1 = jnp.max(lg, axis=-1, keepdims=True)
    i1 = jnp.min(jnp.where(lg == v1, lane, 128.0), axis=-1, keepdims=True)
    lg2 = jnp.where(lane == i1, neg, lg)
    v2 = jnp.max(lg2, axis=-1, keepdims=True)
    i2 = jnp.min(jnp.where(lg2 == v2, lane, 128.0), axis=-1, keepdims=True)
    e2 = jnp.exp(v2 - v1)
    return i1, i2, 1.0 / (1.0 + e2), e2 / (1.0 + e2)


R_E1, R_E2, R_W1, R_W2, R_RANK1, R_RANK2 = range(6)


def _route_kernel(x_ref, mod_ref, g_ref, r_ref, tri_ref, route_ref, route_t_ref, cnt_ref, carry_ref):
    @pl.when(pl.program_id(0) == 0)
    def _():
        carry_ref[...] = jnp.zeros(carry_ref.shape, F32)

    t = x_ref.shape[0]
    h = _norm_mod(x_ref[...], g_ref[...], mod_ref[0, 3:4, :], mod_ref[0, 4:5, :])
    r = r_ref[...]
    h_hi, r_hi = h.astype(BF16), r.astype(BF16)
    h_lo, r_lo = (h - h_hi.astype(F32)).astype(BF16), (r - r_hi.astype(F32)).astype(BF16)
    logits = (jnp.dot(h_hi, r_hi, preferred_element_type=F32) + jnp.dot(h_hi, r_lo, preferred_element_type=F32)
              + jnp.dot(h_lo, r_hi, preferred_element_type=F32))
    i1, i2, w1, w2 = _top2(logits)
    lane = lax.broadcasted_iota(jnp.int32, (t, 128), 1).astype(F32)
    oh1 = jnp.where(lane == i1, 1.0, 0.0)
    oh2 = jnp.where(lane == i2, 1.0, 0.0)
    both = oh1 + oh2
    before = carry_ref[...] + jnp.dot(tri_ref[...], both.astype(BF16), preferred_element_type=F32)
    rank1 = jnp.sum(oh1 * before, axis=-1, keepdims=True)
    rank2 = jnp.sum(oh2 * before, axis=-1, keepdims=True)
    carry_ref[...] += jnp.sum(both, axis=0, keepdims=True)
    rec = jnp.zeros((t, 128), F32)
    for col, val in ((R_E1, i1), (R_E2, i2), (R_W1, w1), (R_W2, w2), (R_RANK1, rank1), (R_RANK2, rank2)):
        rec = jnp.where(lane == col, val, rec)
    route_ref[...] = rec
    route_t_ref[0] = rec.T[0:8, :]
    cnt_ref[...] = carry_ref[...]


def moe_route(x2d, mod, g, router_pad, *, tm, tiles_per_mod):
    m, d = x2d.shape
    n_mod = mod.shape[0]
    mod_idx = (lambda i: (i // tiles_per_mod, 0, 0)) if n_mod > 1 else (lambda i: (0, 0, 0))
    tri = jnp.asarray(np.tril(np.ones((tm, tm), np.float32), -1), BF16)
    return pl.pallas_call(
        _route_kernel,
        grid=(m // tm,),
        in_specs=[
            pl.BlockSpec((tm, d), lambda i: (i, 0)),
            pl.BlockSpec((1, 6, d), mod_idx),
            pl.BlockSpec((1, d), lambda i: (0, 0)),
            pl.BlockSpec((d, 128), lambda i: (0, 0)),
            pl.BlockSpec((tm, tm), lambda i: (0, 0)),
        ],
        out_specs=[pl.BlockSpec((tm, 128), lambda i: (i, 0)), pl.BlockSpec((1, 8, tm), lambda i: (i, 0, 0)),
                   pl.BlockSpec((1, 128), lambda i: (0, 0))],
        out_shape=[jax.ShapeDtypeStruct((m, 128), F32), jax.ShapeDtypeStruct((m // tm, 8, tm), F32),
                   jax.ShapeDtypeStruct((1, 128), F32)],
        scratch_shapes=[pltpu.VMEM((1, 128), F32)],
        compiler_params=_cparams(1),
        name="moe_route",
    )(x2d, mod, g.reshape(1, d), router_pad, tri)


def _dispatch_kernel(pos_ref, pad_ref, x_ref, mod_ref, g_ref, xs_ref, h_ref, zero_ref, sem, zsem):
    i = pl.program_id(0)
    t = x_ref.shape[0]
    slot = i % 2
    h_ref[slot] = _norm_mod(x_ref[...], g_ref[...], mod_ref[0, 3:4, :], mod_ref[0, 4:5, :])

    def row_copy(r, dst_row):
        return pltpu.make_async_copy(h_ref.at[slot, pl.ds(r, 1), :], xs_ref.at[pl.ds(dst_row, 1), :], sem.at[slot])

    def issue(r, carry):
        row_copy(r, pos_ref[0, 0, r]).start()
        row_copy(r, pos_ref[0, 0, t + r]).start()
        return carry

    lax.fori_loop(0, t, issue, 0, unroll=DMA_ISSUE_UNROLL)

    def drain(s):
        for _ in range(2):
            pltpu.make_async_copy(h_ref.at[s], xs_ref.at[pl.ds(0, t), :], sem.at[s]).wait()

    @pl.when(i > 0)
    def _():
        drain(1 - slot)

    def zero_row_copy(r):
        return pltpu.make_async_copy(zero_ref.at[pl.ds(0, 1), :], xs_ref.at[pl.ds(r, 1), :], zsem)

    def zero_tile_copy(k):
        return pltpu.make_async_copy(zero_ref, xs_ref.at[pl.ds(pl.multiple_of(k * t, t), t), :], zsem)

    def for_each_zero_copy(act):
        def row(r, carry):
            act(zero_row_copy(r))
            return carry

        def tile(k, carry):
            act(zero_tile_copy(k))
            return carry

        for e in range(N_EXPERTS):
            lax.fori_loop(pad_ref[0, e], pad_ref[0, N_EXPERTS + e], row, 0)
        lax.fori_loop(pad_ref[0, 2 * N_EXPERTS], xs_ref.shape[0] // t, tile, 0)

    @pl.when(i == 0)
    def _():
        zero_ref[...] = jnp.zeros(zero_ref.shape, F32)
        for_each_zero_copy(lambda cp: cp.start())

    @pl.when(i == pl.num_programs(0) - 1)
    def _():
        drain(slot)
        for_each_zero_copy(lambda cp: cp.wait())


def moe_dispatch(x2d, mod, g, pos_tiles, pad_rows, n_rows, *, tm, tiles_per_mod):
    m, d = x2d.shape
    n_mod = mod.shape[0]
    mod_idx = (lambda i: (i // tiles_per_mod, 0, 0)) if n_mod > 1 else (lambda i: (0, 0, 0))
    return pl.pallas_call(
        _dispatch_kernel,
        grid=(m // tm,),
        in_specs=[
            pl.BlockSpec((1, 1, 2 * tm), lambda i: (i, 0, 0), memory_space=pltpu.SMEM),
            pl.BlockSpec((1, 2 * N_EXPERTS + 1), lambda i: (0, 0), memory_space=pltpu.SMEM),
            pl.BlockSpec((tm, d), lambda i: (i, 0)),
            pl.BlockSpec((1, 6, d), mod_idx),
            pl.BlockSpec((1, d), lambda i: (0, 0)),
        ],
        out_specs=pl.BlockSpec(memory_space=pl.ANY),
        out_shape=jax.ShapeDtypeStruct((n_rows, d), F32),
        scratch_shapes=[pltpu.VMEM((2, tm, d), F32), pltpu.VMEM((tm, d), F32),
                        pltpu.SemaphoreType.DMA((2,)), pltpu.SemaphoreType.DMA(())],
        compiler_params=_cparams(1),
        name="moe_dispatch",
    )(pos_tiles, pad_rows, x2d, mod, g.reshape(1, d))


def _experts_kernel(te_ref, nv_ref, xs_ref, w1_ref, w3_ref, w2_ref, ys_ref, xb_ref, acc_ref):
    i = pl.program_id(0)
    j = pl.program_id(1)
    valid = i < nv_ref[0]

    @pl.when(jnp.logical_and(valid, j == 0))
    def _():
        xb_ref[...] = xs_ref[...].astype(BF16)
        acc_ref[...] = jnp.zeros(acc_ref.shape, F32)

    @pl.when(valid)
    def _():
        h = xb_ref[...]
        tf = w1_ref.shape[2]
        y = None
        for c0 in range(0, tf, 1024):
            cw = min(1024, tf - c0)
            a = jnp.dot(h, w1_ref[0, :, c0:c0 + cw], preferred_element_type=F32)
            b = jnp.dot(h, w3_ref[0, :, c0:c0 + cw], preferred_element_type=F32)
            yc = jnp.dot((_silu(a) * b).astype(BF16), w2_ref[0, c0:c0 + cw, :], preferred_element_type=F32)
            y = yc if y is None else y + yc
        acc_ref[...] += y

    @pl.when(jnp.logical_and(valid, j == pl.num_programs(1) - 1))
    def _():
        ys_ref[...] = acc_ref[...]

    @pl.when(jnp.logical_and(jnp.logical_not(valid), j == pl.num_programs(1) - 1))
    def _():
        ys_ref[...] = jnp.zeros(ys_ref.shape, F32)


def moe_experts_grouped(xs, tile_expert, n_valid, w1, w3, w2, *, tm, tf):
    n_rows, d = xs.shape
    dff = w1.shape[2]
    nf = dff // tf

    def w13_idx(i, j, te, nv):
        return (te[i], 0, jnp.where(i < nv[0], j, nf - 1))

    def w2_idx(i, j, te, nv):
        return (te[i], jnp.where(i < nv[0], j, nf - 1), 0)

    grid_spec = pltpu.PrefetchScalarGridSpec(
        num_scalar_prefetch=2,
        grid=(n_rows // tm, nf),
        in_specs=[
            pl.BlockSpec((tm, d), lambda i, j, te, nv: (jnp.minimum(i, nv[0] - 1), 0)),
            pl.BlockSpec((1, d, tf), w13_idx),
            pl.BlockSpec((1, d, tf), w13_idx),
            pl.BlockSpec((1, tf, d), w2_idx),
        ],
        out_specs=pl.BlockSpec((tm, d), lambda i, j, te, nv: (i, 0)),
        scratch_shapes=[pltpu.VMEM((tm, d), BF16), pltpu.VMEM((tm, d), F32)],
    )
    return pl.pallas_call(
        _experts_kernel,
        grid_spec=grid_spec,
        out_shape=jax.ShapeDtypeStruct((n_rows, d), F32),
        compiler_params=_cparams(2),
        name="moe_experts_grouped",
    )(tile_expert, n_valid, xs, w1, w3, w2)


def _combine_kernel(pos_ref, pos_next_ref, x_ref, mod_ref, rt_ref, ys_ref, o_ref, y_ref, sem):
    i = pl.program_id(0)
    t = x_ref.shape[0]
    slot = i % 2

    def issue_tile(p_ref, s):
        def issue(r, carry):
            for k in range(2):
                pltpu.make_async_copy(ys_ref.at[pl.ds(p_ref[0, 0, k * t + r], 1), :],
                                      y_ref.at[s, k, pl.ds(r, 1), :], sem.at[s]).start()
            return carry

        lax.fori_loop(0, t, issue, 0, unroll=DMA_ISSUE_UNROLL)

    @pl.when(i == 0)
    def _():
        issue_tile(pos_ref, 0)

    @pl.when(i + 1 < pl.num_programs(0))
    def _():
        issue_tile(pos_next_ref, 1 - slot)

    for k in range(2):
        pltpu.make_async_copy(ys_ref.at[pl.ds(0, t), :], y_ref.at[slot, k], sem.at[slot]).wait()
    rt = rt_ref[...]
    mix = rt[:, R_W1:R_W1 + 1] * y_ref[slot, 0] + rt[:, R_W2:R_W2 + 1] * y_ref[slot, 1]
    o_ref[...] = x_ref[...] + mod_ref[0, 5:6, :] * mix


def moe_combine(x2d, mod, route, pos_tiles, ys, *, tm, tiles_per_mod):
    m, d = x2d.shape
    n_mod = mod.shape[0]
    mod_idx = (lambda i: (i // tiles_per_mod, 0, 0)) if n_mod > 1 else (lambda i: (0, 0, 0))
    n_tiles = m // tm
    return pl.pallas_call(
        _combine_kernel,
        grid=(n_tiles,),
        in_specs=[
            pl.BlockSpec((1, 1, 2 * tm), lambda i: (i, 0, 0), memory_space=pltpu.SMEM),
            pl.BlockSpec((1, 1, 2 * tm), lambda i: (jnp.minimum(i + 1, n_tiles - 1), 0, 0), memory_space=pltpu.SMEM),
            pl.BlockSpec((tm, d), lambda i: (i, 0)),
            pl.BlockSpec((1, 6, d), mod_idx),
            pl.BlockSpec((tm, 128), lambda i: (i, 0)),
            pl.BlockSpec(memory_space=pl.ANY),
        ],
        out_specs=pl.BlockSpec((tm, d), lambda i: (i, 0)),
        out_shape=jax.ShapeDtypeStruct((m, d), F32),
        scratch_shapes=[pltpu.VMEM((2, 2, tm, d), F32), pltpu.SemaphoreType.DMA((2,))],
        compiler_params=_cparams(1),
        name="moe_combine",
    )(pos_tiles, pos_tiles, x2d, mod, route, ys)


def _pos_tiles(pos1, pos2, tm):
    n = pos1.shape[0] // tm
    return jnp.concatenate([pos1.reshape(n, 1, tm), pos2.reshape(n, 1, tm)], axis=2)


def moe_sparse(x2d, mod, g, router_pad, w1, w3, w2, *, rows_per_mod):
    m, d = x2d.shape
    tr, td, tc, te = MOE_ROUTE_TM, MOE_DISPATCH_TM, MOE_COMBINE_TM, MOE_EXPERT_TM
    route, route_t, cnt = moe_route(x2d, mod, g, router_pad, tm=tr, tiles_per_mod=rows_per_mod // tr)
    counts = cnt[0, 0:N_EXPERTS].astype(jnp.int32)
    group = ((counts + te - 1) // te) * te
    ends = jnp.cumsum(group)
    starts = ends - group
    field = lambda f: route_t[:, f, :].reshape(m).astype(jnp.int32)
    pos1 = starts[field(R_E1)] + field(R_RANK1)
    pos2 = starts[field(R_E2)] + field(R_RANK2)
    n_rows = 2 * m + N_EXPERTS * te
    n_tiles = n_rows // te
    tile_start = jnp.arange(n_tiles, dtype=jnp.int32)[:, None] * te
    tile_expert = jnp.minimum(jnp.sum((tile_start >= ends[None, :]).astype(jnp.int32), axis=1), N_EXPERTS - 1)
    n_valid = (ends[-1:] // te).astype(jnp.int32)
    assert td == te
    pad_rows = jnp.concatenate([starts + counts, ends, n_valid]).astype(jnp.int32).reshape(1, 2 * N_EXPERTS + 1)
    xs = moe_dispatch(x2d, mod, g, _pos_tiles(pos1, pos2, td), pad_rows, n_rows, tm=td,
                      tiles_per_mod=rows_per_mod // td)
    ys = moe_experts_grouped(xs, tile_expert, n_valid, w1, w3, w2, tm=te, tf=MOE_EXPERT_TF)
    return moe_combine(x2d, mod, route, _pos_tiles(pos1, pos2, tc), ys, tm=tc, tiles_per_mod=rows_per_mod // tc)


def _permute_w_in(w):
    f, c, p, q, kv, gts = w[:, 0:256], w[:, 256:768], w[:, 768:1024], w[:, 1024:1536], w[:, 1536:1792], w[:, 1792:]
    return jnp.concatenate([gts, q, c, p, f, kv], axis=1).astype(BF16)


def _layer_weights(layer, w_br_fourier, conv_dw, conv_b, conv_norm_g, w_br_conv, pool_w, pool_scale, w_br_pool,
                   w_br_attn, w_out):
    pw = jax.scipy.linalg.block_diag(*[pool_w[layer, i] for i in range(len(POOL_WINDOWS))])
    return {
        "wf": w_br_fourier[layer].astype(BF16), "wc": w_br_conv[layer].astype(BF16),
        "wp": w_br_pool[layer].astype(BF16), "wa": w_br_attn[layer].astype(BF16), "wo": w_out[layer].astype(BF16),
        "dw": conv_dw[layer], "cb": conv_b[layer].reshape(1, CONV_W), "cg": conv_norm_g[layer].reshape(1, CONV_W),
        "pw": pw.astype(BF16), "ps": pool_scale[layer].reshape(1, POOL_W),
    }


def _cast_slices(w, max_slices):
    w2 = w.reshape(-1, w.shape[-1])
    rows = w2.shape[0]
    n = 1
    while 2 * n <= max_slices and rows % (32 * n) == 0:
        n *= 2
    return w2.reshape(n, rows // n, w2.shape[1])


def kernel(x, c, ctx, c_ctx, w_mod, b_mod, norm1_g, norm2_g, w_in, w_br_fourier, conv_dw, conv_b, conv_norm_g,
           w_br_conv, pool_w, pool_scale, w_br_pool, q_norm_g, k_norm_g, w_br_attn, w_out, ffn_w1, ffn_w3, ffn_w2,
           moe_router, moe_w1, moe_w3, moe_w2):
    batch, seq, d = x.shape
    ctx_len = ctx.shape[1]
    depth = w_in.shape[0]
    rope = rope_tables(seq)

    c_rows = jnp.zeros((8, d), F32).at[0:batch].set(c).at[batch].set(c_ctx)
    mods = modulation_all(c_rows, w_mod, b_mod).reshape(depth, 8, 6, d)

    xl = x.reshape(batch * seq, d)
    xc = ctx.reshape(batch * ctx_len, d)
    for layer in range(depth):
        is_last = layer == depth - 1
        mod_l = mods[layer, 0:batch]
        mod_c = mods[layer, batch:batch + 1]
        w_in_l = _permute_w_in(w_in[layer])
        lw = _layer_weights(layer, w_br_fourier, conv_dw, conv_b, conv_norm_g, w_br_conv, pool_w, pool_scale,
                            w_br_pool, w_br_attn, w_out)

        proj_c, qc, ktc, vc = input_projection(xc, mod_c, norm1_g[layer], w_in_l, q_norm_g[layer], k_norm_g[layer],
                                               None, batch=batch, seq=ctx_len, tm=256)

        proj, q, kt, v = input_projection(xl, mod_l, norm1_g[layer], w_in_l, q_norm_g[layer], k_norm_g[layer], rope,
                                          batch=batch, seq=seq, tm=512)
        spread = (2.0 * 1.02 * HEAD_DIM * Q_SCALE) * jnp.max(jnp.abs(q_norm_g[layer])) * jnp.max(
            jnp.abs(k_norm_g[layer]))
        n_steps = batch * N_KV_HEADS * (seq // ATTN_TQ)
        mixer_w = ((ffn_w1, ffn_w3, ffn_w2) if layer % 2 == 0 else (moe_w1, moe_w3, moe_w2))
        mixer_w = tuple(w[layer // 2] for w in mixer_w)
        side = tuple(_cast_slices(w, n_steps) for w in mixer_w)
        attn, *side_bf16 = lax.cond(
            spread < STALE_MAX_EXP_LIMIT,
            lambda ops, sd: attention_stale_max(*ops, sd, batch=batch, seq_q=seq, tq=ATTN_TQ, tk=ATTN_TK),
            lambda ops, sd: (attention(*ops, batch=batch, seq_q=seq, tq=ATTN_TQ, tk=ATTN_TK),)
            + tuple(w.astype(BF16) for w in sd),
            (q, kt, v, ktc, vc), side)
        yf = fourier_mix(proj[:, P_OFF_F:P_OFF_F + FOURIER_W], batch=batch, seq=seq, n1=64, n2=seq // 64)
        xl = merge_branches(xl, mod_l, proj, yf, attn, lw, seq=seq, t=512)

        if not is_last:
            attn_c = attention(qc, None, None, ktc, vc, batch=batch, seq_q=ctx_len, tq=256, tk=ctx_len)
            yf_c = fourier_mix(proj_c[:, P_OFF_F:P_OFF_F + FOURIER_W], batch=batch, seq=ctx_len, n1=16,
                               n2=ctx_len // 16)
            xc = merge_branches(xc, mod_c, proj_c, yf_c, attn_c, lw, seq=ctx_len, t=256)

        j = layer // 2
        w1, w3, w2 = (wb.reshape(w.shape) for wb, w in zip(side_bf16, mixer_w))
        if layer % 2 == 0:
            xl = ffn_dense(xl, mod_l, norm2_g[layer], w1, w3, w2, tm=512, tiles_per_mod=seq // 512)
            if not is_last:
                xc = ffn_dense(xc, mod_c, norm2_g[layer], w1, w3, w2, tm=256, tiles_per_mod=1)
        else:
            router_pad = jnp.zeros((d, 128), F32).at[:, 0:N_EXPERTS].set(moe_router[j])
            xl = moe_sparse(xl, mod_l, norm2_g[layer], router_pad, w1, w3, w2, rows_per_mod=seq)
            if not is_last:
                xc = moe_sparse(xc, mod_c, norm2_g[layer], router_pad, w1, w3, w2, rows_per_mod=batch * ctx_len)
    return xl.reshape(batch, seq, d)
```
